```python
import math
import jax, jax.numpy as jnp
from jax import lax
import numpy as np

D_MODEL = 2048
BATCH = 4
SEQ = 2048
DEPTH = 1
DEC_BATCH = 8
DEC_SEQ = 8
PAST_LEN = 16384
PAGE_SIZE = 128

GDN_QK_HEADS = 16
GDN_V_HEADS = 32
GDN_HEAD_DIM = 128
GDN_CONV = 4
GDN_CHUNK = 64
GDN_QK_W = GDN_QK_HEADS * GDN_HEAD_DIM
GDN_V_W = GDN_V_HEADS * GDN_HEAD_DIM
GDN_QKV_W = 2 * GDN_QK_W + GDN_V_W

NSA_HEADS = 16
NSA_KV_HEADS = 4
NSA_GROUP = NSA_HEADS // NSA_KV_HEADS
NSA_HEAD_DIM = 128
NSA_Q_W = NSA_HEADS * NSA_HEAD_DIM
NSA_KV_W = NSA_KV_HEADS * NSA_HEAD_DIM
CMP_LEN = 32
CMP_STRIDE = 16
SEL_BLOCK = 64
SEL_TOP = 16
WINDOW = 512
SEL_QBLK = 64
WIN_QBLK = 128

REL_BUCKETS = 32
REL_MAX_DIST = 128

EPS = 1e-6
NEG = -1e30
BIG = 1e9

IN_SPLIT_SIZES = (GDN_QKV_W, GDN_V_W, GDN_V_HEADS, GDN_V_HEADS, NSA_Q_W) + (NSA_KV_W,) * 6 + (3 * NSA_HEADS, NSA_Q_W, D_MODEL, D_MODEL)
IN_W = sum(IN_SPLIT_SIZES)

kernel_name = 'hybrid_gdn_nsa_decoder_step'


def rmsnorm(x, g):
    xf = x.astype(jnp.float32)
    return xf * lax.rsqrt(jnp.mean(xf * xf, axis=-1, keepdims=True) + EPS) * g.astype(jnp.float32)


def l2norm(x):
    return x * lax.rsqrt(jnp.sum(x * x, axis=-1, keepdims=True) + EPS)


def t5_bucket(rel):
    n = jnp.maximum(rel, 0)
    exact = REL_BUCKETS // 2
    nf = jnp.maximum(n, 1).astype(jnp.float32)
    large = exact + (jnp.log(nf / exact) / math.log(REL_MAX_DIST / exact) * (REL_BUCKETS - exact)).astype(jnp.int32)
    return jnp.where(n < exact, n, jnp.minimum(large, REL_BUCKETS - 1))


def input_projection(x, norm_g, w_in):
    h = rmsnorm(x, norm_g).astype(x.dtype)
    u = jnp.einsum('btd,de->bte', h, w_in)
    points = np.cumsum(IN_SPLIT_SIZES)[:-1].tolist()
    return jnp.split(u, points, axis=-1)


def gated_delta_chunked(q, k, v, beta, g, s0):
    B, T, H, _ = q.shape
    C = GDN_CHUNK
    n = -(-T // C)
    pad = n * C - T

    def prep(a):
        a = jnp.pad(a, [(0, 0), (0, pad)] + [(0, 0)] * (a.ndim - 2))
        a = a.reshape(B, n, C, *a.shape[2:])
        return jnp.swapaxes(a, 2, 3)

    q, k, v, beta, g = prep(q), prep(k), prep(v), prep(beta), prep(g)
    gc = jnp.cumsum(g, axis=-1)
    ii = jnp.arange(C)
    tril = ii[:, None] >= ii[None, :]
    strict = ii[:, None] > ii[None, :]
    gamma = jnp.exp(jnp.where(tril, gc[..., :, None] - gc[..., None, :], -jnp.inf))
    kb = k * beta[..., None]
    a_mat = jnp.eye(C, dtype=q.dtype) + jnp.where(strict, jnp.einsum('bnhid,bnhjd->bnhij', kb, k) * gamma, 0.0)
    u = lax.linalg.triangular_solve(a_mat, v * beta[..., None], left_side=True, lower=True, unit_diagonal=True)
    w = lax.linalg.triangular_solve(a_mat, kb * jnp.exp(gc)[..., None], left_side=True, lower=True, unit_diagonal=True)
    attn = jnp.einsum('bnhid,bnhjd->bnhij', q, k) * gamma
    qg = q * jnp.exp(gc)[..., None]
    kdec = k * jnp.exp(gc[..., -1:] - gc)[..., None]
    glast = jnp.exp(gc[..., -1])

    def step(s, xs):
        u_c, w_c, attn_c, qg_c, kdec_c, gl_c = xs
        v_new = u_c - jnp.einsum('bhcd,bhde->bhce', w_c, s)
        o = jnp.einsum('bhcd,bhde->bhce', qg_c, s) + jnp.einsum('bhij,bhje->bhie', attn_c, v_new)
        s = s * gl_c[..., None, None] + jnp.einsum('bhcd,bhce->bhde', kdec_c, v_new)
        return s, o

    xs = tuple(jnp.moveaxis(a, 1, 0) for a in (u, w, attn, qg, kdec, glast))
    s_fin, o = lax.scan(step, s0, xs)
    o = o.transpose(1, 0, 3, 2, 4).reshape(B, n * C, H, v.shape[-1])[:, :T]
    return o, s_fin


def gdn_branch(qkv, z, b_raw, a_raw, conv_state, s0, conv_w, a_log, dt_bias, norm_g):
    B, T, _ = qkv.shape
    xp = jnp.concatenate([conv_state.astype(qkv.dtype), qkv], axis=1)
    conv = xp[:, 0:T] * conv_w[0]
    for j in range(1, GDN_CONV):
        conv = conv + xp[:, j:j + T] * conv_w[j]
    new_conv = xp[:, T:]
    act = jax.nn.silu(conv.astype(jnp.float32))
    q, k, v = jnp.split(act, [GDN_QK_W, 2 * GDN_QK_W], axis=-1)
    rep = GDN_V_HEADS // GDN_QK_HEADS
    q = jnp.repeat(l2norm(q.reshape(B, T, GDN_QK_HEADS, GDN_HEAD_DIM)), rep, axis=2) * GDN_HEAD_DIM ** -0.5
    k = jnp.repeat(l2norm(k.reshape(B, T, GDN_QK_HEADS, GDN_HEAD_DIM)), rep, axis=2)
    v = v.reshape(B, T, GDN_V_HEADS, GDN_HEAD_DIM)
    beta = jax.nn.sigmoid(b_raw.astype(jnp.float32))
    g = -jnp.exp(a_log.astype(jnp.float32)) * jax.nn.softplus(a_raw.astype(jnp.float32) + dt_bias.astype(jnp.float32))
    o, s_fin = gated_delta_chunked(q, k, v, beta, g, s0.astype(jnp.float32))
    o = rmsnorm(o, norm_g) * jax.nn.silu(z.astype(jnp.float32).reshape(B, T, GDN_V_HEADS, GDN_HEAD_DIM))
    return o.reshape(B, T, GDN_V_W), new_conv, s_fin.astype(s0.dtype)


def nsa_inputs(q_raw, kc, vc, ks, vs, kw, vw, q_norm_g, k_norm_g):
    B, T, _ = q_raw.shape
    heads = lambda a: a.reshape(B, T, NSA_KV_HEADS, NSA_HEAD_DIM)
    q = rmsnorm(q_raw.reshape(B, T, NSA_KV_HEADS, NSA_GROUP, NSA_HEAD_DIM), q_norm_g)
    ks_n = rmsnorm(heads(ks), k_norm_g[1]).astype(ks.dtype)
    kw_n = rmsnorm(heads(kw), k_norm_g[2]).astype(kw.dtype)
    return q, heads(kc), heads(vc), ks_n, heads(vs), kw_n, heads(vw)


def compress(kv, pe, w, proj):
    B, L, G, D = kv.shape
    nc = (L - CMP_LEN) // CMP_STRIDE + 1
    r = CMP_LEN // CMP_STRIDE
    chunks = kv[:, :(nc + r - 1) * CMP_STRIDE].astype(jnp.float32).reshape(B, nc + r - 1, CMP_STRIDE, G, D)
    pe = pe.astype(jnp.float32)
    w = w.astype(jnp.float32)
    pooled = None
    for s in range(r):
        sl = slice(s * CMP_STRIDE, (s + 1) * CMP_STRIDE)
        term = jnp.einsum('bnpgd,p->bngd', chunks[:, s:s + nc] + pe[sl, None, :], w[sl])
        pooled = term if pooled is None else pooled + term
    return jnp.einsum('bngd,de->bnge', pooled, proj.astype(jnp.float32))


def dense_attend(q, k, v, rel, mask, rel_bias):
    G, Hpg, D = q.shape[2:]
    T, K = rel.shape
    s = jnp.einsum('btghd,bkgd->btghk', q, k.astype(jnp.float32)) * D ** -0.5
    bias = rel_bias.astype(jnp.float32)[t5_bucket(rel)].reshape(T, K, G, Hpg).transpose(0, 2, 3, 1)
    m = mask[:, None, None, :]
    p = jax.nn.softmax(jnp.where(m, s + bias, NEG), axis=-1) * m
    o = jnp.einsum('btghk,bkgd->btghd', p, v.astype(jnp.float32))
    return o, p


def sparse_attend(q, q_pos, kg, vg, key_pos, rel_bias):
    G, Hpg, D = q.shape[2:]
    s = jnp.einsum('btghd,btgkd->btghk', q, kg.astype(jnp.float32)) * D ** -0.5
    rel = q_pos[None, :, None, None] - key_pos
    table = rel_bias.astype(jnp.float32).reshape(REL_BUCKETS, G, Hpg)
    bias = jnp.moveaxis(table[t5_bucket(rel), jnp.arange(G)[None, None, :, None]], -1, 3)
    m = (rel >= 0)[:, :, :, None, :]
    p = jax.nn.softmax(jnp.where(m, s + bias, NEG), axis=-1) * m
    return jnp.einsum('btghk,btgkd->btghd', p, vg.astype(jnp.float32))


def block_positions(idx):
    pos = idx[..., None] * SEL_BLOCK + jnp.arange(SEL_BLOCK, dtype=idx.dtype)
    return pos.reshape(*idx.shape[:-1], idx.shape[-1] * SEL_BLOCK)


def cmp_select(q, q_pos, kc_full, vc_full, pe_k, w_k, proj_k, pe_v, w_v, proj_v, k_gain, rel_bias):
    L = kc_full.shape[1]
    kc = rmsnorm(compress(kc_full, pe_k, w_k, proj_k), k_gain)
    vc = compress(vc_full, pe_v, w_v, proj_v)
    nc = kc.shape[1]
    ends = jnp.arange(nc, dtype=jnp.int32) * CMP_STRIDE + (CMP_LEN - 1)
    rel = q_pos[:, None] - ends[None, :]
    o, p = dense_attend(q, kc, vc, rel, rel >= 0, rel_bias)
    imp = p.sum(axis=3)
    nb = -(-L // SEL_BLOCK)
    j = np.arange(nb)
    lo = np.clip((SEL_BLOCK * j - CMP_LEN) // CMP_STRIDE + 1, 0, nc)
    hi = np.clip(-(-(SEL_BLOCK * (j + 1)) // CMP_STRIDE), 0, nc)
    csum = jnp.concatenate([jnp.zeros_like(imp[..., :1]), jnp.cumsum(imp, axis=-1)], axis=-1)
    score = csum[..., hi] - csum[..., lo]
    qblk = (q_pos // SEL_BLOCK)[:, None]
    jj = jnp.arange(nb, dtype=jnp.int32)[None, :]
    forced = (jj == 0) | (jj == qblk) | (jj == qblk - 1)
    score = jnp.where(forced[None, :, None, :], BIG, score)
    score = jnp.where((jj <= qblk)[None, :, None, :], score, -BIG)
    _, idx = lax.top_k(score, min(SEL_TOP, nb))
    return o, idx


def select_prompt(q, q_pos, idx, ks, vs, rel_bias):
    B, T, G, Hpg, D = q.shape
    nb = T // SEL_BLOCK
    src_k = ks.reshape(B, nb, SEL_BLOCK, G, D)
    src_v = vs.reshape(B, nb, SEL_BLOCK, G, D)
    nq = T // SEL_QBLK
    bi = jnp.arange(B)[:, None, None, None]
    gi = jnp.arange(G)[None, None, :, None]

    def chunk(args):
        qc, idxc, posc = args
        kg = src_k[bi, idxc, :, gi].reshape(B, SEL_QBLK, G, -1, D)
        vg = src_v[bi, idxc, :, gi].reshape(B, SEL_QBLK, G, -1, D)
        return sparse_attend(qc, posc, kg, vg, block_positions(idxc), rel_bias)

    split = lambda a: jnp.swapaxes(a.reshape(B, nq, SEL_QBLK, *a.shape[2:]), 0, 1)
    o = lax.map(chunk, (split(q), split(idx), q_pos.reshape(nq, SEL_QBLK)))
    return jnp.swapaxes(o, 0, 1).reshape(B, T, G, Hpg, D)


def select_sample(q, q_pos, idx, cache_k, cache_v, page_table, k_new, v_new, rel_bias):
    B, T, G, Hpg, D = q.shape
    bpp = PAGE_SIZE // SEL_BLOCK
    nb_past = page_table.shape[1] * bpp
    n_new = -(-T // SEL_BLOCK)
    pad = n_new * SEL_BLOCK - T
    pool_k = cache_k.reshape(-1, SEL_BLOCK, G, D)
    pool_v = cache_v.reshape(-1, SEL_BLOCK, G, D)
    new_k = jnp.pad(k_new, ((0, 0), (0, pad), (0, 0), (0, 0))).reshape(B, n_new, SEL_BLOCK, G, D)
    new_v = jnp.pad(v_new, ((0, 0), (0, pad), (0, 0), (0, 0))).reshape(B, n_new, SEL_BLOCK, G, D)
    bi = jnp.arange(B)[:, None, None, None]
    gi = jnp.arange(G)[None, None, :, None]
    jp = jnp.minimum(idx, nb_past - 1)
    phys = page_table[bi, jp // bpp] * bpp + jp % bpp
    jn = jnp.clip(idx - nb_past, 0, n_new - 1)
    is_past = (idx < nb_past)[..., None, None]

    def gather(pool, new):
        g_past = pool[phys, :, gi].astype(jnp.float32)
        g_new = new[bi, jn, :, gi].astype(jnp.float32)
        return jnp.where(is_past, g_past, g_new).reshape(B, T, G, -1, D)

    return sparse_attend(q, q_pos, gather(pool_k, new_k), gather(pool_v, new_v), block_positions(idx), rel_bias)


def window_prompt(q, kw, vw, rel_bias):
    B, T, G, Hpg, D = q.shape
    nq = T // WIN_QBLK
    kwin = WIN_QBLK + WINDOW
    kp = jnp.concatenate([jnp.zeros((B, WINDOW, G, D), kw.dtype), kw], axis=1)
    vp = jnp.concatenate([jnp.zeros((B, WINDOW, G, D), vw.dtype), vw], axis=1)
    rel = WINDOW + jnp.arange(WIN_QBLK, dtype=jnp.int32)[:, None] - jnp.arange(kwin, dtype=jnp.int32)[None, :]
    band = (rel >= 0) & (rel < WINDOW)

    def blk(i):
        start = i * WIN_QBLK
        qs = lax.dynamic_slice_in_dim(q, start, WIN_QBLK, axis=1)
        ks = lax.dynamic_slice_in_dim(kp, start, kwin, axis=1)
        vs = lax.dynamic_slice_in_dim(vp, start, kwin, axis=1)
        mask = band & ((start - WINDOW + jnp.arange(kwin, dtype=jnp.int32)) >= 0)[None, :]
        return dense_attend(qs, ks, vs, rel, mask, rel_bias)[0]

    o = lax.map(blk, jnp.arange(nq, dtype=jnp.int32))
    return jnp.moveaxis(o, 0, 1).reshape(B, T, G, Hpg, D)


def window_sample(q, q_pos, cache_k_win, cache_v_win, k_new, v_new, rel_bias):
    T = q.shape[1]
    wb = cache_k_win.shape[1]
    k = jnp.concatenate([cache_k_win.astype(jnp.float32), k_new.astype(jnp.float32)], axis=1)
    v = jnp.concatenate([cache_v_win.astype(jnp.float32), v_new.astype(jnp.float32)], axis=1)
    key_pos = PAST_LEN - wb + jnp.arange(wb + T, dtype=jnp.int32)
    rel = q_pos[:, None] - key_pos[None, :]
    o, _ = dense_attend(q, k, v, rel, (rel >= 0) & (rel < WINDOW), rel_bias)
    return o, k[:, -wb:].astype(cache_k_win.dtype), v[:, -wb:].astype(cache_v_win.dtype)


def nsa_merge(o_cmp, o_sel, o_win, gates_raw, z):
    B, T = z.shape[:2]
    g = jax.nn.sigmoid(gates_raw.astype(jnp.float32)).reshape(B, T, 3, NSA_KV_HEADS, NSA_GROUP, 1)
    o = g[:, :, 0] * o_cmp + g[:, :, 1] * o_sel + g[:, :, 2] * o_win
    return o.reshape(B, T, NSA_Q_W) * jax.nn.silu(z.astype(jnp.float32))


def output_merge(x, o_a, o_b, m_a, m_b, w_branch_a, w_branch_b, w_out):
    dt = x.dtype
    p_a = jnp.einsum('bte,ed->btd', o_a.astype(dt), w_branch_a)
    p_b = jnp.einsum('bte,ed->btd', o_b.astype(dt), w_branch_b)
    m = jax.nn.sigmoid(m_a) * p_a + jax.nn.sigmoid(m_b) * p_b
    return x + jnp.einsum('btd,de->bte', m, w_out)


def setup_inputs(seed: int = 0) -> dict:
    key = jax.random.key(seed)
    ks = jax.random.split(key, 32)
    f32 = jnp.float32
    n_pages = PAST_LEN // PAGE_SIZE
    n_pool = (5 * DEC_BATCH * n_pages + 3) // 4
    wb = min(WINDOW, PAST_LEN)

    def nrm(k, shape, scale=1.0):
        return scale * jax.random.normal(k, shape, f32)

    kv_page = (n_pool, PAGE_SIZE, NSA_KV_HEADS, NSA_HEAD_DIM)
    kv_win = (DEC_BATCH, wb, NSA_KV_HEADS, NSA_HEAD_DIM)
    dt = jnp.exp(jax.random.uniform(ks[14], (GDN_V_HEADS,), f32, math.log(1e-3), math.log(1e-1)))
    page_table = jax.random.permutation(ks[10], n_pool)[:DEC_BATCH * n_pages].reshape(DEC_BATCH, n_pages).astype(jnp.int32)
    return {
        'x_prompt': nrm(ks[0], (BATCH, SEQ, D_MODEL)),
        'x_sample': nrm(ks[1], (DEC_BATCH, DEC_SEQ, D_MODEL)),
        'cache_k_cmp': nrm(ks[2], kv_page),
        'cache_v_cmp': nrm(ks[3], kv_page),
        'cache_k_sel': nrm(ks[4], kv_page),
        'cache_v_sel': nrm(ks[5], kv_page),
        'cache_k_win': nrm(ks[6], kv_win),
        'cache_v_win': nrm(ks[7], kv_win),
        'state_conv': nrm(ks[8], (DEC_BATCH, GDN_CONV - 1, GDN_QKV_W)),
        'state_gdn': nrm(ks[9], (DEC_BATCH, GDN_V_HEADS, GDN_HEAD_DIM, GDN_HEAD_DIM), 0.05),
        'page_table': page_table,
        'norm_g': 1.0 + nrm(ks[11], (D_MODEL,), 0.02),
        'w_in': nrm(ks[12], (D_MODEL, IN_W), D_MODEL ** -0.5),
        'gdn_conv_w': nrm(ks[13], (GDN_CONV, GDN_QKV_W), GDN_CONV ** -0.5),
        'gdn_a_log': jnp.log(jax.random.uniform(ks[15], (GDN_V_HEADS,), f32, 1.0, 16.0)),
        'gdn_dt_bias': dt + jnp.log(-jnp.expm1(-dt)),
        'gdn_norm_g': 1.0 + nrm(ks[16], (GDN_HEAD_DIM,), 0.02),
        'q_norm_g': 1.0 + nrm(ks[17], (NSA_HEAD_DIM,), 0.02),
        'k_norm_g': 1.0 + nrm(ks[18], (3, NSA_HEAD_DIM), 0.02),
        'cmp_pe_k': nrm(ks[19], (CMP_LEN, NSA_HEAD_DIM), 0.1),
        'cmp_w_k': (1.0 + nrm(ks[20], (CMP_LEN,), 0.1)) / CMP_LEN,
        'cmp_proj_k': nrm(ks[21], (NSA_HEAD_DIM, NSA_HEAD_DIM), NSA_HEAD_DIM ** -0.5),
        'cmp_pe_v': nrm(ks[22], (CMP_LEN, NSA_HEAD_DIM), 0.1),
        'cmp_w_v': (1.0 + nrm(ks[23], (CMP_LEN,), 0.1)) / CMP_LEN,
        'cmp_proj_v': nrm(ks[24], (NSA_HEAD_DIM, NSA_HEAD_DIM), NSA_HEAD_DIM ** -0.5),
        'rel_bias': nrm(ks[25], (REL_BUCKETS, NSA_HEADS), 0.2),
        'w_branch_a': nrm(ks[26], (GDN_V_W, D_MODEL), GDN_V_W ** -0.5),
        'w_branch_b': nrm(ks[27], (NSA_Q_W, D_MODEL), NSA_Q_W ** -0.5),
        'w_out': nrm(ks[28], (D_MODEL, D_MODEL), D_MODEL ** -0.5),
    }


def reference(x_prompt, x_sample, cache_k_cmp, cache_v_cmp, cache_k_sel, cache_v_sel, cache_k_win, cache_v_win,
              state_conv, state_gdn, page_table, norm_g, w_in, gdn_conv_w, gdn_a_log, gdn_dt_bias, gdn_norm_g,
              q_norm_g, k_norm_g, cmp_pe_k, cmp_w_k, cmp_proj_k, cmp_pe_v, cmp_w_v, cmp_proj_v, rel_bias,
              w_branch_a, w_branch_b, w_out):
    f32 = jnp.float32
    G, Dh = NSA_KV_HEADS, NSA_HEAD_DIM
    wb = cache_k_win.shape[1]
    gdn_p = (gdn_conv_w, gdn_a_log, gdn_dt_bias, gdn_norm_g)
    cmp_p = (cmp_pe_k, cmp_w_k, cmp_proj_k, cmp_pe_v, cmp_w_v, cmp_proj_v)

    bp, tp = x_prompt.shape[:2]
    (qkv_a, z_a, b_a, a_a, q_b, kc_b, vc_b, ks_b, vs_b, kw_b, vw_b, g_b, z_b, m_a, m_b) = input_projection(x_prompt, norm_g, w_in)
    conv0 = jnp.zeros((bp, GDN_CONV - 1, GDN_QKV_W), x_prompt.dtype)
    s0 = jnp.zeros((bp, GDN_V_HEADS, GDN_HEAD_DIM, GDN_HEAD_DIM), f32)
    o_a, p_conv, p_gdn = gdn_branch(qkv_a, z_a, b_a, a_a, conv0, s0, *gdn_p)
    q, p_kc, p_vc, p_ks, p_vs, kw, vw = nsa_inputs(q_b, kc_b, vc_b, ks_b, vs_b, kw_b, vw_b, q_norm_g, k_norm_g)
    pos = jnp.arange(tp, dtype=jnp.int32)
    o_cmp, idx = cmp_select(q, pos, p_kc, p_vc, *cmp_p, k_norm_g[0], rel_bias)
    o_sel = select_prompt(q, pos, idx, p_ks, p_vs, rel_bias)
    o_win = window_prompt(q, kw, vw, rel_bias)
    o_b = nsa_merge(o_cmp, o_sel, o_win, g_b, z_b)
    y_prompt = output_merge(x_prompt, o_a, o_b, m_a, m_b, w_branch_a, w_branch_b, w_out)
    p_kw = jnp.concatenate([jnp.zeros((bp, wb, G, Dh), kw.dtype), kw], axis=1)[:, -wb:]
    p_vw = jnp.concatenate([jnp.zeros((bp, wb, G, Dh), vw.dtype), vw], axis=1)[:, -wb:]

    bs, ts = x_sample.shape[:2]
    (qkv_a, z_a, b_a, a_a, q_b, kc_b, vc_b, ks_b, vs_b, kw_b, vw_b, g_b, z_b, m_a, m_b) = input_projection(x_sample, norm_g, w_in)
    o_a_s, s_conv, s_gdn = gdn_branch(qkv_a, z_a, b_a, a_a, state_conv, state_gdn, *gdn_p)
    q_s, s_kc, s_vc, s_ks, s_vs, kw_s, vw_s = nsa_inputs(q_b, kc_b, vc_b, ks_b, vs_b, kw_b, vw_b, q_norm_g, k_norm_g)
    pos_s = PAST_LEN + jnp.arange(ts, dtype=jnp.int32)
    n_pages = PAST_LEN // PAGE_SIZE
    past = lambda c: c[page_table].reshape(bs, n_pages * PAGE_SIZE, G, Dh).astype(f32)
    kc_full = jnp.concatenate([past(cache_k_cmp), s_kc.astype(f32)], axis=1)
    vc_full = jnp.concatenate([past(cache_v_cmp), s_vc.astype(f32)], axis=1)
    o_cmp_s, idx_s = cmp_select(q_s, pos_s, kc_full, vc_full, *cmp_p, k_norm_g[0], rel_bias)
    o_sel_s = select_sample(q_s, pos_s, idx_s, cache_k_sel, cache_v_sel, page_table, s_ks, s_vs, rel_bias)
    o_win_s, s_kw, s_vw = window_sample(q_s, pos_s, cache_k_win, cache_v_win, kw_s, vw_s, rel_bias)
    o_b_s = nsa_merge(o_cmp_s, o_sel_s, o_win_s, g_b, z_b)
    y_sample = output_merge(x_sample, o_a_s, o_b_s, m_a, m_b, w_branch_a, w_branch_b, w_out)

    return (y_prompt, y_sample, p_kc, p_vc, p_ks, p_vs, p_kw, p_vw, p_conv, p_gdn,
            s_kc, s_vc, s_ks, s_vs, s_kw, s_vw, s_conv, s_gdn)
```

```python
import functools
import math

import jax
import jax.numpy as jnp
import numpy as np
from jax import lax
from jax.experimental import pallas as pl
from jax.experimental.pallas import tpu as pltpu

F32 = jnp.float32
BF16 = jnp.bfloat16
HI = lax.Precision.HIGHEST

D_MODEL = 2048
PAST_LEN = 16384
PAGE = 128

GDN_QK_HEADS = 16
GDN_V_HEADS = 32
HEAD_DIM = 128
GDN_CONV = 4
GDN_CHUNK = 64
GDN_QK_W = GDN_QK_HEADS * HEAD_DIM
GDN_V_W = GDN_V_HEADS * HEAD_DIM
GDN_QKV_W = 2 * GDN_QK_W + GDN_V_W

NSA_HEADS = 16
NSA_G = 4
NSA_HPG = NSA_HEADS // NSA_G
NSA_Q_W = NSA_HEADS * HEAD_DIM
NSA_KV_W = NSA_G * HEAD_DIM
CMP_LEN = 32
CMP_STRIDE = 16
SEL_BLOCK = 64
SEL_TOP = 16
WINDOW = 512
REL_BUCKETS = 32
REL_MAX_DIST = 128

EPS = 1e-6
NEG = -1e30
BIG = 1e9

LANE = 128
VMEM_LIMIT = 56 * 1024 * 1024

UA_W = GDN_QKV_W + GDN_V_W
UB_Q, UB_Z, UB_MA, UB_MB, UB_KV, UB_BA, UB_G = 0, 2048, 4096, 6144, 8192, 11264, 11392
UB_W = 11520


def _mm(a, b):
    return jnp.dot(a.astype(BF16), b.astype(BF16), preferred_element_type=F32)


def _mm_nt(a, b):
    return lax.dot_general(a.astype(BF16), b.astype(BF16), (((1,), (1,)), ((), ())), preferred_element_type=F32)


def _mm_tn(a, b):
    return lax.dot_general(a.astype(BF16), b.astype(BF16), (((0,), (0,)), ((), ())), preferred_element_type=F32)


def _mm_hi(a, b):
    return jnp.dot(a, b, precision=HI, preferred_element_type=F32)


def _mm_nt_hi(a, b):
    return lax.dot_general(a, b, (((1,), (1,)), ((), ())), precision=HI, preferred_element_type=F32)


def _silu(x):
    return x * jax.nn.sigmoid(x)


def _softplus(x):
    return jnp.maximum(x, 0.0) + jnp.log1p(jnp.exp(-jnp.abs(x)))


def _rms(x, gain):
    return x * lax.rsqrt(jnp.mean(x * x, axis=-1, keepdims=True) + EPS) * gain


def _cparams(sem):
    return pltpu.CompilerParams(dimension_semantics=sem, vmem_limit_bytes=VMEM_LIMIT)


def _proj_body(x_ref, g_ref, w_ref, o_ref, h_ref):
    @pl.when(pl.program_id(1) == 0)
    def _():
        h_ref[...] = _rms(x_ref[...], g_ref[...]).astype(BF16)

    o_ref[...] = jnp.dot(h_ref[...], w_ref[...].astype(BF16), preferred_element_type=F32)


def _project(x2d, norm_g, w, n_cols, tn):
    m = x2d.shape[0]
    tm = min(m, 1024)
    return pl.pallas_call(
        _proj_body,
        grid=(m // tm, n_cols // tn),
        in_specs=[
            pl.BlockSpec((tm, D_MODEL), lambda i, j: (i, 0)),
            pl.BlockSpec((1, D_MODEL), lambda i, j: (0, 0)),
            pl.BlockSpec((D_MODEL, tn), lambda i, j: (0, j)),
        ],
        out_specs=pl.BlockSpec((tm, tn), lambda i, j: (i, j)),
        out_shape=jax.ShapeDtypeStruct((m, n_cols), F32),
        scratch_shapes=[pltpu.VMEM((tm, D_MODEL), BF16)],
        compiler_params=_cparams(("arbitrary", "arbitrary")),
        name="proj",
    )(x2d, norm_g.reshape(1, D_MODEL), w)


def _unit_lower_inverse(nmat, c):
    row = lax.broadcasted_iota(jnp.int32, (c, c), 0)
    col = lax.broadcasted_iota(jnp.int32, (c, c), 1)
    p = jnp.where(row == col, 1.0, 0.0) + nmat
    m = nmat
    span = 2
    while span < c:
        m = _mm_hi(m, m)
        p = p + _mm_hi(p, m)
        span *= 2
    return p


def _gdn_body(q_ref, k_ref, v_ref, z_ref, ba_ref, cwq_ref, cwk_ref, cwv_ref, csq_ref, csk_ref, csv_ref,
              s0_ref, gp_ref, ng_ref, o_ref, sfin_ref, xq_s, xk_s, xv_s, st_s, *, t_len, c):
    head = pl.program_id(1)
    for xs, cs, src in ((xq_s, csq_ref, q_ref), (xk_s, csk_ref, k_ref), (xv_s, csv_ref, v_ref)):
        xs[0:8, :] = jnp.zeros((8, xs.shape[1]), F32)
        xs[5:8, :] = cs[0]
        xs[8:8 + t_len, :] = src[0]
    st_s[...] = s0_ref[0]

    row = lax.broadcasted_iota(jnp.int32, (c, c), 0)
    col = lax.broadcasted_iota(jnp.int32, (c, c), 1)
    tril = row >= col
    strict = row > col
    eye = row == col
    lane = lax.broadcasted_iota(jnp.int32, (c, LANE), 1)
    tril_f = jnp.where(tril, 1.0, 0.0)

    def conv(xs, cw_ref, base):
        win = xs[pl.ds(base, c + 8), :]
        w = cw_ref[...]
        a = win[5:5 + c] * w[0:1] + win[6:6 + c] * w[1:2] + win[7:7 + c] * w[2:3] + win[8:8 + c] * w[3:4]
        return _silu(a)

    def pick(x, idx):
        return jnp.sum(jnp.where(lane == idx, x, 0.0), axis=-1, keepdims=True)

    def chunk(ci, carry):
        base = pl.multiple_of(ci * c, c)
        qa = conv(xq_s, cwq_ref, base)
        ka = conv(xk_s, cwk_ref, base)
        va = conv(xv_s, cwv_ref, base)
        qn = qa * lax.rsqrt(jnp.sum(qa * qa, axis=-1, keepdims=True) + EPS) * (HEAD_DIM ** -0.5)
        kn = ka * lax.rsqrt(jnp.sum(ka * ka, axis=-1, keepdims=True) + EPS)
        ba = ba_ref[0, pl.ds(base, c), :]
        beta_all = jax.nn.sigmoid(ba)
        g_all = -jnp.exp(gp_ref[0:1, :]) * _softplus(ba + gp_ref[1:2, :])
        gc_all = _mm_hi(tril_f, g_all)
        kk = _mm_nt(kn, kn)
        qk = _mm_nt(qn, kn)
        for hh in range(2):
            hv = 2 * head + hh
            beta = pick(beta_all, hv)
            gcol = pick(gc_all, GDN_V_HEADS + hv)
            grow = jnp.sum(jnp.where(eye, gcol, 0.0), axis=0, keepdims=True)
            gamma = jnp.where(tril, jnp.exp(jnp.minimum(gcol - grow, 0.0)), 0.0)
            nmat = jnp.where(strict, -(kk * beta * gamma), 0.0)
            tinv = _unit_lower_inverse(nmat, c)
            egc = jnp.exp(gcol)
            v_h = va[:, hh * HEAD_DIM:(hh + 1) * HEAD_DIM]
            rhs = jnp.concatenate([v_h * beta, kn * (beta * egc)], axis=1)
            uw = _mm_hi(tinv, rhs)
            u = uw[:, :HEAD_DIM]
            w = uw[:, HEAD_DIM:]
            s = st_s[hh]
            v_new = u - _mm(w, s)
            o = _mm(qn * egc, s) + _mm(qk * gamma, v_new)
            glast = gcol[c - 1:c, :]
            kdec = kn * jnp.exp(glast - gcol)
            st_s[hh] = s * jnp.exp(glast) + _mm_tn(kdec, v_new)
            z_h = z_ref[0, pl.ds(base, c), hh * HEAD_DIM:(hh + 1) * HEAD_DIM]
            o_ref[0, pl.ds(base, c), hh * HEAD_DIM:(hh + 1) * HEAD_DIM] = (_rms(o, ng_ref[...]) * _silu(z_h)).astype(o_ref.dtype)
        return carry

    lax.fori_loop(0, t_len // c, chunk, 0)
    sfin_ref[0] = st_s[...]


def _gdn(ua, ub, conv_state, s0, conv_w, a_log, dt_bias, norm_g):
    b, t_len, _ = ua.shape
    c = min(GDN_CHUNK, t_len)
    hd = HEAD_DIM
    nqk = GDN_QK_HEADS
    gp = jnp.zeros((2, LANE), F32)
    gp = gp.at[0, GDN_V_HEADS:2 * GDN_V_HEADS].set(a_log).at[1, GDN_V_HEADS:2 * GDN_V_HEADS].set(dt_bias)
    body = functools.partial(_gdn_body, t_len=t_len, c=c)
    return pl.pallas_call(
        body,
        grid=(b, nqk),
        in_specs=[
            pl.BlockSpec((1, t_len, hd), lambda bi, i: (bi, 0, i)),
            pl.BlockSpec((1, t_len, hd), lambda bi, i: (bi, 0, nqk + i)),
            pl.BlockSpec((1, t_len, 2 * hd), lambda bi, i: (bi, 0, nqk + i)),
            pl.BlockSpec((1, t_len, 2 * hd), lambda bi, i: (bi, 0, 2 * nqk + i)),
            pl.BlockSpec((1, t_len, LANE), lambda bi, i: (bi, 0, UB_BA // LANE)),
            pl.BlockSpec((GDN_CONV, hd), lambda bi, i: (0, i)),
            pl.BlockSpec((GDN_CONV, hd), lambda bi, i: (0, nqk + i)),
            pl.BlockSpec((GDN_CONV, 2 * hd), lambda bi, i: (0, nqk + i)),
            pl.BlockSpec((1, GDN_CONV - 1, hd), lambda bi, i: (bi, 0, i)),
            pl.BlockSpec((1, GDN_CONV - 1, hd), lambda bi, i: (bi, 0, nqk + i)),
            pl.BlockSpec((1, GDN_CONV - 1, 2 * hd), lambda bi, i: (bi, 0, nqk + i)),
            pl.BlockSpec((1, 2, hd, hd), lambda bi, i: (bi, i, 0, 0)),
            pl.BlockSpec((2, LANE), lambda bi, i: (0, 0)),
            pl.BlockSpec((1, hd), lambda bi, i: (0, 0)),
        ],
        out_specs=[
            pl.BlockSpec((1, t_len, 2 * hd), lambda bi, i: (bi, 0, i)),
            pl.BlockSpec((1, 2, hd, hd), lambda bi, i: (bi, i, 0, 0)),
        ],
        out_shape=[
            jax.ShapeDtypeStruct((b, t_len, GDN_V_W), BF16),
            jax.ShapeDtypeStruct((b, GDN_V_HEADS, hd, hd), F32),
        ],
        scratch_shapes=[
            pltpu.VMEM((t_len + 8, hd), F32),
            pltpu.VMEM((t_len + 8, hd), F32),
            pltpu.VMEM((t_len + 8, 2 * hd), F32),
            pltpu.VMEM((2, hd, hd), F32),
        ],
        compiler_params=_cparams(("arbitrary", "arbitrary")),
        name="gdn",
    )(ua, ua, ua, ua, ub, conv_w, conv_w, conv_w, conv_state, conv_state, conv_state, s0, gp, norm_g.reshape(1, hd))


def _t5_bucket(rel):
    n = jnp.maximum(rel, 0)
    exact = REL_BUCKETS // 2
    nf = jnp.maximum(n, 1).astype(F32)
    large = exact + (jnp.log(nf / exact) / math.log(REL_MAX_DIST / exact) * (REL_BUCKETS - exact)).astype(jnp.int32)
    return jnp.where(n < exact, n, jnp.minimum(large, REL_BUCKETS - 1))


def _bias_of(rel_bias, rel):
    return jnp.moveaxis(rel_bias.astype(F32)[_t5_bucket(rel)], -1, 0)


def _cmp_bias(rel_bias, q_pos, ncp, nc):
    n = jnp.arange(ncp, dtype=jnp.int32)
    rel = q_pos[:, None] - (n * CMP_STRIDE + (CMP_LEN - 1))[None, :]
    ok = (rel >= 0) & (n < nc)[None, :]
    return jnp.where(ok[None], _bias_of(rel_bias, rel), NEG)


def _compress_body(pt_ref, *refs, pps, n_steps, nch):
    del pt_ref
    k_refs = refs[:pps]
    v_refs = refs[pps:2 * pps]
    (wabk_ref, wabv_ref, wk_ref, pek_ref, wv_ref, pev_ref, projk_ref, projv_ref, gain_ref,
     outk_ref, outv_ref, ak_s, bk_s, av_s, bv_s) = refs[2 * pps:]
    s = pl.program_id(1)

    @pl.when(s == 0)
    def _():
        bk_s[nch:nch + 8, :] = jnp.zeros((8, NSA_KV_W), F32)
        bv_s[nch:nch + 8, :] = jnp.zeros((8, NSA_KV_W), F32)

    for j in range(pps):
        row0 = pl.multiple_of((s * pps + j) * 8, 8)
        abk = _mm_hi(wabk_ref[...], k_refs[j][0])
        ak_s[pl.ds(row0, 8), :] = abk[0:8]
        bk_s[pl.ds(row0, 8), :] = abk[8:16]
        abv = _mm_hi(wabv_ref[...], v_refs[j][0])
        av_s[pl.ds(row0, 8), :] = abv[0:8]
        bv_s[pl.ds(row0, 8), :] = abv[8:16]

    @pl.when(s == n_steps - 1)
    def _():
        cpe_k = jnp.sum(wk_ref[...] * pek_ref[...], axis=0, keepdims=True)
        cpe_v = jnp.sum(wv_ref[...] * pev_ref[...], axis=0, keepdims=True)
        rb = min(nch, 128)

        def fin(r, carry):
            r0 = pl.multiple_of(r * rb, rb)
            pk = ak_s[pl.ds(r0, rb), :] + bk_s[pl.ds(r0, rb + 8), :][1:rb + 1]
            pv = av_s[pl.ds(r0, rb), :] + bv_s[pl.ds(r0, rb + 8), :][1:rb + 1]
            for g in range(NSA_G):
                sl = slice(g * HEAD_DIM, (g + 1) * HEAD_DIM)
                yk = _mm_hi(pk[:, sl] + cpe_k, projk_ref[...])
                outk_ref[0, pl.ds(r0, rb), sl] = _rms(yk, gain_ref[...])
                outv_ref[0, pl.ds(r0, rb), sl] = _mm_hi(pv[:, sl] + cpe_v, projv_ref[...])
            return carry

        lax.fori_loop(0, nch // rb, fin, 0)


def _pool_weights(w):
    c = np.arange(8)[:, None]
    t = np.arange(PAGE)[None, :]
    off = t - CMP_STRIDE * c
    inside = (off >= 0) & (off < CMP_STRIDE)
    idx = np.clip(off, 0, CMP_STRIDE - 1)
    wa = jnp.where(inside, w[idx], 0.0)
    wb = jnp.where(inside, w[idx + CMP_STRIDE], 0.0)
    return jnp.concatenate([wa, wb], axis=0).astype(F32)


def _compress(pages_k, pages_v, col_k, col_v, table, pe_k, w_k, proj_k, pe_v, w_v, proj_v, k_gain):
    b, n_pages = table.shape
    pps = 4
    n_steps = n_pages // pps
    nch = 8 * n_pages
    body = functools.partial(_compress_body, pps=pps, n_steps=n_steps, nch=nch)

    def page_spec(j, col):
        return pl.BlockSpec((1, PAGE, NSA_KV_W), lambda bi, s, pt: (pt[bi, s * pps + j], 0, col))

    full = lambda shape: pl.BlockSpec(shape, lambda bi, s, pt: (0,) * len(shape))
    in_specs = [page_spec(j, col_k) for j in range(pps)] + [page_spec(j, col_v) for j in range(pps)]
    in_specs += [full((16, PAGE)), full((16, PAGE)), full((CMP_LEN, 1)), full((CMP_LEN, HEAD_DIM)),
                 full((CMP_LEN, 1)), full((CMP_LEN, HEAD_DIM)), full((HEAD_DIM, HEAD_DIM)), full((HEAD_DIM, HEAD_DIM)),
                 full((1, HEAD_DIM))]
    out_spec = pl.BlockSpec((1, nch, NSA_KV_W), lambda bi, s, pt: (bi, 0, 0))
    return pl.pallas_call(
        body,
        grid_spec=pltpu.PrefetchScalarGridSpec(
            num_scalar_prefetch=1, grid=(b, n_steps), in_specs=in_specs, out_specs=[out_spec, out_spec],
            scratch_shapes=[pltpu.VMEM((nch, NSA_KV_W), F32), pltpu.VMEM((nch + 8, NSA_KV_W), F32),
                            pltpu.VMEM((nch, NSA_KV_W), F32), pltpu.VMEM((nch + 8, NSA_KV_W), F32)]),
        out_shape=[jax.ShapeDtypeStruct((b, nch, NSA_KV_W), F32)] * 2,
        compiler_params=_cparams(("arbitrary", "arbitrary")),
        name="compress",
    )(table, *([pages_k] * pps), *([pages_v] * pps), _pool_weights(w_k), _pool_weights(w_v),
      w_k.reshape(CMP_LEN, 1), pe_k, w_v.reshape(CMP_LEN, 1), pe_v, proj_k, proj_v, k_gain.reshape(1, HEAD_DIM))


def _prep_body(ks_ref, vs_ref, kw_ref, vw_ref, kg_ref, pks_ref, pkw_ref, ksb_ref, vsb_ref, kwb_ref, vwb_ref):
    for g in range(NSA_G):
        sl = slice(g * HEAD_DIM, (g + 1) * HEAD_DIM)
        ksn = _rms(ks_ref[0, :, sl], kg_ref[1:2, :])
        kwn = _rms(kw_ref[0, :, sl], kg_ref[2:3, :])
        pks_ref[0, :, sl] = ksn
        pkw_ref[0, :, sl] = kwn
        ksb_ref[0, :, sl] = ksn.astype(BF16)
        kwb_ref[0, :, sl] = kwn.astype(BF16)
    vsb_ref[0] = vs_ref[0].astype(BF16)
    vwb_ref[0] = vw_ref[0].astype(BF16)


def _nsa_prep(ub, k_norm_g):
    b, t_len, _ = ub.shape
    tp = min(t_len, 512)
    kv0 = UB_KV // NSA_KV_W
    spec = lambda c: pl.BlockSpec((1, tp, NSA_KV_W), lambda bi, i: (bi, i, c))
    o_spec = pl.BlockSpec((1, tp, NSA_KV_W), lambda bi, i: (bi, i, 0))
    return pl.pallas_call(
        _prep_body,
        grid=(b, t_len // tp),
        in_specs=[spec(kv0 + 2), spec(kv0 + 3), spec(kv0 + 4), spec(kv0 + 5),
                  pl.BlockSpec((3, HEAD_DIM), lambda bi, i: (0, 0))],
        out_specs=[o_spec] * 6,
        out_shape=[jax.ShapeDtypeStruct((b, t_len, NSA_KV_W), F32)] * 2
        + [jax.ShapeDtypeStruct((b, t_len, NSA_KV_W), BF16)] * 4,
        compiler_params=_cparams(("arbitrary", "arbitrary")),
        name="nsa_prep",
    )(ub, ub, ub, ub, k_norm_g)


def _sel_matrix(nc, nb, ncp, nbp):
    j = np.arange(nb)
    lo = np.clip((SEL_BLOCK * j - CMP_LEN) // CMP_STRIDE + 1, 0, nc)
    hi = np.clip(-(-(SEL_BLOCK * (j + 1)) // CMP_STRIDE), 0, nc)
    n = np.arange(ncp)[:, None]
    m = np.zeros((ncp, nbp), np.float32)
    m[:, :nb] = (n >= lo[None, :]) & (n < hi[None, :])
    return jnp.asarray(m)


def _cmp_body(q_ref, kc_ref, vc_ref, bias_ref, msel_ref, qg_ref, ocmp_ref, nsel_ref, *, tq, nb):
    qi = pl.program_id(1)
    lane = lax.broadcasted_iota(jnp.int32, (tq, LANE), 1)
    tpos = qi * tq + lax.broadcasted_iota(jnp.int32, (tq, LANE), 0)
    qblk = tpos >> 6
    scale = HEAD_DIM ** -0.5
    for g in range(NSA_G):
        gsl = slice(g * HEAD_DIM, (g + 1) * HEAD_DIM)
        kc = kc_ref[0, :, gsl]
        vc = vc_ref[0, :, gsl].astype(BF16)
        imp = jnp.zeros((tq, LANE), F32)
        for hh in range(NSA_HPG):
            h = g * NSA_HPG + hh
            hsl = slice(h * HEAD_DIM, (h + 1) * HEAD_DIM)
            qh = _rms(q_ref[0, :, hsl], qg_ref[...]) * scale
            bias = bias_ref[h]
            s = _mm_nt_hi(qh, kc) + bias
            e = jnp.exp(s - jnp.max(s, axis=-1, keepdims=True))
            p = e / jnp.sum(e, axis=-1, keepdims=True) * jnp.where(bias > 0.5 * NEG, 1.0, 0.0)
            ocmp_ref[0, :, hsl] = _mm(p, vc)
            imp = imp + p
        score = _mm_hi(imp, msel_ref[...])
        forced = (lane == 0) | (lane == qblk) | (lane == qblk - 1)
        score = jnp.where(forced, BIG, score)
        score = jnp.where(lane <= qblk, score, -BIG)
        rank = jnp.zeros((tq, LANE), F32)
        for i in range(nb):
            si = score[:, i:i + 1]
            beats = (si > score) | ((si == score) & (lane > i))
            rank = rank + jnp.where(beats, 1.0, 0.0)
        nsel = jnp.where((rank < SEL_TOP) | (lane >= nb), 0.0, 1.0)
        nsel_ref[0, :, gsl] = nsel.astype(BF16)


def _cmp_select_prompt(ub, kcmp, vcmp, bias_cmp, q_norm_g):
    b, t_len, _ = ub.shape
    tq = 128
    nb = t_len // SEL_BLOCK
    ncp = kcmp.shape[1]
    nc = (t_len - CMP_LEN) // CMP_STRIDE + 1
    assert nb <= LANE and ncp == LANE
    body = functools.partial(_cmp_body, tq=tq, nb=nb)
    return pl.pallas_call(
        body,
        grid=(b, t_len // tq),
        in_specs=[
            pl.BlockSpec((1, tq, NSA_Q_W), lambda bi, i: (bi, i, UB_Q // NSA_Q_W)),
            pl.BlockSpec((1, ncp, NSA_KV_W), lambda bi, i: (bi, 0, 0)),
            pl.BlockSpec((1, ncp, NSA_KV_W), lambda bi, i: (bi, 0, 0)),
            pl.BlockSpec((NSA_HEADS, tq, ncp), lambda bi, i: (0, i, 0)),
            pl.BlockSpec((ncp, LANE), lambda bi, i: (0, 0)),
            pl.BlockSpec((1, HEAD_DIM), lambda bi, i: (0, 0)),
        ],
        out_specs=[
            pl.BlockSpec((1, tq, NSA_Q_W), lambda bi, i: (bi, i, 0)),
            pl.BlockSpec((1, tq, NSA_G * LANE), lambda bi, i: (bi, i, 0)),
        ],
        out_shape=[jax.ShapeDtypeStruct((b, t_len, NSA_Q_W), F32),
                   jax.ShapeDtypeStruct((b, t_len, NSA_G * LANE), BF16)],
        compiler_params=_cparams(("arbitrary", "arbitrary")),
        name="cmp_select",
    )(ub, kcmp, vcmp, bias_cmp, _sel_matrix(nc, nb, ncp, LANE), q_norm_g.reshape(1, HEAD_DIM))


def _attn_body(q_ref, ocmp_ref, nsel_ref, ks_ref, vs_ref, kw_ref, vw_ref, epen_ref, bt_ref, gate_ref, z_ref, qg_ref,
               o_ref, m_s, l_s, acc_s, *, tq):
    qi = pl.program_id(1)
    t0 = qi * tq
    rows = NSA_HPG * tq
    tk = 128
    r_tok = t0 + (lax.broadcasted_iota(jnp.int32, (rows, tk), 0) & (tq - 1))
    c_idx = lax.broadcasted_iota(jnp.int32, (rows, tk), 1)
    gates = jax.nn.sigmoid(gate_ref[0])
    scale = HEAD_DIM ** -0.5
    ndi = bt_ref.shape[0] - 1

    for g in range(NSA_G):
        gsl = slice(g * HEAD_DIM, (g + 1) * HEAD_DIM)
        q4 = jnp.concatenate(
            [(_rms(q_ref[0, :, (g * NSA_HPG + hh) * HEAD_DIM:(g * NSA_HPG + hh + 1) * HEAD_DIM], qg_ref[...]) * scale).astype(BF16)
             for hh in range(NSA_HPG)], axis=0)
        ns = nsel_ref[0, :, g * LANE:(g + 1) * LANE]
        qa = jnp.concatenate([q4, jnp.concatenate([ns] * NSA_HPG, axis=0)], axis=1)

        def run(k_ref, v_ref, qmat, lo, hi, window):
            m_s[...] = jnp.full((rows, 1), NEG, F32)
            l_s[...] = jnp.zeros((rows, 1), F32)
            acc_s[...] = jnp.zeros((rows, HEAD_DIM), F32)

            def step(j, carry):
                kb = pl.multiple_of(j * tk, tk)
                kt = k_ref[0, pl.ds(kb, tk), gsl]
                if not window:
                    kt = jnp.concatenate([kt, epen_ref[pl.ds(kb, tk), :]], axis=1)
                s = lax.dot_general(qmat, kt, (((1,), (1,)), ((), ())), preferred_element_type=F32)
                di = jnp.minimum(qi - j, ndi)
                s = s + bt_ref[di, g * rows:(g + 1) * rows, :]
                rel = r_tok - (kb + c_idx)
                ok = (rel >= 0) & (rel < WINDOW) if window else rel >= 0
                s = jnp.where(ok, s, NEG)
                m_old = m_s[...]
                m_new = jnp.maximum(m_old, jnp.max(s, axis=-1, keepdims=True))
                alpha = jnp.exp(m_old - m_new)
                p = jnp.exp(s - m_new)
                l_s[...] = alpha * l_s[...] + jnp.sum(p, axis=-1, keepdims=True)
                acc_s[...] = alpha * acc_s[...] + jnp.dot(p.astype(BF16), v_ref[0, pl.ds(kb, tk), gsl],
                                                          preferred_element_type=F32)
                m_s[...] = m_new
                return carry

            lax.fori_loop(lo, hi, step, 0)
            return acc_s[...] / l_s[...]

        o_sel = run(ks_ref, vs_ref, qa, 0, qi + 1, False)
        o_win = run(kw_ref, vw_ref, q4, jnp.maximum(qi - WINDOW // tk, 0), qi + 1, True)
        for hh in range(NSA_HPG):
            h = g * NSA_HPG + hh
            hsl = slice(h * HEAD_DIM, (h + 1) * HEAD_DIM)
            o = (gates[:, h:h + 1] * ocmp_ref[0, :, hsl]
                 + gates[:, NSA_HEADS + h:NSA_HEADS + h + 1] * o_sel[hh * tq:(hh + 1) * tq]
                 + gates[:, 2 * NSA_HEADS + h:2 * NSA_HEADS + h + 1] * o_win[hh * tq:(hh + 1) * tq])
            o_ref[0, :, hsl] = (o * _silu(z_ref[0, :, hsl])).astype(o_ref.dtype)


def _toeplitz_bias(rel_bias, tq, n_tiles):
    d = jnp.arange(n_tiles, dtype=jnp.int32)[:, None, None]
    r = jnp.arange(tq, dtype=jnp.int32)[None, :, None]
    c = jnp.arange(128, dtype=jnp.int32)[None, None, :]
    bt = _bias_of(rel_bias, d * 128 + r - c)
    return jnp.swapaxes(bt, 0, 1).reshape(n_tiles, NSA_HEADS * tq, 128)


def _block_penalty(t_len):
    key = np.arange(t_len)[:, None]
    j = np.arange(LANE)[None, :]
    return jnp.asarray(np.where(key // SEL_BLOCK == j, NEG, 0.0), BF16)


def _attn_prompt(ub, ocmp, nsel, ksb, vsb, kwb, vwb, rel_bias, q_norm_g):
    b, t_len, _ = ub.shape
    tq = 128
    rows = NSA_HPG * tq
    n_di = (REL_MAX_DIST + tq - 1) // 128 + 1
    bt = _toeplitz_bias(rel_bias, tq, n_di + 1)
    body = functools.partial(_attn_body, tq=tq)
    kv_spec = pl.BlockSpec((1, t_len, NSA_KV_W), lambda bi, i: (bi, 0, 0))
    return pl.pallas_call(
        body,
        grid=(b, t_len // tq),
        in_specs=[
            pl.BlockSpec((1, tq, NSA_Q_W), lambda bi, i: (bi, i, UB_Q // NSA_Q_W)),
            pl.BlockSpec((1, tq, NSA_Q_W), lambda bi, i: (bi, i, 0)),
            pl.BlockSpec((1, tq, NSA_G * LANE), lambda bi, i: (bi, i, 0)),
            kv_spec, kv_spec, kv_spec, kv_spec,
            pl.BlockSpec((t_len, LANE), lambda bi, i: (0, 0)),
            pl.BlockSpec(bt.shape, lambda bi, i: (0, 0, 0)),
            pl.BlockSpec((1, tq, LANE), lambda bi, i: (bi, i, UB_G // LANE)),
            pl.BlockSpec((1, tq, NSA_Q_W), lambda bi, i: (bi, i, UB_Z // NSA_Q_W)),
            pl.BlockSpec((1, HEAD_DIM), lambda bi, i: (0, 0)),
        ],
        out_specs=pl.BlockSpec((1, tq, NSA_Q_W), lambda bi, i: (bi, i, 0)),
        out_shape=jax.ShapeDtypeStruct((b, t_len, NSA_Q_W), BF16),
        scratch_shapes=[pltpu.VMEM((rows, 1), F32), pltpu.VMEM((rows, 1), F32), pltpu.VMEM((rows, HEAD_DIM), F32)],
        compiler_params=_cparams(("arbitrary", "arbitrary")),
        name="nsa_attn",
    )(ub, ocmp, nsel, ksb, vsb, kwb, vwb, _block_penalty(t_len), bt, ub, ub, q_norm_g.reshape(1, HEAD_DIM))


def _q_all(q_ref, qg_ref, ts):
    scale = HEAD_DIM ** -0.5
    zero = jnp.zeros((NSA_HPG * ts, HEAD_DIM), BF16)
    blocks = []
    for g in range(NSA_G):
        q4 = jnp.concatenate(
            [(_rms(q_ref[0, :, (g * NSA_HPG + hh) * HEAD_DIM:(g * NSA_HPG + hh + 1) * HEAD_DIM], qg_ref[...]) * scale).astype(BF16)
             for hh in range(NSA_HPG)], axis=0)
        blocks.append(jnp.concatenate([q4 if gg == g else zero for gg in range(NSA_G)], axis=1))
    return jnp.concatenate(blocks, axis=0)


def _row_to_col(row, n):
    eye = lax.broadcasted_iota(jnp.int32, (n, n), 0) == lax.broadcasted_iota(jnp.int32, (n, n), 1)
    return jnp.sum(jnp.where(eye, row, 0.0), axis=1, keepdims=True)


def _cmp_sample_body(q_ref, kc_ref, vc_ref, bias_ref, msel_ref, rep_ref, qg_ref, ocmp_ref, selt_ref, *, ts, nb, nbp):
    scale = HEAD_DIM ** -0.5
    ncp = kc_ref.shape[1]
    scores = []
    for g in range(NSA_G):
        gsl = slice(g * HEAD_DIM, (g + 1) * HEAD_DIM)
        q4 = jnp.concatenate(
            [_rms(q_ref[0, :, (g * NSA_HPG + hh) * HEAD_DIM:(g * NSA_HPG + hh + 1) * HEAD_DIM], qg_ref[...]) * scale
             for hh in range(NSA_HPG)], axis=0)
        bias = bias_ref[g * NSA_HPG:(g + 1) * NSA_HPG].reshape(NSA_HPG * ts, ncp)
        s = _mm_nt_hi(q4, kc_ref[0, :, gsl]) + bias
        e = jnp.exp(s - jnp.max(s, axis=-1, keepdims=True))
        p = e / jnp.sum(e, axis=-1, keepdims=True) * jnp.where(bias > 0.5 * NEG, 1.0, 0.0)
        o = _mm(p, vc_ref[0, :, gsl])
        imp = p[0:ts]
        for hh in range(NSA_HPG):
            h = g * NSA_HPG + hh
            ocmp_ref[0, :, h * HEAD_DIM:(h + 1) * HEAD_DIM] = o[hh * ts:(hh + 1) * ts]
            if hh:
                imp = imp + p[hh * ts:(hh + 1) * ts]
        scores.append(_mm_hi(imp, msel_ref[...]))
    score = jnp.concatenate(scores, axis=0)
    rows = NSA_G * ts
    lane = lax.broadcasted_iota(jnp.int32, (rows, nbp), 1)
    tok = lax.broadcasted_iota(jnp.int32, (rows, nbp), 0) & (ts - 1)
    qblk = (PAST_LEN + tok) >> 6
    forced = (lane == 0) | (lane == qblk) | (lane == qblk - 1)
    score = jnp.where(forced, BIG, score)
    score = jnp.where(lane <= qblk, score, -BIG)
    sel = jnp.zeros((rows, nbp), F32)
    lane_f = lane.astype(F32)
    for _ in range(SEL_TOP):
        mx = jnp.max(score, axis=-1, keepdims=True)
        first = jnp.min(jnp.where(score == mx, lane_f, float(nbp)), axis=-1, keepdims=True)
        pick = lane_f == first
        sel = jnp.where(pick, 1.0, sel)
        score = jnp.where(pick, -3e38, score)
    selt_ref[0] = _mm_tn(sel, rep_ref[...])


def _cmp_select_sample(ub, kcmp, vcmp, bias_cmp, q_norm_g, nb):
    b, ts, _ = ub.shape
    ncp = kcmp.shape[1]
    nc = ncp - 1
    nbp = -(-nb // LANE) * LANE
    assert ts & (ts - 1) == 0 and NSA_HEADS * ts == LANE
    rep = np.zeros((NSA_G * ts, LANE), np.float32)
    for g in range(NSA_G):
        for hh in range(NSA_HPG):
            for t in range(ts):
                rep[g * ts + t, (g * NSA_HPG + hh) * ts + t] = 1.0
    body = functools.partial(_cmp_sample_body, ts=ts, nb=nb, nbp=nbp)
    return pl.pallas_call(
        body,
        grid=(b,),
        in_specs=[
            pl.BlockSpec((1, ts, NSA_Q_W), lambda bi: (bi, 0, UB_Q // NSA_Q_W)),
            pl.BlockSpec((1, ncp, NSA_KV_W), lambda bi: (bi, 0, 0)),
            pl.BlockSpec((1, ncp, NSA_KV_W), lambda bi: (bi, 0, 0)),
            pl.BlockSpec((NSA_HEADS, ts, ncp), lambda bi: (0, 0, 0)),
            pl.BlockSpec((ncp, nbp), lambda bi: (0, 0)),
            pl.BlockSpec((NSA_G * ts, LANE), lambda bi: (0, 0)),
            pl.BlockSpec((1, HEAD_DIM), lambda bi: (0, 0)),
        ],
        out_specs=[
            pl.BlockSpec((1, ts, NSA_Q_W), lambda bi: (bi, 0, 0)),
            pl.BlockSpec((1, nbp, LANE), lambda bi: (bi, 0, 0)),
        ],
        out_shape=[jax.ShapeDtypeStruct((b, ts, NSA_Q_W), F32), jax.ShapeDtypeStruct((b, nbp, LANE), F32)],
        compiler_params=_cparams(("arbitrary",)),
        name="cmp_select_sample",
    )(ub, kcmp, vcmp, bias_cmp, _sel_matrix(nc, nb, ncp, nbp), jnp.asarray(rep), q_norm_g.reshape(1, HEAD_DIM))


def _sel_sample_body(pt_ref, *refs, pps, n_steps, ts):
    del pt_ref
    k_refs = refs[:pps]
    v_refs = refs[pps:2 * pps]
    (q_ref, selt_ref, knew_ref, vnew_ref, blast_ref, bfar_ref, bnew_ref, qg_ref,
     o_ref, qall_s, sc_s, snew_s, m_s, l_s, acc_s) = refs[2 * pps:]
    s = pl.program_id(1)
    n_pages = n_steps * pps
    sub = lax.broadcasted_iota(jnp.int32, (PAGE, LANE), 0)
    tok = lax.broadcasted_iota(jnp.int32, (ts, LANE), 1) & (ts - 1)
    new_t = lax.broadcasted_iota(jnp.int32, (ts, LANE), 0)

    @pl.when(s == 0)
    def _():
        qall = _q_all(q_ref, qg_ref, ts)
        qall_s[...] = qall
        sn = _mm_nt(knew_ref[0], qall) + bnew_ref[...]
        ok = (new_t <= tok) & (selt_ref[0, 2 * n_pages:2 * n_pages + 1, :] > 0.5)
        sn = jnp.where(ok, sn, NEG)
        snew_s[...] = sn
        m_s[...] = jnp.max(sn, axis=0, keepdims=True)
        l_s[...] = jnp.zeros((1, LANE), F32)
        acc_s[...] = jnp.zeros((LANE, NSA_KV_W), F32)

    @pl.when(s < n_steps)
    def _():
        m = m_s[...]
        for j in range(pps):
            p = s * pps + j
            st = _mm_nt(k_refs[j][0], qall_s[...])
            st = st + jnp.where(p == n_pages - 1, blast_ref[...], bfar_ref[...])
            r0 = selt_ref[0, pl.ds(2 * p, 1), :]
            r1 = selt_ref[0, pl.ds(2 * p + 1, 1), :]
            ok = jnp.where(sub < SEL_BLOCK, r0, r1) > 0.5
            st = jnp.where(ok, st, NEG)
            sc_s[pl.ds(pl.multiple_of(p * PAGE, PAGE), PAGE), :] = st
            m = jnp.maximum(m, jnp.max(st, axis=0, keepdims=True))
        m_s[...] = m

    @pl.when(s >= n_steps)
    def _():
        m = m_s[...]
        l = l_s[...]
        acc = acc_s[...]
        for j in range(pps):
            p = (s - n_steps) * pps + j
            pt = jnp.exp(sc_s[pl.ds(pl.multiple_of(p * PAGE, PAGE), PAGE), :] - m)
            l = l + jnp.sum(pt, axis=0, keepdims=True)
            acc = acc + _mm_tn(pt, v_refs[j][0])
        l_s[...] = l
        acc_s[...] = acc

    @pl.when(s == 2 * n_steps - 1)
    def _():
        pn = jnp.exp(snew_s[...] - m_s[...])
        l = l_s[...] + jnp.sum(pn, axis=0, keepdims=True)
        acc = acc_s[...] + _mm_tn(pn, vnew_ref[0])
        out = acc / _row_to_col(l, LANE)
        for h in range(NSA_HEADS):
            g = h // NSA_HPG
            o_ref[0, :, h * HEAD_DIM:(h + 1) * HEAD_DIM] = out[h * ts:(h + 1) * ts, g * HEAD_DIM:(g + 1) * HEAD_DIM]


def _lane_bias(rel_bias, key_pos, ts):
    rel = (PAST_LEN + jnp.arange(ts, dtype=jnp.int32))[None, :] - key_pos[:, None]
    return jnp.moveaxis(_bias_of(rel_bias, rel), 0, 1).reshape(key_pos.shape[0], NSA_HEADS * ts)


def _sel_sample(ub, selt, cache_k, cache_v, table, k_new, rel_bias, q_norm_g):
    b, ts, _ = ub.shape
    n_pages = table.shape[1]
    assert PAGE >= REL_MAX_DIST and n_pages * PAGE == PAST_LEN
    pps = 4
    n_steps = n_pages // pps
    ar = jnp.arange
    blast = _lane_bias(rel_bias, PAST_LEN - PAGE + ar(PAGE, dtype=jnp.int32), ts)
    bfar = _lane_bias(rel_bias, jnp.zeros((1,), jnp.int32), ts)
    bnew = _lane_bias(rel_bias, PAST_LEN + ar(ts, dtype=jnp.int32), ts)
    body = functools.partial(_sel_sample_body, pps=pps, n_steps=n_steps, ts=ts)
    last = n_steps - 1
    k_spec = lambda j: pl.BlockSpec((1, PAGE, NSA_KV_W), lambda bi, s, pt: (pt[bi, jnp.minimum(s, last) * pps + j], 0, 0))
    v_spec = lambda j: pl.BlockSpec((1, PAGE, NSA_KV_W), lambda bi, s, pt: (pt[bi, jnp.maximum(s - n_steps, 0) * pps + j], 0, 0))
    full = lambda shape: pl.BlockSpec(shape, lambda bi, s, pt: (0,) * len(shape))
    in_specs = [k_spec(j) for j in range(pps)] + [v_spec(j) for j in range(pps)] + [
        pl.BlockSpec((1, ts, NSA_Q_W), lambda bi, s, pt: (bi, 0, UB_Q // NSA_Q_W)),
        pl.BlockSpec((1,) + selt.shape[1:], lambda bi, s, pt: (bi, 0, 0)),
        pl.BlockSpec((1, ts, NSA_KV_W), lambda bi, s, pt: (bi, 0, 0)),
        pl.BlockSpec((1, ts, NSA_KV_W), lambda bi, s, pt: (bi, 0, UB_KV // NSA_KV_W + 3)),
        full((PAGE, LANE)), full((1, LANE)), full((ts, LANE)), full((1, HEAD_DIM)),
    ]
    return pl.pallas_call(
        body,
        grid_spec=pltpu.PrefetchScalarGridSpec(
            num_scalar_prefetch=1, grid=(b, 2 * n_steps), in_specs=in_specs,
            out_specs=pl.BlockSpec((1, ts, NSA_Q_W), lambda bi, s, pt: (bi, 0, 0)),
            scratch_shapes=[pltpu.VMEM((LANE, NSA_KV_W), BF16), pltpu.VMEM((PAST_LEN, LANE), F32),
                            pltpu.VMEM((ts, LANE), F32), pltpu.VMEM((1, LANE), F32), pltpu.VMEM((1, LANE), F32),
                            pltpu.VMEM((LANE, NSA_KV_W), F32)]),
        out_shape=jax.ShapeDtypeStruct((b, ts, NSA_Q_W), F32),
        compiler_params=_cparams(("arbitrary", "arbitrary")),
        name="sel_sample",
    )(table, *([cache_k] * pps), *([cache_v] * pps), ub, selt, k_new, ub, blast, bfar, bnew, q_norm_g.reshape(1, HEAD_DIM))


def _win_sample_body(q_ref, kc_ref, vc_ref, knew_ref, vnew_ref, bc_ref, bn_ref, gate_ref, z_ref, ocmp_ref, osel_ref, qg_ref,
                     o_ref, kout_ref, vout_ref, *, ts, wb):
    qall = _q_all(q_ref, qg_ref, ts)
    key_i = lax.broadcasted_iota(jnp.int32, (wb, LANE), 0)
    tok_c = lax.broadcasted_iota(jnp.int32, (wb, LANE), 1) & (ts - 1)
    tok_n = lax.broadcasted_iota(jnp.int32, (ts, LANE), 1) & (ts - 1)
    new_t = lax.broadcasted_iota(jnp.int32, (ts, LANE), 0)
    sc = _mm_nt(kc_ref[0], qall) + bc_ref[...]
    sc = jnp.where(wb + tok_c - key_i < WINDOW, sc, NEG)
    sn = _mm_nt(knew_ref[0], qall) + bn_ref[...]
    sn = jnp.where(new_t <= tok_n, sn, NEG)
    m = jnp.maximum(jnp.max(sc, axis=0, keepdims=True), jnp.max(sn, axis=0, keepdims=True))
    pc = jnp.exp(sc - m)
    pn = jnp.exp(sn - m)
    l = jnp.sum(pc, axis=0, keepdims=True) + jnp.sum(pn, axis=0, keepdims=True)
    out = (_mm_tn(pc, vc_ref[0]) + _mm_tn(pn, vnew_ref[0])) / _row_to_col(l, LANE)
    gates = jax.nn.sigmoid(gate_ref[0])
    for h in range(NSA_HEADS):
        g = h // NSA_HPG
        hsl = slice(h * HEAD_DIM, (h + 1) * HEAD_DIM)
        o_win = out[h * ts:(h + 1) * ts, g * HEAD_DIM:(g + 1) * HEAD_DIM]
        o = (gates[:, h:h + 1] * ocmp_ref[0, :, hsl] + gates[:, NSA_HEADS + h:NSA_HEADS + h + 1] * osel_ref[0, :, hsl]
             + gates[:, 2 * NSA_HEADS + h:2 * NSA_HEADS + h + 1] * o_win)
        o_ref[0, :, hsl] = (o * _silu(z_ref[0, :, hsl])).astype(o_ref.dtype)
    kout_ref[0, 0:wb - ts, :] = kc_ref[0, ts:wb, :]
    kout_ref[0, wb - ts:wb, :] = knew_ref[0]
    vout_ref[0, 0:wb - ts, :] = vc_ref[0, ts:wb, :]
    vout_ref[0, wb - ts:wb, :] = vnew_ref[0]


def _win_sample(ub, cache_k, cache_v, k_new, ocmp, osel, rel_bias, q_norm_g):
    b, ts, _ = ub.shape
    wb = cache_k.shape[1]
    assert wb == WINDOW and ts % 8 == 0
    bc = _lane_bias(rel_bias, PAST_LEN - wb + jnp.arange(wb, dtype=jnp.int32), ts)
    bn = _lane_bias(rel_bias, PAST_LEN + jnp.arange(ts, dtype=jnp.int32), ts)
    body = functools.partial(_win_sample_body, ts=ts, wb=wb)
    tok_spec = lambda w, c: pl.BlockSpec((1, ts, w), lambda bi: (bi, 0, c))
    win_spec = pl.BlockSpec((1, wb, NSA_KV_W), lambda bi: (bi, 0, 0))
    return pl.pallas_call(
        body,
        grid=(b,),
        in_specs=[
            tok_spec(NSA_Q_W, UB_Q // NSA_Q_W), win_spec, win_spec,
            tok_spec(NSA_KV_W, 0), tok_spec(NSA_KV_W, UB_KV // NSA_KV_W + 5),
            pl.BlockSpec((wb, LANE), lambda bi: (0, 0)), pl.BlockSpec((ts, LANE), lambda bi: (0, 0)),
            tok_spec(LANE, UB_G // LANE), tok_spec(NSA_Q_W, UB_Z // NSA_Q_W),
            tok_spec(NSA_Q_W, 0), tok_spec(NSA_Q_W, 0),
            pl.BlockSpec((1, HEAD_DIM), lambda bi: (0, 0)),
        ],
        out_specs=[tok_spec(NSA_Q_W, 0), win_spec, win_spec],
        out_shape=[jax.ShapeDtypeStruct((b, ts, NSA_Q_W), BF16),
                   jax.ShapeDtypeStruct((b, wb, NSA_KV_W), F32), jax.ShapeDtypeStruct((b, wb, NSA_KV_W), F32)],
        compiler_params=_cparams(("arbitrary",)),
        name="win_sample",
    )(ub, cache_k, cache_v, k_new, ub, bc, bn, ub, ub, ocmp, osel, q_norm_g.reshape(1, HEAD_DIM))


def _mix_body(oa_ref, ob_ref, wa_ref, wb_ref, ma_ref, mb_ref, o_ref):
    pa = jnp.dot(oa_ref[...], wa_ref[...], preferred_element_type=F32)
    pb = jnp.dot(ob_ref[...], wb_ref[...], preferred_element_type=F32)
    o_ref[...] = (jax.nn.sigmoid(ma_ref[...]) * pa + jax.nn.sigmoid(mb_ref[...]) * pb).astype(o_ref.dtype)


def _out_body(m_ref, w_ref, x_ref, y_ref):
    y_ref[...] = x_ref[...] + jnp.dot(m_ref[...], w_ref[...], preferred_element_type=F32)


def _output(x2d, oa, ob, ub2d, wa16, wb16, wo16):
    m = x2d.shape[0]
    tm = min(m, 512)
    tn = 512
    grid = (m // tm, D_MODEL // tn)
    mixed = pl.pallas_call(
        _mix_body,
        grid=grid,
        in_specs=[
            pl.BlockSpec((tm, GDN_V_W), lambda i, j: (i, 0)),
            pl.BlockSpec((tm, NSA_Q_W), lambda i, j: (i, 0)),
            pl.BlockSpec((GDN_V_W, tn), lambda i, j: (0, j)),
            pl.BlockSpec((NSA_Q_W, tn), lambda i, j: (0, j)),
            pl.BlockSpec((tm, tn), lambda i, j: (i, UB_MA // tn + j)),
            pl.BlockSpec((tm, tn), lambda i, j: (i, UB_MB // tn + j)),
        ],
        out_specs=pl.BlockSpec((tm, tn), lambda i, j: (i, j)),
        out_shape=jax.ShapeDtypeStruct((m, D_MODEL), BF16),
        compiler_params=_cparams(("arbitrary", "arbitrary")),
        name="branch_mix",
    )(oa, ob, wa16, wb16, ub2d, ub2d)
    return pl.pallas_call(
        _out_body,
        grid=grid,
        in_specs=[
            pl.BlockSpec((tm, D_MODEL), lambda i, j: (i, 0)),
            pl.BlockSpec((D_MODEL, tn), lambda i, j: (0, j)),
            pl.BlockSpec((tm, tn), lambda i, j: (i, j)),
        ],
        out_specs=pl.BlockSpec((tm, tn), lambda i, j: (i, j)),
        out_shape=jax.ShapeDtypeStruct((m, D_MODEL), F32),
        compiler_params=_cparams(("arbitrary", "arbitrary")),
        name="out_proj",
    )(mixed, wo16, x2d)


def _kv_slices(ub, rows=None):
    b, t_len, _ = ub.shape
    out = []
    for i in (0, 1, 3, 5):
        a = ub[:, :, UB_KV + i * NSA_KV_W:UB_KV + (i + 1) * NSA_KV_W]
        out.append(a.reshape(b, t_len, NSA_G, HEAD_DIM))
    return out


def kernel(x_prompt, x_sample, cache_k_cmp, cache_v_cmp, cache_k_sel, cache_v_sel, cache_k_win, cache_v_win, state_conv, state_gdn, page_table, norm_g, w_in, gdn_conv_w, gdn_a_log, gdn_dt_bias, gdn_norm_g, q_norm_g, k_norm_g, cmp_pe_k, cmp_w_k, cmp_proj_k, cmp_pe_v, cmp_w_v, cmp_proj_v, rel_bias, w_branch_a, w_branch_b, w_out):
    bp, tp, _ = x_prompt.shape
    bs, ts, _ = x_sample.shape
    wb = cache_k_win.shape[1]
    n_pool = cache_k_cmp.shape[0]
    kv4 = lambda a: a.reshape(a.shape[0], a.shape[1], NSA_G, HEAD_DIM)
    cmp_p = (cmp_pe_k, cmp_w_k, cmp_proj_k, cmp_pe_v, cmp_w_v, cmp_proj_v)

    c0 = UA_W
    pad = lambda w, n: jnp.pad(w, ((0, 0), (0, n - w.shape[1])))
    w_ba = w_in[:, c0:c0 + 2 * GDN_V_HEADS]
    c1 = c0 + 2 * GDN_V_HEADS
    w_q = w_in[:, c1:c1 + NSA_Q_W]
    c2 = c1 + NSA_Q_W
    w_kv = w_in[:, c2:c2 + 6 * NSA_KV_W]
    c3 = c2 + 6 * NSA_KV_W
    w_g = w_in[:, c3:c3 + 3 * NSA_HEADS]
    c4 = c3 + 3 * NSA_HEADS
    w_z = w_in[:, c4:c4 + NSA_Q_W]
    w_m = w_in[:, c4 + NSA_Q_W:]
    w_b = jnp.concatenate([w_q, w_z, w_m, w_kv, pad(w_ba, LANE), pad(w_g, LANE)], axis=1)
    wa16, wb16, wo16 = w_branch_a.astype(BF16), w_branch_b.astype(BF16), w_out.astype(BF16)

    def project(x):
        x2d = x.reshape(-1, D_MODEL)
        ua = _project(x2d, norm_g, w_in, UA_W, 768).reshape(x.shape[0], x.shape[1], UA_W)
        ub = _project(x2d, norm_g, w_b, UB_W, 768).reshape(x.shape[0], x.shape[1], UB_W)
        return x2d, ua, ub

    x2d, ua, ub = project(x_prompt)
    conv0 = jnp.zeros((bp, GDN_CONV - 1, GDN_QKV_W), F32)
    s0 = jnp.zeros((bp, GDN_V_HEADS, HEAD_DIM, HEAD_DIM), F32)
    o_a, p_gdn = _gdn(ua, ub, conv0, s0, gdn_conv_w, gdn_a_log, gdn_dt_bias, gdn_norm_g)
    p_conv = ua[:, tp - (GDN_CONV - 1):, :GDN_QKV_W]
    n_pages_p = tp // PAGE
    pages = ub.reshape(bp * n_pages_p, PAGE, UB_W)
    ident = jnp.arange(bp * n_pages_p, dtype=jnp.int32).reshape(bp, n_pages_p)
    kcmp, vcmp = _compress(pages, pages, UB_KV // NSA_KV_W, UB_KV // NSA_KV_W + 1, ident, *cmp_p, k_norm_g[0])
    p_ks, p_kw, ksb, vsb, kwb, vwb = _nsa_prep(ub, k_norm_g)
    nc_p = (tp - CMP_LEN) // CMP_STRIDE + 1
    bias_cmp = _cmp_bias(rel_bias, jnp.arange(tp, dtype=jnp.int32), kcmp.shape[1], nc_p)
    ocmp, nsel = _cmp_select_prompt(ub, kcmp, vcmp, bias_cmp, q_norm_g)
    o_b = _attn_prompt(ub, ocmp, nsel, ksb, vsb, kwb, vwb, rel_bias, q_norm_g)
    y_prompt = _output(x2d, o_a.reshape(-1, GDN_V_W), o_b.reshape(-1, NSA_Q_W), ub.reshape(-1, UB_W), wa16, wb16, wo16)
    p_kc, p_vc, p_vs, p_vw = _kv_slices(ub)
    prompt_out = (y_prompt.reshape(x_prompt.shape), p_kc, p_vc, kv4(p_ks), p_vs, kv4(p_kw[:, tp - wb:]), p_vw[:, tp - wb:],
                  p_conv, p_gdn)

    x2d, ua, ub = project(x_sample)
    o_a, s_gdn = _gdn(ua, ub, state_conv, state_gdn, gdn_conv_w, gdn_a_log, gdn_dt_bias, gdn_norm_g)
    s_conv = ua[:, ts - (GDN_CONV - 1):, :GDN_QKV_W]
    pool = lambda c: c.reshape(n_pool, PAGE, NSA_KV_W)
    kcmp, vcmp = _compress(pool(cache_k_cmp), pool(cache_v_cmp), 0, 0, page_table, *cmp_p, k_norm_g[0])
    s_ks, s_kwn, _, _, _, _ = _nsa_prep(ub, k_norm_g)
    total = PAST_LEN + ts
    nc_s = (total - CMP_LEN) // CMP_STRIDE + 1
    nb_s = -(-total // SEL_BLOCK)
    assert nc_s + 1 == kcmp.shape[1]
    pos_s = PAST_LEN + jnp.arange(ts, dtype=jnp.int32)
    bias_cmp = _cmp_bias(rel_bias, pos_s, kcmp.shape[1], nc_s)
    ocmp, selt = _cmp_select_sample(ub, kcmp, vcmp, bias_cmp, q_norm_g, nb_s)
    osel = _sel_sample(ub, selt, pool(cache_k_sel), pool(cache_v_sel), page_table, s_ks, rel_bias, q_norm_g)
    win3 = lambda c: c.reshape(bs, wb, NSA_KV_W)
    o_b, s_kw, s_vw = _win_sample(ub, win3(cache_k_win), win3(cache_v_win), s_kwn, ocmp, osel, rel_bias, q_norm_g)
    y_sample = _output(x2d, o_a.reshape(-1, GDN_V_W), o_b.reshape(-1, NSA_Q_W), ub.reshape(-1, UB_W), wa16, wb16, wo16)
    s_kc, s_vc, s_vs, _ = _kv_slices(ub)
    sample_out = (y_sample.reshape(x_sample.shape), s_kc, s_vc, kv4(s_ks), s_vs, kv4(s_kw), kv4(s_vw), s_conv, s_gdn)

    return (prompt_out[0], sample_out[0]) + prompt_out[1:] + sample_out[1:]
```

```python
import functools
import math

import jax
import jax.numpy as jnp
import numpy as np
from jax import lax
from jax.experimental import pallas as pl
from jax.experimental.pallas import tpu as pltpu

F32 = jnp.float32
BF16 = jnp.bfloat16
HI = lax.Precision.HIGHEST

D_MODEL = 2048
PAST_LEN = 16384
PAGE = 128

GDN_QK_HEADS = 16
GDN_V_HEADS = 32
HEAD_DIM = 128
GDN_CONV = 4
GDN_CHUNK = 64
GDN_GROUP = 4
GDN_QK_W = GDN_QK_HEADS * HEAD_DIM
GDN_V_W = GDN_V_HEADS * HEAD_DIM
GDN_QKV_W = 2 * GDN_QK_W + GDN_V_W

NSA_HEADS = 16
NSA_G = 4
NSA_HPG = NSA_HEADS // NSA_G
NSA_Q_W = NSA_HEADS * HEAD_DIM
NSA_KV_W = NSA_G * HEAD_DIM
CMP_LEN = 32
CMP_STRIDE = 16
SEL_BLOCK = 64
SEL_TOP = 16
WINDOW = 512
REL_BUCKETS = 32
REL_MAX_DIST = 128

EPS = 1e-6
NEG = -1e30
BIG = 1e9

LANE = 128
VMEM_LIMIT = 56 * 1024 * 1024

UA_W = GDN_QKV_W + GDN_V_W
UB_Q, UB_Z, UB_MA, UB_MB, UB_KV, UB_BA, UB_G = 0, 2048, 4096, 6144, 8192, 11264, 11776
UB_W = 12288
PROJ_TN = 512

_IN_QB = UA_W + 2 * GDN_V_HEADS
_IN_KV = _IN_QB + NSA_Q_W
_IN_G = _IN_KV + 6 * NSA_KV_W
_IN_ZB = _IN_G + 3 * NSA_HEADS
_IN_MA = _IN_ZB + NSA_Q_W
_IN_MB = _IN_MA + D_MODEL


def _tiles(start, width):
    return [start + k * PROJ_TN for k in range(width // PROJ_TN)]


UA_OFFSETS = _tiles(0, UA_W)
UB_OFFSETS = (_tiles(_IN_QB, NSA_Q_W) + _tiles(_IN_ZB, NSA_Q_W) + _tiles(_IN_MA, D_MODEL) + _tiles(_IN_MB, D_MODEL)
              + _tiles(_IN_KV, 6 * NSA_KV_W) + [UA_W, _IN_G])


def _mm(a, b):
    return jnp.dot(a.astype(BF16), b.astype(BF16), preferred_element_type=F32)


def _mm_nt(a, b):
    return lax.dot_general(a.astype(BF16), b.astype(BF16), (((1,), (1,)), ((), ())), preferred_element_type=F32)


def _mm_tn(a, b):
    return lax.dot_general(a.astype(BF16), b.astype(BF16), (((0,), (0,)), ((), ())), preferred_element_type=F32)


def _mm_hi(a, b):
    return jnp.dot(a, b, precision=HI, preferred_element_type=F32)


def _mm_nt_hi(a, b):
    return lax.dot_general(a, b, (((1,), (1,)), ((), ())), precision=HI, preferred_element_type=F32)


def _silu(x):
    return x * jax.nn.sigmoid(x)


def _softplus(x):
    return jnp.maximum(x, 0.0) + jnp.log1p(jnp.exp(-jnp.abs(x)))


def _rms(x, gain):
    return x * lax.rsqrt(jnp.mean(x * x, axis=-1, keepdims=True) + EPS) * gain


def _cparams(sem):
    return pltpu.CompilerParams(dimension_semantics=sem, vmem_limit_bytes=VMEM_LIMIT)


def _proj_body(offs_ref, x_ref, g_ref, w_ref, o_ref, h_ref):
    del offs_ref

    @pl.when(pl.program_id(1) == 0)
    def _():
        h_ref[...] = _rms(x_ref[...], g_ref[...]).astype(BF16)

    o_ref[...] = lax.dot_general(h_ref[...], w_ref[...].astype(BF16), (((1,), (1,)), ((), ())),
                                 preferred_element_type=F32)


def _project(x2d, norm_g, w_t, offsets):
    m = x2d.shape[0]
    tm = min(m, 1024)
    n_tiles = len(offsets)
    return pl.pallas_call(
        _proj_body,
        grid_spec=pltpu.PrefetchScalarGridSpec(
            num_scalar_prefetch=1,
            grid=(m // tm, n_tiles),
            in_specs=[
                pl.BlockSpec((tm, D_MODEL), lambda i, j, offs: (i, 0)),
                pl.BlockSpec((1, D_MODEL), lambda i, j, offs: (0, 0)),
                pl.BlockSpec((pl.Element(PROJ_TN), pl.Element(D_MODEL)), lambda i, j, offs: (pl.multiple_of(offs[j], 16), 0)),
            ],
            out_specs=pl.BlockSpec((tm, PROJ_TN), lambda i, j, offs: (i, j)),
            scratch_shapes=[pltpu.VMEM((tm, D_MODEL), BF16)]),
        out_shape=jax.ShapeDtypeStruct((m, n_tiles * PROJ_TN), F32),
        compiler_params=_cparams(("arbitrary", "arbitrary")),
        name="proj",
    )(jnp.asarray(offsets, jnp.int32), x2d, norm_g.reshape(1, D_MODEL), w_t)


def _split2(a):
    hi = a.astype(BF16)
    return hi, (a - hi.astype(F32)).astype(BF16)


def _dot16(a, b):
    return jnp.dot(a, b, preferred_element_type=F32)


def _bdot(a, b):
    return lax.dot_general(a, b, (((2,), (1,)), ((0,), (0,))), preferred_element_type=F32)


def _bdot_nt(a, b):
    return lax.dot_general(a, b, (((2,), (2,)), ((0,), (0,))), preferred_element_type=F32)


def _bmm_x3(a2, b2):
    return _bdot(a2[0], b2[0]) + (_bdot(a2[0], b2[1]) + _bdot(a2[1], b2[0]))


def _unit_lower_inverse(nmat, c):
    row = lax.broadcasted_iota(jnp.int32, (1, c, c), 1)
    col = lax.broadcasted_iota(jnp.int32, (1, c, c), 2)
    p = jnp.where(row == col, 1.0, 0.0) + nmat
    m2 = _split2(nmat)
    span = 2
    while span < c:
        m2 = _split2(_bmm_x3(m2, m2))
        p = p + _bmm_x3(_split2(p), m2)
        span *= 2
    return p


def _cumsum_rows(tril16, g):
    h = g.astype(BF16)
    r = g - h.astype(F32)
    m = r.astype(BF16)
    l = (r - m.astype(F32)).astype(BF16)
    return _dot16(tril16, h) + (_dot16(tril16, m) + _dot16(tril16, l))


def _gdn_body(q_ref, k_ref, v_ref, z_ref, ba_ref, cwq_ref, cwk_ref, cwv_ref, csq_ref, csk_ref, csv_ref,
              s0_ref, gp_ref, ng_ref, o_ref, sfin_ref, xq_s, xk_s, xv_s, u_s, w_s, qg_s, kd_s, at_s, gl_s, st_s,
              *, t_len, c):
    head = pl.program_id(1)
    n_chunks = t_len // c
    for xs, cs, src in ((xq_s, csq_ref, q_ref), (xk_s, csk_ref, k_ref), (xv_s, csv_ref, v_ref)):
        xs[0:8, :] = jnp.zeros((8, xs.shape[1]), F32)
        xs[5:8, :] = cs[0]
        xs[8:8 + t_len, :] = src[0]
    st_s[...] = s0_ref[0]

    gsz = min(GDN_GROUP, n_chunks)
    rg = gsz * c
    row = lax.broadcasted_iota(jnp.int32, (1, c, c), 1)
    col = lax.broadcasted_iota(jnp.int32, (1, c, c), 2)
    tril = row >= col
    strict = row > col
    eye = row == col
    lane = lax.broadcasted_iota(jnp.int32, (rg, LANE), 1)
    rr = lax.broadcasted_iota(jnp.int32, (rg, rg), 0)
    cc = lax.broadcasted_iota(jnp.int32, (rg, rg), 1)
    shift = c.bit_length() - 1
    assert 1 << shift == c
    tril16 = jnp.where((rr >= cc) & ((rr >> shift) == (cc >> shift)), 1.0, 0.0).astype(BF16)

    def conv(xs, cw_ref, base):
        win = xs[pl.ds(base, rg + 8), :]
        w = cw_ref[...]
        a = win[5:5 + rg] * w[0:1] + win[6:6 + rg] * w[1:2] + win[7:7 + rg] * w[2:3] + win[8:8 + rg] * w[3:4]
        return _silu(a)

    def pick(x, idx):
        return jnp.sum(jnp.where(lane == idx, x, 0.0), axis=-1, keepdims=True).reshape(gsz, c, 1)

    def prepare(gi):
        base = gi * rg if isinstance(gi, int) else pl.multiple_of(gi * rg, rg)
        rows = pl.ds(base, rg)
        qa = conv(xq_s, cwq_ref, base)
        ka = conv(xk_s, cwk_ref, base)
        va = conv(xv_s, cwv_ref, base)
        qn = (qa * lax.rsqrt(jnp.sum(qa * qa, axis=-1, keepdims=True) + EPS) * (HEAD_DIM ** -0.5)).reshape(gsz, c, HEAD_DIM)
        kn = (ka * lax.rsqrt(jnp.sum(ka * ka, axis=-1, keepdims=True) + EPS)).reshape(gsz, c, HEAD_DIM)
        ba = ba_ref[0, rows, :]
        beta_all = jax.nn.sigmoid(ba)
        g_all = -jnp.exp(gp_ref[0:1, :]) * _softplus(ba + gp_ref[1:2, :])
        gc_all = _cumsum_rows(tril16, g_all)
        k16 = kn.astype(BF16)
        kk = _bdot_nt(k16, k16)
        qk = _bdot_nt(qn.astype(BF16), k16)
        nmats, rhss, gammas = [], [], []
        for hh in range(2):
            hv = 2 * head + hh
            beta = pick(beta_all, hv)
            gcol = pick(gc_all, GDN_V_HEADS + hv)
            grow = jnp.sum(jnp.where(eye, gcol, 0.0), axis=1, keepdims=True)
            gamma = jnp.where(tril, jnp.exp(jnp.minimum(gcol - grow, 0.0)), 0.0)
            nmats.append(jnp.where(strict, -(kk * beta * gamma), 0.0))
            egc = jnp.exp(gcol)
            glast = gcol[:, c - 1:c, :]
            v_h = va[:, hh * HEAD_DIM:(hh + 1) * HEAD_DIM].reshape(gsz, c, HEAD_DIM)
            rhss.append(jnp.concatenate([v_h * beta, kn * (beta * egc)], axis=2))
            gammas.append(gamma)
            qg_s[hh, rows, :] = (qn * egc).reshape(rg, HEAD_DIM)
            kd_s[hh, rows, :] = (kn * jnp.exp(glast - gcol)).reshape(rg, HEAD_DIM)
            egl = jnp.exp(glast)
            for j in range(gsz):
                gl_s[hh, gi * gsz + j] = jnp.broadcast_to(egl[j], (8, HEAD_DIM))
        tinv = _unit_lower_inverse(jnp.concatenate(nmats, axis=0), c)
        uw = _bmm_x3(_split2(tinv), _split2(jnp.concatenate(rhss, axis=0)))
        for hh in range(2):
            uw_h = uw[hh * gsz:(hh + 1) * gsz]
            u_s[hh, rows, :] = uw_h[:, :, :HEAD_DIM].reshape(rg, HEAD_DIM)
            w_s[hh, rows, :] = uw_h[:, :, HEAD_DIM:].reshape(rg, HEAD_DIM)
        for hh in range(2):
            at_s[hh, rows, :] = (qk * gammas[hh]).reshape(rg, c)

    def recur(ci):
        rows = pl.ds(ci * c if isinstance(ci, int) else pl.multiple_of(ci * c, c), c)
        s = st_s[...]
        s16 = s.astype(BF16)
        attn = at_s[:, rows, :].astype(BF16)
        v_new = u_s[:, rows, :] - _bdot(w_s[:, rows, :].astype(BF16), s16)
        v16 = v_new.astype(BF16)
        o = _bdot(qg_s[:, rows, :].astype(BF16), s16) + _bdot(attn, v16)
        kd16 = kd_s[:, rows, :].astype(BF16)
        upd = jnp.stack([_mm_tn(kd16[hh], v16[hh]) for hh in range(2)], axis=0)
        st_s[...] = s * gl_s[:, ci][:, 0:1, :] + upd
        for hh in range(2):
            hsl = slice(hh * HEAD_DIM, (hh + 1) * HEAD_DIM)
            o_ref[0, rows, hsl] = (_rms(o[hh], ng_ref[...]) * _silu(z_ref[0, rows, hsl])).astype(o_ref.dtype)

    if n_chunks == 1:
        prepare(0)
        recur(0)
    else:
        def prepare_step(gi, carry):
            prepare(gi)
            return carry

        def recur_step(ci, carry):
            recur(ci)
            return carry

        lax.fori_loop(0, n_chunks // gsz, prepare_step, 0)
        lax.fori_loop(0, n_chunks, recur_step, 0)
    sfin_ref[0] = st_s[...]


def _gdn(ua, ub, conv_state, s0, conv_w, a_log, dt_bias, norm_g):
    b, t_len, _ = ua.shape
    c = min(GDN_CHUNK, t_len)
    hd = HEAD_DIM
    nqk = GDN_QK_HEADS
    gp = jnp.zeros((2, LANE), F32)
    gp = gp.at[0, GDN_V_HEADS:2 * GDN_V_HEADS].set(a_log).at[1, GDN_V_HEADS:2 * GDN_V_HEADS].set(dt_bias)
    body = functools.partial(_gdn_body, t_len=t_len, c=c)
    return pl.pallas_call(
        body,
        grid=(b, nqk),
        in_specs=[
            pl.BlockSpec((1, t_len, hd), lambda bi, i: (bi, 0, i)),
            pl.BlockSpec((1, t_len, hd), lambda bi, i: (bi, 0, nqk + i)),
            pl.BlockSpec((1, t_len, 2 * hd), lambda bi, i: (bi, 0, nqk + i)),
            pl.BlockSpec((1, t_len, 2 * hd), lambda bi, i: (bi, 0, 2 * nqk + i)),
            pl.BlockSpec((1, t_len, LANE), lambda bi, i: (bi, 0, UB_BA // LANE)),
            pl.BlockSpec((GDN_CONV, hd), lambda bi, i: (0, i)),
            pl.BlockSpec((GDN_CONV, hd), lambda bi, i: (0, nqk + i)),
            pl.BlockSpec((GDN_CONV, 2 * hd), lambda bi, i: (0, nqk + i)),
            pl.BlockSpec((1, GDN_CONV - 1, hd), lambda bi, i: (bi, 0, i)),
            pl.BlockSpec((1, GDN_CONV - 1, hd), lambda bi, i: (bi, 0, nqk + i)),
            pl.BlockSpec((1, GDN_CONV - 1, 2 * hd), lambda bi, i: (bi, 0, nqk + i)),
            pl.BlockSpec((1, 2, hd, hd), lambda bi, i: (bi, i, 0, 0)),
            pl.BlockSpec((2, LANE), lambda bi, i: (0, 0)),
            pl.BlockSpec((1, hd), lambda bi, i: (0, 0)),
        ],
        out_specs=[
            pl.BlockSpec((1, t_len, 2 * hd), lambda bi, i: (bi, 0, i)),
            pl.BlockSpec((1, 2, hd, hd), lambda bi, i: (bi, i, 0, 0)),
        ],
        out_shape=[
            jax.ShapeDtypeStruct((b, t_len, GDN_V_W), BF16),
            jax.ShapeDtypeStruct((b, GDN_V_HEADS, hd, hd), F32),
        ],
        scratch_shapes=[
            pltpu.VMEM((t_len + 8, hd), F32),
            pltpu.VMEM((t_len + 8, hd), F32),
            pltpu.VMEM((t_len + 8, 2 * hd), F32),
            pltpu.VMEM((2, t_len, hd), F32),
            pltpu.VMEM((2, t_len, hd), F32),
            pltpu.VMEM((2, t_len, hd), F32),
            pltpu.VMEM((2, t_len, hd), F32),
            pltpu.VMEM((2, t_len, c), F32),
            pltpu.VMEM((2, t_len // c, 8, hd), F32),
            pltpu.VMEM((2, hd, hd), F32),
        ],
        compiler_params=_cparams(("arbitrary", "arbitrary")),
        name="gdn",
    )(ua, ua, ua, ua, ub, conv_w, conv_w, conv_w, conv_state, conv_state, conv_state, s0, gp, norm_g.reshape(1, hd))


def _t5_bucket(rel):
    n = jnp.maximum(rel, 0)
    exact = REL_BUCKETS // 2
    nf = jnp.maximum(n, 1).astype(F32)
    large = exact + (jnp.log(nf / exact) / math.log(REL_MAX_DIST / exact) * (REL_BUCKETS - exact)).astype(jnp.int32)
    return jnp.where(n < exact, n, jnp.minimum(large, REL_BUCKETS - 1))


def _bias_of(rel_bias, rel):
    bucket = _t5_bucket(rel)[None]
    table = rel_bias.astype(F32)
    out = jnp.zeros((NSA_HEADS,) + rel.shape, F32)
    for k in range(REL_BUCKETS):
        out = jnp.where(bucket == k, table[k].reshape((NSA_HEADS,) + (1,) * rel.ndim), out)
    return out


def _cmp_bias(rel_bias, q_pos, ncp, nc):
    n = jnp.arange(ncp, dtype=jnp.int32)
    rel = q_pos[:, None] - (n * CMP_STRIDE + (CMP_LEN - 1))[None, :]
    ok = (rel >= 0) & (n < nc)[None, :]
    return jnp.where(ok[None], _bias_of(rel_bias, rel), NEG)


def _load_page(ref, interleaved):
    if not interleaved:
        return ref[0]
    n_rows = ref.shape[1] // NSA_G
    return jnp.concatenate([ref[0, pl.ds(g, n_rows, stride=NSA_G), :] for g in range(NSA_G)], axis=1)


def _compress_body(pt_ref, *refs, pps, n_steps, nch, interleaved):
    del pt_ref
    k_refs = refs[:pps]
    v_refs = refs[pps:2 * pps]
    (wabk_ref, wabv_ref, wk_ref, pek_ref, wv_ref, pev_ref, projk_ref, projv_ref, gain_ref,
     outk_ref, outv_ref, ak_s, bk_s, av_s, bv_s) = refs[2 * pps:]
    s = pl.program_id(1)

    @pl.when(s == 0)
    def _():
        bk_s[nch:nch + 8, :] = jnp.zeros((8, NSA_KV_W), F32)
        bv_s[nch:nch + 8, :] = jnp.zeros((8, NSA_KV_W), F32)

    for j in range(pps):
        row0 = pl.multiple_of((s * pps + j) * 8, 8)
        abk = _mm_hi(wabk_ref[...], _load_page(k_refs[j], interleaved))
        ak_s[pl.ds(row0, 8), :] = abk[0:8]
        bk_s[pl.ds(row0, 8), :] = abk[8:16]
        abv = _mm_hi(wabv_ref[...], _load_page(v_refs[j], interleaved))
        av_s[pl.ds(row0, 8), :] = abv[0:8]
        bv_s[pl.ds(row0, 8), :] = abv[8:16]

    @pl.when(s == n_steps - 1)
    def _():
        cpe_k = jnp.sum(wk_ref[...] * pek_ref[...], axis=0, keepdims=True)
        cpe_v = jnp.sum(wv_ref[...] * pev_ref[...], axis=0, keepdims=True)
        rb = min(nch, 128)

        def fin(r, carry):
            r0 = pl.multiple_of(r * rb, rb)
            pk = ak_s[pl.ds(r0, rb), :] + bk_s[pl.ds(r0, rb + 8), :][1:rb + 1]
            pv = av_s[pl.ds(r0, rb), :] + bv_s[pl.ds(r0, rb + 8), :][1:rb + 1]
            for g in range(NSA_G):
                sl = slice(g * HEAD_DIM, (g + 1) * HEAD_DIM)
                yk = _mm_hi(pk[:, sl] + cpe_k, projk_ref[...])
                outk_ref[0, pl.ds(r0, rb), sl] = _rms(yk, gain_ref[...])
                outv_ref[0, pl.ds(r0, rb), sl] = _mm_hi(pv[:, sl] + cpe_v, projv_ref[...])
            return carry

        lax.fori_loop(0, nch // rb, fin, 0)


def _pool_weights(w):
    c = np.arange(8)[:, None]
    t = np.arange(PAGE)[None, :]
    off = t - CMP_STRIDE * c
    inside = (off >= 0) & (off < CMP_STRIDE)
    idx = np.clip(off, 0, CMP_STRIDE - 1)
    wa = jnp.where(inside, w[idx], 0.0)
    wb = jnp.where(inside, w[idx + CMP_STRIDE], 0.0)
    return jnp.concatenate([wa, wb], axis=0).astype(F32)


def _compress(pages_k, pages_v, col_k, col_v, table, pe_k, w_k, proj_k, pe_v, w_v, proj_v, k_gain):
    b, n_pages = table.shape
    pps = 4
    n_steps = n_pages // pps
    nch = 8 * n_pages
    interleaved = pages_k.shape[1] == NSA_G * PAGE
    body = functools.partial(_compress_body, pps=pps, n_steps=n_steps, nch=nch, interleaved=interleaved)

    def page_spec(j, col):
        if interleaved:
            return pl.BlockSpec((1, NSA_G * PAGE, HEAD_DIM), lambda bi, s, pt: (pt[bi, s * pps + j], 0, 0))
        return pl.BlockSpec((1, PAGE, NSA_KV_W), lambda bi, s, pt: (pt[bi, s * pps + j], 0, col))

    full = lambda shape: pl.BlockSpec(shape, lambda bi, s, pt: (0,) * len(shape))
    in_specs = [page_spec(j, col_k) for j in range(pps)] + [page_spec(j, col_v) for j in range(pps)]
    in_specs += [full((16, PAGE)), full((16, PAGE)), full((CMP_LEN, 1)), full((CMP_LEN, HEAD_DIM)),
                 full((CMP_LEN, 1)), full((CMP_LEN, HEAD_DIM)), full((HEAD_DIM, HEAD_DIM)), full((HEAD_DIM, HEAD_DIM)),
                 full((1, HEAD_DIM))]
    out_spec = pl.BlockSpec((1, nch, NSA_KV_W), lambda bi, s, pt: (bi, 0, 0))
    return pl.pallas_call(
        body,
        grid_spec=pltpu.PrefetchScalarGridSpec(
            num_scalar_prefetch=1, grid=(b, n_steps), in_specs=in_specs, out_specs=[out_spec, out_spec],
            scratch_shapes=[pltpu.VMEM((nch, NSA_KV_W), F32), pltpu.VMEM((nch + 8, NSA_KV_W), F32),
                            pltpu.VMEM((nch, NSA_KV_W), F32), pltpu.VMEM((nch + 8, NSA_KV_W), F32)]),
        out_shape=[jax.ShapeDtypeStruct((b, nch, NSA_KV_W), F32)] * 2,
        compiler_params=_cparams(("arbitrary", "arbitrary")),
        name="compress",
    )(table, *([pages_k] * pps), *([pages_v] * pps), _pool_weights(w_k), _pool_weights(w_v),
      w_k.reshape(CMP_LEN, 1), pe_k, w_v.reshape(CMP_LEN, 1), pe_v, proj_k, proj_v, k_gain.reshape(1, HEAD_DIM))


def _prep_body(ks_ref, vs_ref, kw_ref, vw_ref, kg_ref, pks_ref, pkw_ref, ksb_ref, vsb_ref, kwb_ref, vwb_ref):
    for g in range(NSA_G):
        sl = slice(g * HEAD_DIM, (g + 1) * HEAD_DIM)
        ksn = _rms(ks_ref[0, :, sl], kg_ref[1:2, :])
        kwn = _rms(kw_ref[0, :, sl], kg_ref[2:3, :])
        pks_ref[0, :, sl] = ksn
        pkw_ref[0, :, sl] = kwn
        ksb_ref[0, :, sl] = ksn.astype(BF16)
        kwb_ref[0, :, sl] = kwn.astype(BF16)
    vsb_ref[0] = vs_ref[0].astype(BF16)
    vwb_ref[0] = vw_ref[0].astype(BF16)


def _nsa_prep(ub, k_norm_g):
    b, t_len, _ = ub.shape
    tp = min(t_len, 512)
    kv0 = UB_KV // NSA_KV_W
    spec = lambda c: pl.BlockSpec((1, tp, NSA_KV_W), lambda bi, i: (bi, i, c))
    o_spec = pl.BlockSpec((1, tp, NSA_KV_W), lambda bi, i: (bi, i, 0))
    return pl.pallas_call(
        _prep_body,
        grid=(b, t_len // tp),
        in_specs=[spec(kv0 + 2), spec(kv0 + 3), spec(kv0 + 4), spec(kv0 + 5),
                  pl.BlockSpec((3, HEAD_DIM), lambda bi, i: (0, 0))],
        out_specs=[o_spec] * 6,
        out_shape=[jax.ShapeDtypeStruct((b, t_len, NSA_KV_W), F32)] * 2
        + [jax.ShapeDtypeStruct((b, t_len, NSA_KV_W), BF16)] * 4,
        compiler_params=_cparams(("arbitrary", "arbitrary")),
        name="nsa_prep",
    )(ub, ub, ub, ub, k_norm_g)


def _sel_matrix(nc, nb, ncp, nbp):
    j = np.arange(nb)
    lo = np.clip((SEL_BLOCK * j - CMP_LEN) // CMP_STRIDE + 1, 0, nc)
    hi = np.clip(-(-(SEL_BLOCK * (j + 1)) // CMP_STRIDE), 0, nc)
    n = np.arange(ncp)[:, None]
    m = np.zeros((ncp, nbp), np.float32)
    m[:, :nb] = (n >= lo[None, :]) & (n < hi[None, :])
    return jnp.asarray(m)


def _cmp_body(q_ref, kc_ref, vc_ref, bias_ref, msel_ref, qg_ref, ocmp_ref, nsel_ref, *, tq, nb):
    qi = pl.program_id(1)
    lane = lax.broadcasted_iota(jnp.int32, (tq, LANE), 1)
    tpos = qi * tq + lax.broadcasted_iota(jnp.int32, (tq, LANE), 0)
    qblk = tpos >> 6
    scale = HEAD_DIM ** -0.5
    for g in range(NSA_G):
        gsl = slice(g * HEAD_DIM, (g + 1) * HEAD_DIM)
        kc = kc_ref[0, :, gsl]
        vc = vc_ref[0, :, gsl].astype(BF16)
        imp = jnp.zeros((tq, LANE), F32)
        for hh in range(NSA_HPG):
            h = g * NSA_HPG + hh
            hsl = slice(h * HEAD_DIM, (h + 1) * HEAD_DIM)
            qh = _rms(q_ref[0, :, hsl], qg_ref[...]) * scale
            bias = bias_ref[h]
            s = _mm_nt_hi(qh, kc) + bias
            e = jnp.exp(s - jnp.max(s, axis=-1, keepdims=True))
            p = e / jnp.sum(e, axis=-1, keepdims=True) * jnp.where(bias > 0.5 * NEG, 1.0, 0.0)
            ocmp_ref[0, :, hsl] = _mm(p, vc)
            imp = imp + p
        score = _mm_hi(imp, msel_ref[...])
        forced = (lane == 0) | (lane == qblk) | (lane == qblk - 1)
        score = jnp.where(forced, BIG, score)
        score = jnp.where(lane <= qblk, score, -BIG)
        rank = jnp.zeros((tq, LANE), F32)
        for i in range(nb):
            si = score[:, i:i + 1]
            beats = (si > score) | ((si == score) & (lane > i))
            rank = rank + jnp.where(beats, 1.0, 0.0)
        nsel = jnp.where((rank < SEL_TOP) | (lane >= nb), 0.0, 1.0)
        nsel_ref[0, :, gsl] = nsel.astype(BF16)


def _cmp_select_prompt(ub, kcmp, vcmp, bias_cmp, q_norm_g):
    b, t_len, _ = ub.shape
    tq = 128
    nb = t_len // SEL_BLOCK
    ncp = kcmp.shape[1]
    nc = (t_len - CMP_LEN) // CMP_STRIDE + 1
    assert nb <= LANE and ncp == LANE
    body = functools.partial(_cmp_body, tq=tq, nb=nb)
    return pl.pallas_call(
        body,
        grid=(b, t_len // tq),
        in_specs=[
            pl.BlockSpec((1, tq, NSA_Q_W), lambda bi, i: (bi, i, UB_Q // NSA_Q_W)),
            pl.BlockSpec((1, ncp, NSA_KV_W), lambda bi, i: (bi, 0, 0)),
            pl.BlockSpec((1, ncp, NSA_KV_W), lambda bi, i: (bi, 0, 0)),
            pl.BlockSpec((NSA_HEADS, tq, ncp), lambda bi, i: (0, i, 0)),
            pl.BlockSpec((ncp, LANE), lambda bi, i: (0, 0)),
            pl.BlockSpec((1, HEAD_DIM), lambda bi, i: (0, 0)),
        ],
        out_specs=[
            pl.BlockSpec((1, tq, NSA_Q_W), lambda bi, i: (bi, i, 0)),
            pl.BlockSpec((1, tq, NSA_G * LANE), lambda bi, i: (bi, i, 0)),
        ],
        out_shape=[jax.ShapeDtypeStruct((b, t_len, NSA_Q_W), F32),
                   jax.ShapeDtypeStruct((b, t_len, NSA_G * LANE), BF16)],
        compiler_params=_cparams(("arbitrary", "arbitrary")),
        name="cmp_select",
    )(ub, kcmp, vcmp, bias_cmp, _sel_matrix(nc, nb, ncp, LANE), q_norm_g.reshape(1, HEAD_DIM))


def _attn_body(q_ref, ocmp_ref, nsel_ref, ks_ref, vs_ref, kw_ref, vw_ref, epen_ref, bt_ref, gate_ref, z_ref, qg_ref,
               o_ref, m_s, l_s, acc_s, *, tq):
    qi = pl.program_id(1)
    t0 = qi * tq
    rows = NSA_HPG * tq
    tk = 128
    r_tok = t0 + (lax.broadcasted_iota(jnp.int32, (rows, tk), 0) & (tq - 1))
    c_idx = lax.broadcasted_iota(jnp.int32, (rows, tk), 1)
    gates = jax.nn.sigmoid(gate_ref[0])
    scale = HEAD_DIM ** -0.5
    ndi = bt_ref.shape[0] - 1

    for g in range(NSA_G):
        gsl = slice(g * HEAD_DIM, (g + 1) * HEAD_DIM)
        q4 = jnp.concatenate(
            [(_rms(q_ref[0, :, (g * NSA_HPG + hh) * HEAD_DIM:(g * NSA_HPG + hh + 1) * HEAD_DIM], qg_ref[...]) * scale).astype(BF16)
             for hh in range(NSA_HPG)], axis=0)
        ns = nsel_ref[0, :, g * LANE:(g + 1) * LANE]
        qa = jnp.concatenate([q4, jnp.concatenate([ns] * NSA_HPG, axis=0)], axis=1)

        def run(k_ref, v_ref, qmat, lo, hi, window):
            m_s[...] = jnp.full((rows, 1), NEG, F32)
            l_s[...] = jnp.zeros((rows, 1), F32)
            acc_s[...] = jnp.zeros((rows, HEAD_DIM), F32)

            def step(j, carry):
                kb = pl.multiple_of(j * tk, tk)
                kt = k_ref[0, pl.ds(kb, tk), gsl]
                if not window:
                    kt = jnp.concatenate([kt, epen_ref[pl.ds(kb, tk), :]], axis=1)
                s = lax.dot_general(qmat, kt, (((1,), (1,)), ((), ())), preferred_element_type=F32)
                di = jnp.minimum(qi - j, ndi)
                s = s + bt_ref[di, g * rows:(g + 1) * rows, :]
                rel = r_tok - (kb + c_idx)
                ok = (rel >= 0) & (rel < WINDOW) if window else rel >= 0
                s = jnp.where(ok, s, NEG)
                m_old = m_s[...]
                m_new = jnp.maximum(m_old, jnp.max(s, axis=-1, keepdims=True))
                alpha = jnp.exp(m_old - m_new)
                p = jnp.exp(s - m_new)
                l_s[...] = alpha * l_s[...] + jnp.sum(p, axis=-1, keepdims=True)
                acc_s[...] = alpha * acc_s[...] + jnp.dot(p.astype(BF16), v_ref[0, pl.ds(kb, tk), gsl],
                                                          preferred_element_type=F32)
                m_s[...] = m_new
                return carry

            lax.fori_loop(lo, hi, step, 0)
            return acc_s[...] / l_s[...]

        o_sel = run(ks_ref, vs_ref, qa, 0, qi + 1, False)
        o_win = run(kw_ref, vw_ref, q4, jnp.maximum(qi - WINDOW // tk, 0), qi + 1, True)
        for hh in range(NSA_HPG):
            h = g * NSA_HPG + hh
            hsl = slice(h * HEAD_DIM, (h + 1) * HEAD_DIM)
            o = (gates[:, h:h + 1] * ocmp_ref[0, :, hsl]
                 + gates[:, NSA_HEADS + h:NSA_HEADS + h + 1] * o_sel[hh * tq:(hh + 1) * tq]
                 + gates[:, 2 * NSA_HEADS + h:2 * NSA_HEADS + h + 1] * o_win[hh * tq:(hh + 1) * tq])
            o_ref[0, :, hsl] = (o * _silu(z_ref[0, :, hsl])).astype(o_ref.dtype)


def _toeplitz_bias(rel_bias, tq, n_tiles):
    d = jnp.arange(n_tiles, dtype=jnp.int32)[:, None, None]
    r = jnp.arange(tq, dtype=jnp.int32)[None, :, None]
    c = jnp.arange(128, dtype=jnp.int32)[None, None, :]
    bt = _bias_of(rel_bias, d * 128 + r - c)
    return jnp.swapaxes(bt, 0, 1).reshape(n_tiles, NSA_HEADS * tq, 128)


def _block_penalty(t_len):
    key = np.arange(t_len)[:, None]
    j = np.arange(LANE)[None, :]
    return jnp.asarray(np.where(key // SEL_BLOCK == j, NEG, 0.0), BF16)


def _attn_prompt(ub, ocmp, nsel, ksb, vsb, kwb, vwb, rel_bias, q_norm_g):
    b, t_len, _ = ub.shape
    tq = 128
    rows = NSA_HPG * tq
    n_di = (REL_MAX_DIST + tq - 1) // 128 + 1
    bt = _toeplitz_bias(rel_bias, tq, n_di + 1)
    body = functools.partial(_attn_body, tq=tq)
    kv_spec = pl.BlockSpec((1, t_len, NSA_KV_W), lambda bi, i: (bi, 0, 0))
    return pl.pallas_call(
        body,
        grid=(b, t_len // tq),
        in_specs=[
            pl.BlockSpec((1, tq, NSA_Q_W), lambda bi, i: (bi, i, UB_Q // NSA_Q_W)),
            pl.BlockSpec((1, tq, NSA_Q_W), lambda bi, i: (bi, i, 0)),
            pl.BlockSpec((1, tq, NSA_G * LANE), lambda bi, i: (bi, i, 0)),
            kv_spec, kv_spec, kv_spec, kv_spec,
            pl.BlockSpec((t_len, LANE), lambda bi, i: (0, 0)),
            pl.BlockSpec(bt.shape, lambda bi, i: (0, 0, 0)),
            pl.BlockSpec((1, tq, LANE), lambda bi, i: (bi, i, UB_G // LANE)),
            pl.BlockSpec((1, tq, NSA_Q_W), lambda bi, i: (bi, i, UB_Z // NSA_Q_W)),
            pl.BlockSpec((1, HEAD_DIM), lambda bi, i: (0, 0)),
        ],
        out_specs=pl.BlockSpec((1, tq, NSA_Q_W), lambda bi, i: (bi, i, 0)),
        out_shape=jax.ShapeDtypeStruct((b, t_len, NSA_Q_W), BF16),
        scratch_shapes=[pltpu.VMEM((rows, 1), F32), pltpu.VMEM((rows, 1), F32), pltpu.VMEM((rows, HEAD_DIM), F32)],
        compiler_params=_cparams(("arbitrary", "arbitrary")),
        name="nsa_attn",
    )(ub, ocmp, nsel, ksb, vsb, kwb, vwb, _block_penalty(t_len), bt, ub, ub, q_norm_g.reshape(1, HEAD_DIM))


def _q_all(q_ref, qg_ref, ts):
    scale = HEAD_DIM ** -0.5
    zero = jnp.zeros((NSA_HPG * ts, HEAD_DIM), BF16)
    blocks = []
    for g in range(NSA_G):
        q4 = jnp.concatenate(
            [(_rms(q_ref[0, :, (g * NSA_HPG + hh) * HEAD_DIM:(g * NSA_HPG + hh + 1) * HEAD_DIM], qg_ref[...]) * scale).astype(BF16)
             for hh in range(NSA_HPG)], axis=0)
        blocks.append(jnp.concatenate([q4 if gg == g else zero for gg in range(NSA_G)], axis=1))
    return jnp.concatenate(blocks, axis=0)


def _row_to_col(row, n):
    eye = lax.broadcasted_iota(jnp.int32, (n, n), 0) == lax.broadcasted_iota(jnp.int32, (n, n), 1)
    return jnp.sum(jnp.where(eye, row, 0.0), axis=1, keepdims=True)


def _cmp_sample_body(q_ref, kc_ref, vc_ref, bias_ref, msel_ref, rep_ref, qg_ref, ocmp_ref, selt_ref, *, ts, nb, nbp):
    scale = HEAD_DIM ** -0.5
    ncp = kc_ref.shape[1]
    scores = []
    for g in range(NSA_G):
        gsl = slice(g * HEAD_DIM, (g + 1) * HEAD_DIM)
        q4 = jnp.concatenate(
            [_rms(q_ref[0, :, (g * NSA_HPG + hh) * HEAD_DIM:(g * NSA_HPG + hh + 1) * HEAD_DIM], qg_ref[...]) * scale
             for hh in range(NSA_HPG)], axis=0)
        bias = bias_ref[g * NSA_HPG:(g + 1) * NSA_HPG].reshape(NSA_HPG * ts, ncp)
        s = _mm_nt_hi(q4, kc_ref[0, :, gsl]) + bias
        e = jnp.exp(s - jnp.max(s, axis=-1, keepdims=True))
        p = e / jnp.sum(e, axis=-1, keepdims=True) * jnp.where(bias > 0.5 * NEG, 1.0, 0.0)
        o = _mm(p, vc_ref[0, :, gsl])
        imp = p[0:ts]
        for hh in range(NSA_HPG):
            h = g * NSA_HPG + hh
            ocmp_ref[0, :, h * HEAD_DIM:(h + 1) * HEAD_DIM] = o[hh * ts:(hh + 1) * ts]
            if hh:
                imp = imp + p[hh * ts:(hh + 1) * ts]
        scores.append(_mm_hi(imp, msel_ref[...]))
    score = jnp.concatenate(scores, axis=0)
    rows = NSA_G * ts
    lane = lax.broadcasted_iota(jnp.int32, (rows, nbp), 1)
    tok = lax.broadcasted_iota(jnp.int32, (rows, nbp), 0) & (ts - 1)
    qblk = (PAST_LEN + tok) >> 6
    forced = (lane == 0) | (lane == qblk) | (lane == qblk - 1)
    score = jnp.where(forced, BIG, score)
    score = jnp.where(lane <= qblk, score, -BIG)
    sel = jnp.zeros((rows, nbp), F32)
    lane_f = lane.astype(F32)
    for _ in range(SEL_TOP):
        mx = jnp.max(score, axis=-1, keepdims=True)
        first = jnp.min(jnp.where(score == mx, lane_f, float(nbp)), axis=-1, keepdims=True)
        pick = lane_f == first
        sel = jnp.where(pick, 1.0, sel)
        score = jnp.where(pick, -3e38, score)
    selt_ref[0] = _mm_tn(sel, rep_ref[...])


def _cmp_select_sample(ub, kcmp, vcmp, bias_cmp, q_norm_g, nb):
    b, ts, _ = ub.shape
    ncp = kcmp.shape[1]
    nc = ncp - 1
    nbp = -(-nb // LANE) * LANE
    assert ts & (ts - 1) == 0 and NSA_HEADS * ts == LANE
    rep = np.zeros((NSA_G * ts, LANE), np.float32)
    for g in range(NSA_G):
        for hh in range(NSA_HPG):
            for t in range(ts):
                rep[g * ts + t, (g * NSA_HPG + hh) * ts + t] = 1.0
    body = functools.partial(_cmp_sample_body, ts=ts, nb=nb, nbp=nbp)
    return pl.pallas_call(
        body,
        grid=(b,),
        in_specs=[
            pl.BlockSpec((1, ts, NSA_Q_W), lambda bi: (bi, 0, UB_Q // NSA_Q_W)),
            pl.BlockSpec((1, ncp, NSA_KV_W), lambda bi: (bi, 0, 0)),
            pl.BlockSpec((1, ncp, NSA_KV_W), lambda bi: (bi, 0, 0)),
            pl.BlockSpec((NSA_HEADS, ts, ncp), lambda bi: (0, 0, 0)),
            pl.BlockSpec((ncp, nbp), lambda bi: (0, 0)),
            pl.BlockSpec((NSA_G * ts, LANE), lambda bi: (0, 0)),
            pl.BlockSpec((1, HEAD_DIM), lambda bi: (0, 0)),
        ],
        out_specs=[
            pl.BlockSpec((1, ts, NSA_Q_W), lambda bi: (bi, 0, 0)),
            pl.BlockSpec((1, nbp, LANE), lambda bi: (bi, 0, 0)),
        ],
        out_shape=[jax.ShapeDtypeStruct((b, ts, NSA_Q_W), F32), jax.ShapeDtypeStruct((b, nbp, LANE), F32)],
        compiler_params=_cparams(("arbitrary",)),
        name="cmp_select_sample",
    )(ub, kcmp, vcmp, bias_cmp, _sel_matrix(nc, nb, ncp, nbp), jnp.asarray(rep), q_norm_g.reshape(1, HEAD_DIM))


def _sel_sample_body(pt_ref, *refs, pps, n_steps, ts):
    del pt_ref
    k_refs = refs[:pps]
    v_refs = refs[pps:2 * pps]
    (q_ref, selt_ref, knew_ref, vnew_ref, blast_ref, bfar_ref, bnew_ref, qg_ref,
     o_ref, qall_s, sc_s, snew_s, m_s, l_s, acc_s) = refs[2 * pps:]
    s = pl.program_id(1)
    n_pages = n_steps * pps
    sub = lax.broadcasted_iota(jnp.int32, (PAGE, LANE), 0)
    tok = lax.broadcasted_iota(jnp.int32, (ts, LANE), 1) & (ts - 1)
    new_t = lax.broadcasted_iota(jnp.int32, (ts, LANE), 0)

    @pl.when(s == 0)
    def _():
        qall = _q_all(q_ref, qg_ref, ts)
        qall_s[...] = qall
        sn = _mm_nt(knew_ref[0], qall) + bnew_ref[...]
        ok = (new_t <= tok) & (selt_ref[0, 2 * n_pages:2 * n_pages + 1, :] > 0.5)
        sn = jnp.where(ok, sn, NEG)
        snew_s[...] = sn
        m_s[...] = jnp.max(sn, axis=0, keepdims=True)
        l_s[...] = jnp.zeros((1, LANE), F32)
        acc_s[...] = jnp.zeros((LANE, NSA_KV_W), F32)

    @pl.when(s < n_steps)
    def _():
        m = m_s[...]
        for j in range(pps):
            p = s * pps + j
            st = _mm_nt(_load_page(k_refs[j], True), qall_s[...])
            st = st + jnp.where(p == n_pages - 1, blast_ref[...], bfar_ref[...])
            r0 = selt_ref[0, pl.ds(2 * p, 1), :]
            r1 = selt_ref[0, pl.ds(2 * p + 1, 1), :]
            ok = jnp.where(sub < SEL_BLOCK, r0, r1) > 0.5
            st = jnp.where(ok, st, NEG)
            sc_s[pl.ds(pl.multiple_of(p * PAGE, PAGE), PAGE), :] = st
            m = jnp.maximum(m, jnp.max(st, axis=0, keepdims=True))
        m_s[...] = m

    @pl.when(s >= n_steps)
    def _():
        m = m_s[...]
        l = l_s[...]
        acc = acc_s[...]
        for j in range(pps):
            p = (s - n_steps) * pps + j
            pt = jnp.exp(sc_s[pl.ds(pl.multiple_of(p * PAGE, PAGE), PAGE), :] - m)
            l = l + jnp.sum(pt, axis=0, keepdims=True)
            acc = acc + _mm_tn(pt, _load_page(v_refs[j], True))
        l_s[...] = l
        acc_s[...] = acc

    @pl.when(s == 2 * n_steps - 1)
    def _():
        pn = jnp.exp(snew_s[...] - m_s[...])
        l = l_s[...] + jnp.sum(pn, axis=0, keepdims=True)
        acc = acc_s[...] + _mm_tn(pn, vnew_ref[0])
        out = acc / _row_to_col(l, LANE)
        for h in range(NSA_HEADS):
            g = h // NSA_HPG
            o_ref[0, :, h * HEAD_DIM:(h + 1) * HEAD_DIM] = out[h * ts:(h + 1) * ts, g * HEAD_DIM:(g + 1) * HEAD_DIM]


def _lane_bias(rel_bias, key_pos, ts):
    rel = (PAST_LEN + jnp.arange(ts, dtype=jnp.int32))[None, :] - key_pos[:, None]
    return jnp.moveaxis(_bias_of(rel_bias, rel), 0, 1).reshape(key_pos.shape[0], NSA_HEADS * ts)


def _sel_sample(ub, selt, cache_k, cache_v, table, k_new, rel_bias, q_norm_g):
    b, ts, _ = ub.shape
    n_pages = table.shape[1]
    assert PAGE >= REL_MAX_DIST and n_pages * PAGE == PAST_LEN
    pps = 4
    n_steps = n_pages // pps
    ar = jnp.arange
    blast = _lane_bias(rel_bias, PAST_LEN - PAGE + ar(PAGE, dtype=jnp.int32), ts)
    bfar = _lane_bias(rel_bias, jnp.zeros((1,), jnp.int32), ts)
    bnew = _lane_bias(rel_bias, PAST_LEN + ar(ts, dtype=jnp.int32), ts)
    body = functools.partial(_sel_sample_body, pps=pps, n_steps=n_steps, ts=ts)
    last = n_steps - 1
    page_block = (1, NSA_G * PAGE, HEAD_DIM)
    k_spec = lambda j: pl.BlockSpec(page_block, lambda bi, s, pt: (pt[bi, jnp.minimum(s, last) * pps + j], 0, 0))
    v_spec = lambda j: pl.BlockSpec(page_block, lambda bi, s, pt: (pt[bi, jnp.maximum(s - n_steps, 0) * pps + j], 0, 0))
    full = lambda shape: pl.BlockSpec(shape, lambda bi, s, pt: (0,) * len(shape))
    in_specs = [k_spec(j) for j in range(pps)] + [v_spec(j) for j in range(pps)] + [
        pl.BlockSpec((1, ts, NSA_Q_W), lambda bi, s, pt: (bi, 0, UB_Q // NSA_Q_W)),
        pl.BlockSpec((1,) + selt.shape[1:], lambda bi, s, pt: (bi, 0, 0)),
        pl.BlockSpec((1, ts, NSA_KV_W), lambda bi, s, pt: (bi, 0, 0)),
        pl.BlockSpec((1, ts, NSA_KV_W), lambda bi, s, pt: (bi, 0, UB_KV // NSA_KV_W + 3)),
        full((PAGE, LANE)), full((1, LANE)), full((ts, LANE)), full((1, HEAD_DIM)),
    ]
    return pl.pallas_call(
        body,
        grid_spec=pltpu.PrefetchScalarGridSpec(
            num_scalar_prefetch=1, grid=(b, 2 * n_steps), in_specs=in_specs,
            out_specs=pl.BlockSpec((1, ts, NSA_Q_W), lambda bi, s, pt: (bi, 0, 0)),
            scratch_shapes=[pltpu.VMEM((LANE, NSA_KV_W), BF16), pltpu.VMEM((PAST_LEN, LANE), F32),
                            pltpu.VMEM((ts, LANE), F32), pltpu.VMEM((1, LANE), F32), pltpu.VMEM((1, LANE), F32),
                            pltpu.VMEM((LANE, NSA_KV_W), F32)]),
        out_shape=jax.ShapeDtypeStruct((b, ts, NSA_Q_W), F32),
        compiler_params=_cparams(("arbitrary", "arbitrary")),
        name="sel_sample",
    )(table, *([cache_k] * pps), *([cache_v] * pps), ub, selt, k_new, ub, blast, bfar, bnew, q_norm_g.reshape(1, HEAD_DIM))


def _win_sample_body(q_ref, kc_ref, vc_ref, knew_ref, vnew_ref, bc_ref, bn_ref, gate_ref, z_ref, ocmp_ref, osel_ref, qg_ref,
                     o_ref, kout_ref, vout_ref, *, ts, wb):
    qall = _q_all(q_ref, qg_ref, ts)
    key_i = lax.broadcasted_iota(jnp.int32, (wb, LANE), 0)
    tok_c = lax.broadcasted_iota(jnp.int32, (wb, LANE), 1) & (ts - 1)
    tok_n = lax.broadcasted_iota(jnp.int32, (ts, LANE), 1) & (ts - 1)
    new_t = lax.broadcasted_iota(jnp.int32, (ts, LANE), 0)
    sc = _mm_nt(_load_page(kc_ref, True), qall) + bc_ref[...]
    sc = jnp.where(wb + tok_c - key_i < WINDOW, sc, NEG)
    sn = _mm_nt(knew_ref[0], qall) + bn_ref[...]
    sn = jnp.where(new_t <= tok_n, sn, NEG)
    m = jnp.maximum(jnp.max(sc, axis=0, keepdims=True), jnp.max(sn, axis=0, keepdims=True))
    pc = jnp.exp(sc - m)
    pn = jnp.exp(sn - m)
    l = jnp.sum(pc, axis=0, keepdims=True) + jnp.sum(pn, axis=0, keepdims=True)
    out = (_mm_tn(pc, _load_page(vc_ref, True)) + _mm_tn(pn, vnew_ref[0])) / _row_to_col(l, LANE)
    gates = jax.nn.sigmoid(gate_ref[0])
    for h in range(NSA_HEADS):
        g = h // NSA_HPG
        hsl = slice(h * HEAD_DIM, (h + 1) * HEAD_DIM)
        o_win = out[h * ts:(h + 1) * ts, g * HEAD_DIM:(g + 1) * HEAD_DIM]
        o = (gates[:, h:h + 1] * ocmp_ref[0, :, hsl] + gates[:, NSA_HEADS + h:NSA_HEADS + h + 1] * osel_ref[0, :, hsl]
             + gates[:, 2 * NSA_HEADS + h:2 * NSA_HEADS + h + 1] * o_win)
        o_ref[0, :, hsl] = (o * _silu(z_ref[0, :, hsl])).astype(o_ref.dtype)
    keep = (wb - ts) * NSA_G
    for out_ref, old_ref, new_ref in ((kout_ref, kc_ref, knew_ref), (vout_ref, vc_ref, vnew_ref)):
        out_ref[0, 0:keep, :] = old_ref[0, ts * NSA_G:wb * NSA_G, :]
        for g in range(NSA_G):
            out_ref[0, pl.ds(keep + g, ts, stride=NSA_G), :] = new_ref[0, :, g * HEAD_DIM:(g + 1) * HEAD_DIM]


def _win_sample(ub, cache_k, cache_v, k_new, ocmp, osel, rel_bias, q_norm_g):
    b, ts, _ = ub.shape
    wb = cache_k.shape[1] // NSA_G
    assert wb == WINDOW and ts % 8 == 0
    bc = _lane_bias(rel_bias, PAST_LEN - wb + jnp.arange(wb, dtype=jnp.int32), ts)
    bn = _lane_bias(rel_bias, PAST_LEN + jnp.arange(ts, dtype=jnp.int32), ts)
    body = functools.partial(_win_sample_body, ts=ts, wb=wb)
    tok_spec = lambda w, c: pl.BlockSpec((1, ts, w), lambda bi: (bi, 0, c))
    win_spec = pl.BlockSpec((1, wb * NSA_G, HEAD_DIM), lambda bi: (bi, 0, 0))
    return pl.pallas_call(
        body,
        grid=(b,),
        in_specs=[
            tok_spec(NSA_Q_W, UB_Q // NSA_Q_W), win_spec, win_spec,
            tok_spec(NSA_KV_W, 0), tok_spec(NSA_KV_W, UB_KV // NSA_KV_W + 5),
            pl.BlockSpec((wb, LANE), lambda bi: (0, 0)), pl.BlockSpec((ts, LANE), lambda bi: (0, 0)),
            tok_spec(LANE, UB_G // LANE), tok_spec(NSA_Q_W, UB_Z // NSA_Q_W),
            tok_spec(NSA_Q_W, 0), tok_spec(NSA_Q_W, 0),
            pl.BlockSpec((1, HEAD_DIM), lambda bi: (0, 0)),
        ],
        out_specs=[tok_spec(NSA_Q_W, 0), win_spec, win_spec],
        out_shape=[jax.ShapeDtypeStruct((b, ts, NSA_Q_W), BF16),
                   jax.ShapeDtypeStruct((b, wb * NSA_G, HEAD_DIM), F32),
                   jax.ShapeDtypeStruct((b, wb * NSA_G, HEAD_DIM), F32)],
        compiler_params=_cparams(("arbitrary",)),
        name="win_sample",
    )(ub, cache_k, cache_v, k_new, ub, bc, bn, ub, ub, ocmp, osel, q_norm_g.reshape(1, HEAD_DIM))


def _mix_body(oa_ref, ob_ref, wa_ref, wb_ref, ma_ref, mb_ref, o_ref):
    pa = jnp.dot(oa_ref[...], wa_ref[...], preferred_element_type=F32)
    pb = jnp.dot(ob_ref[...], wb_ref[...], preferred_element_type=F32)
    o_ref[...] = (jax.nn.sigmoid(ma_ref[...]) * pa + jax.nn.sigmoid(mb_ref[...]) * pb).astype(o_ref.dtype)


def _out_body(m_ref, w_ref, x_ref, y_ref):
    y_ref[...] = x_ref[...] + jnp.dot(m_ref[...], w_ref[...], preferred_element_type=F32)


def _output(x2d, oa, ob, ub2d, wa16, wb16, wo16):
    m = x2d.shape[0]
    tm = min(m, 512)
    tn = 512
    grid = (m // tm, D_MODEL // tn)
    mixed = pl.pallas_call(
        _mix_body,
        grid=grid,
        in_specs=[
            pl.BlockSpec((tm, GDN_V_W), lambda i, j: (i, 0)),
            pl.BlockSpec((tm, NSA_Q_W), lambda i, j: (i, 0)),
            pl.BlockSpec((GDN_V_W, tn), lambda i, j: (0, j)),
            pl.BlockSpec((NSA_Q_W, tn), lambda i, j: (0, j)),
            pl.BlockSpec((tm, tn), lambda i, j: (i, UB_MA // tn + j)),
            pl.BlockSpec((tm, tn), lambda i, j: (i, UB_MB // tn + j)),
        ],
        out_specs=pl.BlockSpec((tm, tn), lambda i, j: (i, j)),
        out_shape=jax.ShapeDtypeStruct((m, D_MODEL), BF16),
        compiler_params=_cparams(("arbitrary", "arbitrary")),
        name="branch_mix",
    )(oa, ob, wa16, wb16, ub2d, ub2d)
    return pl.pallas_call(
        _out_body,
        grid=grid,
        in_specs=[
            pl.BlockSpec((tm, D_MODEL), lambda i, j: (i, 0)),
            pl.BlockSpec((D_MODEL, tn), lambda i, j: (0, j)),
            pl.BlockSpec((tm, tn), lambda i, j: (i, j)),
        ],
        out_specs=pl.BlockSpec((tm, tn), lambda i, j: (i, j)),
        out_shape=jax.ShapeDtypeStruct((m, D_MODEL), F32),
        compiler_params=_cparams(("arbitrary", "arbitrary")),
        name="out_proj",
    )(mixed, wo16, x2d)


def _kv_slices(ub, rows=None):
    b, t_len, _ = ub.shape
    out = []
    for i in (0, 1, 3, 5):
        a = ub[:, :, UB_KV + i * NSA_KV_W:UB_KV + (i + 1) * NSA_KV_W]
        out.append(a.reshape(b, t_len, NSA_G, HEAD_DIM))
    return out


def kernel(x_prompt, x_sample, cache_k_cmp, cache_v_cmp, cache_k_sel, cache_v_sel, cache_k_win, cache_v_win, state_conv, state_gdn, page_table, norm_g, w_in, gdn_conv_w, gdn_a_log, gdn_dt_bias, gdn_norm_g, q_norm_g, k_norm_g, cmp_pe_k, cmp_w_k, cmp_proj_k, cmp_pe_v, cmp_w_v, cmp_proj_v, rel_bias, w_branch_a, w_branch_b, w_out):
    bp, tp, _ = x_prompt.shape
    bs, ts, _ = x_sample.shape
    wb = cache_k_win.shape[1]
    n_pool = cache_k_cmp.shape[0]
    kv4 = lambda a: a.reshape(a.shape[0], a.shape[1], NSA_G, HEAD_DIM)
    cmp_p = (cmp_pe_k, cmp_w_k, cmp_proj_k, cmp_pe_v, cmp_w_v, cmp_proj_v)

    assert _IN_MB + D_MODEL == w_in.shape[1]
    w_t = w_in.T
    wa16, wb16, wo16 = w_branch_a.astype(BF16), w_branch_b.astype(BF16), w_out.astype(BF16)

    def project(x):
        x2d = x.reshape(-1, D_MODEL)
        ua = _project(x2d, norm_g, w_t, UA_OFFSETS).reshape(x.shape[0], x.shape[1], UA_W)
        ub = _project(x2d, norm_g, w_t, UB_OFFSETS).reshape(x.shape[0], x.shape[1], UB_W)
        return x2d, ua, ub

    x2d, ua, ub = project(x_prompt)
    conv0 = jnp.zeros((bp, GDN_CONV - 1, GDN_QKV_W), F32)
    s0 = jnp.zeros((bp, GDN_V_HEADS, HEAD_DIM, HEAD_DIM), F32)
    o_a, p_gdn = _gdn(ua, ub, conv0, s0, gdn_conv_w, gdn_a_log, gdn_dt_bias, gdn_norm_g)
    p_conv = ua[:, tp - (GDN_CONV - 1):, :GDN_QKV_W]
    n_pages_p = tp // PAGE
    pages = ub.reshape(bp * n_pages_p, PAGE, UB_W)
    ident = jnp.arange(bp * n_pages_p, dtype=jnp.int32).reshape(bp, n_pages_p)
    kcmp, vcmp = _compress(pages, pages, UB_KV // NSA_KV_W, UB_KV // NSA_KV_W + 1, ident, *cmp_p, k_norm_g[0])
    p_ks, p_kw, ksb, vsb, kwb, vwb = _nsa_prep(ub, k_norm_g)
    nc_p = (tp - CMP_LEN) // CMP_STRIDE + 1
    bias_cmp = _cmp_bias(rel_bias, jnp.arange(tp, dtype=jnp.int32), kcmp.shape[1], nc_p)
    ocmp, nsel = _cmp_select_prompt(ub, kcmp, vcmp, bias_cmp, q_norm_g)
    o_b = _attn_prompt(ub, ocmp, nsel, ksb, vsb, kwb, vwb, rel_bias, q_norm_g)
    y_prompt = _output(x2d, o_a.reshape(-1, GDN_V_W), o_b.reshape(-1, NSA_Q_W), ub.reshape(-1, UB_W), wa16, wb16, wo16)
    p_kc, p_vc, p_vs, p_vw = _kv_slices(ub)
    prompt_out = (y_prompt.reshape(x_prompt.shape), p_kc, p_vc, kv4(p_ks), p_vs, kv4(p_kw[:, tp - wb:]), p_vw[:, tp - wb:],
                  p_conv, p_gdn)

    x2d, ua, ub = project(x_sample)
    o_a, s_gdn = _gdn(ua, ub, state_conv, state_gdn, gdn_conv_w, gdn_a_log, gdn_dt_bias, gdn_norm_g)
    s_conv = ua[:, ts - (GDN_CONV - 1):, :GDN_QKV_W]
    pool = lambda c: c.reshape(n_pool, PAGE * NSA_G, HEAD_DIM)
    kcmp, vcmp = _compress(pool(cache_k_cmp), pool(cache_v_cmp), 0, 0, page_table, *cmp_p, k_norm_g[0])
    s_ks, s_kwn, _, _, _, _ = _nsa_prep(ub, k_norm_g)
    total = PAST_LEN + ts
    nc_s = (total - CMP_LEN) // CMP_STRIDE + 1
    nb_s = -(-total // SEL_BLOCK)
    assert nc_s + 1 == kcmp.shape[1]
    pos_s = PAST_LEN + jnp.arange(ts, dtype=jnp.int32)
    bias_cmp = _cmp_bias(rel_bias, pos_s, kcmp.shape[1], nc_s)
    ocmp, selt = _cmp_select_sample(ub, kcmp, vcmp, bias_cmp, q_norm_g, nb_s)
    osel = _sel_sample(ub, selt, pool(cache_k_sel), pool(cache_v_sel), page_table, s_ks, rel_bias, q_norm_g)
    win3 = lambda c: c.reshape(bs, wb * NSA_G, HEAD_DIM)
    o_b, s_kw, s_vw = _win_sample(ub, win3(cache_k_win), win3(cache_v_win), s_kwn, ocmp, osel, rel_bias, q_norm_g)
    y_sample = _output(x2d, o_a.reshape(-1, GDN_V_W), o_b.reshape(-1, NSA_Q_W), ub.reshape(-1, UB_W), wa16, wb16, wo16)
    s_kc, s_vc, s_vs, _ = _kv_slices(ub)
    sample_out = (y_sample.reshape(x_sample.shape), s_kc, s_vc, kv4(s_ks), s_vs, s_kw.reshape(cache_k_win.shape),
                  s_vw.reshape(cache_v_win.shape), s_conv, s_gdn)

    return (prompt_out[0], sample_out[0]) + prompt_out[1:] + sample_out[1:]
```

```python
import functools
import math

import jax
import jax.numpy as jnp
import numpy as np
from jax import lax
from jax.experimental import pallas as pl
from jax.experimental.pallas import tpu as pltpu

F32 = jnp.float32
BF16 = jnp.bfloat16
HI = lax.Precision.HIGHEST

D_MODEL = 2048
PAST_LEN = 16384
PAGE = 128

GDN_QK_HEADS = 16
GDN_V_HEADS = 32
HEAD_DIM = 128
GDN_CONV = 4
GDN_CHUNK = 64
GDN_GROUP = 4
GDN_QK_W = GDN_QK_HEADS * HEAD_DIM
GDN_V_W = GDN_V_HEADS * HEAD_DIM
GDN_QKV_W = 2 * GDN_QK_W + GDN_V_W

NSA_HEADS = 16
NSA_G = 4
NSA_HPG = NSA_HEADS // NSA_G
NSA_Q_W = NSA_HEADS * HEAD_DIM
NSA_KV_W = NSA_G * HEAD_DIM
CMP_LEN = 32
CMP_STRIDE = 16
SEL_BLOCK = 64
SEL_TOP = 16
WINDOW = 512
REL_BUCKETS = 32
REL_MAX_DIST = 128

EPS = 1e-6
NEG = -1e30
BIG = 1e9

LANE = 128
VMEM_LIMIT = 56 * 1024 * 1024

UA_W = GDN_QKV_W + GDN_V_W
UB_Q, UB_Z, UB_MA, UB_MB, UB_KV, UB_BA, UB_G = 0, 2048, 4096, 6144, 8192, 11264, 11776
UB_W = 12288
PROJ_TN = 512

_IN_QB = UA_W + 2 * GDN_V_HEADS
_IN_KV = _IN_QB + NSA_Q_W
_IN_G = _IN_KV + 6 * NSA_KV_W
_IN_ZB = _IN_G + 3 * NSA_HEADS
_IN_MA = _IN_ZB + NSA_Q_W
_IN_MB = _IN_MA + D_MODEL


def _tiles(start, width):
    return [start + k * PROJ_TN for k in range(width // PROJ_TN)]


UA_OFFSETS = _tiles(0, UA_W)
UB_OFFSETS = (_tiles(_IN_QB, NSA_Q_W) + _tiles(_IN_ZB, NSA_Q_W) + _tiles(_IN_MA, D_MODEL) + _tiles(_IN_MB, D_MODEL)
              + _tiles(_IN_KV, 6 * NSA_KV_W) + [UA_W, _IN_G])


def _mm(a, b):
    return jnp.dot(a.astype(BF16), b.astype(BF16), preferred_element_type=F32)


def _mm_nt(a, b):
    return lax.dot_general(a.astype(BF16), b.astype(BF16), (((1,), (1,)), ((), ())), preferred_element_type=F32)


def _mm_tn(a, b):
    return lax.dot_general(a.astype(BF16), b.astype(BF16), (((0,), (0,)), ((), ())), preferred_element_type=F32)


def _mm_hi(a, b):
    return jnp.dot(a, b, precision=HI, preferred_element_type=F32)


def _mm_nt_hi(a, b):
    return lax.dot_general(a, b, (((1,), (1,)), ((), ())), precision=HI, preferred_element_type=F32)


def _silu(x):
    return x * jax.nn.sigmoid(x)


def _softplus(x):
    return jnp.maximum(x, 0.0) + jnp.log1p(jnp.exp(-jnp.abs(x)))


def _rms(x, gain):
    return x * lax.rsqrt(jnp.mean(x * x, axis=-1, keepdims=True) + EPS) * gain


def _cparams(sem):
    return pltpu.CompilerParams(dimension_semantics=sem, vmem_limit_bytes=VMEM_LIMIT)


def _proj_body(offs_ref, x_ref, g_ref, w_ref, o_ref, h_ref):
    del offs_ref

    @pl.when(pl.program_id(1) == 0)
    def _():
        h_ref[...] = _rms(x_ref[...], g_ref[...]).astype(BF16)

    o_ref[...] = lax.dot_general(h_ref[...], w_ref[...].astype(BF16), (((1,), (1,)), ((), ())),
                                 preferred_element_type=F32)


def _project(x2d, norm_g, w_t, offsets):
    m = x2d.shape[0]
    tm = min(m, 1024)
    n_tiles = len(offsets)
    return pl.pallas_call(
        _proj_body,
        grid_spec=pltpu.PrefetchScalarGridSpec(
            num_scalar_prefetch=1,
            grid=(m // tm, n_tiles),
            in_specs=[
                pl.BlockSpec((tm, D_MODEL), lambda i, j, offs: (i, 0)),
                pl.BlockSpec((1, D_MODEL), lambda i, j, offs: (0, 0)),
                pl.BlockSpec((pl.Element(PROJ_TN), pl.Element(D_MODEL)), lambda i, j, offs: (pl.multiple_of(offs[j], 16), 0)),
            ],
            out_specs=pl.BlockSpec((tm, PROJ_TN), lambda i, j, offs: (i, j)),
            scratch_shapes=[pltpu.VMEM((tm, D_MODEL), BF16)]),
        out_shape=jax.ShapeDtypeStruct((m, n_tiles * PROJ_TN), F32),
        compiler_params=_cparams(("arbitrary", "arbitrary")),
        name="proj",
    )(jnp.asarray(offsets, jnp.int32), x2d, norm_g.reshape(1, D_MODEL), w_t)


def _split2(a):
    hi = a.astype(BF16)
    return hi, (a - hi.astype(F32)).astype(BF16)


def _dot16(a, b):
    return jnp.dot(a, b, preferred_element_type=F32)


def _bdot(a, b):
    return lax.dot_general(a, b, (((2,), (1,)), ((0,), (0,))), preferred_element_type=F32)


def _bdot_nt(a, b):
    return lax.dot_general(a, b, (((2,), (2,)), ((0,), (0,))), preferred_element_type=F32)


def _bmm_x3(a2, b2):
    return _bdot(a2[0], b2[0]) + (_bdot(a2[0], b2[1]) + _bdot(a2[1], b2[0]))


def _unit_lower_inverse(nmat, c):
    row = lax.broadcasted_iota(jnp.int32, (1, c, c), 1)
    col = lax.broadcasted_iota(jnp.int32, (1, c, c), 2)
    p = jnp.where(row == col, 1.0, 0.0) + nmat
    m2 = _split2(nmat)
    span = 2
    while span < c:
        m2 = _split2(_bmm_x3(m2, m2))
        p = p + _bmm_x3(_split2(p), m2)
        span *= 2
    return p


def _cumsum_rows(tril16, g):
    h = g.astype(BF16)
    r = g - h.astype(F32)
    m = r.astype(BF16)
    l = (r - m.astype(F32)).astype(BF16)
    return _dot16(tril16, h) + (_dot16(tril16, m) + _dot16(tril16, l))


def _gdn_body(q_ref, k_ref, v_ref, z_ref, ba_ref, cwq_ref, cwk_ref, cwv_ref, csq_ref, csk_ref, csv_ref,
              s0_ref, gp_ref, ng_ref, o_ref, sfin_ref, xq_s, xk_s, xv_s, u_s, w_s, qg_s, kd_s, at_s, gl_s, st_s,
              *, t_len, c):
    head = pl.program_id(1)
    n_chunks = t_len // c
    for xs, cs, src in ((xq_s, csq_ref, q_ref), (xk_s, csk_ref, k_ref), (xv_s, csv_ref, v_ref)):
        xs[0:8, :] = jnp.zeros((8, xs.shape[1]), F32)
        xs[5:8, :] = cs[0]
        xs[8:8 + t_len, :] = src[0]
    st_s[...] = s0_ref[0]

    gsz = min(GDN_GROUP, n_chunks)
    rg = gsz * c
    row = lax.broadcasted_iota(jnp.int32, (1, c, c), 1)
    col = lax.broadcasted_iota(jnp.int32, (1, c, c), 2)
    tril = row >= col
    strict = row > col
    eye = row == col
    lane = lax.broadcasted_iota(jnp.int32, (rg, LANE), 1)
    rr = lax.broadcasted_iota(jnp.int32, (rg, rg), 0)
    cc = lax.broadcasted_iota(jnp.int32, (rg, rg), 1)
    shift = c.bit_length() - 1
    assert 1 << shift == c
    tril16 = jnp.where((rr >= cc) & ((rr >> shift) == (cc >> shift)), 1.0, 0.0).astype(BF16)

    def conv(xs, cw_ref, base):
        win = xs[pl.ds(base, rg + 8), :]
        w = cw_ref[...]
        a = win[5:5 + rg] * w[0:1] + win[6:6 + rg] * w[1:2] + win[7:7 + rg] * w[2:3] + win[8:8 + rg] * w[3:4]
        return _silu(a)

    def pick(x, idx):
        return jnp.sum(jnp.where(lane == idx, x, 0.0), axis=-1, keepdims=True).reshape(gsz, c, 1)

    def prepare(gi):
        base = gi * rg if isinstance(gi, int) else pl.multiple_of(gi * rg, rg)
        rows = pl.ds(base, rg)
        qa = conv(xq_s, cwq_ref, base)
        ka = conv(xk_s, cwk_ref, base)
        va = conv(xv_s, cwv_ref, base)
        qn = (qa * lax.rsqrt(jnp.sum(qa * qa, axis=-1, keepdims=True) + EPS) * (HEAD_DIM ** -0.5)).reshape(gsz, c, HEAD_DIM)
        kn = (ka * lax.rsqrt(jnp.sum(ka * ka, axis=-1, keepdims=True) + EPS)).reshape(gsz, c, HEAD_DIM)
        ba = ba_ref[0, rows, :]
        beta_all = jax.nn.sigmoid(ba)
        g_all = -jnp.exp(gp_ref[0:1, :]) * _softplus(ba + gp_ref[1:2, :])
        gc_all = _cumsum_rows(tril16, g_all)
        k16 = kn.astype(BF16)
        kk = _bdot_nt(k16, k16)
        qk = _bdot_nt(qn.astype(BF16), k16)
        nmats, rhss, gammas = [], [], []
        for hh in range(2):
            hv = 2 * head + hh
            beta = pick(beta_all, hv)
            gcol = pick(gc_all, GDN_V_HEADS + hv)
            grow = jnp.sum(jnp.where(eye, gcol, 0.0), axis=1, keepdims=True)
            gamma = jnp.where(tril, jnp.exp(jnp.minimum(gcol - grow, 0.0)), 0.0)
            nmats.append(jnp.where(strict, -(kk * beta * gamma), 0.0))
            egc = jnp.exp(gcol)
            glast = gcol[:, c - 1:c, :]
            v_h = va[:, hh * HEAD_DIM:(hh + 1) * HEAD_DIM].reshape(gsz, c, HEAD_DIM)
            rhss.append(jnp.concatenate([v_h * beta, kn * (beta * egc)], axis=2))
            gammas.append(gamma)
            qg_s[hh, rows, :] = (qn * egc).reshape(rg, HEAD_DIM)
            kd_s[hh, rows, :] = (kn * jnp.exp(glast - gcol)).reshape(rg, HEAD_DIM)
            egl = jnp.exp(glast)
            for j in range(gsz):
                gl_s[hh, gi * gsz + j] = jnp.broadcast_to(egl[j], (8, HEAD_DIM))
        tinv = _unit_lower_inverse(jnp.concatenate(nmats, axis=0), c)
        uw = _bmm_x3(_split2(tinv), _split2(jnp.concatenate(rhss, axis=0)))
        for hh in range(2):
            uw_h = uw[hh * gsz:(hh + 1) * gsz]
            u_s[hh, rows, :] = uw_h[:, :, :HEAD_DIM].reshape(rg, HEAD_DIM)
            w_s[hh, rows, :] = uw_h[:, :, HEAD_DIM:].reshape(rg, HEAD_DIM)
        for hh in range(2):
            at_s[hh, rows, :] = (qk * gammas[hh]).reshape(rg, c)

    def recur(ci):
        rows = pl.ds(ci * c if isinstance(ci, int) else pl.multiple_of(ci * c, c), c)
        s = st_s[...]
        s16 = s.astype(BF16)
        attn = at_s[:, rows, :].astype(BF16)
        v_new = u_s[:, rows, :] - _bdot(w_s[:, rows, :].astype(BF16), s16)
        v16 = v_new.astype(BF16)
        o = _bdot(qg_s[:, rows, :].astype(BF16), s16) + _bdot(attn, v16)
        kd16 = kd_s[:, rows, :].astype(BF16)
        upd = jnp.stack([_mm_tn(kd16[hh], v16[hh]) for hh in range(2)], axis=0)
        st_s[...] = s * gl_s[:, ci][:, 0:1, :] + upd
        for hh in range(2):
            hsl = slice(hh * HEAD_DIM, (hh + 1) * HEAD_DIM)
            o_ref[0, rows, hsl] = (_rms(o[hh], ng_ref[...]) * _silu(z_ref[0, rows, hsl])).astype(o_ref.dtype)

    if n_chunks == 1:
        prepare(0)
        recur(0)
    else:
        def prepare_step(gi, carry):
            prepare(gi)
            return carry

        def recur_step(ci, carry):
            recur(ci)
            return carry

        lax.fori_loop(0, n_chunks // gsz, prepare_step, 0)
        lax.fori_loop(0, n_chunks, recur_step, 0)
    sfin_ref[0] = st_s[...]


def _gdn(ua, ub, conv_state, s0, conv_w, a_log, dt_bias, norm_g):
    b, t_len, _ = ua.shape
    c = min(GDN_CHUNK, t_len)
    hd = HEAD_DIM
    nqk = GDN_QK_HEADS
    gp = jnp.zeros((2, LANE), F32)
    gp = gp.at[0, GDN_V_HEADS:2 * GDN_V_HEADS].set(a_log).at[1, GDN_V_HEADS:2 * GDN_V_HEADS].set(dt_bias)
    body = functools.partial(_gdn_body, t_len=t_len, c=c)
    return pl.pallas_call(
        body,
        grid=(b, nqk),
        in_specs=[
            pl.BlockSpec((1, t_len, hd), lambda bi, i: (bi, 0, i)),
            pl.BlockSpec((1, t_len, hd), lambda bi, i: (bi, 0, nqk + i)),
            pl.BlockSpec((1, t_len, 2 * hd), lambda bi, i: (bi, 0, nqk + i)),
            pl.BlockSpec((1, t_len, 2 * hd), lambda bi, i: (bi, 0, 2 * nqk + i)),
            pl.BlockSpec((1, t_len, LANE), lambda bi, i: (bi, 0, UB_BA // LANE)),
            pl.BlockSpec((GDN_CONV, hd), lambda bi, i: (0, i)),
            pl.BlockSpec((GDN_CONV, hd), lambda bi, i: (0, nqk + i)),
            pl.BlockSpec((GDN_CONV, 2 * hd), lambda bi, i: (0, nqk + i)),
            pl.BlockSpec((1, GDN_CONV - 1, hd), lambda bi, i: (bi, 0, i)),
            pl.BlockSpec((1, GDN_CONV - 1, hd), lambda bi, i: (bi, 0, nqk + i)),
            pl.BlockSpec((1, GDN_CONV - 1, 2 * hd), lambda bi, i: (bi, 0, nqk + i)),
            pl.BlockSpec((1, 2, hd, hd), lambda bi, i: (bi, i, 0, 0)),
            pl.BlockSpec((2, LANE), lambda bi, i: (0, 0)),
            pl.BlockSpec((1, hd), lambda bi, i: (0, 0)),
        ],
        out_specs=[
            pl.BlockSpec((1, t_len, 2 * hd), lambda bi, i: (bi, 0, i)),
            pl.BlockSpec((1, 2, hd, hd), lambda bi, i: (bi, i, 0, 0)),
        ],
        out_shape=[
            jax.ShapeDtypeStruct((b, t_len, GDN_V_W), BF16),
            jax.ShapeDtypeStruct((b, GDN_V_HEADS, hd, hd), F32),
        ],
        scratch_shapes=[
            pltpu.VMEM((t_len + 8, hd), F32),
            pltpu.VMEM((t_len + 8, hd), F32),
            pltpu.VMEM((t_len + 8, 2 * hd), F32),
            pltpu.VMEM((2, t_len, hd), F32),
            pltpu.VMEM((2, t_len, hd), F32),
            pltpu.VMEM((2, t_len, hd), F32),
            pltpu.VMEM((2, t_len, hd), F32),
            pltpu.VMEM((2, t_len, c), F32),
            pltpu.VMEM((2, t_len // c, 8, hd), F32),
            pltpu.VMEM((2, hd, hd), F32),
        ],
        compiler_params=_cparams(("arbitrary", "arbitrary")),
        name="gdn",
    )(ua, ua, ua, ua, ub, conv_w, conv_w, conv_w, conv_state, conv_state, conv_state, s0, gp, norm_g.reshape(1, hd))


def _t5_bucket(rel):
    n = jnp.maximum(rel, 0)
    exact = REL_BUCKETS // 2
    nf = jnp.maximum(n, 1).astype(F32)
    large = exact + (jnp.log(nf / exact) / math.log(REL_MAX_DIST / exact) * (REL_BUCKETS - exact)).astype(jnp.int32)
    return jnp.where(n < exact, n, jnp.minimum(large, REL_BUCKETS - 1))


def _bias_of(rel_bias, rel):
    bucket = _t5_bucket(rel)[None]
    table = rel_bias.astype(F32)
    out = jnp.zeros((NSA_HEADS,) + rel.shape, F32)
    for k in range(REL_BUCKETS):
        out = jnp.where(bucket == k, table[k].reshape((NSA_HEADS,) + (1,) * rel.ndim), out)
    return out


def _cmp_bias(rel_bias, q_pos, ncp, nc):
    n = jnp.arange(ncp, dtype=jnp.int32)
    rel = q_pos[:, None] - (n * CMP_STRIDE + (CMP_LEN - 1))[None, :]
    ok = (rel >= 0) & (n < nc)[None, :]
    return jnp.where(ok[None], _bias_of(rel_bias, rel), NEG)


def _load_page(ref, interleaved):
    if not interleaved:
        return ref[0]
    n_rows = ref.shape[1] // NSA_G
    return jnp.concatenate([ref[0, pl.ds(g, n_rows, stride=NSA_G), :] for g in range(NSA_G)], axis=1)


def _compress_body(pt_ref, *refs, pps, n_steps, nch, interleaved):
    del pt_ref
    k_refs = refs[:pps]
    v_refs = refs[pps:2 * pps]
    (wabk_ref, wabv_ref, wk_ref, pek_ref, wv_ref, pev_ref, projk_ref, projv_ref, gain_ref,
     outk_ref, outv_ref, ak_s, bk_s, av_s, bv_s) = refs[2 * pps:]
    s = pl.program_id(1)

    @pl.when(s == 0)
    def _():
        bk_s[nch:nch + 8, :] = jnp.zeros((8, NSA_KV_W), F32)
        bv_s[nch:nch + 8, :] = jnp.zeros((8, NSA_KV_W), F32)

    for j in range(pps):
        row0 = pl.multiple_of((s * pps + j) * 8, 8)
        abk = _mm_hi(wabk_ref[...], _load_page(k_refs[j], interleaved))
        ak_s[pl.ds(row0, 8), :] = abk[0:8]
        bk_s[pl.ds(row0, 8), :] = abk[8:16]
        abv = _mm_hi(wabv_ref[...], _load_page(v_refs[j], interleaved))
        av_s[pl.ds(row0, 8), :] = abv[0:8]
        bv_s[pl.ds(row0, 8), :] = abv[8:16]

    @pl.when(s == n_steps - 1)
    def _():
        cpe_k = jnp.sum(wk_ref[...] * pek_ref[...], axis=0, keepdims=True)
        cpe_v = jnp.sum(wv_ref[...] * pev_ref[...], axis=0, keepdims=True)
        rb = min(nch, 128)

        def fin(r, carry):
            r0 = pl.multiple_of(r * rb, rb)
            pk = ak_s[pl.ds(r0, rb), :] + bk_s[pl.ds(r0, rb + 8), :][1:rb + 1]
            pv = av_s[pl.ds(r0, rb), :] + bv_s[pl.ds(r0, rb + 8), :][1:rb + 1]
            for g in range(NSA_G):
                sl = slice(g * HEAD_DIM, (g + 1) * HEAD_DIM)
                yk = _mm_hi(pk[:, sl] + cpe_k, projk_ref[...])
                outk_ref[0, pl.ds(r0, rb), sl] = _rms(yk, gain_ref[...])
                outv_ref[0, pl.ds(r0, rb), sl] = _mm_hi(pv[:, sl] + cpe_v, projv_ref[...])
            return carry

        lax.fori_loop(0, nch // rb, fin, 0)


def _pool_weights(w):
    c = np.arange(8)[:, None]
    t = np.arange(PAGE)[None, :]
    off = t - CMP_STRIDE * c
    inside = (off >= 0) & (off < CMP_STRIDE)
    idx = np.clip(off, 0, CMP_STRIDE - 1)
    wa = jnp.where(inside, w[idx], 0.0)
    wb = jnp.where(inside, w[idx + CMP_STRIDE], 0.0)
    return jnp.concatenate([wa, wb], axis=0).astype(F32)


def _compress(pages_k, pages_v, col_k, col_v, table, pe_k, w_k, proj_k, pe_v, w_v, proj_v, k_gain):
    b, n_pages = table.shape
    pps = 4
    n_steps = n_pages // pps
    nch = 8 * n_pages
    interleaved = pages_k.shape[1] == NSA_G * PAGE
    body = functools.partial(_compress_body, pps=pps, n_steps=n_steps, nch=nch, interleaved=interleaved)

    def page_spec(j, col):
        if interleaved:
            return pl.BlockSpec((1, NSA_G * PAGE, HEAD_DIM), lambda bi, s, pt: (pt[bi, s * pps + j], 0, 0))
        return pl.BlockSpec((1, PAGE, NSA_KV_W), lambda bi, s, pt: (pt[bi, s * pps + j], 0, col))

    full = lambda shape: pl.BlockSpec(shape, lambda bi, s, pt: (0,) * len(shape))
    in_specs = [page_spec(j, col_k) for j in range(pps)] + [page_spec(j, col_v) for j in range(pps)]
    in_specs += [full((16, PAGE)), full((16, PAGE)), full((CMP_LEN, 1)), full((CMP_LEN, HEAD_DIM)),
                 full((CMP_LEN, 1)), full((CMP_LEN, HEAD_DIM)), full((HEAD_DIM, HEAD_DIM)), full((HEAD_DIM, HEAD_DIM)),
                 full((1, HEAD_DIM))]
    out_spec = pl.BlockSpec((1, nch, NSA_KV_W), lambda bi, s, pt: (bi, 0, 0))
    return pl.pallas_call(
        body,
        grid_spec=pltpu.PrefetchScalarGridSpec(
            num_scalar_prefetch=1, grid=(b, n_steps), in_specs=in_specs, out_specs=[out_spec, out_spec],
            scratch_shapes=[pltpu.VMEM((nch, NSA_KV_W), F32), pltpu.VMEM((nch + 8, NSA_KV_W), F32),
                            pltpu.VMEM((nch, NSA_KV_W), F32), pltpu.VMEM((nch + 8, NSA_KV_W), F32)]),
        out_shape=[jax.ShapeDtypeStruct((b, nch, NSA_KV_W), F32)] * 2,
        compiler_params=_cparams(("arbitrary", "arbitrary")),
        name="compress",
    )(table, *([pages_k] * pps), *([pages_v] * pps), _pool_weights(w_k), _pool_weights(w_v),
      w_k.reshape(CMP_LEN, 1), pe_k, w_v.reshape(CMP_LEN, 1), pe_v, proj_k, proj_v, k_gain.reshape(1, HEAD_DIM))


def _prep_body(ks_ref, vs_ref, kw_ref, vw_ref, kg_ref, pks_ref, pkw_ref, *attn_refs):
    for g in range(NSA_G):
        sl = slice(g * HEAD_DIM, (g + 1) * HEAD_DIM)
        ksn = _rms(ks_ref[0, :, sl], kg_ref[1:2, :])
        kwn = _rms(kw_ref[0, :, sl], kg_ref[2:3, :])
        pks_ref[0, :, sl] = ksn
        pkw_ref[0, :, sl] = kwn
        if attn_refs:
            attn_refs[0][0, :, sl] = ksn.astype(BF16)
            attn_refs[2][0, :, sl] = kwn.astype(BF16)
    if attn_refs:
        for src, dst in ((vs_ref, attn_refs[1]), (vw_ref, attn_refs[3])):
            for j in range(dst.shape[1]):
                dst[0, j] = src[0, j * LANE:(j + 1) * LANE, :].T.astype(BF16)


def _nsa_prep(ub, k_norm_g, attn_operands):
    b, t_len, _ = ub.shape
    tp = min(t_len, 512)
    kv0 = UB_KV // NSA_KV_W
    spec = lambda c: pl.BlockSpec((1, tp, NSA_KV_W), lambda bi, i: (bi, i, c))
    o_spec = pl.BlockSpec((1, tp, NSA_KV_W), lambda bi, i: (bi, i, 0))
    out_specs = [o_spec] * 2
    out_shape = [jax.ShapeDtypeStruct((b, t_len, NSA_KV_W), F32)] * 2
    if attn_operands:
        t_spec = pl.BlockSpec((1, tp // LANE, NSA_KV_W, LANE), lambda bi, i: (bi, i, 0, 0))
        k16 = jax.ShapeDtypeStruct((b, t_len, NSA_KV_W), BF16)
        v16 = jax.ShapeDtypeStruct((b, t_len // LANE, NSA_KV_W, LANE), BF16)
        out_specs += [o_spec, t_spec, o_spec, t_spec]
        out_shape += [k16, v16, k16, v16]
    return pl.pallas_call(
        _prep_body,
        grid=(b, t_len // tp),
        in_specs=[spec(kv0 + 2), spec(kv0 + 3), spec(kv0 + 4), spec(kv0 + 5),
                  pl.BlockSpec((3, HEAD_DIM), lambda bi, i: (0, 0))],
        out_specs=out_specs,
        out_shape=out_shape,
        compiler_params=_cparams(("arbitrary", "arbitrary")),
        name="nsa_prep",
    )(ub, ub, ub, ub, k_norm_g)


def _sel_matrix(nc, nb, ncp, nbp):
    j = np.arange(nb)
    lo = np.clip((SEL_BLOCK * j - CMP_LEN) // CMP_STRIDE + 1, 0, nc)
    hi = np.clip(-(-(SEL_BLOCK * (j + 1)) // CMP_STRIDE), 0, nc)
    n = np.arange(ncp)[:, None]
    m = np.zeros((ncp, nbp), np.float32)
    m[:, :nb] = (n >= lo[None, :]) & (n < hi[None, :])
    return jnp.asarray(m)


def _cmp_body(q_ref, kc_ref, vc_ref, bias_ref, msel_ref, qg_ref, gate_ref, ocmp_ref, nsel_ref, *, tq, nb):
    qi = pl.program_id(1)
    gates = jax.nn.sigmoid(gate_ref[0])
    lane = lax.broadcasted_iota(jnp.int32, (tq, LANE), 1)
    tpos = qi * tq + lax.broadcasted_iota(jnp.int32, (tq, LANE), 0)
    qblk = tpos >> 6
    scale = HEAD_DIM ** -0.5
    for g in range(NSA_G):
        gsl = slice(g * HEAD_DIM, (g + 1) * HEAD_DIM)
        kc = kc_ref[0, :, gsl]
        vc = vc_ref[0, :, gsl].astype(BF16)
        imp = jnp.zeros((tq, LANE), F32)
        for hh in range(NSA_HPG):
            h = g * NSA_HPG + hh
            hsl = slice(h * HEAD_DIM, (h + 1) * HEAD_DIM)
            qh = _rms(q_ref[0, :, hsl], qg_ref[...]) * scale
            bias = bias_ref[h]
            s = _mm_nt_hi(qh, kc) + bias
            e = jnp.exp(s - jnp.max(s, axis=-1, keepdims=True))
            p = e / jnp.sum(e, axis=-1, keepdims=True) * jnp.where(bias > 0.5 * NEG, 1.0, 0.0)
            ocmp_ref[0, :, hsl] = gates[:, h:h + 1] * _mm(p, vc)
            imp = imp + p
        score = _mm_hi(imp, msel_ref[...])
        forced = (lane == 0) | (lane == qblk) | (lane == qblk - 1)
        score = jnp.where(forced, BIG, score)
        score = jnp.where(lane <= qblk, score, -BIG)
        rank = jnp.zeros((tq, LANE), F32)
        for i in range(nb):
            si = score[:, i:i + 1]
            beats = (si > score) | ((si == score) & (lane > i))
            rank = rank + jnp.where(beats, 1.0, 0.0)
        nsel = jnp.where((rank < SEL_TOP) | (lane >= nb), 0.0, 1.0)
        nsel_ref[0, :, gsl] = nsel.astype(BF16)


def _cmp_select_prompt(ub, kcmp, vcmp, bias_cmp, q_norm_g):
    b, t_len, _ = ub.shape
    tq = 128
    nb = t_len // SEL_BLOCK
    ncp = kcmp.shape[1]
    nc = (t_len - CMP_LEN) // CMP_STRIDE + 1
    assert nb <= LANE and ncp == LANE
    body = functools.partial(_cmp_body, tq=tq, nb=nb)
    return pl.pallas_call(
        body,
        grid=(b, t_len // tq),
        in_specs=[
            pl.BlockSpec((1, tq, NSA_Q_W), lambda bi, i: (bi, i, UB_Q // NSA_Q_W)),
            pl.BlockSpec((1, ncp, NSA_KV_W), lambda bi, i: (bi, 0, 0)),
            pl.BlockSpec((1, ncp, NSA_KV_W), lambda bi, i: (bi, 0, 0)),
            pl.BlockSpec((NSA_HEADS, tq, ncp), lambda bi, i: (0, i, 0)),
            pl.BlockSpec((ncp, LANE), lambda bi, i: (0, 0)),
            pl.BlockSpec((1, HEAD_DIM), lambda bi, i: (0, 0)),
            pl.BlockSpec((1, tq, LANE), lambda bi, i: (bi, i, UB_G // LANE)),
        ],
        out_specs=[
            pl.BlockSpec((1, tq, NSA_Q_W), lambda bi, i: (bi, i, 0)),
            pl.BlockSpec((1, tq, NSA_G * LANE), lambda bi, i: (bi, i, 0)),
        ],
        out_shape=[jax.ShapeDtypeStruct((b, t_len, NSA_Q_W), F32),
                   jax.ShapeDtypeStruct((b, t_len, NSA_G * LANE), BF16)],
        compiler_params=_cparams(("arbitrary", "arbitrary")),
        name="cmp_select",
    )(ub, kcmp, vcmp, bias_cmp, _sel_matrix(nc, nb, ncp, LANE), q_norm_g.reshape(1, HEAD_DIM), ub)


def _attn_body(q_ref, ocmp_ref, nsel_ref, ks_ref, vst_ref, kw_ref, vwt_ref, epen_ref, bt_ref, cb_ref, gate_ref, z_ref, qg_ref,
               o_ref, m_s, l_s, acc_s, osw_s, *, tq):
    qi = pl.program_id(1)
    t0 = qi * tq
    rows = NSA_HPG * tq
    tk = LANE
    tok = t0 + (lax.broadcasted_iota(jnp.int32, (tk, rows), 1) & (tq - 1))
    key = lax.broadcasted_iota(jnp.int32, (tk, rows), 0)
    gates_t = jax.nn.sigmoid(gate_ref[0]).T
    scale = HEAD_DIM ** -0.5
    j_near = jnp.maximum(qi + 1 - bt_ref.shape[0], 0)

    groups = range(NSA_G)
    gsl = [slice(g * HEAD_DIM, (g + 1) * HEAD_DIM) for g in groups]
    heads = [[g * NSA_HPG + hh for hh in range(NSA_HPG)] for g in groups]
    q_sel, q_win = [], []
    for g in groups:
        q4 = jnp.concatenate(
            [(_rms(q_ref[0, :, h * HEAD_DIM:(h + 1) * HEAD_DIM], qg_ref[...]) * scale).astype(BF16) for h in heads[g]], axis=0)
        ns = nsel_ref[0, :, g * LANE:(g + 1) * LANE].astype(F32)
        aug_sel = jnp.concatenate([ns + cb_ref[h:h + 1, :] for h in heads[g]], axis=0).astype(BF16)
        aug_win = jnp.concatenate([jnp.broadcast_to(cb_ref[h:h + 1, :], (tq, LANE)) for h in heads[g]], axis=0).astype(BF16)
        q_sel.append(jnp.concatenate([q4, aug_sel], axis=1))
        q_win.append(jnp.concatenate([q4, aug_win], axis=1))

    def run(k_ref, vt_ref, qmats, lo, window):
        m_s[...] = jnp.full(m_s.shape, NEG, F32)
        l_s[...] = jnp.zeros(l_s.shape, F32)
        acc_s[...] = jnp.zeros(acc_s.shape, F32)

        def step(j, near):
            kb = pl.multiple_of(j * tk, tk)
            kt = k_ref[0, pl.ds(kb, tk), :]
            ep = epen_ref[pl.ds(kb, tk), :]
            ss = [lax.dot_general(jnp.concatenate([kt[:, gsl[g]], ep], axis=1), qmats[g], (((1,), (1,)), ((), ())),
                                  preferred_element_type=F32) for g in groups]
            rel = tok - (kb + key)
            vt = vt_ref[0, j]
            for g in groups:
                s = ss[g]
                if near:
                    s = jnp.where(rel >= 0, s + bt_ref[qi - j, :, g * rows:(g + 1) * rows], NEG)
                elif window:
                    s = jnp.where(rel < WINDOW, s, NEG)
                m_old = m_s[g]
                m_new = jnp.maximum(m_old, jnp.max(s, axis=0, keepdims=True))
                alpha = jnp.exp(m_old - m_new)
                p = jnp.exp(s - m_new)
                l_s[g] = alpha * l_s[g] + jnp.sum(p, axis=0, keepdims=True)
                acc_s[g] = alpha * acc_s[g] + jnp.dot(vt[gsl[g], :], p.astype(BF16), preferred_element_type=F32)
                m_s[g] = m_new

        def far_step(j, carry):
            step(j, False)
            return carry

        def near_step(j, carry):
            step(j, True)
            return carry

        lax.fori_loop(lo, j_near, far_step, 0)
        lax.fori_loop(j_near, qi + 1, near_step, 0)

    def gate_rows(branch, g):
        return jnp.concatenate([gates_t[branch * NSA_HEADS + h:branch * NSA_HEADS + h + 1, :] for h in heads[g]], axis=1)

    run(ks_ref, vst_ref, q_sel, 0, False)
    for g in groups:
        osw_s[g] = acc_s[g] * (gate_rows(1, g) / l_s[g])
    run(kw_ref, vwt_ref, q_win, jnp.maximum(qi - WINDOW // tk, 0), True)
    for g in groups:
        osw = osw_s[g] + acc_s[g] * (gate_rows(2, g) / l_s[g])
        for hh, h in enumerate(heads[g]):
            hsl = slice(h * HEAD_DIM, (h + 1) * HEAD_DIM)
            o = ocmp_ref[0, :, hsl] + osw[:, hh * tq:(hh + 1) * tq].T
            o_ref[0, :, hsl] = (o * _silu(z_ref[0, :, hsl])).astype(o_ref.dtype)


def _far_bias(rel_bias):
    return _bias_of(rel_bias, jnp.full((1,), REL_MAX_DIST, jnp.int32))[:, 0]


def _toeplitz_bias_t(rel_bias, tq, n_near):
    d = jnp.arange(n_near, dtype=jnp.int32)[:, None, None]
    c = jnp.arange(LANE, dtype=jnp.int32)[None, :, None]
    r = jnp.arange(tq, dtype=jnp.int32)[None, None, :]
    bt = _bias_of(rel_bias, d * LANE + r - c) - _far_bias(rel_bias)[:, None, None, None]
    return jnp.transpose(bt, (1, 2, 0, 3)).reshape(n_near, LANE, NSA_HEADS * tq)


AUG_BIAS_LANES = (LANE - 2, LANE - 1)


def _far_bias_columns(rel_bias):
    c = _far_bias(rel_bias)
    hi = c.astype(BF16).astype(F32)
    lo = (c - hi).astype(BF16).astype(F32)
    return jnp.zeros((NSA_HEADS, LANE), F32).at[:, AUG_BIAS_LANES[0]].set(hi).at[:, AUG_BIAS_LANES[1]].set(lo)


def _block_penalty(t_len):
    key = np.arange(t_len)[:, None]
    j = np.arange(LANE)[None, :]
    pen = np.where(key // SEL_BLOCK == j, NEG, 0.0)
    pen[:, AUG_BIAS_LANES[0]:] = 1.0
    return jnp.asarray(pen, BF16)


def _attn_prompt(ub, ocmp, nsel, ksb, vst, kwb, vwt, rel_bias, q_norm_g):
    b, t_len, _ = ub.shape
    tq = 128
    rows = NSA_HPG * tq
    assert t_len // SEL_BLOCK <= AUG_BIAS_LANES[0]
    n_near = -(-(REL_MAX_DIST + LANE - 1) // LANE)
    bt = _toeplitz_bias_t(rel_bias, tq, n_near)
    body = functools.partial(_attn_body, tq=tq)
    k_spec = pl.BlockSpec((1, t_len, NSA_KV_W), lambda bi, i: (bi, 0, 0))
    v_spec = pl.BlockSpec((1, t_len // LANE, NSA_KV_W, LANE), lambda bi, i: (bi, 0, 0, 0))
    return pl.pallas_call(
        body,
        grid=(b, t_len // tq),
        in_specs=[
            pl.BlockSpec((1, tq, NSA_Q_W), lambda bi, i: (bi, i, UB_Q // NSA_Q_W)),
            pl.BlockSpec((1, tq, NSA_Q_W), lambda bi, i: (bi, i, 0)),
            pl.BlockSpec((1, tq, NSA_G * LANE), lambda bi, i: (bi, i, 0)),
            k_spec, v_spec, k_spec, v_spec,
            pl.BlockSpec((t_len, LANE), lambda bi, i: (0, 0)),
            pl.BlockSpec(bt.shape, lambda bi, i: (0, 0, 0)),
            pl.BlockSpec((NSA_HEADS, LANE), lambda bi, i: (0, 0)),
            pl.BlockSpec((1, tq, LANE), lambda bi, i: (bi, i, UB_G // LANE)),
            pl.BlockSpec((1, tq, NSA_Q_W), lambda bi, i: (bi, i, UB_Z // NSA_Q_W)),
            pl.BlockSpec((1, HEAD_DIM), lambda bi, i: (0, 0)),
        ],
        out_specs=pl.BlockSpec((1, tq, NSA_Q_W), lambda bi, i: (bi, i, 0)),
        out_shape=jax.ShapeDtypeStruct((b, t_len, NSA_Q_W), BF16),
        scratch_shapes=[pltpu.VMEM((NSA_G, 1, rows), F32), pltpu.VMEM((NSA_G, 1, rows), F32),
                        pltpu.VMEM((NSA_G, HEAD_DIM, rows), F32), pltpu.VMEM((NSA_G, HEAD_DIM, rows), F32)],
        compiler_params=_cparams(("arbitrary", "arbitrary")),
        name="nsa_attn",
    )(ub, ocmp, nsel, ksb, vst, kwb, vwt, _block_penalty(t_len), bt, _far_bias_columns(rel_bias), ub, ub,
      q_norm_g.reshape(1, HEAD_DIM))


def _q_all(q_ref, qg_ref, ts):
    scale = HEAD_DIM ** -0.5
    zero = jnp.zeros((NSA_HPG * ts, HEAD_DIM), BF16)
    blocks = []
    for g in range(NSA_G):
        q4 = jnp.concatenate(
            [(_rms(q_ref[0, :, (g * NSA_HPG + hh) * HEAD_DIM:(g * NSA_HPG + hh + 1) * HEAD_DIM], qg_ref[...]) * scale).astype(BF16)
             for hh in range(NSA_HPG)], axis=0)
        blocks.append(jnp.concatenate([q4 if gg == g else zero for gg in range(NSA_G)], axis=1))
    return jnp.concatenate(blocks, axis=0)


def _row_to_col(row, n):
    eye = lax.broadcasted_iota(jnp.int32, (n, n), 0) == lax.broadcasted_iota(jnp.int32, (n, n), 1)
    return jnp.sum(jnp.where(eye, row, 0.0), axis=1, keepdims=True)


def _cmp_sample_body(q_ref, kc_ref, vc_ref, bias_ref, msel_ref, rep_ref, qg_ref, ocmp_ref, selt_ref, *, ts, nb, nbp):
    scale = HEAD_DIM ** -0.5
    ncp = kc_ref.shape[1]
    scores = []
    for g in range(NSA_G):
        gsl = slice(g * HEAD_DIM, (g + 1) * HEAD_DIM)
        q4 = jnp.concatenate(
            [_rms(q_ref[0, :, (g * NSA_HPG + hh) * HEAD_DIM:(g * NSA_HPG + hh + 1) * HEAD_DIM], qg_ref[...]) * scale
             for hh in range(NSA_HPG)], axis=0)
        bias = bias_ref[g * NSA_HPG:(g + 1) * NSA_HPG].reshape(NSA_HPG * ts, ncp)
        s = _mm_nt_hi(q4, kc_ref[0, :, gsl]) + bias
        e = jnp.exp(s - jnp.max(s, axis=-1, keepdims=True))
        p = e / jnp.sum(e, axis=-1, keepdims=True) * jnp.where(bias > 0.5 * NEG, 1.0, 0.0)
        o = _mm(p, vc_ref[0, :, gsl])
        imp = p[0:ts]
        for hh in range(NSA_HPG):
            h = g * NSA_HPG + hh
            ocmp_ref[0, :, h * HEAD_DIM:(h + 1) * HEAD_DIM] = o[hh * ts:(hh + 1) * ts]
            if hh:
                imp = imp + p[hh * ts:(hh + 1) * ts]
        scores.append(_mm_hi(imp, msel_ref[...]))
    score = jnp.concatenate(scores, axis=0)
    rows = NSA_G * ts
    lane = lax.broadcasted_iota(jnp.int32, (rows, nbp), 1)
    tok = lax.broadcasted_iota(jnp.int32, (rows, nbp), 0) & (ts - 1)
    qblk = (PAST_LEN + tok) >> 6
    forced = (lane == 0) | (lane == qblk) | (lane == qblk - 1)
    score = jnp.where(forced, BIG, score)
    score = jnp.where(lane <= qblk, score, -BIG)
    sel = jnp.zeros((rows, nbp), F32)
    lane_f = lane.astype(F32)
    for _ in range(SEL_TOP):
        mx = jnp.max(score, axis=-1, keepdims=True)
        first = jnp.min(jnp.where(score == mx, lane_f, float(nbp)), axis=-1, keepdims=True)
        pick = lane_f == first
        sel = jnp.where(pick, 1.0, sel)
        score = jnp.where(pick, -3e38, score)
    selt_ref[0] = _mm_tn(sel, rep_ref[...])


def _cmp_select_sample(ub, kcmp, vcmp, bias_cmp, q_norm_g, nb):
    b, ts, _ = ub.shape
    ncp = kcmp.shape[1]
    nc = ncp - 1
    nbp = -(-nb // LANE) * LANE
    assert ts & (ts - 1) == 0 and NSA_HEADS * ts == LANE
    rep = np.zeros((NSA_G * ts, LANE), np.float32)
    for g in range(NSA_G):
        for hh in range(NSA_HPG):
            for t in range(ts):
                rep[g * ts + t, (g * NSA_HPG + hh) * ts + t] = 1.0
    body = functools.partial(_cmp_sample_body, ts=ts, nb=nb, nbp=nbp)
    return pl.pallas_call(
        body,
        grid=(b,),
        in_specs=[
            pl.BlockSpec((1, ts, NSA_Q_W), lambda bi: (bi, 0, UB_Q // NSA_Q_W)),
            pl.BlockSpec((1, ncp, NSA_KV_W), lambda bi: (bi, 0, 0)),
            pl.BlockSpec((1, ncp, NSA_KV_W), lambda bi: (bi, 0, 0)),
            pl.BlockSpec((NSA_HEADS, ts, ncp), lambda bi: (0, 0, 0)),
            pl.BlockSpec((ncp, nbp), lambda bi: (0, 0)),
            pl.BlockSpec((NSA_G * ts, LANE), lambda bi: (0, 0)),
            pl.BlockSpec((1, HEAD_DIM), lambda bi: (0, 0)),
        ],
        out_specs=[
            pl.BlockSpec((1, ts, NSA_Q_W), lambda bi: (bi, 0, 0)),
            pl.BlockSpec((1, nbp, LANE), lambda bi: (bi, 0, 0)),
        ],
        out_shape=[jax.ShapeDtypeStruct((b, ts, NSA_Q_W), F32), jax.ShapeDtypeStruct((b, nbp, LANE), F32)],
        compiler_params=_cparams(("arbitrary",)),
        name="cmp_select_sample",
    )(ub, kcmp, vcmp, bias_cmp, _sel_matrix(nc, nb, ncp, nbp), jnp.asarray(rep), q_norm_g.reshape(1, HEAD_DIM))


def _sel_sample_body(pt_ref, *refs, pps, n_steps, ts):
    del pt_ref
    k_refs = refs[:pps]
    v_refs = refs[pps:2 * pps]
    (q_ref, selt_ref, knew_ref, vnew_ref, blast_ref, bfar_ref, bnew_ref, qg_ref,
     o_ref, qall_s, sc_s, snew_s, m_s, l_s, acc_s) = refs[2 * pps:]
    s = pl.program_id(1)
    n_pages = n_steps * pps
    sub = lax.broadcasted_iota(jnp.int32, (PAGE, LANE), 0)
    tok = lax.broadcasted_iota(jnp.int32, (ts, LANE), 1) & (ts - 1)
    new_t = lax.broadcasted_iota(jnp.int32, (ts, LANE), 0)

    @pl.when(s == 0)
    def _():
        qall = _q_all(q_ref, qg_ref, ts)
        qall_s[...] = qall
        sn = _mm_nt(knew_ref[0], qall) + bnew_ref[...]
        ok = (new_t <= tok) & (selt_ref[0, 2 * n_pages:2 * n_pages + 1, :] > 0.5)
        sn = jnp.where(ok, sn, NEG)
        snew_s[...] = sn
        m_s[...] = jnp.max(sn, axis=0, keepdims=True)
        l_s[...] = jnp.zeros((1, LANE), F32)
        acc_s[...] = jnp.zeros((LANE, NSA_KV_W), F32)

    @pl.when(s < n_steps)
    def _():
        m = m_s[...]
        for j in range(pps):
            p = s * pps + j
            st = _mm_nt(_load_page(k_refs[j], True), qall_s[...])
            st = st + jnp.where(p == n_pages - 1, blast_ref[...], bfar_ref[...])
            r0 = selt_ref[0, pl.ds(2 * p, 1), :]
            r1 = selt_ref[0, pl.ds(2 * p + 1, 1), :]
            ok = jnp.where(sub < SEL_BLOCK, r0, r1) > 0.5
            st = jnp.where(ok, st, NEG)
            sc_s[pl.ds(pl.multiple_of(p * PAGE, PAGE), PAGE), :] = st
            m = jnp.maximum(m, jnp.max(st, axis=0, keepdims=True))
        m_s[...] = m

    @pl.when(s >= n_steps)
    def _():
        m = m_s[...]
        l = l_s[...]
        acc = acc_s[...]
        for j in range(pps):
            p = (s - n_steps) * pps + j
            pt = jnp.exp(sc_s[pl.ds(pl.multiple_of(p * PAGE, PAGE), PAGE), :] - m)
            l = l + jnp.sum(pt, axis=0, keepdims=True)
            acc = acc + _mm_tn(pt, _load_page(v_refs[j], True))
        l_s[...] = l
        acc_s[...] = acc

    @pl.when(s == 2 * n_steps - 1)
    def _():
        pn = jnp.exp(snew_s[...] - m_s[...])
        l = l_s[...] + jnp.sum(pn, axis=0, keepdims=True)
        acc = acc_s[...] + _mm_tn(pn, vnew_ref[0])
        out = acc / _row_to_col(l, LANE)
        for h in range(NSA_HEADS):
            g = h // NSA_HPG
            o_ref[0, :, h * HEAD_DIM:(h + 1) * HEAD_DIM] = out[h * ts:(h + 1) * ts, g * HEAD_DIM:(g + 1) * HEAD_DIM]


def _lane_bias(rel_bias, key_pos, ts):
    rel = (PAST_LEN + jnp.arange(ts, dtype=jnp.int32))[None, :] - key_pos[:, None]
    return jnp.moveaxis(_bias_of(rel_bias, rel), 0, 1).reshape(key_pos.shape[0], NSA_HEADS * ts)


def _sel_sample(ub, selt, cache_k, cache_v, table, k_new, rel_bias, q_norm_g):
    b, ts, _ = ub.shape
    n_pages = table.shape[1]
    assert PAGE >= REL_MAX_DIST and n_pages * PAGE == PAST_LEN
    pps = 4
    n_steps = n_pages // pps
    ar = jnp.arange
    blast = _lane_bias(rel_bias, PAST_LEN - PAGE + ar(PAGE, dtype=jnp.int32), ts)
    bfar = _lane_bias(rel_bias, jnp.zeros((1,), jnp.int32), ts)
    bnew = _lane_bias(rel_bias, PAST_LEN + ar(ts, dtype=jnp.int32), ts)
    body = functools.partial(_sel_sample_body, pps=pps, n_steps=n_steps, ts=ts)
    last = n_steps - 1
    page_block = (1, NSA_G * PAGE, HEAD_DIM)
    k_spec = lambda j: pl.BlockSpec(page_block, lambda bi, s, pt: (pt[bi, jnp.minimum(s, last) * pps + j], 0, 0))
    v_spec = lambda j: pl.BlockSpec(page_block, lambda bi, s, pt: (pt[bi, jnp.maximum(s - n_steps, 0) * pps + j], 0, 0))
    full = lambda shape: pl.BlockSpec(shape, lambda bi, s, pt: (0,) * len(shape))
    in_specs = [k_spec(j) for j in range(pps)] + [v_spec(j) for j in range(pps)] + [
        pl.BlockSpec((1, ts, NSA_Q_W), lambda bi, s, pt: (bi, 0, UB_Q // NSA_Q_W)),
        pl.BlockSpec((1,) + selt.shape[1:], lambda bi, s, pt: (bi, 0, 0)),
        pl.BlockSpec((1, ts, NSA_KV_W), lambda bi, s, pt: (bi, 0, 0)),
        pl.BlockSpec((1, ts, NSA_KV_W), lambda bi, s, pt: (bi, 0, UB_KV // NSA_KV_W + 3)),
        full((PAGE, LANE)), full((1, LANE)), full((ts, LANE)), full((1, HEAD_DIM)),
    ]
    return pl.pallas_call(
        body,
        grid_spec=pltpu.PrefetchScalarGridSpec(
            num_scalar_prefetch=1, grid=(b, 2 * n_steps), in_specs=in_specs,
            out_specs=pl.BlockSpec((1, ts, NSA_Q_W), lambda bi, s, pt: (bi, 0, 0)),
            scratch_shapes=[pltpu.VMEM((LANE, NSA_KV_W), BF16), pltpu.VMEM((PAST_LEN, LANE), F32),
                            pltpu.VMEM((ts, LANE), F32), pltpu.VMEM((1, LANE), F32), pltpu.VMEM((1, LANE), F32),
                            pltpu.VMEM((LANE, NSA_KV_W), F32)]),
        out_shape=jax.ShapeDtypeStruct((b, ts, NSA_Q_W), F32),
        compiler_params=_cparams(("arbitrary", "arbitrary")),
        name="sel_sample",
    )(table, *([cache_k] * pps), *([cache_v] * pps), ub, selt, k_new, ub, blast, bfar, bnew, q_norm_g.reshape(1, HEAD_DIM))


def _win_sample_body(q_ref, kc_ref, vc_ref, knew_ref, vnew_ref, bc_ref, bn_ref, gate_ref, z_ref, ocmp_ref, osel_ref, qg_ref,
                     o_ref, kout_ref, vout_ref, *, ts, wb):
    qall = _q_all(q_ref, qg_ref, ts)
    key_i = lax.broadcasted_iota(jnp.int32, (wb, LANE), 0)
    tok_c = lax.broadcasted_iota(jnp.int32, (wb, LANE), 1) & (ts - 1)
    tok_n = lax.broadcasted_iota(jnp.int32, (ts, LANE), 1) & (ts - 1)
    new_t = lax.broadcasted_iota(jnp.int32, (ts, LANE), 0)
    sc = _mm_nt(_load_page(kc_ref, True), qall) + bc_ref[...]
    sc = jnp.where(wb + tok_c - key_i < WINDOW, sc, NEG)
    sn = _mm_nt(knew_ref[0], qall) + bn_ref[...]
    sn = jnp.where(new_t <= tok_n, sn, NEG)
    m = jnp.maximum(jnp.max(sc, axis=0, keepdims=True), jnp.max(sn, axis=0, keepdims=True))
    pc = jnp.exp(sc - m)
    pn = jnp.exp(sn - m)
    l = jnp.sum(pc, axis=0, keepdims=True) + jnp.sum(pn, axis=0, keepdims=True)
    out = (_mm_tn(pc, _load_page(vc_ref, True)) + _mm_tn(pn, vnew_ref[0])) / _row_to_col(l, LANE)
    gates = jax.nn.sigmoid(gate_ref[0])
    for h in range(NSA_HEADS):
        g = h // NSA_HPG
        hsl = slice(h * HEAD_DIM, (h + 1) * HEAD_DIM)
        o_win = out[h * ts:(h + 1) * ts, g * HEAD_DIM:(g + 1) * HEAD_DIM]
        o = (gates[:, h:h + 1] * ocmp_ref[0, :, hsl] + gates[:, NSA_HEADS + h:NSA_HEADS + h + 1] * osel_ref[0, :, hsl]
             + gates[:, 2 * NSA_HEADS + h:2 * NSA_HEADS + h + 1] * o_win)
        o_ref[0, :, hsl] = (o * _silu(z_ref[0, :, hsl])).astype(o_ref.dtype)
    keep = (wb - ts) * NSA_G
    for out_ref, old_ref, new_ref in ((kout_ref, kc_ref, knew_ref), (vout_ref, vc_ref, vnew_ref)):
        out_ref[0, 0:keep, :] = old_ref[0, ts * NSA_G:wb * NSA_G, :]
        for g in range(NSA_G):
            out_ref[0, pl.ds(keep + g, ts, stride=NSA_G), :] = new_ref[0, :, g * HEAD_DIM:(g + 1) * HEAD_DIM]


def _win_sample(ub, cache_k, cache_v, k_new, ocmp, osel, rel_bias, q_norm_g):
    b, ts, _ = ub.shape
    wb = cache_k.shape[1] // NSA_G
    assert wb == WINDOW and ts % 8 == 0
    bc = _lane_bias(rel_bias, PAST_LEN - wb + jnp.arange(wb, dtype=jnp.int32), ts)
    bn = _lane_bias(rel_bias, PAST_LEN + jnp.arange(ts, dtype=jnp.int32), ts)
    body = functools.partial(_win_sample_body, ts=ts, wb=wb)
    tok_spec = lambda w, c: pl.BlockSpec((1, ts, w), lambda bi: (bi, 0, c))
    win_spec = pl.BlockSpec((1, wb * NSA_G, HEAD_DIM), lambda bi: (bi, 0, 0))
    return pl.pallas_call(
        body,
        grid=(b,),
        in_specs=[
            tok_spec(NSA_Q_W, UB_Q // NSA_Q_W), win_spec, win_spec,
            tok_spec(NSA_KV_W, 0), tok_spec(NSA_KV_W, UB_KV // NSA_KV_W + 5),
            pl.BlockSpec((wb, LANE), lambda bi: (0, 0)), pl.BlockSpec((ts, LANE), lambda bi: (0, 0)),
            tok_spec(LANE, UB_G // LANE), tok_spec(NSA_Q_W, UB_Z // NSA_Q_W),
            tok_spec(NSA_Q_W, 0), tok_spec(NSA_Q_W, 0),
            pl.BlockSpec((1, HEAD_DIM), lambda bi: (0, 0)),
        ],
        out_specs=[tok_spec(NSA_Q_W, 0), win_spec, win_spec],
        out_shape=[jax.ShapeDtypeStruct((b, ts, NSA_Q_W), BF16),
                   jax.ShapeDtypeStruct((b, wb * NSA_G, HEAD_DIM), F32),
                   jax.ShapeDtypeStruct((b, wb * NSA_G, HEAD_DIM), F32)],
        compiler_params=_cparams(("arbitrary",)),
        name="win_sample",
    )(ub, cache_k, cache_v, k_new, ub, bc, bn, ub, ub, ocmp, osel, q_norm_g.reshape(1, HEAD_DIM))


def _mix_body(oa_ref, ob_ref, wa_ref, wb_ref, ma_ref, mb_ref, o_ref):
    pa = jnp.dot(oa_ref[...], wa_ref[...], preferred_element_type=F32)
    pb = jnp.dot(ob_ref[...], wb_ref[...], preferred_element_type=F32)
    o_ref[...] = (jax.nn.sigmoid(ma_ref[...]) * pa + jax.nn.sigmoid(mb_ref[...]) * pb).astype(o_ref.dtype)


def _out_body(m_ref, w_ref, x_ref, y_ref):
    y_ref[...] = x_ref[...] + jnp.dot(m_ref[...], w_ref[...], preferred_element_type=F32)


def _output(x2d, oa, ob, ub2d, wa16, wb16, wo16):
    m = x2d.shape[0]
    tm = min(m, 512)
    tn = 512
    grid = (m // tm, D_MODEL // tn)
    mixed = pl.pallas_call(
        _mix_body,
        grid=grid,
        in_specs=[
            pl.BlockSpec((tm, GDN_V_W), lambda i, j: (i, 0)),
            pl.BlockSpec((tm, NSA_Q_W), lambda i, j: (i, 0)),
            pl.BlockSpec((GDN_V_W, tn), lambda i, j: (0, j)),
            pl.BlockSpec((NSA_Q_W, tn), lambda i, j: (0, j)),
            pl.BlockSpec((tm, tn), lambda i, j: (i, UB_MA // tn + j)),
            pl.BlockSpec((tm, tn), lambda i, j: (i, UB_MB // tn + j)),
        ],
        out_specs=pl.BlockSpec((tm, tn), lambda i, j: (i, j)),
        out_shape=jax.ShapeDtypeStruct((m, D_MODEL), BF16),
        compiler_params=_cparams(("arbitrary", "arbitrary")),
        name="branch_mix",
    )(oa, ob, wa16, wb16, ub2d, ub2d)
    return pl.pallas_call(
        _out_body,
        grid=grid,
        in_specs=[
            pl.BlockSpec((tm, D_MODEL), lambda i, j: (i, 0)),
            pl.BlockSpec((D_MODEL, tn), lambda i, j: (0, j)),
            pl.BlockSpec((tm, tn), lambda i, j: (i, j)),
        ],
        out_specs=pl.BlockSpec((tm, tn), lambda i, j: (i, j)),
        out_shape=jax.ShapeDtypeStruct((m, D_MODEL), F32),
        compiler_params=_cparams(("arbitrary", "arbitrary")),
        name="out_proj",
    )(mixed, wo16, x2d)


def _kv_slices(ub, rows=None):
    b, t_len, _ = ub.shape
    out = []
    for i in (0, 1, 3, 5):
        a = ub[:, :, UB_KV + i * NSA_KV_W:UB_KV + (i + 1) * NSA_KV_W]
        out.append(a.reshape(b, t_len, NSA_G, HEAD_DIM))
    return out


def kernel(x_prompt, x_sample, cache_k_cmp, cache_v_cmp, cache_k_sel, cache_v_sel, cache_k_win, cache_v_win, state_conv, state_gdn, page_table, norm_g, w_in, gdn_conv_w, gdn_a_log, gdn_dt_bias, gdn_norm_g, q_norm_g, k_norm_g, cmp_pe_k, cmp_w_k, cmp_proj_k, cmp_pe_v, cmp_w_v, cmp_proj_v, rel_bias, w_branch_a, w_branch_b, w_out):
    bp, tp, _ = x_prompt.shape
    bs, ts, _ = x_sample.shape
    wb = cache_k_win.shape[1]
    n_pool = cache_k_cmp.shape[0]
    kv4 = lambda a: a.reshape(a.shape[0], a.shape[1], NSA_G, HEAD_DIM)
    cmp_p = (cmp_pe_k, cmp_w_k, cmp_proj_k, cmp_pe_v, cmp_w_v, cmp_proj_v)

    assert _IN_MB + D_MODEL == w_in.shape[1]
    w_t = w_in.T
    wa16, wb16, wo16 = w_branch_a.astype(BF16), w_branch_b.astype(BF16), w_out.astype(BF16)

    def project(x):
        x2d = x.reshape(-1, D_MODEL)
        ua = _project(x2d, norm_g, w_t, UA_OFFSETS).reshape(x.shape[0], x.shape[1], UA_W)
        ub = _project(x2d, norm_g, w_t, UB_OFFSETS).reshape(x.shape[0], x.shape[1], UB_W)
        return x2d, ua, ub

    x2d, ua, ub = project(x_prompt)
    conv0 = jnp.zeros((bp, GDN_CONV - 1, GDN_QKV_W), F32)
    s0 = jnp.zeros((bp, GDN_V_HEADS, HEAD_DIM, HEAD_DIM), F32)
    o_a, p_gdn = _gdn(ua, ub, conv0, s0, gdn_conv_w, gdn_a_log, gdn_dt_bias, gdn_norm_g)
    p_conv = ua[:, tp - (GDN_CONV - 1):, :GDN_QKV_W]
    n_pages_p = tp // PAGE
    pages = ub.reshape(bp * n_pages_p, PAGE, UB_W)
    ident = jnp.arange(bp * n_pages_p, dtype=jnp.int32).reshape(bp, n_pages_p)
    kcmp, vcmp = _compress(pages, pages, UB_KV // NSA_KV_W, UB_KV // NSA_KV_W + 1, ident, *cmp_p, k_norm_g[0])
    p_ks, p_kw, ksb, vsb, kwb, vwb = _nsa_prep(ub, k_norm_g, True)
    nc_p = (tp - CMP_LEN) // CMP_STRIDE + 1
    bias_cmp = _cmp_bias(rel_bias, jnp.arange(tp, dtype=jnp.int32), kcmp.shape[1], nc_p)
    ocmp, nsel = _cmp_select_prompt(ub, kcmp, vcmp, bias_cmp, q_norm_g)
    o_b = _attn_prompt(ub, ocmp, nsel, ksb, vsb, kwb, vwb, rel_bias, q_norm_g)
    y_prompt = _output(x2d, o_a.reshape(-1, GDN_V_W), o_b.reshape(-1, NSA_Q_W), ub.reshape(-1, UB_W), wa16, wb16, wo16)
    p_kc, p_vc, p_vs, p_vw = _kv_slices(ub)
    prompt_out = (y_prompt.reshape(x_prompt.shape), p_kc, p_vc, kv4(p_ks), p_vs, kv4(p_kw[:, tp - wb:]), p_vw[:, tp - wb:],
                  p_conv, p_gdn)

    x2d, ua, ub = project(x_sample)
    o_a, s_gdn = _gdn(ua, ub, state_conv, state_gdn, gdn_conv_w, gdn_a_log, gdn_dt_bias, gdn_norm_g)
    s_conv = ua[:, ts - (GDN_CONV - 1):, :GDN_QKV_W]
    pool = lambda c: c.reshape(n_pool, PAGE * NSA_G, HEAD_DIM)
    kcmp, vcmp = _compress(pool(cache_k_cmp), pool(cache_v_cmp), 0, 0, page_table, *cmp_p, k_norm_g[0])
    s_ks, s_kwn = _nsa_prep(ub, k_norm_g, False)
    total = PAST_LEN + ts
    nc_s = (total - CMP_LEN) // CMP_STRIDE + 1
    nb_s = -(-total // SEL_BLOCK)
    assert nc_s + 1 == kcmp.shape[1]
    pos_s = PAST_LEN + jnp.arange(ts, dtype=jnp.int32)
    bias_cmp = _cmp_bias(rel_bias, pos_s, kcmp.shape[1], nc_s)
    ocmp, selt = _cmp_select_sample(ub, kcmp, vcmp, bias_cmp, q_norm_g, nb_s)
    osel = _sel_sample(ub, selt, pool(cache_k_sel), pool(cache_v_sel), page_table, s_ks, rel_bias, q_norm_g)
    win3 = lambda c: c.reshape(bs, wb * NSA_G, HEAD_DIM)
    o_b, s_kw, s_vw = _win_sample(ub, win3(cache_k_win), win3(cache_v_win), s_kwn, ocmp, osel, rel_bias, q_norm_g)
    y_sample = _output(x2d, o_a.reshape(-1, GDN_V_W), o_b.reshape(-1, NSA_Q_W), ub.reshape(-1, UB_W), wa16, wb16, wo16)
    s_kc, s_vc, s_vs, _ = _kv_slices(ub)
    sample_out = (y_sample.reshape(x_sample.shape), s_kc, s_vc, kv4(s_ks), s_vs, s_kw.reshape(cache_k_win.shape),
                  s_vw.reshape(cache_v_win.shape), s_conv, s_gdn)

    return (prompt_out[0], sample_out[0]) + prompt_out[1:] + sample_out[1:]
```

```python
import functools
import math

import jax
import jax.numpy as jnp
import numpy as np
from jax import lax
from jax.experimental import pallas as pl
from jax.experimental.pallas import tpu as pltpu

F32 = jnp.float32
BF16 = jnp.bfloat16
HI = lax.Precision.HIGHEST

D_MODEL = 2048
PAST_LEN = 16384
PAGE = 128

GDN_QK_HEADS = 16
GDN_V_HEADS = 32
HEAD_DIM = 128
GDN_CONV = 4
GDN_CHUNK = 64
GDN_GROUP = 4
GDN_QK_PER_STEP = 2
GDN_QK_W = GDN_QK_HEADS * HEAD_DIM
GDN_V_W = GDN_V_HEADS * HEAD_DIM
GDN_QKV_W = 2 * GDN_QK_W + GDN_V_W

NSA_HEADS = 16
NSA_G = 4
NSA_HPG = NSA_HEADS // NSA_G
NSA_Q_W = NSA_HEADS * HEAD_DIM
NSA_KV_W = NSA_G * HEAD_DIM
CMP_LEN = 32
CMP_STRIDE = 16
SEL_BLOCK = 64
SEL_TOP = 16
WINDOW = 512
REL_BUCKETS = 32
REL_MAX_DIST = 128

EPS = 1e-6
NEG = -1e30
BIG = 1e9

LANE = 128
VMEM_LIMIT = 56 * 1024 * 1024
PAGES_PER_STEP = 16

UA_W = GDN_QKV_W + GDN_V_W
UB_Q, UB_Z, UB_MA, UB_MB, UB_KV, UB_BA, UB_G = 0, 2048, 4096, 6144, 8192, 11264, 11776
UB_W = 12288
PROJ_TN = 512

_IN_QB = UA_W + 2 * GDN_V_HEADS
_IN_KV = _IN_QB + NSA_Q_W
_IN_G = _IN_KV + 6 * NSA_KV_W
_IN_ZB = _IN_G + 3 * NSA_HEADS
_IN_MA = _IN_ZB + NSA_Q_W
_IN_MB = _IN_MA + D_MODEL


def _tiles(start, width):
    return [start + k * PROJ_TN for k in range(width // PROJ_TN)]


UA_OFFSETS = _tiles(0, UA_W)
UB_OFFSETS = (_tiles(_IN_QB, NSA_Q_W) + _tiles(_IN_ZB, NSA_Q_W) + _tiles(_IN_MA, D_MODEL) + _tiles(_IN_MB, D_MODEL)
              + _tiles(_IN_KV, 6 * NSA_KV_W) + [UA_W, _IN_G])


def _mm(a, b):
    return jnp.dot(a.astype(BF16), b.astype(BF16), preferred_element_type=F32)


def _mm_nt(a, b):
    return lax.dot_general(a.astype(BF16), b.astype(BF16), (((1,), (1,)), ((), ())), preferred_element_type=F32)


def _mm_tn(a, b):
    return lax.dot_general(a.astype(BF16), b.astype(BF16), (((0,), (0,)), ((), ())), preferred_element_type=F32)


def _mm_hi(a, b):
    return jnp.dot(a, b, precision=HI, preferred_element_type=F32)


def _mm_nt_hi(a, b):
    return lax.dot_general(a, b, (((1,), (1,)), ((), ())), precision=HI, preferred_element_type=F32)


def _silu(x):
    return x * jax.nn.sigmoid(x)


def _softplus(x):
    return jnp.maximum(x, 0.0) + jnp.log1p(jnp.exp(-jnp.abs(x)))


def _rms(x, gain):
    return x * lax.rsqrt(jnp.mean(x * x, axis=-1, keepdims=True) + EPS) * gain


def _cparams(sem):
    return pltpu.CompilerParams(dimension_semantics=sem, vmem_limit_bytes=VMEM_LIMIT)


def _proj_body(offs_ref, x_ref, g_ref, w_ref, o_ref, h_ref):
    del offs_ref

    @pl.when(pl.program_id(1) == 0)
    def _():
        h_ref[...] = _rms(x_ref[...], g_ref[...]).astype(BF16)

    o_ref[...] = lax.dot_general(h_ref[...], w_ref[...].astype(BF16), (((1,), (1,)), ((), ())),
                                 preferred_element_type=F32)


def _project(x2d, norm_g, w_t, offsets):
    m = x2d.shape[0]
    tm = min(m, 1024)
    n_tiles = len(offsets)
    return pl.pallas_call(
        _proj_body,
        grid_spec=pltpu.PrefetchScalarGridSpec(
            num_scalar_prefetch=1,
            grid=(m // tm, n_tiles),
            in_specs=[
                pl.BlockSpec((tm, D_MODEL), lambda i, j, offs: (i, 0)),
                pl.BlockSpec((1, D_MODEL), lambda i, j, offs: (0, 0)),
                pl.BlockSpec((pl.Element(PROJ_TN), pl.Element(D_MODEL)), lambda i, j, offs: (pl.multiple_of(offs[j], 16), 0)),
            ],
            out_specs=pl.BlockSpec((tm, PROJ_TN), lambda i, j, offs: (i, j)),
            scratch_shapes=[pltpu.VMEM((tm, D_MODEL), BF16)]),
        out_shape=jax.ShapeDtypeStruct((m, n_tiles * PROJ_TN), F32),
        compiler_params=_cparams(("arbitrary", "arbitrary")),
        name="proj",
    )(jnp.asarray(offsets, jnp.int32), x2d, norm_g.reshape(1, D_MODEL), w_t)


def _split2(a):
    hi = a.astype(BF16)
    return hi, (a - hi.astype(F32)).astype(BF16)


def _dot16(a, b):
    return jnp.dot(a, b, preferred_element_type=F32)


def _bdot(a, b):
    return lax.dot_general(a, b, (((2,), (1,)), ((0,), (0,))), preferred_element_type=F32)


def _bdot_nt(a, b):
    return lax.dot_general(a, b, (((2,), (2,)), ((0,), (0,))), preferred_element_type=F32)


def _bmm_x3(a2, b2):
    return _bdot(a2[0], b2[0]) + (_bdot(a2[0], b2[1]) + _bdot(a2[1], b2[0]))


def _unit_lower_inverse(nmat, c):
    row = lax.broadcasted_iota(jnp.int32, (1, c, c), 1)
    col = lax.broadcasted_iota(jnp.int32, (1, c, c), 2)
    p = jnp.where(row == col, 1.0, 0.0) + nmat
    m2 = _split2(nmat)
    span = 2
    while span < c:
        m2 = _split2(_bmm_x3(m2, m2))
        p = p + _bmm_x3(_split2(p), m2)
        span *= 2
    return p


def _cumsum_rows(tril16, g):
    h = g.astype(BF16)
    r = g - h.astype(F32)
    m = r.astype(BF16)
    l = (r - m.astype(F32)).astype(BF16)
    return _dot16(tril16, h) + (_dot16(tril16, m) + _dot16(tril16, l))


def _gdn_body(q_ref, k_ref, v_ref, z_ref, ba_ref, cwq_ref, cwk_ref, cwv_ref, csq_ref, csk_ref, csv_ref,
              s0_ref, gp_ref, ng_ref, o_ref, sfin_ref, cq_s, ck_s, cv_s, u_s, w_s, qg_s, kd_s, at_s, gl_s, st_s,
              *, t_len, c, nq):
    nv = 2 * nq
    first_head = nv * pl.program_id(1)
    n_chunks = t_len // c
    for cs, carry in ((csq_ref, cq_s), (csk_ref, ck_s), (csv_ref, cv_s)):
        carry[...] = jnp.zeros(carry.shape, F32)
        carry[5:8, :] = cs[0]
    st_s[...] = s0_ref[0]

    gsz = min(GDN_GROUP, n_chunks)
    rg = gsz * c
    row = lax.broadcasted_iota(jnp.int32, (1, c, c), 1)
    col = lax.broadcasted_iota(jnp.int32, (1, c, c), 2)
    tril = row >= col
    strict = row > col
    eye = row == col
    lane = lax.broadcasted_iota(jnp.int32, (rg, LANE), 1)
    rr = lax.broadcasted_iota(jnp.int32, (rg, rg), 0)
    cc = lax.broadcasted_iota(jnp.int32, (rg, rg), 1)
    shift = c.bit_length() - 1
    assert 1 << shift == c
    tril16 = jnp.where((rr >= cc) & ((rr >> shift) == (cc >> shift)), 1.0, 0.0).astype(BF16)

    def conv(src_ref, carry, cw_ref, rows):
        cur = src_ref[0, rows, :]
        win = jnp.concatenate([carry[...], cur], axis=0)
        carry[...] = cur[rg - 8:rg]
        w = cw_ref[...]
        a = win[5:5 + rg] * w[0:1] + win[6:6 + rg] * w[1:2] + win[7:7 + rg] * w[2:3] + win[8:8 + rg] * w[3:4]
        return _silu(a)

    def pick(x, idx):
        return jnp.sum(jnp.where(lane == idx, x, 0.0), axis=-1, keepdims=True).reshape(gsz, c, 1)

    def per_head(x, n):
        return [x[:, i * HEAD_DIM:(i + 1) * HEAD_DIM].reshape(gsz, c, HEAD_DIM) for i in range(n)]

    def prepare(gi):
        base = gi * rg if isinstance(gi, int) else pl.multiple_of(gi * rg, rg)
        rows = pl.ds(base, rg)
        qa = per_head(conv(q_ref, cq_s, cwq_ref, rows), nq)
        ka = per_head(conv(k_ref, ck_s, cwk_ref, rows), nq)
        va = per_head(conv(v_ref, cv_s, cwv_ref, rows), nv)
        qn = [x * lax.rsqrt(jnp.sum(x * x, axis=-1, keepdims=True) + EPS) * (HEAD_DIM ** -0.5) for x in qa]
        kn = [x * lax.rsqrt(jnp.sum(x * x, axis=-1, keepdims=True) + EPS) for x in ka]
        ba = ba_ref[0, rows, :]
        beta_all = jax.nn.sigmoid(ba)
        g_all = -jnp.exp(gp_ref[0:1, :]) * _softplus(ba + gp_ref[1:2, :])
        gc_all = _cumsum_rows(tril16, g_all)
        k16 = jnp.concatenate(kn, axis=0).astype(BF16)
        kk = _bdot_nt(k16, k16)
        qk = _bdot_nt(jnp.concatenate(qn, axis=0).astype(BF16), k16)
        nmats, rhss, gammas = [], [], []
        for hh in range(nv):
            qi = hh // 2
            hv = first_head + hh
            beta = pick(beta_all, hv)
            gcol = pick(gc_all, GDN_V_HEADS + hv)
            grow = jnp.sum(jnp.where(eye, gcol, 0.0), axis=1, keepdims=True)
            gamma = jnp.where(tril, jnp.exp(jnp.minimum(gcol - grow, 0.0)), 0.0)
            nmats.append(jnp.where(strict, -(kk[qi * gsz:(qi + 1) * gsz] * beta * gamma), 0.0))
            egc = jnp.exp(gcol)
            glast = gcol[:, c - 1:c, :]
            rhss.append(jnp.concatenate([va[hh] * beta, kn[qi] * (beta * egc)], axis=2))
            gammas.append(gamma)
            qg_s[hh, rows, :] = (qn[qi] * egc).reshape(rg, HEAD_DIM).astype(qg_s.dtype)
            kd_s[hh, rows, :] = (kn[qi] * jnp.exp(glast - gcol)).reshape(rg, HEAD_DIM).astype(kd_s.dtype)
            egl = jnp.exp(glast)
            for j in range(gsz):
                gl_s[hh, gi * gsz + j] = jnp.broadcast_to(egl[j], (8, HEAD_DIM))
        tinv = _unit_lower_inverse(jnp.concatenate(nmats, axis=0), c)
        uw = _bmm_x3(_split2(tinv), _split2(jnp.concatenate(rhss, axis=0)))
        for hh in range(nv):
            uw_h = uw[hh * gsz:(hh + 1) * gsz]
            u_s[hh, rows, :] = uw_h[:, :, :HEAD_DIM].reshape(rg, HEAD_DIM)
            w_s[hh, rows, :] = uw_h[:, :, HEAD_DIM:].reshape(rg, HEAD_DIM).astype(w_s.dtype)
            at_s[hh, rows, :] = (qk[(hh // 2) * gsz:(hh // 2 + 1) * gsz] * gammas[hh]).reshape(rg, c).astype(at_s.dtype)

    def recur(ci):
        rows = pl.ds(ci * c if isinstance(ci, int) else pl.multiple_of(ci * c, c), c)
        s = st_s[...]
        s16 = s.astype(BF16)
        v_new = u_s[:, rows, :] - _bdot(w_s[:, rows, :].astype(BF16), s16)
        v16 = v_new.astype(BF16)
        o = _bdot(qg_s[:, rows, :].astype(BF16), s16) + _bdot(at_s[:, rows, :].astype(BF16), v16)
        kd16 = kd_s[:, rows, :].astype(BF16)
        upd = jnp.stack([_mm_tn(kd16[hh], v16[hh]) for hh in range(nv)], axis=0)
        st_s[...] = s * gl_s[:, ci][:, 0:1, :] + upd
        for hh in range(nv):
            hsl = slice(hh * HEAD_DIM, (hh + 1) * HEAD_DIM)
            o_ref[0, rows, hsl] = (_rms(o[hh], ng_ref[...]) * _silu(z_ref[0, rows, hsl])).astype(o_ref.dtype)

    if n_chunks == 1:
        prepare(0)
        recur(0)
    else:
        def prepare_step(gi, carry):
            prepare(gi)
            return carry

        def recur_step(ci, carry):
            recur(ci)
            return carry

        lax.fori_loop(0, n_chunks // gsz, prepare_step, 0)
        lax.fori_loop(0, n_chunks, recur_step, 0)
    sfin_ref[0] = st_s[...]


def _gdn(ua, ub, conv_state, s0, conv_w, a_log, dt_bias, norm_g):
    b, t_len, _ = ua.shape
    c = min(GDN_CHUNK, t_len)
    hd = HEAD_DIM
    nq = GDN_QK_PER_STEP
    nv = 2 * nq
    steps = GDN_QK_HEADS // nq
    qw, vw = nq * hd, nv * hd
    op_dtype = BF16 if c % 16 == 0 else F32
    gp = jnp.zeros((2, LANE), F32)
    gp = gp.at[0, GDN_V_HEADS:2 * GDN_V_HEADS].set(a_log).at[1, GDN_V_HEADS:2 * GDN_V_HEADS].set(dt_bias)
    body = functools.partial(_gdn_body, t_len=t_len, c=c, nq=nq)
    return pl.pallas_call(
        body,
        grid=(b, steps),
        in_specs=[
            pl.BlockSpec((1, t_len, qw), lambda bi, i: (bi, 0, i)),
            pl.BlockSpec((1, t_len, qw), lambda bi, i: (bi, 0, steps + i)),
            pl.BlockSpec((1, t_len, vw), lambda bi, i: (bi, 0, steps + i)),
            pl.BlockSpec((1, t_len, vw), lambda bi, i: (bi, 0, 2 * steps + i)),
            pl.BlockSpec((1, t_len, LANE), lambda bi, i: (bi, 0, UB_BA // LANE)),
            pl.BlockSpec((GDN_CONV, qw), lambda bi, i: (0, i)),
            pl.BlockSpec((GDN_CONV, qw), lambda bi, i: (0, steps + i)),
            pl.BlockSpec((GDN_CONV, vw), lambda bi, i: (0, steps + i)),
            pl.BlockSpec((1, GDN_CONV - 1, qw), lambda bi, i: (bi, 0, i)),
            pl.BlockSpec((1, GDN_CONV - 1, qw), lambda bi, i: (bi, 0, steps + i)),
            pl.BlockSpec((1, GDN_CONV - 1, vw), lambda bi, i: (bi, 0, steps + i)),
            pl.BlockSpec((1, nv, hd, hd), lambda bi, i: (bi, i, 0, 0)),
            pl.BlockSpec((2, LANE), lambda bi, i: (0, 0)),
            pl.BlockSpec((1, hd), lambda bi, i: (0, 0)),
        ],
        out_specs=[
            pl.BlockSpec((1, t_len, vw), lambda bi, i: (bi, 0, i)),
            pl.BlockSpec((1, nv, hd, hd), lambda bi, i: (bi, i, 0, 0)),
        ],
        out_shape=[
            jax.ShapeDtypeStruct((b, t_len, GDN_V_W), BF16),
            jax.ShapeDtypeStruct((b, GDN_V_HEADS, hd, hd), F32),
        ],
        scratch_shapes=[
            pltpu.VMEM((8, qw), F32),
            pltpu.VMEM((8, qw), F32),
            pltpu.VMEM((8, vw), F32),
            pltpu.VMEM((nv, t_len, hd), F32),
            pltpu.VMEM((nv, t_len, hd), op_dtype),
            pltpu.VMEM((nv, t_len, hd), op_dtype),
            pltpu.VMEM((nv, t_len, hd), op_dtype),
            pltpu.VMEM((nv, t_len, c), op_dtype),
            pltpu.VMEM((nv, t_len // c, 8, hd), F32),
            pltpu.VMEM((nv, hd, hd), F32),
        ],
        compiler_params=_cparams(("arbitrary", "arbitrary")),
        name="gdn",
    )(ua, ua, ua, ua, ub, conv_w, conv_w, conv_w, conv_state, conv_state, conv_state, s0, gp, norm_g.reshape(1, hd))


def _t5_bucket(rel):
    n = jnp.maximum(rel, 0)
    exact = REL_BUCKETS // 2
    nf = jnp.maximum(n, 1).astype(F32)
    large = exact + (jnp.log(nf / exact) / math.log(REL_MAX_DIST / exact) * (REL_BUCKETS - exact)).astype(jnp.int32)
    return jnp.where(n < exact, n, jnp.minimum(large, REL_BUCKETS - 1))


def _bias_of(rel_bias, rel):
    bucket = _t5_bucket(rel)[None]
    table = rel_bias.astype(F32)
    out = jnp.zeros((NSA_HEADS,) + rel.shape, F32)
    for k in range(REL_BUCKETS):
        out = jnp.where(bucket == k, table[k].reshape((NSA_HEADS,) + (1,) * rel.ndim), out)
    return out


def _cmp_bias(rel_bias, q_pos, ncp, nc):
    n = jnp.arange(ncp, dtype=jnp.int32)
    rel = q_pos[:, None] - (n * CMP_STRIDE + (CMP_LEN - 1))[None, :]
    ok = (rel >= 0) & (n < nc)[None, :]
    return jnp.where(ok[None], _bias_of(rel_bias, rel), NEG)


def _load_page(ref, interleaved):
    if not interleaved:
        return ref[0]
    n_rows = ref.shape[1] // NSA_G
    return jnp.concatenate([ref[0, pl.ds(g, n_rows, stride=NSA_G), :] for g in range(NSA_G)], axis=1)


def _compress_body(pt_ref, *refs, pps, n_steps, nch, interleaved):
    del pt_ref
    k_refs = refs[:pps]
    v_refs = refs[pps:2 * pps]
    (wabk_ref, wabv_ref, wk_ref, pek_ref, wv_ref, pev_ref, projk_ref, projv_ref, gain_ref,
     outk_ref, outv_ref, ak_s, bk_s, av_s, bv_s) = refs[2 * pps:]
    s = pl.program_id(1)

    @pl.when(s == 0)
    def _():
        bk_s[nch:nch + 8, :] = jnp.zeros((8, NSA_KV_W), F32)
        bv_s[nch:nch + 8, :] = jnp.zeros((8, NSA_KV_W), F32)

    for j in range(pps):
        row0 = pl.multiple_of((s * pps + j) * 8, 8)
        abk = _mm_hi(wabk_ref[...], _load_page(k_refs[j], interleaved))
        ak_s[pl.ds(row0, 8), :] = abk[0:8]
        bk_s[pl.ds(row0, 8), :] = abk[8:16]
        abv = _mm_hi(wabv_ref[...], _load_page(v_refs[j], interleaved))
        av_s[pl.ds(row0, 8), :] = abv[0:8]
        bv_s[pl.ds(row0, 8), :] = abv[8:16]

    @pl.when(s == n_steps - 1)
    def _():
        cpe_k = jnp.sum(wk_ref[...] * pek_ref[...], axis=0, keepdims=True)
        cpe_v = jnp.sum(wv_ref[...] * pev_ref[...], axis=0, keepdims=True)
        rb = min(nch, 128)

        def fin(r, carry):
            r0 = pl.multiple_of(r * rb, rb)
            pk = ak_s[pl.ds(r0, rb), :] + bk_s[pl.ds(r0, rb + 8), :][1:rb + 1]
            pv = av_s[pl.ds(r0, rb), :] + bv_s[pl.ds(r0, rb + 8), :][1:rb + 1]
            for g in range(NSA_G):
                sl = slice(g * HEAD_DIM, (g + 1) * HEAD_DIM)
                yk = _mm_hi(pk[:, sl] + cpe_k, projk_ref[...])
                outk_ref[0, pl.ds(r0, rb), sl] = _rms(yk, gain_ref[...])
                outv_ref[0, pl.ds(r0, rb), sl] = _mm_hi(pv[:, sl] + cpe_v, projv_ref[...])
            return carry

        lax.fori_loop(0, nch // rb, fin, 0)


def _pool_weights(w):
    c = np.arange(8)[:, None]
    t = np.arange(PAGE)[None, :]
    off = t - CMP_STRIDE * c
    inside = (off >= 0) & (off < CMP_STRIDE)
    idx = np.clip(off, 0, CMP_STRIDE - 1)
    wa = jnp.where(inside, w[idx], 0.0)
    wb = jnp.where(inside, w[idx + CMP_STRIDE], 0.0)
    return jnp.concatenate([wa, wb], axis=0).astype(F32)


def _compress(pages_k, pages_v, col_k, col_v, table, pe_k, w_k, proj_k, pe_v, w_v, proj_v, k_gain):
    b, n_pages = table.shape
    pps = min(PAGES_PER_STEP, n_pages)
    n_steps = n_pages // pps
    nch = 8 * n_pages
    interleaved = pages_k.shape[1] == NSA_G * PAGE
    body = functools.partial(_compress_body, pps=pps, n_steps=n_steps, nch=nch, interleaved=interleaved)

    def page_spec(j, col):
        if interleaved:
            return pl.BlockSpec((1, NSA_G * PAGE, HEAD_DIM), lambda bi, s, pt: (pt[bi, s * pps + j], 0, 0))
        return pl.BlockSpec((1, PAGE, NSA_KV_W), lambda bi, s, pt: (pt[bi, s * pps + j], 0, col))

    full = lambda shape: pl.BlockSpec(shape, lambda bi, s, pt: (0,) * len(shape))
    in_specs = [page_spec(j, col_k) for j in range(pps)] + [page_spec(j, col_v) for j in range(pps)]
    in_specs += [full((16, PAGE)), full((16, PAGE)), full((CMP_LEN, 1)), full((CMP_LEN, HEAD_DIM)),
                 full((CMP_LEN, 1)), full((CMP_LEN, HEAD_DIM)), full((HEAD_DIM, HEAD_DIM)), full((HEAD_DIM, HEAD_DIM)),
                 full((1, HEAD_DIM))]
    out_spec = pl.BlockSpec((1, nch, NSA_KV_W), lambda bi, s, pt: (bi, 0, 0))
    return pl.pallas_call(
        body,
        grid_spec=pltpu.PrefetchScalarGridSpec(
            num_scalar_prefetch=1, grid=(b, n_steps), in_specs=in_specs, out_specs=[out_spec, out_spec],
            scratch_shapes=[pltpu.VMEM((nch, NSA_KV_W), F32), pltpu.VMEM((nch + 8, NSA_KV_W), F32),
                            pltpu.VMEM((nch, NSA_KV_W), F32), pltpu.VMEM((nch + 8, NSA_KV_W), F32)]),
        out_shape=[jax.ShapeDtypeStruct((b, nch, NSA_KV_W), F32)] * 2,
        compiler_params=_cparams(("arbitrary", "arbitrary")),
        name="compress",
    )(table, *([pages_k] * pps), *([pages_v] * pps), _pool_weights(w_k), _pool_weights(w_v),
      w_k.reshape(CMP_LEN, 1), pe_k, w_v.reshape(CMP_LEN, 1), pe_v, proj_k, proj_v, k_gain.reshape(1, HEAD_DIM))


def _prep_body(ks_ref, vs_ref, kw_ref, vw_ref, kg_ref, pks_ref, pkw_ref, *attn_refs):
    for g in range(NSA_G):
        sl = slice(g * HEAD_DIM, (g + 1) * HEAD_DIM)
        ksn = _rms(ks_ref[0, :, sl], kg_ref[1:2, :])
        kwn = _rms(kw_ref[0, :, sl], kg_ref[2:3, :])
        pks_ref[0, :, sl] = ksn
        pkw_ref[0, :, sl] = kwn
        if attn_refs:
            attn_refs[0][0, :, sl] = ksn.astype(BF16)
            attn_refs[2][0, :, sl] = kwn.astype(BF16)
    if attn_refs:
        for src, dst in ((vs_ref, attn_refs[1]), (vw_ref, attn_refs[3])):
            for j in range(dst.shape[1]):
                dst[0, j] = src[0, j * LANE:(j + 1) * LANE, :].T.astype(BF16)


def _nsa_prep(ub, k_norm_g, attn_operands):
    b, t_len, _ = ub.shape
    tp = min(t_len, 512)
    kv0 = UB_KV // NSA_KV_W
    spec = lambda c: pl.BlockSpec((1, tp, NSA_KV_W), lambda bi, i: (bi, i, c))
    o_spec = pl.BlockSpec((1, tp, NSA_KV_W), lambda bi, i: (bi, i, 0))
    out_specs = [o_spec] * 2
    out_shape = [jax.ShapeDtypeStruct((b, t_len, NSA_KV_W), F32)] * 2
    if attn_operands:
        t_spec = pl.BlockSpec((1, tp // LANE, NSA_KV_W, LANE), lambda bi, i: (bi, i, 0, 0))
        k16 = jax.ShapeDtypeStruct((b, t_len, NSA_KV_W), BF16)
        v16 = jax.ShapeDtypeStruct((b, t_len // LANE, NSA_KV_W, LANE), BF16)
        out_specs += [o_spec, t_spec, o_spec, t_spec]
        out_shape += [k16, v16, k16, v16]
    return pl.pallas_call(
        _prep_body,
        grid=(b, t_len // tp),
        in_specs=[spec(kv0 + 2), spec(kv0 + 3), spec(kv0 + 4), spec(kv0 + 5),
                  pl.BlockSpec((3, HEAD_DIM), lambda bi, i: (0, 0))],
        out_specs=out_specs,
        out_shape=out_shape,
        compiler_params=_cparams(("arbitrary", "arbitrary")),
        name="nsa_prep",
    )(ub, ub, ub, ub, k_norm_g)


def _sel_matrix(nc, nb, ncp, nbp):
    j = np.arange(nb)
    lo = np.clip((SEL_BLOCK * j - CMP_LEN) // CMP_STRIDE + 1, 0, nc)
    hi = np.clip(-(-(SEL_BLOCK * (j + 1)) // CMP_STRIDE), 0, nc)
    n = np.arange(ncp)[:, None]
    m = np.zeros((ncp, nbp), np.float32)
    m[:, :nb] = (n >= lo[None, :]) & (n < hi[None, :])
    return jnp.asarray(m)


def _cmp_body(q_ref, kc_ref, vc_ref, bias_ref, mselt_ref, qg_ref, gate_ref, ocmpt_ref, nsel_ref, *, tq, nb):
    qi = pl.program_id(1)
    nbp = -(-nb // 8) * 8
    gates_t = jax.nn.sigmoid(gate_ref[0]).T
    blk = lax.broadcasted_iota(jnp.int32, (nbp, tq), 0)
    qblk = (qi * tq + lax.broadcasted_iota(jnp.int32, (nbp, tq), 1)) >> 6
    scale = HEAD_DIM ** -0.5
    for g in range(NSA_G):
        gsl = slice(g * HEAD_DIM, (g + 1) * HEAD_DIM)
        kc = kc_ref[0, :, gsl]
        vct = vc_ref[0, :, gsl].T.astype(BF16)
        imp = jnp.zeros((kc.shape[0], tq), F32)
        heads = [g * NSA_HPG + hh for hh in range(NSA_HPG)]
        kc2 = _split2(kc)
        scores = []
        for h in heads:
            q2 = _split2(_rms(q_ref[0, :, h * HEAD_DIM:(h + 1) * HEAD_DIM], qg_ref[...]) * scale)
            nt = lambda a, b: lax.dot_general(a, b, (((1,), (1,)), ((), ())), preferred_element_type=F32)
            scores.append(nt(kc2[0], q2[0]) + (nt(kc2[0], q2[1]) + nt(kc2[1], q2[0])))
        probs = []
        for h, s in zip(heads, scores):
            bias = bias_ref[h]
            s = s + bias
            e = jnp.exp(s - jnp.max(s, axis=0, keepdims=True))
            p = e * (1.0 / jnp.sum(e, axis=0, keepdims=True)) * jnp.where(bias > 0.5 * NEG, 1.0, 0.0)
            probs.append(p)
            imp = imp + p
        for h, p in zip(heads, probs):
            hsl = slice(h * HEAD_DIM, (h + 1) * HEAD_DIM)
            ocmpt_ref[0, hsl, :] = gates_t[h:h + 1, :] * _dot16(vct, p.astype(BF16))
        score = _cumsum_rows(mselt_ref[0:nbp, :].astype(BF16), imp)
        score = jnp.where(blk == 0, BIG, score)
        score = jnp.where(blk == qblk, BIG, score)
        score = jnp.where(blk == qblk - 1, BIG, score)
        score = jnp.where(blk <= qblk, score, -BIG)
        rank = jnp.zeros((nbp, tq), F32)
        for i in range(nb):
            si = score[i:i + 1, :]
            ge = jnp.where(si >= score, 1.0, 0.0)
            gt = jnp.where(si > score, 1.0, 0.0)
            rank = rank + jnp.where(blk > i, ge, gt)
        nsel_t = jnp.where(rank < SEL_TOP, 0.0, jnp.where(blk < nb, 1.0, 0.0))
        nsel = jnp.concatenate([nsel_t, jnp.zeros((LANE - nbp, tq), F32)], axis=0).T
        nsel_ref[0, :, gsl] = nsel.astype(BF16)


def _cmp_select_prompt(ub, kcmp, vcmp, bias_cmp, q_norm_g):
    b, t_len, _ = ub.shape
    tq = 128
    nb = t_len // SEL_BLOCK
    ncp = kcmp.shape[1]
    nc = (t_len - CMP_LEN) // CMP_STRIDE + 1
    assert nb <= LANE and ncp == LANE
    body = functools.partial(_cmp_body, tq=tq, nb=nb)
    return pl.pallas_call(
        body,
        grid=(b, t_len // tq),
        in_specs=[
            pl.BlockSpec((1, tq, NSA_Q_W), lambda bi, i: (bi, i, UB_Q // NSA_Q_W)),
            pl.BlockSpec((1, ncp, NSA_KV_W), lambda bi, i: (bi, 0, 0)),
            pl.BlockSpec((1, ncp, NSA_KV_W), lambda bi, i: (bi, 0, 0)),
            pl.BlockSpec((NSA_HEADS, ncp, tq), lambda bi, i: (0, 0, i)),
            pl.BlockSpec((LANE, ncp), lambda bi, i: (0, 0)),
            pl.BlockSpec((1, HEAD_DIM), lambda bi, i: (0, 0)),
            pl.BlockSpec((1, tq, LANE), lambda bi, i: (bi, i, UB_G // LANE)),
        ],
        out_specs=[
            pl.BlockSpec((1, NSA_Q_W, tq), lambda bi, i: (bi, 0, i)),
            pl.BlockSpec((1, tq, NSA_G * LANE), lambda bi, i: (bi, i, 0)),
        ],
        out_shape=[jax.ShapeDtypeStruct((b, NSA_Q_W, t_len), F32),
                   jax.ShapeDtypeStruct((b, t_len, NSA_G * LANE), BF16)],
        compiler_params=_cparams(("arbitrary", "arbitrary")),
        name="cmp_select",
    )(ub, kcmp, vcmp, jnp.swapaxes(bias_cmp, 1, 2), _sel_matrix(nc, nb, ncp, LANE).T, q_norm_g.reshape(1, HEAD_DIM), ub)


def _attn_body(q_ref, ocmp_ref, nsel_ref, ks_ref, vst_ref, kw_ref, vwt_ref, epen_ref, bt_ref, cb_ref, gate_ref, z_ref, qg_ref,
               o_ref, m_s, l_s, acc_s, osw_s, *, tq):
    qi = pl.program_id(1)
    t0 = qi * tq
    rows = NSA_HPG * tq
    tk = LANE
    tok = t0 + (lax.broadcasted_iota(jnp.int32, (tk, rows), 1) & (tq - 1))
    key = lax.broadcasted_iota(jnp.int32, (tk, rows), 0)
    gates_t = jax.nn.sigmoid(gate_ref[0]).T
    scale = HEAD_DIM ** -0.5
    j_near = jnp.maximum(qi + 1 - bt_ref.shape[0], 0)

    groups = range(NSA_G)
    gsl = [slice(g * HEAD_DIM, (g + 1) * HEAD_DIM) for g in groups]
    heads = [[g * NSA_HPG + hh for hh in range(NSA_HPG)] for g in groups]
    q_sel, q_win = [], []
    for g in groups:
        q4 = jnp.concatenate(
            [(_rms(q_ref[0, :, h * HEAD_DIM:(h + 1) * HEAD_DIM], qg_ref[...]) * scale).astype(BF16) for h in heads[g]], axis=0)
        ns = nsel_ref[0, :, g * LANE:(g + 1) * LANE].astype(F32)
        aug_sel = jnp.concatenate([ns + cb_ref[h:h + 1, :] for h in heads[g]], axis=0).astype(BF16)
        aug_win = jnp.concatenate([jnp.broadcast_to(cb_ref[h:h + 1, :], (tq, LANE)) for h in heads[g]], axis=0).astype(BF16)
        q_sel.append(jnp.concatenate([q4, aug_sel], axis=1))
        q_win.append(jnp.concatenate([q4, aug_win], axis=1))

    def run(k_ref, vt_ref, qmats, lo, window):
        m_s[...] = jnp.full(m_s.shape, NEG, F32)
        l_s[...] = jnp.zeros(l_s.shape, F32)
        acc_s[...] = jnp.zeros(acc_s.shape, F32)

        def step(j, near):
            kb = pl.multiple_of(j * tk, tk)
            kt = k_ref[0, pl.ds(kb, tk), :]
            ep = epen_ref[pl.ds(kb, tk), :]
            ss = [lax.dot_general(jnp.concatenate([kt[:, gsl[g]], ep], axis=1), qmats[g], (((1,), (1,)), ((), ())),
                                  preferred_element_type=F32) for g in groups]
            rel = tok - (kb + key)
            vt = vt_ref[0, j]
            for g in groups:
                s = ss[g]
                if near:
                    s = jnp.where(rel >= 0, s + bt_ref[qi - j, :, g * rows:(g + 1) * rows], NEG)
                elif window:
                    s = jnp.where(rel < WINDOW, s, NEG)
                m_old = m_s[g]
                m_new = jnp.maximum(m_old, jnp.max(s, axis=0, keepdims=True))
                alpha = jnp.exp(m_old - m_new)
                p = jnp.exp(s - m_new)
                l_s[g] = alpha * l_s[g] + jnp.sum(p, axis=0, keepdims=True)
                acc_s[g] = alpha * acc_s[g] + jnp.dot(vt[gsl[g], :], p.astype(BF16), preferred_element_type=F32)
                m_s[g] = m_new

        def far_step(j, carry):
            step(j, False)
            return carry

        def near_step(j, carry):
            step(j, True)
            return carry

        lax.fori_loop(lo, j_near, far_step, 0)
        lax.fori_loop(j_near, qi + 1, near_step, 0)

    def gate_rows(branch, g):
        return jnp.concatenate([gates_t[branch * NSA_HEADS + h:branch * NSA_HEADS + h + 1, :] for h in heads[g]], axis=1)

    run(ks_ref, vst_ref, q_sel, 0, False)
    for g in groups:
        osw_s[g] = acc_s[g] * (gate_rows(1, g) / l_s[g])
    run(kw_ref, vwt_ref, q_win, jnp.maximum(qi - WINDOW // tk, 0), True)
    for g in groups:
        osw = osw_s[g] + acc_s[g] * (gate_rows(2, g) / l_s[g])
        for hh, h in enumerate(heads[g]):
            hsl = slice(h * HEAD_DIM, (h + 1) * HEAD_DIM)
            o = (ocmp_ref[0, hsl, :] + osw[:, hh * tq:(hh + 1) * tq]).T
            o_ref[0, :, hsl] = (o * _silu(z_ref[0, :, hsl])).astype(o_ref.dtype)


def _far_bias(rel_bias):
    return _bias_of(rel_bias, jnp.full((1,), REL_MAX_DIST, jnp.int32))[:, 0]


def _toeplitz_bias_t(rel_bias, tq, n_near):
    d = jnp.arange(n_near, dtype=jnp.int32)[:, None, None]
    c = jnp.arange(LANE, dtype=jnp.int32)[None, :, None]
    r = jnp.arange(tq, dtype=jnp.int32)[None, None, :]
    bt = _bias_of(rel_bias, d * LANE + r - c) - _far_bias(rel_bias)[:, None, None, None]
    return jnp.transpose(bt, (1, 2, 0, 3)).reshape(n_near, LANE, NSA_HEADS * tq)


AUG_BIAS_LANES = (LANE - 2, LANE - 1)


def _far_bias_columns(rel_bias):
    c = _far_bias(rel_bias)
    hi = c.astype(BF16).astype(F32)
    lo = (c - hi).astype(BF16).astype(F32)
    return jnp.zeros((NSA_HEADS, LANE), F32).at[:, AUG_BIAS_LANES[0]].set(hi).at[:, AUG_BIAS_LANES[1]].set(lo)


def _block_penalty(t_len):
    key = np.arange(t_len)[:, None]
    j = np.arange(LANE)[None, :]
    pen = np.where(key // SEL_BLOCK == j, NEG, 0.0)
    pen[:, AUG_BIAS_LANES[0]:] = 1.0
    return jnp.asarray(pen, BF16)


def _attn_prompt(ub, ocmp, nsel, ksb, vst, kwb, vwt, rel_bias, q_norm_g):
    b, t_len, _ = ub.shape
    tq = 128
    rows = NSA_HPG * tq
    assert t_len // SEL_BLOCK <= AUG_BIAS_LANES[0]
    n_near = -(-(REL_MAX_DIST + LANE - 1) // LANE)
    bt = _toeplitz_bias_t(rel_bias, tq, n_near)
    body = functools.partial(_attn_body, tq=tq)
    k_spec = pl.BlockSpec((1, t_len, NSA_KV_W), lambda bi, i: (bi, 0, 0))
    v_spec = pl.BlockSpec((1, t_len // LANE, NSA_KV_W, LANE), lambda bi, i: (bi, 0, 0, 0))
    return pl.pallas_call(
        body,
        grid=(b, t_len // tq),
        in_specs=[
            pl.BlockSpec((1, tq, NSA_Q_W), lambda bi, i: (bi, i, UB_Q // NSA_Q_W)),
            pl.BlockSpec((1, NSA_Q_W, tq), lambda bi, i: (bi, 0, i)),
            pl.BlockSpec((1, tq, NSA_G * LANE), lambda bi, i: (bi, i, 0)),
            k_spec, v_spec, k_spec, v_spec,
            pl.BlockSpec((t_len, LANE), lambda bi, i: (0, 0)),
            pl.BlockSpec(bt.shape, lambda bi, i: (0, 0, 0)),
            pl.BlockSpec((NSA_HEADS, LANE), lambda bi, i: (0, 0)),
            pl.BlockSpec((1, tq, LANE), lambda bi, i: (bi, i, UB_G // LANE)),
            pl.BlockSpec((1, tq, NSA_Q_W), lambda bi, i: (bi, i, UB_Z // NSA_Q_W)),
            pl.BlockSpec((1, HEAD_DIM), lambda bi, i: (0, 0)),
        ],
        out_specs=pl.BlockSpec((1, tq, NSA_Q_W), lambda bi, i: (bi, i, 0)),
        out_shape=jax.ShapeDtypeStruct((b, t_len, NSA_Q_W), BF16),
        scratch_shapes=[pltpu.VMEM((NSA_G, 1, rows), F32), pltpu.VMEM((NSA_G, 1, rows), F32),
                        pltpu.VMEM((NSA_G, HEAD_DIM, rows), F32), pltpu.VMEM((NSA_G, HEAD_DIM, rows), F32)],
        compiler_params=_cparams(("arbitrary", "arbitrary")),
        name="nsa_attn",
    )(ub, ocmp, nsel, ksb, vst, kwb, vwt, _block_penalty(t_len), bt, _far_bias_columns(rel_bias), ub, ub,
      q_norm_g.reshape(1, HEAD_DIM))


def _q_all(q_ref, qg_ref, ts):
    scale = HEAD_DIM ** -0.5
    zero = jnp.zeros((NSA_HPG * ts, HEAD_DIM), BF16)
    blocks = []
    for g in range(NSA_G):
        q4 = jnp.concatenate(
            [(_rms(q_ref[0, :, (g * NSA_HPG + hh) * HEAD_DIM:(g * NSA_HPG + hh + 1) * HEAD_DIM], qg_ref[...]) * scale).astype(BF16)
             for hh in range(NSA_HPG)], axis=0)
        blocks.append(jnp.concatenate([q4 if gg == g else zero for gg in range(NSA_G)], axis=1))
    return jnp.concatenate(blocks, axis=0)


def _row_to_col(row, n):
    eye = lax.broadcasted_iota(jnp.int32, (n, n), 0) == lax.broadcasted_iota(jnp.int32, (n, n), 1)
    return jnp.sum(jnp.where(eye, row, 0.0), axis=1, keepdims=True)


def _cmp_sample_body(q_ref, kc_ref, vc_ref, bias_ref, msel_ref, rep_ref, qg_ref, ocmp_ref, selt_ref, *, ts, nb, nbp):
    scale = HEAD_DIM ** -0.5
    ncp = kc_ref.shape[1]
    scores = []
    for g in range(NSA_G):
        gsl = slice(g * HEAD_DIM, (g + 1) * HEAD_DIM)
        q4 = jnp.concatenate(
            [_rms(q_ref[0, :, (g * NSA_HPG + hh) * HEAD_DIM:(g * NSA_HPG + hh + 1) * HEAD_DIM], qg_ref[...]) * scale
             for hh in range(NSA_HPG)], axis=0)
        bias = bias_ref[g * NSA_HPG:(g + 1) * NSA_HPG].reshape(NSA_HPG * ts, ncp)
        s = _mm_nt_hi(q4, kc_ref[0, :, gsl]) + bias
        e = jnp.exp(s - jnp.max(s, axis=-1, keepdims=True))
        p = e / jnp.sum(e, axis=-1, keepdims=True) * jnp.where(bias > 0.5 * NEG, 1.0, 0.0)
        o = _mm(p, vc_ref[0, :, gsl])
        imp = p[0:ts]
        for hh in range(NSA_HPG):
            h = g * NSA_HPG + hh
            ocmp_ref[0, :, h * HEAD_DIM:(h + 1) * HEAD_DIM] = o[hh * ts:(hh + 1) * ts]
            if hh:
                imp = imp + p[hh * ts:(hh + 1) * ts]
        scores.append(_mm_hi(imp, msel_ref[...]))
    score = jnp.concatenate(scores, axis=0)
    rows = NSA_G * ts
    lane = lax.broadcasted_iota(jnp.int32, (rows, nbp), 1)
    tok = lax.broadcasted_iota(jnp.int32, (rows, nbp), 0) & (ts - 1)
    qblk = (PAST_LEN + tok) >> 6
    forced = (lane == 0) | (lane == qblk) | (lane == qblk - 1)
    score = jnp.where(forced, BIG, score)
    score = jnp.where(lane <= qblk, score, -BIG)
    sel = jnp.zeros((rows, nbp), F32)
    lane_f = lane.astype(F32)
    for _ in range(SEL_TOP):
        mx = jnp.max(score, axis=-1, keepdims=True)
        first = jnp.min(jnp.where(score == mx, lane_f, float(nbp)), axis=-1, keepdims=True)
        pick = lane_f == first
        sel = jnp.where(pick, 1.0, sel)
        score = jnp.where(pick, -3e38, score)
    selt_ref[0] = _mm_tn(sel, rep_ref[...])


def _cmp_select_sample(ub, kcmp, vcmp, bias_cmp, q_norm_g, nb):
    b, ts, _ = ub.shape
    ncp = kcmp.shape[1]
    nc = ncp - 1
    nbp = -(-nb // LANE) * LANE
    assert ts & (ts - 1) == 0 and NSA_HEADS * ts == LANE
    rep = np.zeros((NSA_G * ts, LANE), np.float32)
    for g in range(NSA_G):
        for hh in range(NSA_HPG):
            for t in range(ts):
                rep[g * ts + t, (g * NSA_HPG + hh) * ts + t] = 1.0
    body = functools.partial(_cmp_sample_body, ts=ts, nb=nb, nbp=nbp)
    return pl.pallas_call(
        body,
        grid=(b,),
        in_specs=[
            pl.BlockSpec((1, ts, NSA_Q_W), lambda bi: (bi, 0, UB_Q // NSA_Q_W)),
            pl.BlockSpec((1, ncp, NSA_KV_W), lambda bi: (bi, 0, 0)),
            pl.BlockSpec((1, ncp, NSA_KV_W), lambda bi: (bi, 0, 0)),
            pl.BlockSpec((NSA_HEADS, ts, ncp), lambda bi: (0, 0, 0)),
            pl.BlockSpec((ncp, nbp), lambda bi: (0, 0)),
            pl.BlockSpec((NSA_G * ts, LANE), lambda bi: (0, 0)),
            pl.BlockSpec((1, HEAD_DIM), lambda bi: (0, 0)),
        ],
        out_specs=[
            pl.BlockSpec((1, ts, NSA_Q_W), lambda bi: (bi, 0, 0)),
            pl.BlockSpec((1, nbp, LANE), lambda bi: (bi, 0, 0)),
        ],
        out_shape=[jax.ShapeDtypeStruct((b, ts, NSA_Q_W), F32), jax.ShapeDtypeStruct((b, nbp, LANE), F32)],
        compiler_params=_cparams(("arbitrary",)),
        name="cmp_select_sample",
    )(ub, kcmp, vcmp, bias_cmp, _sel_matrix(nc, nb, ncp, nbp), jnp.asarray(rep), q_norm_g.reshape(1, HEAD_DIM))


def _sel_sample_body(pt_ref, *refs, pps, n_steps, ts):
    del pt_ref
    k_refs = refs[:pps]
    v_refs = refs[pps:2 * pps]
    (q_ref, selt_ref, knew_ref, vnew_ref, blast_ref, bfar_ref, bnew_ref, qg_ref,
     o_ref, qall_s, sc_s, snew_s, m_s, l_s, acc_s) = refs[2 * pps:]
    s = pl.program_id(1)
    n_pages = n_steps * pps
    sub = lax.broadcasted_iota(jnp.int32, (PAGE, LANE), 0)
    tok = lax.broadcasted_iota(jnp.int32, (ts, LANE), 1) & (ts - 1)
    new_t = lax.broadcasted_iota(jnp.int32, (ts, LANE), 0)

    @pl.when(s == 0)
    def _():
        qall = _q_all(q_ref, qg_ref, ts)
        qall_s[...] = qall
        sn = _mm_nt(knew_ref[0], qall) + bnew_ref[...]
        ok = (new_t <= tok) & (selt_ref[0, 2 * n_pages:2 * n_pages + 1, :] > 0.5)
        sn = jnp.where(ok, sn, NEG)
        snew_s[...] = sn
        m_s[...] = jnp.max(sn, axis=0, keepdims=True)
        l_s[...] = jnp.zeros((1, LANE), F32)
        acc_s[...] = jnp.zeros((LANE, NSA_KV_W), F32)

    @pl.when(s < n_steps)
    def _():
        m = m_s[...]
        for j in range(pps):
            p = s * pps + j
            st = _mm_nt(_load_page(k_refs[j], True), qall_s[...])
            st = st + jnp.where(p == n_pages - 1, blast_ref[...], bfar_ref[...])
            r0 = selt_ref[0, pl.ds(2 * p, 1), :]
            r1 = selt_ref[0, pl.ds(2 * p + 1, 1), :]
            ok = jnp.where(sub < SEL_BLOCK, r0, r1) > 0.5
            st = jnp.where(ok, st, NEG)
            sc_s[pl.ds(pl.multiple_of(p * PAGE, PAGE), PAGE), :] = st
            m = jnp.maximum(m, jnp.max(st, axis=0, keepdims=True))
        m_s[...] = m

    @pl.when(s >= n_steps)
    def _():
        m = m_s[...]
        l = l_s[...]
        acc = acc_s[...]
        for j in range(pps):
            p = (s - n_steps) * pps + j
            pt = jnp.exp(sc_s[pl.ds(pl.multiple_of(p * PAGE, PAGE), PAGE), :] - m)
            l = l + jnp.sum(pt, axis=0, keepdims=True)
            acc = acc + _mm_tn(pt, _load_page(v_refs[j], True))
        l_s[...] = l
        acc_s[...] = acc

    @pl.when(s == 2 * n_steps - 1)
    def _():
        pn = jnp.exp(snew_s[...] - m_s[...])
        l = l_s[...] + jnp.sum(pn, axis=0, keepdims=True)
        acc = acc_s[...] + _mm_tn(pn, vnew_ref[0])
        out = acc / _row_to_col(l, LANE)
        for h in range(NSA_HEADS):
            g = h // NSA_HPG
            o_ref[0, :, h * HEAD_DIM:(h + 1) * HEAD_DIM] = out[h * ts:(h + 1) * ts, g * HEAD_DIM:(g + 1) * HEAD_DIM]


def _lane_bias(rel_bias, key_pos, ts):
    rel = (PAST_LEN + jnp.arange(ts, dtype=jnp.int32))[None, :] - key_pos[:, None]
    return jnp.moveaxis(_bias_of(rel_bias, rel), 0, 1).reshape(key_pos.shape[0], NSA_HEADS * ts)


def _sel_sample(ub, selt, cache_k, cache_v, table, k_new, rel_bias, q_norm_g):
    b, ts, _ = ub.shape
    n_pages = table.shape[1]
    assert PAGE >= REL_MAX_DIST and n_pages * PAGE == PAST_LEN
    pps = min(PAGES_PER_STEP, n_pages)
    n_steps = n_pages // pps
    ar = jnp.arange
    blast = _lane_bias(rel_bias, PAST_LEN - PAGE + ar(PAGE, dtype=jnp.int32), ts)
    bfar = _lane_bias(rel_bias, jnp.zeros((1,), jnp.int32), ts)
    bnew = _lane_bias(rel_bias, PAST_LEN + ar(ts, dtype=jnp.int32), ts)
    body = functools.partial(_sel_sample_body, pps=pps, n_steps=n_steps, ts=ts)
    last = n_steps - 1
    page_block = (1, NSA_G * PAGE, HEAD_DIM)
    k_spec = lambda j: pl.BlockSpec(page_block, lambda bi, s, pt: (pt[bi, jnp.minimum(s, last) * pps + j], 0, 0))
    v_spec = lambda j: pl.BlockSpec(page_block, lambda bi, s, pt: (pt[bi, jnp.maximum(s - n_steps, 0) * pps + j], 0, 0))
    full = lambda shape: pl.BlockSpec(shape, lambda bi, s, pt: (0,) * len(shape))
    in_specs = [k_spec(j) for j in range(pps)] + [v_spec(j) for j in range(pps)] + [
        pl.BlockSpec((1, ts, NSA_Q_W), lambda bi, s, pt: (bi, 0, UB_Q // NSA_Q_W)),
        pl.BlockSpec((1,) + selt.shape[1:], lambda bi, s, pt: (bi, 0, 0)),
        pl.BlockSpec((1, ts, NSA_KV_W), lambda bi, s, pt: (bi, 0, 0)),
        pl.BlockSpec((1, ts, NSA_KV_W), lambda bi, s, pt: (bi, 0, UB_KV // NSA_KV_W + 3)),
        full((PAGE, LANE)), full((1, LANE)), full((ts, LANE)), full((1, HEAD_DIM)),
    ]
    return pl.pallas_call(
        body,
        grid_spec=pltpu.PrefetchScalarGridSpec(
            num_scalar_prefetch=1, grid=(b, 2 * n_steps), in_specs=in_specs,
            out_specs=pl.BlockSpec((1, ts, NSA_Q_W), lambda bi, s, pt: (bi, 0, 0)),
            scratch_shapes=[pltpu.VMEM((LANE, NSA_KV_W), BF16), pltpu.VMEM((PAST_LEN, LANE), F32),
                            pltpu.VMEM((ts, LANE), F32), pltpu.VMEM((1, LANE), F32), pltpu.VMEM((1, LANE), F32),
                            pltpu.VMEM((LANE, NSA_KV_W), F32)]),
        out_shape=jax.ShapeDtypeStruct((b, ts, NSA_Q_W), F32),
        compiler_params=_cparams(("arbitrary", "arbitrary")),
        name="sel_sample",
    )(table, *([cache_k] * pps), *([cache_v] * pps), ub, selt, k_new, ub, blast, bfar, bnew, q_norm_g.reshape(1, HEAD_DIM))


def _win_sample_body(q_ref, kc_ref, vc_ref, knew_ref, vnew_ref, bc_ref, bn_ref, gate_ref, z_ref, ocmp_ref, osel_ref, qg_ref,
                     o_ref, kout_ref, vout_ref, *, ts, wb):
    qall = _q_all(q_ref, qg_ref, ts)
    key_i = lax.broadcasted_iota(jnp.int32, (wb, LANE), 0)
    tok_c = lax.broadcasted_iota(jnp.int32, (wb, LANE), 1) & (ts - 1)
    tok_n = lax.broadcasted_iota(jnp.int32, (ts, LANE), 1) & (ts - 1)
    new_t = lax.broadcasted_iota(jnp.int32, (ts, LANE), 0)
    sc = _mm_nt(_load_page(kc_ref, True), qall) + bc_ref[...]
    sc = jnp.where(wb + tok_c - key_i < WINDOW, sc, NEG)
    sn = _mm_nt(knew_ref[0], qall) + bn_ref[...]
    sn = jnp.where(new_t <= tok_n, sn, NEG)
    m = jnp.maximum(jnp.max(sc, axis=0, keepdims=True), jnp.max(sn, axis=0, keepdims=True))
    pc = jnp.exp(sc - m)
    pn = jnp.exp(sn - m)
    l = jnp.sum(pc, axis=0, keepdims=True) + jnp.sum(pn, axis=0, keepdims=True)
    out = (_mm_tn(pc, _load_page(vc_ref, True)) + _mm_tn(pn, vnew_ref[0])) / _row_to_col(l, LANE)
    gates = jax.nn.sigmoid(gate_ref[0])
    for h in range(NSA_HEADS):
        g = h // NSA_HPG
        hsl = slice(h * HEAD_DIM, (h + 1) * HEAD_DIM)
        o_win = out[h * ts:(h + 1) * ts, g * HEAD_DIM:(g + 1) * HEAD_DIM]
        o = (gates[:, h:h + 1] * ocmp_ref[0, :, hsl] + gates[:, NSA_HEADS + h:NSA_HEADS + h + 1] * osel_ref[0, :, hsl]
             + gates[:, 2 * NSA_HEADS + h:2 * NSA_HEADS + h + 1] * o_win)
        o_ref[0, :, hsl] = (o * _silu(z_ref[0, :, hsl])).astype(o_ref.dtype)
    keep = (wb - ts) * NSA_G
    for out_ref, old_ref, new_ref in ((kout_ref, kc_ref, knew_ref), (vout_ref, vc_ref, vnew_ref)):
        out_ref[0, 0:keep, :] = old_ref[0, ts * NSA_G:wb * NSA_G, :]
        for g in range(NSA_G):
            out_ref[0, pl.ds(keep + g, ts, stride=NSA_G), :] = new_ref[0, :, g * HEAD_DIM:(g + 1) * HEAD_DIM]


def _win_sample(ub, cache_k, cache_v, k_new, ocmp, osel, rel_bias, q_norm_g):
    b, ts, _ = ub.shape
    wb = cache_k.shape[1] // NSA_G
    assert wb == WINDOW and ts % 8 == 0
    bc = _lane_bias(rel_bias, PAST_LEN - wb + jnp.arange(wb, dtype=jnp.int32), ts)
    bn = _lane_bias(rel_bias, PAST_LEN + jnp.arange(ts, dtype=jnp.int32), ts)
    body = functools.partial(_win_sample_body, ts=ts, wb=wb)
    tok_spec = lambda w, c: pl.BlockSpec((1, ts, w), lambda bi: (bi, 0, c))
    win_spec = pl.BlockSpec((1, wb * NSA_G, HEAD_DIM), lambda bi: (bi, 0, 0))
    return pl.pallas_call(
        body,
        grid=(b,),
        in_specs=[
            tok_spec(NSA_Q_W, UB_Q // NSA_Q_W), win_spec, win_spec,
            tok_spec(NSA_KV_W, 0), tok_spec(NSA_KV_W, UB_KV // NSA_KV_W + 5),
            pl.BlockSpec((wb, LANE), lambda bi: (0, 0)), pl.BlockSpec((ts, LANE), lambda bi: (0, 0)),
            tok_spec(LANE, UB_G // LANE), tok_spec(NSA_Q_W, UB_Z // NSA_Q_W),
            tok_spec(NSA_Q_W, 0), tok_spec(NSA_Q_W, 0),
            pl.BlockSpec((1, HEAD_DIM), lambda bi: (0, 0)),
        ],
        out_specs=[tok_spec(NSA_Q_W, 0), win_spec, win_spec],
        out_shape=[jax.ShapeDtypeStruct((b, ts, NSA_Q_W), BF16),
                   jax.ShapeDtypeStruct((b, wb * NSA_G, HEAD_DIM), F32),
                   jax.ShapeDtypeStruct((b, wb * NSA_G, HEAD_DIM), F32)],
        compiler_params=_cparams(("arbitrary",)),
        name="win_sample",
    )(ub, cache_k, cache_v, k_new, ub, bc, bn, ub, ub, ocmp, osel, q_norm_g.reshape(1, HEAD_DIM))


def _mix_body(oa_ref, ob_ref, wa_ref, wb_ref, ma_ref, mb_ref, o_ref):
    pa = jnp.dot(oa_ref[...], wa_ref[...], preferred_element_type=F32)
    pb = jnp.dot(ob_ref[...], wb_ref[...], preferred_element_type=F32)
    o_ref[...] = (jax.nn.sigmoid(ma_ref[...]) * pa + jax.nn.sigmoid(mb_ref[...]) * pb).astype(o_ref.dtype)


def _out_body(m_ref, w_ref, x_ref, y_ref):
    y_ref[...] = x_ref[...] + jnp.dot(m_ref[...], w_ref[...], preferred_element_type=F32)


def _output(x2d, oa, ob, ub2d, wa16, wb16, wo16):
    m = x2d.shape[0]
    tm = min(m, 512)
    tn = 512
    grid = (m // tm, D_MODEL // tn)
    mixed = pl.pallas_call(
        _mix_body,
        grid=grid,
        in_specs=[
            pl.BlockSpec((tm, GDN_V_W), lambda i, j: (i, 0)),
            pl.BlockSpec((tm, NSA_Q_W), lambda i, j: (i, 0)),
            pl.BlockSpec((GDN_V_W, tn), lambda i, j: (0, j)),
            pl.BlockSpec((NSA_Q_W, tn), lambda i, j: (0, j)),
            pl.BlockSpec((tm, tn), lambda i, j: (i, UB_MA // tn + j)),
            pl.BlockSpec((tm, tn), lambda i, j: (i, UB_MB // tn + j)),
        ],
        out_specs=pl.BlockSpec((tm, tn), lambda i, j: (i, j)),
        out_shape=jax.ShapeDtypeStruct((m, D_MODEL), BF16),
        compiler_params=_cparams(("arbitrary", "arbitrary")),
        name="branch_mix",
    )(oa, ob, wa16, wb16, ub2d, ub2d)
    return pl.pallas_call(
        _out_body,
        grid=grid,
        in_specs=[
            pl.BlockSpec((tm, D_MODEL), lambda i, j: (i, 0)),
            pl.BlockSpec((D_MODEL, tn), lambda i, j: (0, j)),
            pl.BlockSpec((tm, tn), lambda i, j: (i, j)),
        ],
        out_specs=pl.BlockSpec((tm, tn), lambda i, j: (i, j)),
        out_shape=jax.ShapeDtypeStruct((m, D_MODEL), F32),
        compiler_params=_cparams(("arbitrary", "arbitrary")),
        name="out_proj",
    )(mixed, wo16, x2d)


def _kv_slices(ub, rows=None):
    b, t_len, _ = ub.shape
    out = []
    for i in (0, 1, 3, 5):
        a = ub[:, :, UB_KV + i * NSA_KV_W:UB_KV + (i + 1) * NSA_KV_W]
        out.append(a.reshape(b, t_len, NSA_G, HEAD_DIM))
    return out


def kernel(x_prompt, x_sample, cache_k_cmp, cache_v_cmp, cache_k_sel, cache_v_sel, cache_k_win, cache_v_win, state_conv, state_gdn, page_table, norm_g, w_in, gdn_conv_w, gdn_a_log, gdn_dt_bias, gdn_norm_g, q_norm_g, k_norm_g, cmp_pe_k, cmp_w_k, cmp_proj_k, cmp_pe_v, cmp_w_v, cmp_proj_v, rel_bias, w_branch_a, w_branch_b, w_out):
    bp, tp, _ = x_prompt.shape
    bs, ts, _ = x_sample.shape
    wb = cache_k_win.shape[1]
    n_pool = cache_k_cmp.shape[0]
    kv4 = lambda a: a.reshape(a.shape[0], a.shape[1], NSA_G, HEAD_DIM)
    cmp_p = (cmp_pe_k, cmp_w_k, cmp_proj_k, cmp_pe_v, cmp_w_v, cmp_proj_v)

    assert _IN_MB + D_MODEL == w_in.shape[1]
    w_t = w_in.T
    wa16, wb16, wo16 = w_branch_a.astype(BF16), w_branch_b.astype(BF16), w_out.astype(BF16)

    def project(x):
        x2d = x.reshape(-1, D_MODEL)
        ua = _project(x2d, norm_g, w_t, UA_OFFSETS).reshape(x.shape[0], x.shape[1], UA_W)
        ub = _project(x2d, norm_g, w_t, UB_OFFSETS).reshape(x.shape[0], x.shape[1], UB_W)
        return x2d, ua, ub

    x2d, ua, ub = project(x_prompt)
    conv0 = jnp.zeros((bp, GDN_CONV - 1, GDN_QKV_W), F32)
    s0 = jnp.zeros((bp, GDN_V_HEADS, HEAD_DIM, HEAD_DIM), F32)
    o_a, p_gdn = _gdn(ua, ub, conv0, s0, gdn_conv_w, gdn_a_log, gdn_dt_bias, gdn_norm_g)
    p_conv = ua[:, tp - (GDN_CONV - 1):, :GDN_QKV_W]
    n_pages_p = tp // PAGE
    pages = ub.reshape(bp * n_pages_p, PAGE, UB_W)
    ident = jnp.arange(bp * n_pages_p, dtype=jnp.int32).reshape(bp, n_pages_p)
    kcmp, vcmp = _compress(pages, pages, UB_KV // NSA_KV_W, UB_KV // NSA_KV_W + 1, ident, *cmp_p, k_norm_g[0])
    p_ks, p_kw, ksb, vsb, kwb, vwb = _nsa_prep(ub, k_norm_g, True)
    nc_p = (tp - CMP_LEN) // CMP_STRIDE + 1
    bias_cmp = _cmp_bias(rel_bias, jnp.arange(tp, dtype=jnp.int32), kcmp.shape[1], nc_p)
    ocmp, nsel = _cmp_select_prompt(ub, kcmp, vcmp, bias_cmp, q_norm_g)
    o_b = _attn_prompt(ub, ocmp, nsel, ksb, vsb, kwb, vwb, rel_bias, q_norm_g)
    y_prompt = _output(x2d, o_a.reshape(-1, GDN_V_W), o_b.reshape(-1, NSA_Q_W), ub.reshape(-1, UB_W), wa16, wb16, wo16)
    p_kc, p_vc, p_vs, p_vw = _kv_slices(ub)
    prompt_out = (y_prompt.reshape(x_prompt.shape), p_kc, p_vc, kv4(p_ks), p_vs, kv4(p_kw[:, tp - wb:]), p_vw[:, tp - wb:],
                  p_conv, p_gdn)

    x2d, ua, ub = project(x_sample)
    o_a, s_gdn = _gdn(ua, ub, state_conv, state_gdn, gdn_conv_w, gdn_a_log, gdn_dt_bias, gdn_norm_g)
    s_conv = ua[:, ts - (GDN_CONV - 1):, :GDN_QKV_W]
    pool = lambda c: c.reshape(n_pool, PAGE * NSA_G, HEAD_DIM)
    kcmp, vcmp = _compress(pool(cache_k_cmp), pool(cache_v_cmp), 0, 0, page_table, *cmp_p, k_norm_g[0])
    s_ks, s_kwn = _nsa_prep(ub, k_norm_g, False)
    total = PAST_LEN + ts
    nc_s = (total - CMP_LEN) // CMP_STRIDE + 1
    nb_s = -(-total // SEL_BLOCK)
    assert nc_s + 1 == kcmp.shape[1]
    pos_s = PAST_LEN + jnp.arange(ts, dtype=jnp.int32)
    bias_cmp = _cmp_bias(rel_bias, pos_s, kcmp.shape[1], nc_s)
    ocmp, selt = _cmp_select_sample(ub, kcmp, vcmp, bias_cmp, q_norm_g, nb_s)
    osel = _sel_sample(ub, selt, pool(cache_k_sel), pool(cache_v_sel), page_table, s_ks, rel_bias, q_norm_g)
    win3 = lambda c: c.reshape(bs, wb * NSA_G, HEAD_DIM)
    o_b, s_kw, s_vw = _win_sample(ub, win3(cache_k_win), win3(cache_v_win), s_kwn, ocmp, osel, rel_bias, q_norm_g)
    y_sample = _output(x2d, o_a.reshape(-1, GDN_V_W), o_b.reshape(-1, NSA_Q_W), ub.reshape(-1, UB_W), wa16, wb16, wo16)
    s_kc, s_vc, s_vs, _ = _kv_slices(ub)
    sample_out = (y_sample.reshape(x_sample.shape), s_kc, s_vc, kv4(s_ks), s_vs, s_kw.reshape(cache_k_win.shape),
                  s_vw.reshape(cache_v_win.shape), s_conv, s_gdn)

    return (prompt_out[0], sample_out[0]) + prompt_out[1:] + sample_out[1:]
```

```python
import functools
import math

import jax
import jax.numpy as jnp
import numpy as np
from jax import lax
from jax.experimental import pallas as pl
from jax.experimental.pallas import tpu as pltpu

F32 = jnp.float32
BF16 = jnp.bfloat16
HI = lax.Precision.HIGHEST

D_MODEL = 2048
PAST_LEN = 16384
PAGE = 128

GDN_QK_HEADS = 16
GDN_V_HEADS = 32
HEAD_DIM = 128
GDN_CONV = 4
GDN_CHUNK = 64
GDN_GROUP = 8
GDN_QK_PER_STEP = 2
GDN_QK_W = GDN_QK_HEADS * HEAD_DIM
GDN_V_W = GDN_V_HEADS * HEAD_DIM
GDN_QKV_W = 2 * GDN_QK_W + GDN_V_W

NSA_HEADS = 16
NSA_G = 4
NSA_HPG = NSA_HEADS // NSA_G
NSA_Q_W = NSA_HEADS * HEAD_DIM
NSA_KV_W = NSA_G * HEAD_DIM
CMP_LEN = 32
CMP_STRIDE = 16
SEL_BLOCK = 64
SEL_TOP = 16
WINDOW = 512
REL_BUCKETS = 32
REL_MAX_DIST = 128

EPS = 1e-6
NEG = -1e30
BIG = 1e9

LANE = 128
VMEM_LIMIT = 56 * 1024 * 1024
PAGES_PER_STEP = 16

UA_W = GDN_QKV_W + GDN_V_W
UB_Q, UB_Z, UB_MA, UB_MB, UB_KV, UB_BA, UB_G = 0, 2048, 4096, 6144, 8192, 11264, 11776
UB_W = 12288
PROJ_TN = 512

_IN_QB = UA_W + 2 * GDN_V_HEADS
_IN_KV = _IN_QB + NSA_Q_W
_IN_G = _IN_KV + 6 * NSA_KV_W
_IN_ZB = _IN_G + 3 * NSA_HEADS
_IN_MA = _IN_ZB + NSA_Q_W
_IN_MB = _IN_MA + D_MODEL


def _tiles(start, width):
    return [start + k * PROJ_TN for k in range(width // PROJ_TN)]


UA_OFFSETS = _tiles(0, UA_W)
UB_OFFSETS = (_tiles(_IN_QB, NSA_Q_W) + _tiles(_IN_ZB, NSA_Q_W) + _tiles(_IN_MA, D_MODEL) + _tiles(_IN_MB, D_MODEL)
              + _tiles(_IN_KV, 6 * NSA_KV_W) + [UA_W, _IN_G])


def _mm(a, b):
    return jnp.dot(a.astype(BF16), b.astype(BF16), preferred_element_type=F32)


def _mm_nt(a, b):
    return lax.dot_general(a.astype(BF16), b.astype(BF16), (((1,), (1,)), ((), ())), preferred_element_type=F32)


def _mm_tn(a, b):
    return lax.dot_general(a.astype(BF16), b.astype(BF16), (((0,), (0,)), ((), ())), preferred_element_type=F32)


def _mm_hi(a, b):
    return jnp.dot(a, b, precision=HI, preferred_element_type=F32)


def _mm_nt_hi(a, b):
    return lax.dot_general(a, b, (((1,), (1,)), ((), ())), precision=HI, preferred_element_type=F32)


def _silu(x):
    return x * jax.nn.sigmoid(x)


def _softplus(x):
    return jnp.maximum(x, 0.0) + jnp.log1p(jnp.exp(-jnp.abs(x)))


def _rms(x, gain):
    return x * lax.rsqrt(jnp.mean(x * x, axis=-1, keepdims=True) + EPS) * gain


def _cparams(sem):
    return pltpu.CompilerParams(dimension_semantics=sem, vmem_limit_bytes=VMEM_LIMIT)


def _proj_body(offs_ref, x_ref, g_ref, w_ref, o_ref, h_ref):
    del offs_ref

    @pl.when(pl.program_id(1) == 0)
    def _():
        h_ref[...] = _rms(x_ref[...], g_ref[...]).astype(BF16)

    o_ref[...] = lax.dot_general(h_ref[...], w_ref[...].astype(BF16), (((1,), (1,)), ((), ())),
                                 preferred_element_type=F32)


def _project(x2d, norm_g, w_t, offsets):
    m = x2d.shape[0]
    tm = min(m, 1024)
    n_tiles = len(offsets)
    return pl.pallas_call(
        _proj_body,
        grid_spec=pltpu.PrefetchScalarGridSpec(
            num_scalar_prefetch=1,
            grid=(m // tm, n_tiles),
            in_specs=[
                pl.BlockSpec((tm, D_MODEL), lambda i, j, offs: (i, 0)),
                pl.BlockSpec((1, D_MODEL), lambda i, j, offs: (0, 0)),
                pl.BlockSpec((pl.Element(PROJ_TN), pl.Element(D_MODEL)), lambda i, j, offs: (pl.multiple_of(offs[j], 16), 0)),
            ],
            out_specs=pl.BlockSpec((tm, PROJ_TN), lambda i, j, offs: (i, j)),
            scratch_shapes=[pltpu.VMEM((tm, D_MODEL), BF16)]),
        out_shape=jax.ShapeDtypeStruct((m, n_tiles * PROJ_TN), F32),
        compiler_params=_cparams(("arbitrary", "arbitrary")),
        name="proj",
    )(jnp.asarray(offsets, jnp.int32), x2d, norm_g.reshape(1, D_MODEL), w_t)


def _split2(a):
    hi = a.astype(BF16)
    return hi, (a - hi.astype(F32)).astype(BF16)


def _dot16(a, b):
    return jnp.dot(a, b, preferred_element_type=F32)


def _mm_x3(a2, b2):
    return _dot16(a2[0], b2[0]) + (_dot16(a2[0], b2[1]) + _dot16(a2[1], b2[0]))


def _bdot(a, b):
    return lax.dot_general(a, b, (((2,), (1,)), ((0,), (0,))), preferred_element_type=F32)


def _bdot_nt(a, b):
    return lax.dot_general(a, b, (((2,), (2,)), ((0,), (0,))), preferred_element_type=F32)


def _bmm_x3(a2, b2):
    return _bdot(a2[0], b2[0]) + (_bdot(a2[0], b2[1]) + _bdot(a2[1], b2[0]))


def _unit_lower_inverse(nmat, c):
    row = lax.broadcasted_iota(jnp.int32, (1, c, c), 1)
    col = lax.broadcasted_iota(jnp.int32, (1, c, c), 2)
    p = jnp.where(row == col, 1.0, 0.0) + nmat
    m2 = _split2(nmat)
    span = 2
    while span < c:
        m2 = _split2(_bmm_x3(m2, m2))
        p = p + _bmm_x3(_split2(p), m2)
        span *= 2
    return p


def _unit_lower_inverse_pairs(nmat, eye, left, c):
    def blockdiag(x2):
        return tuple(jnp.concatenate([jnp.where(left, x, jnp.zeros_like(x)), jnp.where(left, jnp.zeros_like(x), x)], axis=1)
                     for x in x2)

    p = jnp.where(eye, 1.0, 0.0) + nmat
    m2 = _split2(nmat)
    bd2 = blockdiag(m2)
    span = 2
    while span < c:
        m2 = _split2(_bmm_x3(m2, bd2))
        bd2 = blockdiag(m2)
        p = p + _bmm_x3(_split2(p), bd2)
        span *= 2
    return p


def _cumsum_rows(tril16, g):
    h = g.astype(BF16)
    r = g - h.astype(F32)
    m = r.astype(BF16)
    l = (r - m.astype(F32)).astype(BF16)
    return _dot16(tril16, h) + (_dot16(tril16, m) + _dot16(tril16, l))


def _gdn_body(q_ref, k_ref, v_ref, z_ref, ba_ref, cwq_ref, cwk_ref, cwv_ref, csq_ref, csk_ref, csv_ref,
              s0_ref, gp_ref, ng_ref, o_ref, sfin_ref, cq_s, ck_s, cv_s, u_s, w_s, qg_s, kd_s, at_s, gl_s, st_s,
              *, t_len, c, nq):
    nv = 2 * nq
    first_head = nv * pl.program_id(1)
    n_chunks = t_len // c
    for cs, carry in ((csq_ref, cq_s), (csk_ref, ck_s), (csv_ref, cv_s)):
        carry[...] = jnp.zeros(carry.shape, F32)
        carry[5:8, :] = cs[0]
    st_s[...] = s0_ref[0]

    gsz = min(GDN_GROUP, n_chunks)
    rg = gsz * c
    row = lax.broadcasted_iota(jnp.int32, (1, c, 2 * c), 1)
    lane2 = lax.broadcasted_iota(jnp.int32, (1, c, 2 * c), 2)
    left = lane2 < c
    col = lane2 & (c - 1)
    tril = row >= col
    strict = row > col
    eye = row == col
    lane = lax.broadcasted_iota(jnp.int32, (rg, LANE), 1)
    rr = lax.broadcasted_iota(jnp.int32, (rg, rg), 0)
    cc = lax.broadcasted_iota(jnp.int32, (rg, rg), 1)
    shift = c.bit_length() - 1
    assert 1 << shift == c
    tril16 = jnp.where((rr >= cc) & ((rr >> shift) == (cc >> shift)), 1.0, 0.0).astype(BF16)

    def conv(src_ref, carry, cw_ref, rows):
        cur = src_ref[0, rows, :]
        win = jnp.concatenate([carry[...], cur], axis=0)
        carry[...] = cur[rg - 8:rg]
        w = cw_ref[...]
        a = win[5:5 + rg] * w[0:1] + win[6:6 + rg] * w[1:2] + win[7:7 + rg] * w[2:3] + win[8:8 + rg] * w[3:4]
        return _silu(a)

    def pick(x, idx):
        return jnp.sum(jnp.where(lane == idx, x, 0.0), axis=-1, keepdims=True).reshape(gsz, c, 1)

    def per_head(x, n):
        return [x[:, i * HEAD_DIM:(i + 1) * HEAD_DIM].reshape(gsz, c, HEAD_DIM) for i in range(n)]

    def prepare(gi):
        base = gi * rg if isinstance(gi, int) else pl.multiple_of(gi * rg, rg)
        rows = pl.ds(base, rg)
        qa = per_head(conv(q_ref, cq_s, cwq_ref, rows), nq)
        ka = per_head(conv(k_ref, ck_s, cwk_ref, rows), nq)
        va = per_head(conv(v_ref, cv_s, cwv_ref, rows), nv)
        qn = [x * lax.rsqrt(jnp.sum(x * x, axis=-1, keepdims=True) + EPS) * (HEAD_DIM ** -0.5) for x in qa]
        kn = [x * lax.rsqrt(jnp.sum(x * x, axis=-1, keepdims=True) + EPS) for x in ka]
        ba = ba_ref[0, rows, :]
        beta_all = jax.nn.sigmoid(ba)
        g_all = -jnp.exp(gp_ref[0:1, :]) * _softplus(ba + gp_ref[1:2, :])
        gc_all = _cumsum_rows(tril16, g_all)
        k16 = jnp.concatenate(kn, axis=0).astype(BF16)
        k2x = jnp.concatenate([k16, k16], axis=1)
        kk = _bdot_nt(k16, k2x)
        qk = _bdot_nt(jnp.concatenate(qn, axis=0).astype(BF16), k2x)
        betas, gcols, blocks = [], [], []
        zeros = jnp.zeros((gsz, c, 2 * HEAD_DIM), F32)
        for qi in range(nq):
            pair = []
            for l in range(2):
                hh = 2 * qi + l
                beta = pick(beta_all, first_head + hh)
                gcol = pick(gc_all, GDN_V_HEADS + first_head + hh)
                egc = jnp.exp(gcol)
                glast = gcol[:, c - 1:c, :]
                rhs = jnp.concatenate([va[hh] * beta, kn[qi] * (beta * egc)], axis=2)
                pair.append((beta, gcol, rhs))
                qg_s[hh, rows, :] = (qn[qi] * egc).reshape(rg, HEAD_DIM).astype(qg_s.dtype)
                kd_s[hh, rows, :] = (kn[qi] * jnp.exp(glast - gcol)).reshape(rg, HEAD_DIM).astype(kd_s.dtype)
                egl = jnp.exp(glast)
                for j in range(gsz):
                    gl_s[hh, gi * gsz + j] = jnp.broadcast_to(egl[j], (8, HEAD_DIM))
            betas.append(jnp.where(left, pair[0][0], pair[1][0]))
            gcols.append(jnp.where(left, pair[0][1], pair[1][1]))
            blocks.append(jnp.concatenate([jnp.concatenate([pair[0][2], zeros], axis=2),
                                           jnp.concatenate([zeros, pair[1][2]], axis=2)], axis=1))
        beta = jnp.concatenate(betas, axis=0)
        gcol = jnp.concatenate(gcols, axis=0)
        grow = jnp.sum(jnp.where(eye, gcol, 0.0), axis=1, keepdims=True)
        gamma = jnp.where(tril, jnp.exp(jnp.minimum(gcol - grow, 0.0)), 0.0)
        tinv = _unit_lower_inverse_pairs(jnp.where(strict, -(kk * beta * gamma), 0.0), eye, left, c)
        rhs2 = _split2(jnp.concatenate(blocks, axis=0))
        uw = _bmm_x3(_split2(tinv), rhs2)
        attn = qk * gamma
        for qi in range(nq):
            at_s[qi, rows, :] = attn[qi * gsz:(qi + 1) * gsz].reshape(rg, 2 * c).astype(at_s.dtype)
            for l in range(2):
                uw_h = uw[qi * gsz:(qi + 1) * gsz, :, 2 * l * HEAD_DIM:2 * (l + 1) * HEAD_DIM]
                u_s[2 * qi + l, rows, :] = uw_h[:, :, :HEAD_DIM].reshape(rg, HEAD_DIM)
                w_s[2 * qi + l, rows, :] = uw_h[:, :, HEAD_DIM:].reshape(rg, HEAD_DIM).astype(w_s.dtype)

    def recur(ci):
        rows = pl.ds(ci * c if isinstance(ci, int) else pl.multiple_of(ci * c, c), c)
        s = st_s[...]
        s16 = s.astype(BF16)
        v_new = u_s[:, rows, :] - _bdot(w_s[:, rows, :].astype(BF16), s16)
        v16 = v_new.astype(BF16)
        zv = jnp.zeros((c, HEAD_DIM), BF16)
        vbd = jnp.stack([jnp.concatenate([jnp.concatenate([v16[2 * qi], zv], axis=1),
                                          jnp.concatenate([zv, v16[2 * qi + 1]], axis=1)], axis=0) for qi in range(nq)], axis=0)
        o_state = _bdot(qg_s[:, rows, :].astype(BF16), s16)
        o_attn = _bdot(at_s[:, rows, :].astype(BF16), vbd)
        kd16 = kd_s[:, rows, :].astype(BF16)
        upd = jnp.stack([_mm_tn(kd16[hh], v16[hh]) for hh in range(nv)], axis=0)
        st_s[...] = s * gl_s[:, ci][:, 0:1, :] + upd
        for hh in range(nv):
            hsl = slice(hh * HEAD_DIM, (hh + 1) * HEAD_DIM)
            o = o_state[hh] + o_attn[hh // 2][:, (hh % 2) * HEAD_DIM:(hh % 2 + 1) * HEAD_DIM]
            o_ref[0, rows, hsl] = (_rms(o, ng_ref[...]) * _silu(z_ref[0, rows, hsl])).astype(o_ref.dtype)

    if n_chunks == 1:
        prepare(0)
        recur(0)
    else:
        def prepare_step(gi, carry):
            prepare(gi)
            return carry

        def recur_step(ci, carry):
            recur(ci)
            return carry

        lax.fori_loop(0, n_chunks // gsz, prepare_step, 0)
        lax.fori_loop(0, n_chunks, recur_step, 0)
    sfin_ref[0] = st_s[...]


def _gdn(ua, ub, conv_state, s0, conv_w, a_log, dt_bias, norm_g):
    b, t_len, _ = ua.shape
    c = min(GDN_CHUNK, t_len)
    hd = HEAD_DIM
    nq = GDN_QK_PER_STEP
    nv = 2 * nq
    steps = GDN_QK_HEADS // nq
    qw, vw = nq * hd, nv * hd
    op_dtype = BF16 if c % 16 == 0 else F32
    gp = jnp.zeros((2, LANE), F32)
    gp = gp.at[0, GDN_V_HEADS:2 * GDN_V_HEADS].set(a_log).at[1, GDN_V_HEADS:2 * GDN_V_HEADS].set(dt_bias)
    body = functools.partial(_gdn_body, t_len=t_len, c=c, nq=nq)
    return pl.pallas_call(
        body,
        grid=(b, steps),
        in_specs=[
            pl.BlockSpec((1, t_len, qw), lambda bi, i: (bi, 0, i)),
            pl.BlockSpec((1, t_len, qw), lambda bi, i: (bi, 0, steps + i)),
            pl.BlockSpec((1, t_len, vw), lambda bi, i: (bi, 0, steps + i)),
            pl.BlockSpec((1, t_len, vw), lambda bi, i: (bi, 0, 2 * steps + i)),
            pl.BlockSpec((1, t_len, LANE), lambda bi, i: (bi, 0, UB_BA // LANE)),
            pl.BlockSpec((GDN_CONV, qw), lambda bi, i: (0, i)),
            pl.BlockSpec((GDN_CONV, qw), lambda bi, i: (0, steps + i)),
            pl.BlockSpec((GDN_CONV, vw), lambda bi, i: (0, steps + i)),
            pl.BlockSpec((1, GDN_CONV - 1, qw), lambda bi, i: (bi, 0, i)),
            pl.BlockSpec((1, GDN_CONV - 1, qw), lambda bi, i: (bi, 0, steps + i)),
            pl.BlockSpec((1, GDN_CONV - 1, vw), lambda bi, i: (bi, 0, steps + i)),
            pl.BlockSpec((1, nv, hd, hd), lambda bi, i: (bi, i, 0, 0)),
            pl.BlockSpec((2, LANE), lambda bi, i: (0, 0)),
            pl.BlockSpec((1, hd), lambda bi, i: (0, 0)),
        ],
        out_specs=[
            pl.BlockSpec((1, t_len, vw), lambda bi, i: (bi, 0, i)),
            pl.BlockSpec((1, nv, hd, hd), lambda bi, i: (bi, i, 0, 0)),
        ],
        out_shape=[
            jax.ShapeDtypeStruct((b, t_len, GDN_V_W), BF16),
            jax.ShapeDtypeStruct((b, GDN_V_HEADS, hd, hd), F32),
        ],
        scratch_shapes=[
            pltpu.VMEM((8, qw), F32),
            pltpu.VMEM((8, qw), F32),
            pltpu.VMEM((8, vw), F32),
            pltpu.VMEM((nv, t_len, hd), F32),
            pltpu.VMEM((nv, t_len, hd), op_dtype),
            pltpu.VMEM((nv, t_len, hd), op_dtype),
            pltpu.VMEM((nv, t_len, hd), op_dtype),
            pltpu.VMEM((nq, t_len, 2 * c), op_dtype),
            pltpu.VMEM((nv, t_len // c, 8, hd), F32),
            pltpu.VMEM((nv, hd, hd), F32),
        ],
        compiler_params=_cparams(("arbitrary", "arbitrary")),
        name="gdn",
    )(ua, ua, ua, ua, ub, conv_w, conv_w, conv_w, conv_state, conv_state, conv_state, s0, gp, norm_g.reshape(1, hd))


def _t5_bucket(rel):
    n = jnp.maximum(rel, 0)
    exact = REL_BUCKETS // 2
    nf = jnp.maximum(n, 1).astype(F32)
    large = exact + (jnp.log(nf / exact) / math.log(REL_MAX_DIST / exact) * (REL_BUCKETS - exact)).astype(jnp.int32)
    return jnp.where(n < exact, n, jnp.minimum(large, REL_BUCKETS - 1))


def _bias_of(rel_bias, rel):
    bucket = _t5_bucket(rel)[None]
    table = rel_bias.astype(F32)
    out = jnp.zeros((NSA_HEADS,) + rel.shape, F32)
    for k in range(REL_BUCKETS):
        out = jnp.where(bucket == k, table[k].reshape((NSA_HEADS,) + (1,) * rel.ndim), out)
    return out


def _cmp_bias(rel_bias, q_pos, ncp, nc):
    n = jnp.arange(ncp, dtype=jnp.int32)
    rel = q_pos[:, None] - (n * CMP_STRIDE + (CMP_LEN - 1))[None, :]
    ok = (rel >= 0) & (n < nc)[None, :]
    return jnp.where(ok[None], _bias_of(rel_bias, rel), NEG)


def _load_page(ref, interleaved):
    if not interleaved:
        return ref[0]
    n_rows = ref.shape[1] // NSA_G
    return jnp.concatenate([ref[0, pl.ds(g, n_rows, stride=NSA_G), :] for g in range(NSA_G)], axis=1)


def _compress_body(pt_ref, *refs, pps, n_steps, nch, interleaved):
    del pt_ref
    k_refs = refs[:pps]
    v_refs = refs[pps:2 * pps]
    (wabk_ref, wabv_ref, wk_ref, pek_ref, wv_ref, pev_ref, projk_ref, projv_ref, gain_ref,
     outk_ref, outv_ref, ak_s, bk_s, av_s, bv_s) = refs[2 * pps:]
    s = pl.program_id(1)

    @pl.when(s == 0)
    def _():
        bk_s[nch:nch + 8, :] = jnp.zeros((8, NSA_KV_W), F32)
        bv_s[nch:nch + 8, :] = jnp.zeros((8, NSA_KV_W), F32)

    def pool(w2, page):
        w_hi, w_lo = w2
        p_hi, p_lo = _split2(page)
        r = _dot16(jnp.concatenate([w_hi, w_lo], axis=0), p_hi)
        return r[0:16] + (r[16:32] + _dot16(w_hi, p_lo))

    wk2 = _split2(wabk_ref[...])
    wv2 = _split2(wabv_ref[...])
    for j in range(pps):
        row0 = pl.multiple_of((s * pps + j) * 8, 8)
        abk = pool(wk2, _load_page(k_refs[j], interleaved))
        ak_s[pl.ds(row0, 8), :] = abk[0:8]
        bk_s[pl.ds(row0, 8), :] = abk[8:16]
        abv = pool(wv2, _load_page(v_refs[j], interleaved))
        av_s[pl.ds(row0, 8), :] = abv[0:8]
        bv_s[pl.ds(row0, 8), :] = abv[8:16]

    @pl.when(s == n_steps - 1)
    def _():
        cpe_k = jnp.sum(wk_ref[...] * pek_ref[...], axis=0, keepdims=True)
        cpe_v = jnp.sum(wv_ref[...] * pev_ref[...], axis=0, keepdims=True)
        projk2 = _split2(projk_ref[...])
        projv2 = _split2(projv_ref[...])
        rb = min(nch, 128)

        def fin(r, carry):
            r0 = pl.multiple_of(r * rb, rb)
            pk = ak_s[pl.ds(r0, rb), :] + bk_s[pl.ds(r0, rb + 8), :][1:rb + 1]
            pv = av_s[pl.ds(r0, rb), :] + bv_s[pl.ds(r0, rb + 8), :][1:rb + 1]
            for g in range(NSA_G):
                sl = slice(g * HEAD_DIM, (g + 1) * HEAD_DIM)
                yk = _mm_x3(_split2(pk[:, sl] + cpe_k), projk2)
                outk_ref[0, pl.ds(r0, rb), sl] = _rms(yk, gain_ref[...])
                outv_ref[0, pl.ds(r0, rb), sl] = _mm_x3(_split2(pv[:, sl] + cpe_v), projv2)
            return carry

        lax.fori_loop(0, nch // rb, fin, 0)


def _pool_weights(w):
    c = np.arange(8)[:, None]
    t = np.arange(PAGE)[None, :]
    off = t - CMP_STRIDE * c
    inside = (off >= 0) & (off < CMP_STRIDE)
    idx = np.clip(off, 0, CMP_STRIDE - 1)
    wa = jnp.where(inside, w[idx], 0.0)
    wb = jnp.where(inside, w[idx + CMP_STRIDE], 0.0)
    return jnp.concatenate([wa, wb], axis=0).astype(F32)


def _compress(pages_k, pages_v, col_k, col_v, table, pe_k, w_k, proj_k, pe_v, w_v, proj_v, k_gain):
    b, n_pages = table.shape
    pps = min(PAGES_PER_STEP, n_pages)
    n_steps = n_pages // pps
    nch = 8 * n_pages
    interleaved = pages_k.shape[1] == NSA_G * PAGE
    body = functools.partial(_compress_body, pps=pps, n_steps=n_steps, nch=nch, interleaved=interleaved)

    def page_spec(j, col):
        if interleaved:
            return pl.BlockSpec((1, NSA_G * PAGE, HEAD_DIM), lambda bi, s, pt: (pt[bi, s * pps + j], 0, 0))
        return pl.BlockSpec((1, PAGE, NSA_KV_W), lambda bi, s, pt: (pt[bi, s * pps + j], 0, col))

    full = lambda shape: pl.BlockSpec(shape, lambda bi, s, pt: (0,) * len(shape))
    in_specs = [page_spec(j, col_k) for j in range(pps)] + [page_spec(j, col_v) for j in range(pps)]
    in_specs += [full((16, PAGE)), full((16, PAGE)), full((CMP_LEN, 1)), full((CMP_LEN, HEAD_DIM)),
                 full((CMP_LEN, 1)), full((CMP_LEN, HEAD_DIM)), full((HEAD_DIM, HEAD_DIM)), full((HEAD_DIM, HEAD_DIM)),
                 full((1, HEAD_DIM))]
    out_spec = pl.BlockSpec((1, nch, NSA_KV_W), lambda bi, s, pt: (bi, 0, 0))
    return pl.pallas_call(
        body,
        grid_spec=pltpu.PrefetchScalarGridSpec(
            num_scalar_prefetch=1, grid=(b, n_steps), in_specs=in_specs, out_specs=[out_spec, out_spec],
            scratch_shapes=[pltpu.VMEM((nch, NSA_KV_W), F32), pltpu.VMEM((nch + 8, NSA_KV_W), F32),
                            pltpu.VMEM((nch, NSA_KV_W), F32), pltpu.VMEM((nch + 8, NSA_KV_W), F32)]),
        out_shape=[jax.ShapeDtypeStruct((b, nch, NSA_KV_W), F32)] * 2,
        compiler_params=_cparams(("arbitrary", "arbitrary")),
        name="compress",
    )(table, *([pages_k] * pps), *([pages_v] * pps), _pool_weights(w_k), _pool_weights(w_v),
      w_k.reshape(CMP_LEN, 1), pe_k, w_v.reshape(CMP_LEN, 1), pe_v, proj_k, proj_v, k_gain.reshape(1, HEAD_DIM))


def _prep_body(ks_ref, vs_ref, kw_ref, vw_ref, kg_ref, pks_ref, pkw_ref, *attn_refs):
    for g in range(NSA_G):
        sl = slice(g * HEAD_DIM, (g + 1) * HEAD_DIM)
        ksn = _rms(ks_ref[0, :, sl], kg_ref[1:2, :])
        kwn = _rms(kw_ref[0, :, sl], kg_ref[2:3, :])
        pks_ref[0, :, sl] = ksn
        pkw_ref[0, :, sl] = kwn
        if attn_refs:
            attn_refs[0][0, :, sl] = ksn.astype(BF16)
            attn_refs[2][0, :, sl] = kwn.astype(BF16)
    if attn_refs:
        for src, dst in ((vs_ref, attn_refs[1]), (vw_ref, attn_refs[3])):
            for j in range(dst.shape[1]):
                dst[0, j] = src[0, j * LANE:(j + 1) * LANE, :].T.astype(BF16)


def _nsa_prep(ub, k_norm_g, attn_operands):
    b, t_len, _ = ub.shape
    tp = min(t_len, 512)
    kv0 = UB_KV // NSA_KV_W
    spec = lambda c: pl.BlockSpec((1, tp, NSA_KV_W), lambda bi, i: (bi, i, c))
    o_spec = pl.BlockSpec((1, tp, NSA_KV_W), lambda bi, i: (bi, i, 0))
    out_specs = [o_spec] * 2
    out_shape = [jax.ShapeDtypeStruct((b, t_len, NSA_KV_W), F32)] * 2
    if attn_operands:
        t_spec = pl.BlockSpec((1, tp // LANE, NSA_KV_W, LANE), lambda bi, i: (bi, i, 0, 0))
        k16 = jax.ShapeDtypeStruct((b, t_len, NSA_KV_W), BF16)
        v16 = jax.ShapeDtypeStruct((b, t_len // LANE, NSA_KV_W, LANE), BF16)
        out_specs += [o_spec, t_spec, o_spec, t_spec]
        out_shape += [k16, v16, k16, v16]
    return pl.pallas_call(
        _prep_body,
        grid=(b, t_len // tp),
        in_specs=[spec(kv0 + 2), spec(kv0 + 3), spec(kv0 + 4), spec(kv0 + 5),
                  pl.BlockSpec((3, HEAD_DIM), lambda bi, i: (0, 0))],
        out_specs=out_specs,
        out_shape=out_shape,
        compiler_params=_cparams(("arbitrary", "arbitrary")),
        name="nsa_prep",
    )(ub, ub, ub, ub, k_norm_g)


def _sel_matrix(nc, nb, ncp, nbp):
    j = np.arange(nb)
    lo = np.clip((SEL_BLOCK * j - CMP_LEN) // CMP_STRIDE + 1, 0, nc)
    hi = np.clip(-(-(SEL_BLOCK * (j + 1)) // CMP_STRIDE), 0, nc)
    n = np.arange(ncp)[:, None]
    m = np.zeros((ncp, nbp), np.float32)
    m[:, :nb] = (n >= lo[None, :]) & (n < hi[None, :])
    return jnp.asarray(m)


def _cmp_body(q_ref, kc_ref, vc_ref, bias_ref, mselt_ref, qg_ref, gate_ref, ocmpt_ref, nsel_ref, *, tq, nb):
    qi = pl.program_id(1)
    nbp = -(-nb // 8) * 8
    gates_t = jax.nn.sigmoid(gate_ref[0]).T
    blk = lax.broadcasted_iota(jnp.int32, (nbp, tq), 0)
    qblk = (qi * tq + lax.broadcasted_iota(jnp.int32, (nbp, tq), 1)) >> 6
    scale = HEAD_DIM ** -0.5
    for g in range(NSA_G):
        gsl = slice(g * HEAD_DIM, (g + 1) * HEAD_DIM)
        kc = kc_ref[0, :, gsl]
        vct = vc_ref[0, :, gsl].T.astype(BF16)
        imp = jnp.zeros((kc.shape[0], tq), F32)
        heads = [g * NSA_HPG + hh for hh in range(NSA_HPG)]
        kc2 = _split2(kc)
        scores = []
        for h in heads:
            q2 = _split2(_rms(q_ref[0, :, h * HEAD_DIM:(h + 1) * HEAD_DIM], qg_ref[...]) * scale)
            nt = lambda a, b: lax.dot_general(a, b, (((1,), (1,)), ((), ())), preferred_element_type=F32)
            scores.append(nt(kc2[0], q2[0]) + (nt(kc2[0], q2[1]) + nt(kc2[1], q2[0])))
        probs = []
        for h, s in zip(heads, scores):
            bias = bias_ref[h]
            s = bias + s
            e = jnp.exp(s - jnp.max(s, axis=0, keepdims=True))
            p = e * (1.0 / jnp.sum(e, axis=0, keepdims=True)) * jnp.where(bias > 0.5 * NEG, 1.0, 0.0)
            probs.append(p)
            imp = imp + p
        for h, p in zip(heads, probs):
            hsl = slice(h * HEAD_DIM, (h + 1) * HEAD_DIM)
            ocmpt_ref[0, hsl, :] = gates_t[h:h + 1, :] * _dot16(vct, p.astype(BF16))
        score = _cumsum_rows(mselt_ref[0:nbp, :].astype(BF16), imp)
        score = jnp.where(blk == 0, BIG, score)
        score = jnp.where(blk == qblk, BIG, score)
        score = jnp.where(blk == qblk - 1, BIG, score)
        score = jnp.where(blk <= qblk, score, -BIG)
        rank = jnp.zeros((nbp, tq), F32)
        for i in range(nb):
            si = score[i:i + 1, :]
            ge = jnp.where(si >= score, 1.0, 0.0)
            gt = jnp.where(si > score, 1.0, 0.0)
            rank = rank + jnp.where(blk > i, ge, gt)
        nsel_t = jnp.where(rank < SEL_TOP, 0.0, jnp.where(blk < nb, 1.0, 0.0))
        nsel = jnp.concatenate([nsel_t, jnp.zeros((LANE - nbp, tq), F32)], axis=0).T
        nsel_ref[0, :, gsl] = nsel.astype(BF16)


def _cmp_select_prompt(ub, kcmp, vcmp, bias_cmp, q_norm_g):
    b, t_len, _ = ub.shape
    tq = 128
    nb = t_len // SEL_BLOCK
    ncp = kcmp.shape[1]
    nc = (t_len - CMP_LEN) // CMP_STRIDE + 1
    assert nb <= LANE and ncp == LANE
    body = functools.partial(_cmp_body, tq=tq, nb=nb)
    return pl.pallas_call(
        body,
        grid=(b, t_len // tq),
        in_specs=[
            pl.BlockSpec((1, tq, NSA_Q_W), lambda bi, i: (bi, i, UB_Q // NSA_Q_W)),
            pl.BlockSpec((1, ncp, NSA_KV_W), lambda bi, i: (bi, 0, 0)),
            pl.BlockSpec((1, ncp, NSA_KV_W), lambda bi, i: (bi, 0, 0)),
            pl.BlockSpec((NSA_HEADS, ncp, tq), lambda bi, i: (0, 0, i)),
            pl.BlockSpec((LANE, ncp), lambda bi, i: (0, 0)),
            pl.BlockSpec((1, HEAD_DIM), lambda bi, i: (0, 0)),
            pl.BlockSpec((1, tq, LANE), lambda bi, i: (bi, i, UB_G // LANE)),
        ],
        out_specs=[
            pl.BlockSpec((1, NSA_Q_W, tq), lambda bi, i: (bi, 0, i)),
            pl.BlockSpec((1, tq, NSA_G * LANE), lambda bi, i: (bi, i, 0)),
        ],
        out_shape=[jax.ShapeDtypeStruct((b, NSA_Q_W, t_len), F32),
                   jax.ShapeDtypeStruct((b, t_len, NSA_G * LANE), BF16)],
        compiler_params=_cparams(("arbitrary", "arbitrary")),
        name="cmp_select",
    )(ub, kcmp, vcmp, jnp.swapaxes(bias_cmp, 1, 2), _sel_matrix(nc, nb, ncp, LANE).T, q_norm_g.reshape(1, HEAD_DIM), ub)


def _attn_body(q_ref, ocmp_ref, nsel_ref, ks_ref, vst_ref, kw_ref, vwt_ref, epen_ref, bt_ref, cb_ref, gate_ref, z_ref, qg_ref,
               o_ref, m_s, l_s, acc_s, osw_s, *, tq):
    qi = pl.program_id(1)
    t0 = qi * tq
    rows = NSA_HPG * tq
    tk = LANE
    tok = t0 + (lax.broadcasted_iota(jnp.int32, (tk, rows), 1) & (tq - 1))
    key = lax.broadcasted_iota(jnp.int32, (tk, rows), 0)
    gates_t = jax.nn.sigmoid(gate_ref[0]).T
    scale = HEAD_DIM ** -0.5
    j_near = jnp.maximum(qi + 1 - bt_ref.shape[0], 0)

    groups = range(NSA_G)
    gsl = [slice(g * HEAD_DIM, (g + 1) * HEAD_DIM) for g in groups]
    heads = [[g * NSA_HPG + hh for hh in range(NSA_HPG)] for g in groups]
    q_sel, q_win = [], []
    for g in groups:
        q4 = jnp.concatenate(
            [(_rms(q_ref[0, :, h * HEAD_DIM:(h + 1) * HEAD_DIM], qg_ref[...]) * scale).astype(BF16) for h in heads[g]], axis=0)
        ns = nsel_ref[0, :, g * LANE:(g + 1) * LANE].astype(F32)
        aug_sel = jnp.concatenate([ns + cb_ref[h:h + 1, :] for h in heads[g]], axis=0).astype(BF16)
        aug_win = jnp.concatenate([jnp.broadcast_to(cb_ref[h:h + 1, :], (tq, LANE)) for h in heads[g]], axis=0).astype(BF16)
        q_sel.append(jnp.concatenate([q4, aug_sel], axis=1))
        q_win.append(jnp.concatenate([q4, aug_win], axis=1))

    def run(k_ref, vt_ref, qmats, lo, window):
        m_s[...] = jnp.full(m_s.shape, NEG, F32)
        l_s[...] = jnp.zeros(l_s.shape, F32)
        acc_s[...] = jnp.zeros(acc_s.shape, F32)

        def step(j, near):
            kb = pl.multiple_of(j * tk, tk)
            kt = k_ref[0, pl.ds(kb, tk), :]
            ep = epen_ref[pl.ds(kb, tk), :]
            ss = [lax.dot_general(jnp.concatenate([kt[:, gsl[g]], ep], axis=1), qmats[g], (((1,), (1,)), ((), ())),
                                  preferred_element_type=F32) for g in groups]
            rel = tok - (kb + key)
            vt = vt_ref[0, j]
            for g in groups:
                s = ss[g]
                if near:
                    s = jnp.where(rel >= 0, bt_ref[qi - j, :, g * rows:(g + 1) * rows] + s, NEG)
                elif window:
                    s = jnp.where(rel < WINDOW, s, NEG)
                m_old = m_s[g]
                m_new = jnp.maximum(m_old, jnp.max(s, axis=0, keepdims=True))
                alpha = jnp.exp(m_old - m_new)
                p = jnp.exp(s - m_new)
                l_s[g] = alpha * l_s[g] + jnp.sum(p, axis=0, keepdims=True)
                acc_s[g] = alpha * acc_s[g] + jnp.dot(vt[gsl[g], :], p.astype(BF16), preferred_element_type=F32)
                m_s[g] = m_new

        def far_step(j, carry):
            step(j, False)
            return carry

        def near_step(j, carry):
            step(j, True)
            return carry

        lax.fori_loop(lo, j_near, far_step, 0)
        lax.fori_loop(j_near, qi + 1, near_step, 0)

    def gate_rows(branch, g):
        return jnp.concatenate([gates_t[branch * NSA_HEADS + h:branch * NSA_HEADS + h + 1, :] for h in heads[g]], axis=1)

    run(ks_ref, vst_ref, q_sel, 0, False)
    for g in groups:
        osw_s[g] = acc_s[g] * (gate_rows(1, g) / l_s[g])
    run(kw_ref, vwt_ref, q_win, jnp.maximum(qi - WINDOW // tk, 0), True)
    for g in groups:
        osw = osw_s[g] + acc_s[g] * (gate_rows(2, g) / l_s[g])
        for hh, h in enumerate(heads[g]):
            hsl = slice(h * HEAD_DIM, (h + 1) * HEAD_DIM)
            o = (ocmp_ref[0, hsl, :] + osw[:, hh * tq:(hh + 1) * tq]).T
            o_ref[0, :, hsl] = (o * _silu(z_ref[0, :, hsl])).astype(o_ref.dtype)


def _far_bias(rel_bias):
    return _bias_of(rel_bias, jnp.full((1,), REL_MAX_DIST, jnp.int32))[:, 0]


def _toeplitz_bias_t(rel_bias, tq, n_near):
    d = jnp.arange(n_near, dtype=jnp.int32)[:, None, None]
    c = jnp.arange(LANE, dtype=jnp.int32)[None, :, None]
    r = jnp.arange(tq, dtype=jnp.int32)[None, None, :]
    bt = _bias_of(rel_bias, d * LANE + r - c) - _far_bias(rel_bias)[:, None, None, None]
    return jnp.transpose(bt, (1, 2, 0, 3)).reshape(n_near, LANE, NSA_HEADS * tq)


AUG_BIAS_LANES = (LANE - 2, LANE - 1)


def _far_bias_columns(rel_bias):
    c = _far_bias(rel_bias)
    hi = c.astype(BF16).astype(F32)
    lo = (c - hi).astype(BF16).astype(F32)
    return jnp.zeros((NSA_HEADS, LANE), F32).at[:, AUG_BIAS_LANES[0]].set(hi).at[:, AUG_BIAS_LANES[1]].set(lo)


def _block_penalty(t_len):
    key = np.arange(t_len)[:, None]
    j = np.arange(LANE)[None, :]
    pen = np.where(key // SEL_BLOCK == j, NEG, 0.0)
    pen[:, AUG_BIAS_LANES[0]:] = 1.0
    return jnp.asarray(pen, BF16)


def _attn_prompt(ub, ocmp, nsel, ksb, vst, kwb, vwt, rel_bias, q_norm_g):
    b, t_len, _ = ub.shape
    tq = 128
    rows = NSA_HPG * tq
    assert t_len // SEL_BLOCK <= AUG_BIAS_LANES[0]
    n_near = -(-(REL_MAX_DIST + LANE - 1) // LANE)
    bt = _toeplitz_bias_t(rel_bias, tq, n_near)
    body = functools.partial(_attn_body, tq=tq)
    k_spec = pl.BlockSpec((1, t_len, NSA_KV_W), lambda bi, i: (bi, 0, 0))
    v_spec = pl.BlockSpec((1, t_len // LANE, NSA_KV_W, LANE), lambda bi, i: (bi, 0, 0, 0))
    return pl.pallas_call(
        body,
        grid=(b, t_len // tq),
        in_specs=[
            pl.BlockSpec((1, tq, NSA_Q_W), lambda bi, i: (bi, i, UB_Q // NSA_Q_W)),
            pl.BlockSpec((1, NSA_Q_W, tq), lambda bi, i: (bi, 0, i)),
            pl.BlockSpec((1, tq, NSA_G * LANE), lambda bi, i: (bi, i, 0)),
            k_spec, v_spec, k_spec, v_spec,
            pl.BlockSpec((t_len, LANE), lambda bi, i: (0, 0)),
            pl.BlockSpec(bt.shape, lambda bi, i: (0, 0, 0)),
            pl.BlockSpec((NSA_HEADS, LANE), lambda bi, i: (0, 0)),
            pl.BlockSpec((1, tq, LANE), lambda bi, i: (bi, i, UB_G // LANE)),
            pl.BlockSpec((1, tq, NSA_Q_W), lambda bi, i: (bi, i, UB_Z // NSA_Q_W)),
            pl.BlockSpec((1, HEAD_DIM), lambda bi, i: (0, 0)),
        ],
        out_specs=pl.BlockSpec((1, tq, NSA_Q_W), lambda bi, i: (bi, i, 0)),
        out_shape=jax.ShapeDtypeStruct((b, t_len, NSA_Q_W), BF16),
        scratch_shapes=[pltpu.VMEM((NSA_G, 1, rows), F32), pltpu.VMEM((NSA_G, 1, rows), F32),
                        pltpu.VMEM((NSA_G, HEAD_DIM, rows), F32), pltpu.VMEM((NSA_G, HEAD_DIM, rows), F32)],
        compiler_params=_cparams(("arbitrary", "arbitrary")),
        name="nsa_attn",
    )(ub, ocmp, nsel, ksb, vst, kwb, vwt, _block_penalty(t_len), bt, _far_bias_columns(rel_bias), ub, ub,
      q_norm_g.reshape(1, HEAD_DIM))


def _q_all(q_ref, qg_ref, ts):
    scale = HEAD_DIM ** -0.5
    zero = jnp.zeros((NSA_HPG * ts, HEAD_DIM), BF16)
    blocks = []
    for g in range(NSA_G):
        q4 = jnp.concatenate(
            [(_rms(q_ref[0, :, (g * NSA_HPG + hh) * HEAD_DIM:(g * NSA_HPG + hh + 1) * HEAD_DIM], qg_ref[...]) * scale).astype(BF16)
             for hh in range(NSA_HPG)], axis=0)
        blocks.append(jnp.concatenate([q4 if gg == g else zero for gg in range(NSA_G)], axis=1))
    return jnp.concatenate(blocks, axis=0)


def _row_to_col(row, n):
    eye = lax.broadcasted_iota(jnp.int32, (n, n), 0) == lax.broadcasted_iota(jnp.int32, (n, n), 1)
    return jnp.sum(jnp.where(eye, row, 0.0), axis=1, keepdims=True)


def _cmp_sample_body(q_ref, kc_ref, vc_ref, bias_ref, msel_ref, rep_ref, qg_ref, ocmp_ref, selt_ref, *, ts, nb, nbp):
    scale = HEAD_DIM ** -0.5
    ncp = kc_ref.shape[1]
    scores = []
    for g in range(NSA_G):
        gsl = slice(g * HEAD_DIM, (g + 1) * HEAD_DIM)
        q4 = jnp.concatenate(
            [_rms(q_ref[0, :, (g * NSA_HPG + hh) * HEAD_DIM:(g * NSA_HPG + hh + 1) * HEAD_DIM], qg_ref[...]) * scale
             for hh in range(NSA_HPG)], axis=0)
        bias = bias_ref[g * NSA_HPG:(g + 1) * NSA_HPG].reshape(NSA_HPG * ts, ncp)
        s = _mm_nt_hi(q4, kc_ref[0, :, gsl]) + bias
        e = jnp.exp(s - jnp.max(s, axis=-1, keepdims=True))
        p = e / jnp.sum(e, axis=-1, keepdims=True) * jnp.where(bias > 0.5 * NEG, 1.0, 0.0)
        o = _mm(p, vc_ref[0, :, gsl])
        imp = p[0:ts]
        for hh in range(NSA_HPG):
            h = g * NSA_HPG + hh
            ocmp_ref[0, :, h * HEAD_DIM:(h + 1) * HEAD_DIM] = o[hh * ts:(hh + 1) * ts]
            if hh:
                imp = imp + p[hh * ts:(hh + 1) * ts]
        scores.append(_mm_hi(imp, msel_ref[...]))
    score = jnp.concatenate(scores, axis=0)
    rows = NSA_G * ts
    lane = lax.broadcasted_iota(jnp.int32, (rows, nbp), 1)
    tok = lax.broadcasted_iota(jnp.int32, (rows, nbp), 0) & (ts - 1)
    qblk = (PAST_LEN + tok) >> 6
    forced = (lane == 0) | (lane == qblk) | (lane == qblk - 1)
    score = jnp.where(forced, BIG, score)
    score = jnp.where(lane <= qblk, score, -BIG)
    sel = jnp.zeros((rows, nbp), F32)
    lane_f = lane.astype(F32)
    for _ in range(SEL_TOP):
        mx = jnp.max(score, axis=-1, keepdims=True)
        first = jnp.min(jnp.where(score == mx, lane_f, float(nbp)), axis=-1, keepdims=True)
        pick = lane_f == first
        sel = jnp.where(pick, 1.0, sel)
        score = jnp.where(pick, -3e38, score)
    selt_ref[0] = _mm_tn(sel, rep_ref[...])


def _cmp_select_sample(ub, kcmp, vcmp, bias_cmp, q_norm_g, nb):
    b, ts, _ = ub.shape
    ncp = kcmp.shape[1]
    nc = ncp - 1
    nbp = -(-nb // LANE) * LANE
    assert ts & (ts - 1) == 0 and NSA_HEADS * ts == LANE
    rep = np.zeros((NSA_G * ts, LANE), np.float32)
    for g in range(NSA_G):
        for hh in range(NSA_HPG):
            for t in range(ts):
                rep[g * ts + t, (g * NSA_HPG + hh) * ts + t] = 1.0
    body = functools.partial(_cmp_sample_body, ts=ts, nb=nb, nbp=nbp)
    return pl.pallas_call(
        body,
        grid=(b,),
        in_specs=[
            pl.BlockSpec((1, ts, NSA_Q_W), lambda bi: (bi, 0, UB_Q // NSA_Q_W)),
            pl.BlockSpec((1, ncp, NSA_KV_W), lambda bi: (bi, 0, 0)),
            pl.BlockSpec((1, ncp, NSA_KV_W), lambda bi: (bi, 0, 0)),
            pl.BlockSpec((NSA_HEADS, ts, ncp), lambda bi: (0, 0, 0)),
            pl.BlockSpec((ncp, nbp), lambda bi: (0, 0)),
            pl.BlockSpec((NSA_G * ts, LANE), lambda bi: (0, 0)),
            pl.BlockSpec((1, HEAD_DIM), lambda bi: (0, 0)),
        ],
        out_specs=[
            pl.BlockSpec((1, ts, NSA_Q_W), lambda bi: (bi, 0, 0)),
            pl.BlockSpec((1, nbp, LANE), lambda bi: (bi, 0, 0)),
        ],
        out_shape=[jax.ShapeDtypeStruct((b, ts, NSA_Q_W), F32), jax.ShapeDtypeStruct((b, nbp, LANE), F32)],
        compiler_params=_cparams(("arbitrary",)),
        name="cmp_select_sample",
    )(ub, kcmp, vcmp, bias_cmp, _sel_matrix(nc, nb, ncp, nbp), jnp.asarray(rep), q_norm_g.reshape(1, HEAD_DIM))


def _sel_sample_body(pt_ref, *refs, pps, n_steps, ts):
    del pt_ref
    k_refs = refs[:pps]
    v_refs = refs[pps:2 * pps]
    (q_ref, selt_ref, knew_ref, vnew_ref, blast_ref, bfar_ref, bnew_ref, qg_ref,
     o_ref, qall_s, sc_s, snew_s, m_s, l_s, acc_s) = refs[2 * pps:]
    s = pl.program_id(1)
    n_pages = n_steps * pps
    sub = lax.broadcasted_iota(jnp.int32, (PAGE, LANE), 0)
    tok = lax.broadcasted_iota(jnp.int32, (ts, LANE), 1) & (ts - 1)
    new_t = lax.broadcasted_iota(jnp.int32, (ts, LANE), 0)

    @pl.when(s == 0)
    def _():
        qall = _q_all(q_ref, qg_ref, ts)
        qall_s[...] = qall
        sn = _mm_nt(knew_ref[0], qall) + bnew_ref[...]
        ok = (new_t <= tok) & (selt_ref[0, 2 * n_pages:2 * n_pages + 1, :] > 0.5)
        sn = jnp.where(ok, sn, NEG)
        snew_s[...] = sn
        m_s[...] = jnp.max(sn, axis=0, keepdims=True)
        l_s[...] = jnp.zeros((1, LANE), F32)
        acc_s[...] = jnp.zeros((LANE, NSA_KV_W), F32)

    @pl.when(s < n_steps)
    def _():
        m = m_s[...]
        for j in range(pps):
            p = s * pps + j
            st = _mm_nt(_load_page(k_refs[j], True), qall_s[...])
            st = st + jnp.where(p == n_pages - 1, blast_ref[...], bfar_ref[...])
            r0 = selt_ref[0, pl.ds(2 * p, 1), :]
            r1 = selt_ref[0, pl.ds(2 * p + 1, 1), :]
            ok = jnp.where(sub < SEL_BLOCK, r0, r1) > 0.5
            st = jnp.where(ok, st, NEG)
            sc_s[pl.ds(pl.multiple_of(p * PAGE, PAGE), PAGE), :] = st
            m = jnp.maximum(m, jnp.max(st, axis=0, keepdims=True))
        m_s[...] = m

    @pl.when(s >= n_steps)
    def _():
        m = m_s[...]
        l = l_s[...]
        acc = acc_s[...]
        for j in range(pps):
            p = (s - n_steps) * pps + j
            pt = jnp.exp(sc_s[pl.ds(pl.multiple_of(p * PAGE, PAGE), PAGE), :] - m)
            l = l + jnp.sum(pt, axis=0, keepdims=True)
            acc = acc + _mm_tn(pt, _load_page(v_refs[j], True))
        l_s[...] = l
        acc_s[...] = acc

    @pl.when(s == 2 * n_steps - 1)
    def _():
        pn = jnp.exp(snew_s[...] - m_s[...])
        l = l_s[...] + jnp.sum(pn, axis=0, keepdims=True)
        acc = acc_s[...] + _mm_tn(pn, vnew_ref[0])
        out = acc / _row_to_col(l, LANE)
        for h in range(NSA_HEADS):
            g = h // NSA_HPG
            o_ref[0, :, h * HEAD_DIM:(h + 1) * HEAD_DIM] = out[h * ts:(h + 1) * ts, g * HEAD_DIM:(g + 1) * HEAD_DIM]


def _lane_bias(rel_bias, key_pos, ts):
    rel = (PAST_LEN + jnp.arange(ts, dtype=jnp.int32))[None, :] - key_pos[:, None]
    return jnp.moveaxis(_bias_of(rel_bias, rel), 0, 1).reshape(key_pos.shape[0], NSA_HEADS * ts)


def _sel_sample(ub, selt, cache_k, cache_v, table, k_new, rel_bias, q_norm_g):
    b, ts, _ = ub.shape
    n_pages = table.shape[1]
    assert PAGE >= REL_MAX_DIST and n_pages * PAGE == PAST_LEN
    pps = min(PAGES_PER_STEP, n_pages)
    n_steps = n_pages // pps
    ar = jnp.arange
    blast = _lane_bias(rel_bias, PAST_LEN - PAGE + ar(PAGE, dtype=jnp.int32), ts)
    bfar = _lane_bias(rel_bias, jnp.zeros((1,), jnp.int32), ts)
    bnew = _lane_bias(rel_bias, PAST_LEN + ar(ts, dtype=jnp.int32), ts)
    body = functools.partial(_sel_sample_body, pps=pps, n_steps=n_steps, ts=ts)
    last = n_steps - 1
    page_block = (1, NSA_G * PAGE, HEAD_DIM)
    k_spec = lambda j: pl.BlockSpec(page_block, lambda bi, s, pt: (pt[bi, jnp.minimum(s, last) * pps + j], 0, 0))
    v_spec = lambda j: pl.BlockSpec(page_block, lambda bi, s, pt: (pt[bi, jnp.maximum(s - n_steps, 0) * pps + j], 0, 0))
    full = lambda shape: pl.BlockSpec(shape, lambda bi, s, pt: (0,) * len(shape))
    in_specs = [k_spec(j) for j in range(pps)] + [v_spec(j) for j in range(pps)] + [
        pl.BlockSpec((1, ts, NSA_Q_W), lambda bi, s, pt: (bi, 0, UB_Q // NSA_Q_W)),
        pl.BlockSpec((1,) + selt.shape[1:], lambda bi, s, pt: (bi, 0, 0)),
        pl.BlockSpec((1, ts, NSA_KV_W), lambda bi, s, pt: (bi, 0, 0)),
        pl.BlockSpec((1, ts, NSA_KV_W), lambda bi, s, pt: (bi, 0, UB_KV // NSA_KV_W + 3)),
        full((PAGE, LANE)), full((1, LANE)), full((ts, LANE)), full((1, HEAD_DIM)),
    ]
    return pl.pallas_call(
        body,
        grid_spec=pltpu.PrefetchScalarGridSpec(
            num_scalar_prefetch=1, grid=(b, 2 * n_steps), in_specs=in_specs,
            out_specs=pl.BlockSpec((1, ts, NSA_Q_W), lambda bi, s, pt: (bi, 0, 0)),
            scratch_shapes=[pltpu.VMEM((LANE, NSA_KV_W), BF16), pltpu.VMEM((PAST_LEN, LANE), F32),
                            pltpu.VMEM((ts, LANE), F32), pltpu.VMEM((1, LANE), F32), pltpu.VMEM((1, LANE), F32),
                            pltpu.VMEM((LANE, NSA_KV_W), F32)]),
        out_shape=jax.ShapeDtypeStruct((b, ts, NSA_Q_W), F32),
        compiler_params=_cparams(("arbitrary", "arbitrary")),
        name="sel_sample",
    )(table, *([cache_k] * pps), *([cache_v] * pps), ub, selt, k_new, ub, blast, bfar, bnew, q_norm_g.reshape(1, HEAD_DIM))


def _win_sample_body(q_ref, kc_ref, vc_ref, knew_ref, vnew_ref, bc_ref, bn_ref, gate_ref, z_ref, ocmp_ref, osel_ref, qg_ref,
                     o_ref, kout_ref, vout_ref, *, ts, wb):
    qall = _q_all(q_ref, qg_ref, ts)
    key_i = lax.broadcasted_iota(jnp.int32, (wb, LANE), 0)
    tok_c = lax.broadcasted_iota(jnp.int32, (wb, LANE), 1) & (ts - 1)
    tok_n = lax.broadcasted_iota(jnp.int32, (ts, LANE), 1) & (ts - 1)
    new_t = lax.broadcasted_iota(jnp.int32, (ts, LANE), 0)
    sc = _mm_nt(_load_page(kc_ref, True), qall) + bc_ref[...]
    sc = jnp.where(wb + tok_c - key_i < WINDOW, sc, NEG)
    sn = _mm_nt(knew_ref[0], qall) + bn_ref[...]
    sn = jnp.where(new_t <= tok_n, sn, NEG)
    m = jnp.maximum(jnp.max(sc, axis=0, keepdims=True), jnp.max(sn, axis=0, keepdims=True))
    pc = jnp.exp(sc - m)
    pn = jnp.exp(sn - m)
    l = jnp.sum(pc, axis=0, keepdims=True) + jnp.sum(pn, axis=0, keepdims=True)
    out = (_mm_tn(pc, _load_page(vc_ref, True)) + _mm_tn(pn, vnew_ref[0])) / _row_to_col(l, LANE)
    gates = jax.nn.sigmoid(gate_ref[0])
    for h in range(NSA_HEADS):
        g = h // NSA_HPG
        hsl = slice(h * HEAD_DIM, (h + 1) * HEAD_DIM)
        o_win = out[h * ts:(h + 1) * ts, g * HEAD_DIM:(g + 1) * HEAD_DIM]
        o = (gates[:, h:h + 1] * ocmp_ref[0, :, hsl] + gates[:, NSA_HEADS + h:NSA_HEADS + h + 1] * osel_ref[0, :, hsl]
             + gates[:, 2 * NSA_HEADS + h:2 * NSA_HEADS + h + 1] * o_win)
        o_ref[0, :, hsl] = (o * _silu(z_ref[0, :, hsl])).astype(o_ref.dtype)
    keep = (wb - ts) * NSA_G
    for out_ref, old_ref, new_ref in ((kout_ref, kc_ref, knew_ref), (vout_ref, vc_ref, vnew_ref)):
        out_ref[0, 0:keep, :] = old_ref[0, ts * NSA_G:wb * NSA_G, :]
        for g in range(NSA_G):
            out_ref[0, pl.ds(keep + g, ts, stride=NSA_G), :] = new_ref[0, :, g * HEAD_DIM:(g + 1) * HEAD_DIM]


def _win_sample(ub, cache_k, cache_v, k_new, ocmp, osel, rel_bias, q_norm_g):
    b, ts, _ = ub.shape
    wb = cache_k.shape[1] // NSA_G
    assert wb == WINDOW and ts % 8 == 0
    bc = _lane_bias(rel_bias, PAST_LEN - wb + jnp.arange(wb, dtype=jnp.int32), ts)
    bn = _lane_bias(rel_bias, PAST_LEN + jnp.arange(ts, dtype=jnp.int32), ts)
    body = functools.partial(_win_sample_body, ts=ts, wb=wb)
    tok_spec = lambda w, c: pl.BlockSpec((1, ts, w), lambda bi: (bi, 0, c))
    win_spec = pl.BlockSpec((1, wb * NSA_G, HEAD_DIM), lambda bi: (bi, 0, 0))
    return pl.pallas_call(
        body,
        grid=(b,),
        in_specs=[
            tok_spec(NSA_Q_W, UB_Q // NSA_Q_W), win_spec, win_spec,
            tok_spec(NSA_KV_W, 0), tok_spec(NSA_KV_W, UB_KV // NSA_KV_W + 5),
            pl.BlockSpec((wb, LANE), lambda bi: (0, 0)), pl.BlockSpec((ts, LANE), lambda bi: (0, 0)),
            tok_spec(LANE, UB_G // LANE), tok_spec(NSA_Q_W, UB_Z // NSA_Q_W),
            tok_spec(NSA_Q_W, 0), tok_spec(NSA_Q_W, 0),
            pl.BlockSpec((1, HEAD_DIM), lambda bi: (0, 0)),
        ],
        out_specs=[tok_spec(NSA_Q_W, 0), win_spec, win_spec],
        out_shape=[jax.ShapeDtypeStruct((b, ts, NSA_Q_W), BF16),
                   jax.ShapeDtypeStruct((b, wb * NSA_G, HEAD_DIM), F32),
                   jax.ShapeDtypeStruct((b, wb * NSA_G, HEAD_DIM), F32)],
        compiler_params=_cparams(("arbitrary",)),
        name="win_sample",
    )(ub, cache_k, cache_v, k_new, ub, bc, bn, ub, ub, ocmp, osel, q_norm_g.reshape(1, HEAD_DIM))


def _mix_body(oa_ref, ob_ref, wa_ref, wb_ref, ma_ref, mb_ref, o_ref):
    pa = jnp.dot(oa_ref[...], wa_ref[...], preferred_element_type=F32)
    pb = jnp.dot(ob_ref[...], wb_ref[...], preferred_element_type=F32)
    o_ref[...] = (jax.nn.sigmoid(ma_ref[...]) * pa + jax.nn.sigmoid(mb_ref[...]) * pb).astype(o_ref.dtype)


def _out_body(m_ref, w_ref, x_ref, y_ref):
    y_ref[...] = x_ref[...] + jnp.dot(m_ref[...], w_ref[...], preferred_element_type=F32)


def _output(x2d, oa, ob, ub2d, wa16, wb16, wo16):
    m = x2d.shape[0]
    tm = min(m, 512)
    tn = 512
    grid = (m // tm, D_MODEL // tn)
    mixed = pl.pallas_call(
        _mix_body,
        grid=grid,
        in_specs=[
            pl.BlockSpec((tm, GDN_V_W), lambda i, j: (i, 0)),
            pl.BlockSpec((tm, NSA_Q_W), lambda i, j: (i, 0)),
            pl.BlockSpec((GDN_V_W, tn), lambda i, j: (0, j)),
            pl.BlockSpec((NSA_Q_W, tn), lambda i, j: (0, j)),
            pl.BlockSpec((tm, tn), lambda i, j: (i, UB_MA // tn + j)),
            pl.BlockSpec((tm, tn), lambda i, j: (i, UB_MB // tn + j)),
        ],
        out_specs=pl.BlockSpec((tm, tn), lambda i, j: (i, j)),
        out_shape=jax.ShapeDtypeStruct((m, D_MODEL), BF16),
        compiler_params=_cparams(("arbitrary", "arbitrary")),
        name="branch_mix",
    )(oa, ob, wa16, wb16, ub2d, ub2d)
    return pl.pallas_call(
        _out_body,
        grid=grid,
        in_specs=[
            pl.BlockSpec((tm, D_MODEL), lambda i, j: (i, 0)),
            pl.BlockSpec((D_MODEL, tn), lambda i, j: (0, j)),
            pl.BlockSpec((tm, tn), lambda i, j: (i, j)),
        ],
        out_specs=pl.BlockSpec((tm, tn), lambda i, j: (i, j)),
        out_shape=jax.ShapeDtypeStruct((m, D_MODEL), F32),
        compiler_params=_cparams(("arbitrary", "arbitrary")),
        name="out_proj",
    )(mixed, wo16, x2d)


def _kv_slices(ub, rows=None):
    b, t_len, _ = ub.shape
    out = []
    for i in (0, 1, 3, 5):
        a = ub[:, :, UB_KV + i * NSA_KV_W:UB_KV + (i + 1) * NSA_KV_W]
        out.append(a.reshape(b, t_len, NSA_G, HEAD_DIM))
    return out


def kernel(x_prompt, x_sample, cache_k_cmp, cache_v_cmp, cache_k_sel, cache_v_sel, cache_k_win, cache_v_win, state_conv, state_gdn, page_table, norm_g, w_in, gdn_conv_w, gdn_a_log, gdn_dt_bias, gdn_norm_g, q_norm_g, k_norm_g, cmp_pe_k, cmp_w_k, cmp_proj_k, cmp_pe_v, cmp_w_v, cmp_proj_v, rel_bias, w_branch_a, w_branch_b, w_out):
    bp, tp, _ = x_prompt.shape
    bs, ts, _ = x_sample.shape
    wb = cache_k_win.shape[1]
    n_pool = cache_k_cmp.shape[0]
    kv4 = lambda a: a.reshape(a.shape[0], a.shape[1], NSA_G, HEAD_DIM)
    cmp_p = (cmp_pe_k, cmp_w_k, cmp_proj_k, cmp_pe_v, cmp_w_v, cmp_proj_v)

    assert _IN_MB + D_MODEL == w_in.shape[1]
    w_t = w_in.T
    wa16, wb16, wo16 = w_branch_a.astype(BF16), w_branch_b.astype(BF16), w_out.astype(BF16)

    def project(x):
        x2d = x.reshape(-1, D_MODEL)
        ua = _project(x2d, norm_g, w_t, UA_OFFSETS).reshape(x.shape[0], x.shape[1], UA_W)
        ub = _project(x2d, norm_g, w_t, UB_OFFSETS).reshape(x.shape[0], x.shape[1], UB_W)
        return x2d, ua, ub

    x2d, ua, ub = project(x_prompt)
    conv0 = jnp.zeros((bp, GDN_CONV - 1, GDN_QKV_W), F32)
    s0 = jnp.zeros((bp, GDN_V_HEADS, HEAD_DIM, HEAD_DIM), F32)
    o_a, p_gdn = _gdn(ua, ub, conv0, s0, gdn_conv_w, gdn_a_log, gdn_dt_bias, gdn_norm_g)
    p_conv = ua[:, tp - (GDN_CONV - 1):, :GDN_QKV_W]
    n_pages_p = tp // PAGE
    pages = ub.reshape(bp * n_pages_p, PAGE, UB_W)
    ident = jnp.arange(bp * n_pages_p, dtype=jnp.int32).reshape(bp, n_pages_p)
    kcmp, vcmp = _compress(pages, pages, UB_KV // NSA_KV_W, UB_KV // NSA_KV_W + 1, ident, *cmp_p, k_norm_g[0])
    p_ks, p_kw, ksb, vsb, kwb, vwb = _nsa_prep(ub, k_norm_g, True)
    nc_p = (tp - CMP_LEN) // CMP_STRIDE + 1
    bias_cmp = _cmp_bias(rel_bias, jnp.arange(tp, dtype=jnp.int32), kcmp.shape[1], nc_p)
    ocmp, nsel = _cmp_select_prompt(ub, kcmp, vcmp, bias_cmp, q_norm_g)
    o_b = _attn_prompt(ub, ocmp, nsel, ksb, vsb, kwb, vwb, rel_bias, q_norm_g)
    y_prompt = _output(x2d, o_a.reshape(-1, GDN_V_W), o_b.reshape(-1, NSA_Q_W), ub.reshape(-1, UB_W), wa16, wb16, wo16)
    p_kc, p_vc, p_vs, p_vw = _kv_slices(ub)
    prompt_out = (y_prompt.reshape(x_prompt.shape), p_kc, p_vc, kv4(p_ks), p_vs, kv4(p_kw[:, tp - wb:]), p_vw[:, tp - wb:],
                  p_conv, p_gdn)

    x2d, ua, ub = project(x_sample)
    o_a, s_gdn = _gdn(ua, ub, state_conv, state_gdn, gdn_conv_w, gdn_a_log, gdn_dt_bias, gdn_norm_g)
    s_conv = ua[:, ts - (GDN_CONV - 1):, :GDN_QKV_W]
    pool = lambda c: c.reshape(n_pool, PAGE * NSA_G, HEAD_DIM)
    kcmp, vcmp = _compress(pool(cache_k_cmp), pool(cache_v_cmp), 0, 0, page_table, *cmp_p, k_norm_g[0])
    s_ks, s_kwn = _nsa_prep(ub, k_norm_g, False)
    total = PAST_LEN + ts
    nc_s = (total - CMP_LEN) // CMP_STRIDE + 1
    nb_s = -(-total // SEL_BLOCK)
    assert nc_s + 1 == kcmp.shape[1]
    pos_s = PAST_LEN + jnp.arange(ts, dtype=jnp.int32)
    bias_cmp = _cmp_bias(rel_bias, pos_s, kcmp.shape[1], nc_s)
    ocmp, selt = _cmp_select_sample(ub, kcmp, vcmp, bias_cmp, q_norm_g, nb_s)
    osel = _sel_sample(ub, selt, pool(cache_k_sel), pool(cache_v_sel), page_table, s_ks, rel_bias, q_norm_g)
    win3 = lambda c: c.reshape(bs, wb * NSA_G, HEAD_DIM)
    o_b, s_kw, s_vw = _win_sample(ub, win3(cache_k_win), win3(cache_v_win), s_kwn, ocmp, osel, rel_bias, q_norm_g)
    y_sample = _output(x2d, o_a.reshape(-1, GDN_V_W), o_b.reshape(-1, NSA_Q_W), ub.reshape(-1, UB_W), wa16, wb16, wo16)
    s_kc, s_vc, s_vs, _ = _kv_slices(ub)
    sample_out = (y_sample.reshape(x_sample.shape), s_kc, s_vc, kv4(s_ks), s_vs, s_kw.reshape(cache_k_win.shape),
                  s_vw.reshape(cache_v_win.shape), s_conv, s_gdn)

    return (prompt_out[0], sample_out[0]) + prompt_out[1:] + sample_out[1:]
```

```python
import functools
import math

import jax
import jax.numpy as jnp
import numpy as np
from jax import lax
from jax.experimental import pallas as pl
from jax.experimental.pallas import tpu as pltpu

F32 = jnp.float32
BF16 = jnp.bfloat16
HI = lax.Precision.HIGHEST

D_MODEL = 2048
PAST_LEN = 16384
PAGE = 128

GDN_QK_HEADS = 16
GDN_V_HEADS = 32
HEAD_DIM = 128
GDN_CONV = 4
GDN_CHUNK = 64
GDN_GROUP = 8
GDN_QK_PER_STEP = 2
GDN_QK_W = GDN_QK_HEADS * HEAD_DIM
GDN_V_W = GDN_V_HEADS * HEAD_DIM
GDN_QKV_W = 2 * GDN_QK_W + GDN_V_W

NSA_HEADS = 16
NSA_G = 4
NSA_HPG = NSA_HEADS // NSA_G
NSA_Q_W = NSA_HEADS * HEAD_DIM
NSA_KV_W = NSA_G * HEAD_DIM
CMP_LEN = 32
CMP_STRIDE = 16
SEL_BLOCK = 64
SEL_TOP = 16
WINDOW = 512
REL_BUCKETS = 32
REL_MAX_DIST = 128

EPS = 1e-6
NEG = -1e30
BIG = 1e9

LANE = 128
VMEM_LIMIT = 56 * 1024 * 1024
PAGES_PER_STEP = 16

UA_W = GDN_QKV_W + GDN_V_W
UB_Q, UB_Z, UB_MA, UB_MB, UB_KV, UB_BA, UB_G = 0, 2048, 4096, 6144, 8192, 11264, 11776
UB_W = 12288
PROJ_TN = 512
PROJ_TM = 2048

_IN_QB = UA_W + 2 * GDN_V_HEADS
_IN_KV = _IN_QB + NSA_Q_W
_IN_G = _IN_KV + 6 * NSA_KV_W
_IN_ZB = _IN_G + 3 * NSA_HEADS
_IN_MA = _IN_ZB + NSA_Q_W
_IN_MB = _IN_MA + D_MODEL


def _tiles(start, width):
    return [start + k * PROJ_TN for k in range(width // PROJ_TN)]


UA_OFFSETS = _tiles(0, UA_W)
UB_OFFSETS = (_tiles(_IN_QB, NSA_Q_W) + _tiles(_IN_ZB, NSA_Q_W) + _tiles(_IN_MA, D_MODEL) + _tiles(_IN_MB, D_MODEL)
              + _tiles(_IN_KV, 6 * NSA_KV_W) + [UA_W, _IN_G])


def _mm(a, b):
    return jnp.dot(a.astype(BF16), b.astype(BF16), preferred_element_type=F32)


def _mm_nt(a, b):
    return lax.dot_general(a.astype(BF16), b.astype(BF16), (((1,), (1,)), ((), ())), preferred_element_type=F32)


def _mm_tn(a, b):
    return lax.dot_general(a.astype(BF16), b.astype(BF16), (((0,), (0,)), ((), ())), preferred_element_type=F32)


def _mm_hi(a, b):
    return jnp.dot(a, b, precision=HI, preferred_element_type=F32)


def _mm_nt_hi(a, b):
    return lax.dot_general(a, b, (((1,), (1,)), ((), ())), precision=HI, preferred_element_type=F32)


def _silu(x):
    return x * jax.nn.sigmoid(x)


def _softplus(x):
    return jnp.maximum(x, 0.0) + jnp.log1p(jnp.exp(-jnp.abs(x)))


def _rms(x, gain):
    return x * lax.rsqrt(jnp.mean(x * x, axis=-1, keepdims=True) + EPS) * gain


def _cparams(sem):
    return pltpu.CompilerParams(dimension_semantics=sem, vmem_limit_bytes=VMEM_LIMIT)


def _proj_body(offs_ref, x_ref, g_ref, w_ref, o_ref, h_ref):
    del offs_ref

    @pl.when(pl.program_id(1) == 0)
    def _():
        h_ref[...] = _rms(x_ref[...], g_ref[...]).astype(BF16)

    o_ref[...] = lax.dot_general(h_ref[...], w_ref[...].astype(BF16), (((1,), (1,)), ((), ())),
                                 preferred_element_type=F32)


def _project(x2d, norm_g, w_t, offsets):
    m = x2d.shape[0]
    tm = min(m, PROJ_TM)
    n_tiles = len(offsets)
    return pl.pallas_call(
        _proj_body,
        grid_spec=pltpu.PrefetchScalarGridSpec(
            num_scalar_prefetch=1,
            grid=(m // tm, n_tiles),
            in_specs=[
                pl.BlockSpec((tm, D_MODEL), lambda i, j, offs: (i, 0), pipeline_mode=pl.Buffered(1)),
                pl.BlockSpec((1, D_MODEL), lambda i, j, offs: (0, 0)),
                pl.BlockSpec((pl.Element(PROJ_TN), pl.Element(D_MODEL)), lambda i, j, offs: (pl.multiple_of(offs[j], 16), 0)),
            ],
            out_specs=pl.BlockSpec((tm, PROJ_TN), lambda i, j, offs: (i, j)),
            scratch_shapes=[pltpu.VMEM((tm, D_MODEL), BF16)]),
        out_shape=jax.ShapeDtypeStruct((m, n_tiles * PROJ_TN), F32),
        compiler_params=_cparams(("arbitrary", "arbitrary")),
        name="proj",
    )(jnp.asarray(offsets, jnp.int32), x2d, norm_g.reshape(1, D_MODEL), w_t)


def _split2(a):
    hi = a.astype(BF16)
    return hi, (a - hi.astype(F32)).astype(BF16)


def _dot16(a, b):
    return jnp.dot(a, b, preferred_element_type=F32)


def _mm_x3(a2, b2):
    return _dot16(a2[0], b2[0]) + (_dot16(a2[0], b2[1]) + _dot16(a2[1], b2[0]))


def _bdot(a, b):
    return lax.dot_general(a, b, (((2,), (1,)), ((0,), (0,))), preferred_element_type=F32)


def _bdot_nt(a, b):
    return lax.dot_general(a, b, (((2,), (2,)), ((0,), (0,))), preferred_element_type=F32)


def _bmm_x3(a2, b2):
    return _bdot(a2[0], b2[0]) + (_bdot(a2[0], b2[1]) + _bdot(a2[1], b2[0]))


def _unit_lower_inverse(nmat, c):
    row = lax.broadcasted_iota(jnp.int32, (1, c, c), 1)
    col = lax.broadcasted_iota(jnp.int32, (1, c, c), 2)
    p = jnp.where(row == col, 1.0, 0.0) + nmat
    m2 = _split2(nmat)
    span = 2
    while span < c:
        m2 = _split2(_bmm_x3(m2, m2))
        p = p + _bmm_x3(_split2(p), m2)
        span *= 2
    return p


def _unit_lower_inverse_pairs(nmat, eye, left, c):
    def blockdiag(x2):
        return tuple(jnp.concatenate([jnp.where(left, x, jnp.zeros_like(x)), jnp.where(left, jnp.zeros_like(x), x)], axis=1)
                     for x in x2)

    p = jnp.where(eye, 1.0, 0.0) + nmat
    m2 = _split2(nmat)
    bd2 = blockdiag(m2)
    span = 2
    while span < c:
        m2 = _split2(_bmm_x3(m2, bd2))
        bd2 = blockdiag(m2)
        p = p + _bmm_x3(_split2(p), bd2)
        span *= 2
    return p


def _cumsum_rows(tril16, g):
    h = g.astype(BF16)
    r = g - h.astype(F32)
    m = r.astype(BF16)
    l = (r - m.astype(F32)).astype(BF16)
    return _dot16(tril16, h) + (_dot16(tril16, m) + _dot16(tril16, l))


def _gdn_body(q_ref, k_ref, v_ref, z_ref, ba_ref, cwq_ref, cwk_ref, cwv_ref, csq_ref, csk_ref, csv_ref,
              s0_ref, gp_ref, ng_ref, o_ref, sfin_ref, cq_s, ck_s, cv_s, u_s, w_s, qg_s, kd_s, at_s, gl_s, st_s,
              *, t_len, c, nq):
    nv = 2 * nq
    first_head = nv * pl.program_id(1)
    n_chunks = t_len // c
    for cs, carry in ((csq_ref, cq_s), (csk_ref, ck_s), (csv_ref, cv_s)):
        carry[...] = jnp.zeros(carry.shape, F32)
        carry[5:8, :] = cs[0]
    st_s[...] = s0_ref[0]

    gsz = min(GDN_GROUP, n_chunks)
    rg = gsz * c
    row = lax.broadcasted_iota(jnp.int32, (1, c, 2 * c), 1)
    lane2 = lax.broadcasted_iota(jnp.int32, (1, c, 2 * c), 2)
    left = lane2 < c
    col = lane2 & (c - 1)
    tril = row >= col
    strict = row > col
    eye = row == col
    lane = lax.broadcasted_iota(jnp.int32, (rg, LANE), 1)
    rr = lax.broadcasted_iota(jnp.int32, (rg, rg), 0)
    cc = lax.broadcasted_iota(jnp.int32, (rg, rg), 1)
    shift = c.bit_length() - 1
    assert 1 << shift == c
    tril16 = jnp.where((rr >= cc) & ((rr >> shift) == (cc >> shift)), 1.0, 0.0).astype(BF16)

    def conv(src_ref, carry, cw_ref, rows):
        cur = src_ref[0, rows, :]
        win = jnp.concatenate([carry[...], cur], axis=0)
        carry[...] = cur[rg - 8:rg]
        w = cw_ref[...]
        a = win[5:5 + rg] * w[0:1] + win[6:6 + rg] * w[1:2] + win[7:7 + rg] * w[2:3] + win[8:8 + rg] * w[3:4]
        return _silu(a)

    def pick(x, idx):
        return jnp.sum(jnp.where(lane == idx, x, 0.0), axis=-1, keepdims=True).reshape(gsz, c, 1)

    def per_head(x, n):
        return [x[:, i * HEAD_DIM:(i + 1) * HEAD_DIM].reshape(gsz, c, HEAD_DIM) for i in range(n)]

    def prepare(gi):
        base = gi * rg if isinstance(gi, int) else pl.multiple_of(gi * rg, rg)
        rows = pl.ds(base, rg)
        qa = per_head(conv(q_ref, cq_s, cwq_ref, rows), nq)
        ka = per_head(conv(k_ref, ck_s, cwk_ref, rows), nq)
        va = per_head(conv(v_ref, cv_s, cwv_ref, rows), nv)
        qn = [x * lax.rsqrt(jnp.sum(x * x, axis=-1, keepdims=True) + EPS) * (HEAD_DIM ** -0.5) for x in qa]
        kn = [x * lax.rsqrt(jnp.sum(x * x, axis=-1, keepdims=True) + EPS) for x in ka]
        ba = ba_ref[0, rows, :]
        beta_all = jax.nn.sigmoid(ba)
        g_all = -jnp.exp(gp_ref[0:1, :]) * _softplus(ba + gp_ref[1:2, :])
        gc_all = _cumsum_rows(tril16, g_all)
        k16 = jnp.concatenate(kn, axis=0).astype(BF16)
        k2x = jnp.concatenate([k16, k16], axis=1)
        kk = _bdot_nt(k16, k2x)
        qk = _bdot_nt(jnp.concatenate(qn, axis=0).astype(BF16), k2x)
        betas, gcols, blocks = [], [], []
        zeros = jnp.zeros((gsz, c, 2 * HEAD_DIM), F32)
        for qi in range(nq):
            pair = []
            for l in range(2):
                hh = 2 * qi + l
                beta = pick(beta_all, first_head + hh)
                gcol = pick(gc_all, GDN_V_HEADS + first_head + hh)
                egc = jnp.exp(gcol)
                glast = gcol[:, c - 1:c, :]
                rhs = jnp.concatenate([va[hh] * beta, kn[qi] * (beta * egc)], axis=2)
                pair.append((beta, gcol, rhs))
                qg_s[hh, rows, :] = (qn[qi] * egc).reshape(rg, HEAD_DIM).astype(qg_s.dtype)
                kd_s[hh, rows, :] = (kn[qi] * jnp.exp(glast - gcol)).reshape(rg, HEAD_DIM).astype(kd_s.dtype)
                egl = jnp.exp(glast)
                for j in range(gsz):
                    gl_s[hh, gi * gsz + j] = jnp.broadcast_to(egl[j], (8, HEAD_DIM))
            betas.append(jnp.where(left, pair[0][0], pair[1][0]))
            gcols.append(jnp.where(left, pair[0][1], pair[1][1]))
            blocks.append(jnp.concatenate([jnp.concatenate([pair[0][2], zeros], axis=2),
                                           jnp.concatenate([zeros, pair[1][2]], axis=2)], axis=1))
        beta = jnp.concatenate(betas, axis=0)
        gcol = jnp.concatenate(gcols, axis=0)
        grow = jnp.sum(jnp.where(eye, gcol, 0.0), axis=1, keepdims=True)
        gamma = jnp.where(tril, jnp.exp(jnp.minimum(gcol - grow, 0.0)), 0.0)
        tinv = _unit_lower_inverse_pairs(jnp.where(strict, -(kk * beta * gamma), 0.0), eye, left, c)
        rhs2 = _split2(jnp.concatenate(blocks, axis=0))
        uw = _bmm_x3(_split2(tinv), rhs2)
        attn = qk * gamma
        for qi in range(nq):
            at_s[qi, rows, :] = attn[qi * gsz:(qi + 1) * gsz].reshape(rg, 2 * c).astype(at_s.dtype)
            for l in range(2):
                uw_h = uw[qi * gsz:(qi + 1) * gsz, :, 2 * l * HEAD_DIM:2 * (l + 1) * HEAD_DIM]
                u_s[2 * qi + l, rows, :] = uw_h[:, :, :HEAD_DIM].reshape(rg, HEAD_DIM)
                w_s[2 * qi + l, rows, :] = uw_h[:, :, HEAD_DIM:].reshape(rg, HEAD_DIM).astype(w_s.dtype)

    def recur(ci):
        rows = pl.ds(ci * c if isinstance(ci, int) else pl.multiple_of(ci * c, c), c)
        s = st_s[...]
        s16 = s.astype(BF16)
        v_new = u_s[:, rows, :] - _bdot(w_s[:, rows, :].astype(BF16), s16)
        v16 = v_new.astype(BF16)
        zv = jnp.zeros((c, HEAD_DIM), BF16)
        vbd = jnp.stack([jnp.concatenate([jnp.concatenate([v16[2 * qi], zv], axis=1),
                                          jnp.concatenate([zv, v16[2 * qi + 1]], axis=1)], axis=0) for qi in range(nq)], axis=0)
        o_state = _bdot(qg_s[:, rows, :].astype(BF16), s16)
        o_attn = _bdot(at_s[:, rows, :].astype(BF16), vbd)
        kd16 = kd_s[:, rows, :].astype(BF16)
        upd = jnp.stack([_mm_tn(kd16[hh], v16[hh]) for hh in range(nv)], axis=0)
        st_s[...] = s * gl_s[:, ci][:, 0:1, :] + upd
        for hh in range(nv):
            hsl = slice(hh * HEAD_DIM, (hh + 1) * HEAD_DIM)
            o = o_state[hh] + o_attn[hh // 2][:, (hh % 2) * HEAD_DIM:(hh % 2 + 1) * HEAD_DIM]
            o_ref[0, rows, hsl] = (_rms(o, ng_ref[...]) * _silu(z_ref[0, rows, hsl])).astype(o_ref.dtype)

    if n_chunks == 1:
        prepare(0)
        recur(0)
    else:
        def prepare_step(gi, carry):
            prepare(gi)
            return carry

        def recur_step(ci, carry):
            recur(ci)
            return carry

        lax.fori_loop(0, n_chunks // gsz, prepare_step, 0)
        lax.fori_loop(0, n_chunks, recur_step, 0)
    sfin_ref[0] = st_s[...]


def _gdn(ua, ub, conv_state, s0, conv_w, a_log, dt_bias, norm_g):
    b, t_len, _ = ua.shape
    c = min(GDN_CHUNK, t_len)
    hd = HEAD_DIM
    nq = GDN_QK_PER_STEP if t_len > GDN_CHUNK else 4 * GDN_QK_PER_STEP
    nv = 2 * nq
    steps = GDN_QK_HEADS // nq
    qw, vw = nq * hd, nv * hd
    op_dtype = BF16 if c % 16 == 0 else F32
    gp = jnp.zeros((2, LANE), F32)
    gp = gp.at[0, GDN_V_HEADS:2 * GDN_V_HEADS].set(a_log).at[1, GDN_V_HEADS:2 * GDN_V_HEADS].set(dt_bias)
    body = functools.partial(_gdn_body, t_len=t_len, c=c, nq=nq)
    return pl.pallas_call(
        body,
        grid=(b, steps),
        in_specs=[
            pl.BlockSpec((1, t_len, qw), lambda bi, i: (bi, 0, i)),
            pl.BlockSpec((1, t_len, qw), lambda bi, i: (bi, 0, steps + i)),
            pl.BlockSpec((1, t_len, vw), lambda bi, i: (bi, 0, steps + i)),
            pl.BlockSpec((1, t_len, vw), lambda bi, i: (bi, 0, 2 * steps + i)),
            pl.BlockSpec((1, t_len, LANE), lambda bi, i: (bi, 0, UB_BA // LANE)),
            pl.BlockSpec((GDN_CONV, qw), lambda bi, i: (0, i)),
            pl.BlockSpec((GDN_CONV, qw), lambda bi, i: (0, steps + i)),
            pl.BlockSpec((GDN_CONV, vw), lambda bi, i: (0, steps + i)),
            pl.BlockSpec((1, GDN_CONV - 1, qw), lambda bi, i: (bi, 0, i)),
            pl.BlockSpec((1, GDN_CONV - 1, qw), lambda bi, i: (bi, 0, steps + i)),
            pl.BlockSpec((1, GDN_CONV - 1, vw), lambda bi, i: (bi, 0, steps + i)),
            pl.BlockSpec((1, nv, hd, hd), lambda bi, i: (bi, i, 0, 0)),
            pl.BlockSpec((2, LANE), lambda bi, i: (0, 0)),
            pl.BlockSpec((1, hd), lambda bi, i: (0, 0)),
        ],
        out_specs=[
            pl.BlockSpec((1, t_len, vw), lambda bi, i: (bi, 0, i)),
            pl.BlockSpec((1, nv, hd, hd), lambda bi, i: (bi, i, 0, 0)),
        ],
        out_shape=[
            jax.ShapeDtypeStruct((b, t_len, GDN_V_W), BF16),
            jax.ShapeDtypeStruct((b, GDN_V_HEADS, hd, hd), F32),
        ],
        scratch_shapes=[
            pltpu.VMEM((8, qw), F32),
            pltpu.VMEM((8, qw), F32),
            pltpu.VMEM((8, vw), F32),
            pltpu.VMEM((nv, t_len, hd), F32),
            pltpu.VMEM((nv, t_len, hd), op_dtype),
            pltpu.VMEM((nv, t_len, hd), op_dtype),
            pltpu.VMEM((nv, t_len, hd), op_dtype),
            pltpu.VMEM((nq, t_len, 2 * c), op_dtype),
            pltpu.VMEM((nv, t_len // c, 8, hd), F32),
            pltpu.VMEM((nv, hd, hd), F32),
        ],
        compiler_params=_cparams(("arbitrary", "arbitrary")),
        name="gdn",
    )(ua, ua, ua, ua, ub, conv_w, conv_w, conv_w, conv_state, conv_state, conv_state, s0, gp, norm_g.reshape(1, hd))


def _t5_bucket(rel):
    n = jnp.maximum(rel, 0)
    exact = REL_BUCKETS // 2
    nf = jnp.maximum(n, 1).astype(F32)
    large = exact + (jnp.log(nf / exact) / math.log(REL_MAX_DIST / exact) * (REL_BUCKETS - exact)).astype(jnp.int32)
    return jnp.where(n < exact, n, jnp.minimum(large, REL_BUCKETS - 1))


def _bias_of(rel_bias, rel):
    bucket = _t5_bucket(rel)[None]
    table = rel_bias.astype(F32)
    out = jnp.zeros((NSA_HEADS,) + rel.shape, F32)
    for k in range(REL_BUCKETS):
        out = jnp.where(bucket == k, table[k].reshape((NSA_HEADS,) + (1,) * rel.ndim), out)
    return out


def _cmp_bias(rel_bias, q_pos, ncp, nc):
    n = jnp.arange(ncp, dtype=jnp.int32)
    rel = q_pos[:, None] - (n * CMP_STRIDE + (CMP_LEN - 1))[None, :]
    ok = (rel >= 0) & (n < nc)[None, :]
    return jnp.where(ok[None], _bias_of(rel_bias, rel), NEG)


def _load_page(ref, interleaved):
    if not interleaved:
        return ref[0]
    n_rows = ref.shape[1] // NSA_G
    return jnp.concatenate([ref[0, pl.ds(g, n_rows, stride=NSA_G), :] for g in range(NSA_G)], axis=1)


def _compress_body(pt_ref, *refs, pps, n_steps, nch, interleaved):
    del pt_ref
    k_refs = refs[:pps]
    v_refs = refs[pps:2 * pps]
    (wabk_ref, wabv_ref, wk_ref, pek_ref, wv_ref, pev_ref, projk_ref, projv_ref, gain_ref,
     outk_ref, outv_ref, ak_s, bk_s, av_s, bv_s) = refs[2 * pps:]
    s = pl.program_id(1)

    @pl.when(s == 0)
    def _():
        bk_s[nch:nch + 8, :] = jnp.zeros((8, NSA_KV_W), F32)
        bv_s[nch:nch + 8, :] = jnp.zeros((8, NSA_KV_W), F32)

    def pool(w2, page):
        w_hi, w_lo = w2
        p_hi, p_lo = _split2(page)
        r = _dot16(jnp.concatenate([w_hi, w_lo], axis=0), p_hi)
        return r[0:16] + (r[16:32] + _dot16(w_hi, p_lo))

    wk2 = _split2(wabk_ref[...])
    wv2 = _split2(wabv_ref[...])
    for j in range(pps):
        row0 = pl.multiple_of((s * pps + j) * 8, 8)
        abk = pool(wk2, _load_page(k_refs[j], interleaved))
        ak_s[pl.ds(row0, 8), :] = abk[0:8]
        bk_s[pl.ds(row0, 8), :] = abk[8:16]
        abv = pool(wv2, _load_page(v_refs[j], interleaved))
        av_s[pl.ds(row0, 8), :] = abv[0:8]
        bv_s[pl.ds(row0, 8), :] = abv[8:16]

    @pl.when(s == n_steps - 1)
    def _():
        cpe_k = jnp.sum(wk_ref[...] * pek_ref[...], axis=0, keepdims=True)
        cpe_v = jnp.sum(wv_ref[...] * pev_ref[...], axis=0, keepdims=True)
        projk2 = _split2(projk_ref[...])
        projv2 = _split2(projv_ref[...])
        rb = min(nch, 128)

        def fin(r, carry):
            r0 = pl.multiple_of(r * rb, rb)
            pk = ak_s[pl.ds(r0, rb), :] + bk_s[pl.ds(r0, rb + 8), :][1:rb + 1]
            pv = av_s[pl.ds(r0, rb), :] + bv_s[pl.ds(r0, rb + 8), :][1:rb + 1]
            for g in range(NSA_G):
                sl = slice(g * HEAD_DIM, (g + 1) * HEAD_DIM)
                yk = _mm_x3(_split2(pk[:, sl] + cpe_k), projk2)
                outk_ref[0, pl.ds(r0, rb), sl] = _rms(yk, gain_ref[...])
                outv_ref[0, pl.ds(r0, rb), sl] = _mm_x3(_split2(pv[:, sl] + cpe_v), projv2)
            return carry

        lax.fori_loop(0, nch // rb, fin, 0)


def _pool_weights(w):
    c = np.arange(8)[:, None]
    t = np.arange(PAGE)[None, :]
    off = t - CMP_STRIDE * c
    inside = (off >= 0) & (off < CMP_STRIDE)
    idx = np.clip(off, 0, CMP_STRIDE - 1)
    wa = jnp.where(inside, w[idx], 0.0)
    wb = jnp.where(inside, w[idx + CMP_STRIDE], 0.0)
    return jnp.concatenate([wa, wb], axis=0).astype(F32)


def _compress(pages_k, pages_v, col_k, col_v, table, pe_k, w_k, proj_k, pe_v, w_v, proj_v, k_gain):
    b, n_pages = table.shape
    pps = min(PAGES_PER_STEP, n_pages)
    n_steps = n_pages // pps
    nch = 8 * n_pages
    interleaved = pages_k.shape[1] == NSA_G * PAGE
    body = functools.partial(_compress_body, pps=pps, n_steps=n_steps, nch=nch, interleaved=interleaved)

    def page_spec(j, col):
        if interleaved:
            return pl.BlockSpec((1, NSA_G * PAGE, HEAD_DIM), lambda bi, s, pt: (pt[bi, s * pps + j], 0, 0))
        return pl.BlockSpec((1, PAGE, NSA_KV_W), lambda bi, s, pt: (pt[bi, s * pps + j], 0, col))

    full = lambda shape: pl.BlockSpec(shape, lambda bi, s, pt: (0,) * len(shape))
    in_specs = [page_spec(j, col_k) for j in range(pps)] + [page_spec(j, col_v) for j in range(pps)]
    in_specs += [full((16, PAGE)), full((16, PAGE)), full((CMP_LEN, 1)), full((CMP_LEN, HEAD_DIM)),
                 full((CMP_LEN, 1)), full((CMP_LEN, HEAD_DIM)), full((HEAD_DIM, HEAD_DIM)), full((HEAD_DIM, HEAD_DIM)),
                 full((1, HEAD_DIM))]
    out_spec = pl.BlockSpec((1, nch, NSA_KV_W), lambda bi, s, pt: (bi, 0, 0))
    return pl.pallas_call(
        body,
        grid_spec=pltpu.PrefetchScalarGridSpec(
            num_scalar_prefetch=1, grid=(b, n_steps), in_specs=in_specs, out_specs=[out_spec, out_spec],
            scratch_shapes=[pltpu.VMEM((nch, NSA_KV_W), F32), pltpu.VMEM((nch + 8, NSA_KV_W), F32),
                            pltpu.VMEM((nch, NSA_KV_W), F32), pltpu.VMEM((nch + 8, NSA_KV_W), F32)]),
        out_shape=[jax.ShapeDtypeStruct((b, nch, NSA_KV_W), F32)] * 2,
        compiler_params=_cparams(("arbitrary", "arbitrary")),
        name="compress",
    )(table, *([pages_k] * pps), *([pages_v] * pps), _pool_weights(w_k), _pool_weights(w_v),
      w_k.reshape(CMP_LEN, 1), pe_k, w_v.reshape(CMP_LEN, 1), pe_v, proj_k, proj_v, k_gain.reshape(1, HEAD_DIM))


def _prep_body(ks_ref, kw_ref, kg_ref, pks_ref, pkw_ref):
    for g in range(NSA_G):
        sl = slice(g * HEAD_DIM, (g + 1) * HEAD_DIM)
        pks_ref[0, :, sl] = _rms(ks_ref[0, :, sl], kg_ref[1:2, :])
        pkw_ref[0, :, sl] = _rms(kw_ref[0, :, sl], kg_ref[2:3, :])


def _prep_prompt_body(kc_ref, vc_ref, ks_ref, vs_ref, kw_ref, vw_ref, kg_ref,
                      okc_ref, ovc_ref, oks_ref, ovs_ref, okw_ref, ovw_ref, ks16_ref, vst_ref, kw16_ref, vwt_ref):
    tp = ks_ref.shape[1]

    def put(dst, g, x):
        dst[0, pl.ds(g, tp, stride=NSA_G), :] = x

    for g in range(NSA_G):
        sl = slice(g * HEAD_DIM, (g + 1) * HEAD_DIM)
        ksn = _rms(ks_ref[0, :, sl], kg_ref[1:2, :])
        kwn = _rms(kw_ref[0, :, sl], kg_ref[2:3, :])
        ks16_ref[0, :, sl] = ksn.astype(BF16)
        kw16_ref[0, :, sl] = kwn.astype(BF16)
        put(oks_ref, g, ksn)
        put(okw_ref, g, kwn)
        for src, dst in ((kc_ref, okc_ref), (vc_ref, ovc_ref), (vs_ref, ovs_ref), (vw_ref, ovw_ref)):
            put(dst, g, src[0, :, sl])
    for src, dst in ((vs_ref, vst_ref), (vw_ref, vwt_ref)):
        for j in range(dst.shape[1]):
            dst[0, j] = src[0, j * LANE:(j + 1) * LANE, :].T.astype(BF16)


def _nsa_prep_prompt(ub, k_norm_g, wb):
    b, t_len, _ = ub.shape
    tp = min(t_len, 512)
    assert wb % tp == 0 and t_len % tp == 0 and t_len >= wb
    first_win = (t_len - wb) // tp
    kv0 = UB_KV // NSA_KV_W
    spec = lambda c: pl.BlockSpec((1, tp, NSA_KV_W), lambda bi, i: (bi, i, c))
    o_spec = pl.BlockSpec((1, tp, NSA_KV_W), lambda bi, i: (bi, i, 0))
    il_spec = pl.BlockSpec((1, NSA_G * tp, HEAD_DIM), lambda bi, i: (bi, i, 0))
    win_spec = pl.BlockSpec((1, NSA_G * tp, HEAD_DIM), lambda bi, i: (bi, jnp.maximum(i - first_win, 0), 0))
    t_spec = pl.BlockSpec((1, tp // LANE, NSA_KV_W, LANE), lambda bi, i: (bi, i, 0, 0))
    il = jax.ShapeDtypeStruct((b, NSA_G * t_len, HEAD_DIM), F32)
    il_win = jax.ShapeDtypeStruct((b, NSA_G * wb, HEAD_DIM), F32)
    k16 = jax.ShapeDtypeStruct((b, t_len, NSA_KV_W), BF16)
    v16 = jax.ShapeDtypeStruct((b, t_len // LANE, NSA_KV_W, LANE), BF16)
    return pl.pallas_call(
        _prep_prompt_body,
        grid=(b, t_len // tp),
        in_specs=[spec(kv0 + c) for c in range(6)] + [pl.BlockSpec((3, HEAD_DIM), lambda bi, i: (0, 0))],
        out_specs=[il_spec] * 4 + [win_spec] * 2 + [o_spec, t_spec, o_spec, t_spec],
        out_shape=[il] * 4 + [il_win] * 2 + [k16, v16, k16, v16],
        compiler_params=_cparams(("arbitrary", "arbitrary")),
        name="nsa_prep_prompt",
    )(ub, ub, ub, ub, ub, ub, k_norm_g)


def _nsa_prep(ub, k_norm_g):
    b, t_len, _ = ub.shape
    tp = min(t_len, 512)
    kv0 = UB_KV // NSA_KV_W
    spec = lambda c: pl.BlockSpec((1, tp, NSA_KV_W), lambda bi, i: (bi, i, c))
    o_spec = pl.BlockSpec((1, tp, NSA_KV_W), lambda bi, i: (bi, i, 0))
    return pl.pallas_call(
        _prep_body,
        grid=(b, t_len // tp),
        in_specs=[spec(kv0 + 2), spec(kv0 + 4), pl.BlockSpec((3, HEAD_DIM), lambda bi, i: (0, 0))],
        out_specs=[o_spec] * 2,
        out_shape=[jax.ShapeDtypeStruct((b, t_len, NSA_KV_W), F32)] * 2,
        compiler_params=_cparams(("arbitrary", "arbitrary")),
        name="nsa_prep",
    )(ub, ub, k_norm_g)


def _sel_matrix(nc, nb, ncp, nbp):
    j = np.arange(nb)
    lo = np.clip((SEL_BLOCK * j - CMP_LEN) // CMP_STRIDE + 1, 0, nc)
    hi = np.clip(-(-(SEL_BLOCK * (j + 1)) // CMP_STRIDE), 0, nc)
    n = np.arange(ncp)[:, None]
    m = np.zeros((ncp, nbp), np.float32)
    m[:, :nb] = (n >= lo[None, :]) & (n < hi[None, :])
    return jnp.asarray(m)


def _cmp_body(q_ref, kc_ref, vc_ref, bias_ref, mselt_ref, qg_ref, gate_ref, ocmpt_ref, nsel_ref, *, tq, nb):
    qi = pl.program_id(1)
    nbp = -(-nb // 8) * 8
    gates_t = jax.nn.sigmoid(gate_ref[0]).T
    blk = lax.broadcasted_iota(jnp.int32, (nbp, tq), 0)
    qblk = (qi * tq + lax.broadcasted_iota(jnp.int32, (nbp, tq), 1)) >> 6
    scale = HEAD_DIM ** -0.5
    for g in range(NSA_G):
        gsl = slice(g * HEAD_DIM, (g + 1) * HEAD_DIM)
        kc = kc_ref[0, :, gsl]
        vct = vc_ref[0, :, gsl].T.astype(BF16)
        imp = jnp.zeros((kc.shape[0], tq), F32)
        heads = [g * NSA_HPG + hh for hh in range(NSA_HPG)]
        kc2 = _split2(kc)
        scores = []
        for h in heads:
            q2 = _split2(_rms(q_ref[0, :, h * HEAD_DIM:(h + 1) * HEAD_DIM], qg_ref[...]) * scale)
            nt = lambda a, b: lax.dot_general(a, b, (((1,), (1,)), ((), ())), preferred_element_type=F32)
            scores.append(nt(kc2[0], q2[0]) + (nt(kc2[0], q2[1]) + nt(kc2[1], q2[0])))
        probs = []
        for h, s in zip(heads, scores):
            bias = bias_ref[h]
            s = bias + s
            e = jnp.exp(s - jnp.max(s, axis=0, keepdims=True))
            p = e * (1.0 / jnp.sum(e, axis=0, keepdims=True)) * jnp.where(bias > 0.5 * NEG, 1.0, 0.0)
            probs.append(p)
            imp = imp + p
        for h, p in zip(heads, probs):
            hsl = slice(h * HEAD_DIM, (h + 1) * HEAD_DIM)
            ocmpt_ref[0, hsl, :] = gates_t[h:h + 1, :] * _dot16(vct, p.astype(BF16))
        score = _cumsum_rows(mselt_ref[0:nbp, :].astype(BF16), imp)
        score = jnp.where(blk == 0, BIG, score)
        score = jnp.where(blk == qblk, BIG, score)
        score = jnp.where(blk == qblk - 1, BIG, score)
        score = jnp.where(blk <= qblk, score, -BIG)
        rank = jnp.zeros((nbp, tq), F32)
        for i in range(nb):
            si = score[i:i + 1, :]
            ge = jnp.where(si >= score, 1.0, 0.0)
            gt = jnp.where(si > score, 1.0, 0.0)
            rank = rank + jnp.where(blk > i, ge, gt)
        nsel_t = jnp.where(rank < SEL_TOP, 0.0, jnp.where(blk < nb, 1.0, 0.0))
        nsel = jnp.concatenate([nsel_t, jnp.zeros((LANE - nbp, tq), F32)], axis=0).T
        nsel_ref[0, :, gsl] = nsel.astype(BF16)


def _cmp_select_prompt(ub, kcmp, vcmp, bias_cmp, q_norm_g):
    b, t_len, _ = ub.shape
    tq = 128
    nb = t_len // SEL_BLOCK
    ncp = kcmp.shape[1]
    nc = (t_len - CMP_LEN) // CMP_STRIDE + 1
    assert nb <= LANE and ncp == LANE
    body = functools.partial(_cmp_body, tq=tq, nb=nb)
    return pl.pallas_call(
        body,
        grid=(b, t_len // tq),
        in_specs=[
            pl.BlockSpec((1, tq, NSA_Q_W), lambda bi, i: (bi, i, UB_Q // NSA_Q_W)),
            pl.BlockSpec((1, ncp, NSA_KV_W), lambda bi, i: (bi, 0, 0)),
            pl.BlockSpec((1, ncp, NSA_KV_W), lambda bi, i: (bi, 0, 0)),
            pl.BlockSpec((NSA_HEADS, ncp, tq), lambda bi, i: (0, 0, i)),
            pl.BlockSpec((LANE, ncp), lambda bi, i: (0, 0)),
            pl.BlockSpec((1, HEAD_DIM), lambda bi, i: (0, 0)),
            pl.BlockSpec((1, tq, LANE), lambda bi, i: (bi, i, UB_G // LANE)),
        ],
        out_specs=[
            pl.BlockSpec((1, NSA_Q_W, tq), lambda bi, i: (bi, 0, i)),
            pl.BlockSpec((1, tq, NSA_G * LANE), lambda bi, i: (bi, i, 0)),
        ],
        out_shape=[jax.ShapeDtypeStruct((b, NSA_Q_W, t_len), F32),
                   jax.ShapeDtypeStruct((b, t_len, NSA_G * LANE), BF16)],
        compiler_params=_cparams(("arbitrary", "arbitrary")),
        name="cmp_select",
    )(ub, kcmp, vcmp, jnp.swapaxes(bias_cmp, 1, 2), _sel_matrix(nc, nb, ncp, LANE).T, q_norm_g.reshape(1, HEAD_DIM), ub)


def _attn_body(q_ref, ocmp_ref, nsel_ref, ks_ref, vst_ref, kw_ref, vwt_ref, epen_ref, bt_ref, cb_ref, gate_ref, z_ref, qg_ref,
               o_ref, m_s, l_s, acc_s, osw_s, *, tq):
    qi = pl.program_id(1)
    t0 = qi * tq
    rows = NSA_HPG * tq
    tk = LANE
    tok = t0 + (lax.broadcasted_iota(jnp.int32, (tk, rows), 1) & (tq - 1))
    key = lax.broadcasted_iota(jnp.int32, (tk, rows), 0)
    gates_t = jax.nn.sigmoid(gate_ref[0]).T
    scale = HEAD_DIM ** -0.5
    j_near = jnp.maximum(qi + 1 - bt_ref.shape[0], 0)

    groups = range(NSA_G)
    gsl = [slice(g * HEAD_DIM, (g + 1) * HEAD_DIM) for g in groups]
    heads = [[g * NSA_HPG + hh for hh in range(NSA_HPG)] for g in groups]
    q_sel, q_win = [], []
    for g in groups:
        q4 = jnp.concatenate(
            [(_rms(q_ref[0, :, h * HEAD_DIM:(h + 1) * HEAD_DIM], qg_ref[...]) * scale).astype(BF16) for h in heads[g]], axis=0)
        ns = nsel_ref[0, :, g * LANE:(g + 1) * LANE].astype(F32)
        aug_sel = jnp.concatenate([ns + cb_ref[h:h + 1, :] for h in heads[g]], axis=0).astype(BF16)
        aug_win = jnp.concatenate([jnp.broadcast_to(cb_ref[h:h + 1, :], (tq, LANE)) for h in heads[g]], axis=0).astype(BF16)
        q_sel.append(jnp.concatenate([q4, aug_sel], axis=1))
        q_win.append(jnp.concatenate([q4, aug_win], axis=1))

    def run(k_ref, vt_ref, qmats, lo, window):
        m_s[...] = jnp.full(m_s.shape, NEG, F32)
        l_s[...] = jnp.zeros(l_s.shape, F32)
        acc_s[...] = jnp.zeros(acc_s.shape, F32)

        def step(j, near):
            kb = pl.multiple_of(j * tk, tk)
            kt = k_ref[0, pl.ds(kb, tk), :]
            ep = epen_ref[pl.ds(kb, tk), :]
            ss = [lax.dot_general(jnp.concatenate([kt[:, gsl[g]], ep], axis=1), qmats[g], (((1,), (1,)), ((), ())),
                                  preferred_element_type=F32) for g in groups]
            rel = tok - (kb + key)
            vt = vt_ref[0, j]
            for g in groups:
                s = ss[g]
                if near:
                    s = jnp.where(rel >= 0, bt_ref[qi - j, :, g * rows:(g + 1) * rows] + s, NEG)
                elif window:
                    s = jnp.where(rel < WINDOW, s, NEG)
                m_old = m_s[g]
                m_new = jnp.maximum(m_old, jnp.max(s, axis=0, keepdims=True))
                alpha = jnp.exp(m_old - m_new)
                p = jnp.exp(s - m_new)
                l_s[g] = alpha * l_s[g] + jnp.sum(p, axis=0, keepdims=True)
                acc_s[g] = alpha * acc_s[g] + jnp.dot(vt[gsl[g], :], p.astype(BF16), preferred_element_type=F32)
                m_s[g] = m_new

        def far_step(j, carry):
            step(j, False)
            return carry

        def near_step(j, carry):
            step(j, True)
            return carry

        lax.fori_loop(lo, j_near, far_step, 0)
        lax.fori_loop(j_near, qi + 1, near_step, 0)

    def gate_rows(branch, g):
        return jnp.concatenate([gates_t[branch * NSA_HEADS + h:branch * NSA_HEADS + h + 1, :] for h in heads[g]], axis=1)

    run(ks_ref, vst_ref, q_sel, 0, False)
    for g in groups:
        osw_s[g] = acc_s[g] * (gate_rows(1, g) / l_s[g])
    run(kw_ref, vwt_ref, q_win, jnp.maximum(qi - WINDOW // tk, 0), True)
    for g in groups:
        osw = osw_s[g] + acc_s[g] * (gate_rows(2, g) / l_s[g])
        for hh, h in enumerate(heads[g]):
            hsl = slice(h * HEAD_DIM, (h + 1) * HEAD_DIM)
            o = (ocmp_ref[0, hsl, :] + osw[:, hh * tq:(hh + 1) * tq]).T
            o_ref[0, :, hsl] = (o * _silu(z_ref[0, :, hsl])).astype(o_ref.dtype)


def _far_bias(rel_bias):
    return _bias_of(rel_bias, jnp.full((1,), REL_MAX_DIST, jnp.int32))[:, 0]


def _toeplitz_bias_t(rel_bias, tq, n_near):
    d = jnp.arange(n_near, dtype=jnp.int32)[:, None, None]
    c = jnp.arange(LANE, dtype=jnp.int32)[None, :, None]
    r = jnp.arange(tq, dtype=jnp.int32)[None, None, :]
    bt = _bias_of(rel_bias, d * LANE + r - c) - _far_bias(rel_bias)[:, None, None, None]
    return jnp.transpose(bt, (1, 2, 0, 3)).reshape(n_near, LANE, NSA_HEADS * tq)


AUG_BIAS_LANES = (LANE - 2, LANE - 1)


def _far_bias_columns(rel_bias):
    c = _far_bias(rel_bias)
    hi = c.astype(BF16).astype(F32)
    lo = (c - hi).astype(BF16).astype(F32)
    return jnp.zeros((NSA_HEADS, LANE), F32).at[:, AUG_BIAS_LANES[0]].set(hi).at[:, AUG_BIAS_LANES[1]].set(lo)


def _block_penalty(t_len):
    key = np.arange(t_len)[:, None]
    j = np.arange(LANE)[None, :]
    pen = np.where(key // SEL_BLOCK == j, NEG, 0.0)
    pen[:, AUG_BIAS_LANES[0]:] = 1.0
    return jnp.asarray(pen, BF16)


def _attn_prompt(ub, ocmp, nsel, ksb, vst, kwb, vwt, rel_bias, q_norm_g):
    b, t_len, _ = ub.shape
    tq = 128
    rows = NSA_HPG * tq
    assert t_len // SEL_BLOCK <= AUG_BIAS_LANES[0]
    n_near = -(-(REL_MAX_DIST + LANE - 1) // LANE)
    bt = _toeplitz_bias_t(rel_bias, tq, n_near)
    body = functools.partial(_attn_body, tq=tq)
    k_spec = pl.BlockSpec((1, t_len, NSA_KV_W), lambda bi, i: (bi, 0, 0))
    v_spec = pl.BlockSpec((1, t_len // LANE, NSA_KV_W, LANE), lambda bi, i: (bi, 0, 0, 0))
    return pl.pallas_call(
        body,
        grid=(b, t_len // tq),
        in_specs=[
            pl.BlockSpec((1, tq, NSA_Q_W), lambda bi, i: (bi, i, UB_Q // NSA_Q_W)),
            pl.BlockSpec((1, NSA_Q_W, tq), lambda bi, i: (bi, 0, i)),
            pl.BlockSpec((1, tq, NSA_G * LANE), lambda bi, i: (bi, i, 0)),
            k_spec, v_spec, k_spec, v_spec,
            pl.BlockSpec((t_len, LANE), lambda bi, i: (0, 0)),
            pl.BlockSpec(bt.shape, lambda bi, i: (0, 0, 0)),
            pl.BlockSpec((NSA_HEADS, LANE), lambda bi, i: (0, 0)),
            pl.BlockSpec((1, tq, LANE), lambda bi, i: (bi, i, UB_G // LANE)),
            pl.BlockSpec((1, tq, NSA_Q_W), lambda bi, i: (bi, i, UB_Z // NSA_Q_W)),
            pl.BlockSpec((1, HEAD_DIM), lambda bi, i: (0, 0)),
        ],
        out_specs=pl.BlockSpec((1, tq, NSA_Q_W), lambda bi, i: (bi, i, 0)),
        out_shape=jax.ShapeDtypeStruct((b, t_len, NSA_Q_W), BF16),
        scratch_shapes=[pltpu.VMEM((NSA_G, 1, rows), F32), pltpu.VMEM((NSA_G, 1, rows), F32),
                        pltpu.VMEM((NSA_G, HEAD_DIM, rows), F32), pltpu.VMEM((NSA_G, HEAD_DIM, rows), F32)],
        compiler_params=_cparams(("arbitrary", "arbitrary")),
        name="nsa_attn",
    )(ub, ocmp, nsel, ksb, vst, kwb, vwt, _block_penalty(t_len), bt, _far_bias_columns(rel_bias), ub, ub,
      q_norm_g.reshape(1, HEAD_DIM))


def _q_all(q_ref, qg_ref, ts):
    scale = HEAD_DIM ** -0.5
    zero = jnp.zeros((NSA_HPG * ts, HEAD_DIM), BF16)
    blocks = []
    for g in range(NSA_G):
        q4 = jnp.concatenate(
            [(_rms(q_ref[0, :, (g * NSA_HPG + hh) * HEAD_DIM:(g * NSA_HPG + hh + 1) * HEAD_DIM], qg_ref[...]) * scale).astype(BF16)
             for hh in range(NSA_HPG)], axis=0)
        blocks.append(jnp.concatenate([q4 if gg == g else zero for gg in range(NSA_G)], axis=1))
    return jnp.concatenate(blocks, axis=0)


def _row_to_col(row, n):
    eye = lax.broadcasted_iota(jnp.int32, (n, n), 0) == lax.broadcasted_iota(jnp.int32, (n, n), 1)
    return jnp.sum(jnp.where(eye, row, 0.0), axis=1, keepdims=True)


def _cmp_sample_body(q_ref, kc_ref, vc_ref, bias_ref, msel_ref, rep_ref, qg_ref, ocmp_ref, selt_ref, *, ts, nb, nbp):
    scale = HEAD_DIM ** -0.5
    ncp = kc_ref.shape[1]
    scores = []
    for g in range(NSA_G):
        gsl = slice(g * HEAD_DIM, (g + 1) * HEAD_DIM)
        q4 = jnp.concatenate(
            [_rms(q_ref[0, :, (g * NSA_HPG + hh) * HEAD_DIM:(g * NSA_HPG + hh + 1) * HEAD_DIM], qg_ref[...]) * scale
             for hh in range(NSA_HPG)], axis=0)
        bias = bias_ref[g * NSA_HPG:(g + 1) * NSA_HPG].reshape(NSA_HPG * ts, ncp)
        s = _mm_nt_hi(q4, kc_ref[0, :, gsl]) + bias
        e = jnp.exp(s - jnp.max(s, axis=-1, keepdims=True))
        p = e / jnp.sum(e, axis=-1, keepdims=True) * jnp.where(bias > 0.5 * NEG, 1.0, 0.0)
        o = _mm(p, vc_ref[0, :, gsl])
        imp = p[0:ts]
        for hh in range(NSA_HPG):
            h = g * NSA_HPG + hh
            ocmp_ref[0, :, h * HEAD_DIM:(h + 1) * HEAD_DIM] = o[hh * ts:(hh + 1) * ts]
            if hh:
                imp = imp + p[hh * ts:(hh + 1) * ts]
        scores.append(_mm_hi(imp, msel_ref[...]))
    score = jnp.concatenate(scores, axis=0)
    rows = NSA_G * ts
    lane = lax.broadcasted_iota(jnp.int32, (rows, nbp), 1)
    tok = lax.broadcasted_iota(jnp.int32, (rows, nbp), 0) & (ts - 1)
    qblk = (PAST_LEN + tok) >> 6
    forced = (lane == 0) | (lane == qblk) | (lane == qblk - 1)
    score = jnp.where(forced, BIG, score)
    score = jnp.where(lane <= qblk, score, -BIG)
    sel = jnp.zeros((rows, nbp), F32)
    lane_f = lane.astype(F32)
    for _ in range(SEL_TOP):
        mx = jnp.max(score, axis=-1, keepdims=True)
        first = jnp.min(jnp.where(score == mx, lane_f, float(nbp)), axis=-1, keepdims=True)
        pick = lane_f == first
        sel = jnp.where(pick, 1.0, sel)
        score = jnp.where(pick, -3e38, score)
    selt_ref[0] = _mm_tn(sel, rep_ref[...])


def _cmp_select_sample(ub, kcmp, vcmp, bias_cmp, q_norm_g, nb):
    b, ts, _ = ub.shape
    ncp = kcmp.shape[1]
    nc = ncp - 1
    nbp = -(-nb // LANE) * LANE
    assert ts & (ts - 1) == 0 and NSA_HEADS * ts == LANE
    rep = np.zeros((NSA_G * ts, LANE), np.float32)
    for g in range(NSA_G):
        for hh in range(NSA_HPG):
            for t in range(ts):
                rep[g * ts + t, (g * NSA_HPG + hh) * ts + t] = 1.0
    body = functools.partial(_cmp_sample_body, ts=ts, nb=nb, nbp=nbp)
    return pl.pallas_call(
        body,
        grid=(b,),
        in_specs=[
            pl.BlockSpec((1, ts, NSA_Q_W), lambda bi: (bi, 0, UB_Q // NSA_Q_W)),
            pl.BlockSpec((1, ncp, NSA_KV_W), lambda bi: (bi, 0, 0)),
            pl.BlockSpec((1, ncp, NSA_KV_W), lambda bi: (bi, 0, 0)),
            pl.BlockSpec((NSA_HEADS, ts, ncp), lambda bi: (0, 0, 0)),
            pl.BlockSpec((ncp, nbp), lambda bi: (0, 0)),
            pl.BlockSpec((NSA_G * ts, LANE), lambda bi: (0, 0)),
            pl.BlockSpec((1, HEAD_DIM), lambda bi: (0, 0)),
        ],
        out_specs=[
            pl.BlockSpec((1, ts, NSA_Q_W), lambda bi: (bi, 0, 0)),
            pl.BlockSpec((1, nbp, LANE), lambda bi: (bi, 0, 0)),
        ],
        out_shape=[jax.ShapeDtypeStruct((b, ts, NSA_Q_W), F32), jax.ShapeDtypeStruct((b, nbp, LANE), F32)],
        compiler_params=_cparams(("arbitrary",)),
        name="cmp_select_sample",
    )(ub, kcmp, vcmp, bias_cmp, _sel_matrix(nc, nb, ncp, nbp), jnp.asarray(rep), q_norm_g.reshape(1, HEAD_DIM))


def _sel_sample_body(pt_ref, *refs, pps, n_steps, ts):
    del pt_ref
    k_refs = refs[:pps]
    v_refs = refs[pps:2 * pps]
    (q_ref, selt_ref, knew_ref, vnew_ref, blast_ref, bfar_ref, bnew_ref, qg_ref,
     o_ref, qall_s, sc_s, snew_s, m_s, l_s, acc_s) = refs[2 * pps:]
    s = pl.program_id(1)
    n_pages = n_steps * pps
    sub = lax.broadcasted_iota(jnp.int32, (PAGE, LANE), 0)
    tok = lax.broadcasted_iota(jnp.int32, (ts, LANE), 1) & (ts - 1)
    new_t = lax.broadcasted_iota(jnp.int32, (ts, LANE), 0)

    @pl.when(s == 0)
    def _():
        qall = _q_all(q_ref, qg_ref, ts)
        qall_s[...] = qall
        sn = _mm_nt(knew_ref[0], qall) + bnew_ref[...]
        ok = (new_t <= tok) & (selt_ref[0, 2 * n_pages:2 * n_pages + 1, :] > 0.5)
        sn = jnp.where(ok, sn, NEG)
        snew_s[...] = sn
        m_s[...] = jnp.max(sn, axis=0, keepdims=True)
        l_s[...] = jnp.zeros((1, LANE), F32)
        acc_s[...] = jnp.zeros((LANE, NSA_KV_W), F32)

    @pl.when(s < n_steps)
    def _():
        m = m_s[...]
        for j in range(pps):
            p = s * pps + j
            st = _mm_nt(_load_page(k_refs[j], True), qall_s[...])
            st = st + jnp.where(p == n_pages - 1, blast_ref[...], bfar_ref[...])
            r0 = selt_ref[0, pl.ds(2 * p, 1), :]
            r1 = selt_ref[0, pl.ds(2 * p + 1, 1), :]
            ok = jnp.where(sub < SEL_BLOCK, r0, r1) > 0.5
            st = jnp.where(ok, st, NEG)
            sc_s[pl.ds(pl.multiple_of(p * PAGE, PAGE), PAGE), :] = st
            m = jnp.maximum(m, jnp.max(st, axis=0, keepdims=True))
        m_s[...] = m

    @pl.when(s >= n_steps)
    def _():
        m = m_s[...]
        l = l_s[...]
        acc = acc_s[...]
        for j in range(pps):
            p = (s - n_steps) * pps + j
            pt = jnp.exp(sc_s[pl.ds(pl.multiple_of(p * PAGE, PAGE), PAGE), :] - m)
            l = l + jnp.sum(pt, axis=0, keepdims=True)
            acc = acc + _mm_tn(pt, _load_page(v_refs[j], True))
        l_s[...] = l
        acc_s[...] = acc

    @pl.when(s == 2 * n_steps - 1)
    def _():
        pn = jnp.exp(snew_s[...] - m_s[...])
        l = l_s[...] + jnp.sum(pn, axis=0, keepdims=True)
        acc = acc_s[...] + _mm_tn(pn, vnew_ref[0])
        out = acc / _row_to_col(l, LANE)
        for h in range(NSA_HEADS):
            g = h // NSA_HPG
            o_ref[0, :, h * HEAD_DIM:(h + 1) * HEAD_DIM] = out[h * ts:(h + 1) * ts, g * HEAD_DIM:(g + 1) * HEAD_DIM]


def _lane_bias(rel_bias, key_pos, ts):
    rel = (PAST_LEN + jnp.arange(ts, dtype=jnp.int32))[None, :] - key_pos[:, None]
    return jnp.moveaxis(_bias_of(rel_bias, rel), 0, 1).reshape(key_pos.shape[0], NSA_HEADS * ts)


def _sel_sample(ub, selt, cache_k, cache_v, table, k_new, rel_bias, q_norm_g):
    b, ts, _ = ub.shape
    n_pages = table.shape[1]
    assert PAGE >= REL_MAX_DIST and n_pages * PAGE == PAST_LEN
    pps = min(PAGES_PER_STEP, n_pages)
    n_steps = n_pages // pps
    ar = jnp.arange
    blast = _lane_bias(rel_bias, PAST_LEN - PAGE + ar(PAGE, dtype=jnp.int32), ts)
    bfar = _lane_bias(rel_bias, jnp.zeros((1,), jnp.int32), ts)
    bnew = _lane_bias(rel_bias, PAST_LEN + ar(ts, dtype=jnp.int32), ts)
    body = functools.partial(_sel_sample_body, pps=pps, n_steps=n_steps, ts=ts)
    last = n_steps - 1
    page_block = (1, NSA_G * PAGE, HEAD_DIM)
    k_spec = lambda j: pl.BlockSpec(page_block, lambda bi, s, pt: (pt[bi, jnp.minimum(s, last) * pps + j], 0, 0))
    v_spec = lambda j: pl.BlockSpec(page_block, lambda bi, s, pt: (pt[bi, jnp.maximum(s - n_steps, 0) * pps + j], 0, 0))
    full = lambda shape: pl.BlockSpec(shape, lambda bi, s, pt: (0,) * len(shape))
    in_specs = [k_spec(j) for j in range(pps)] + [v_spec(j) for j in range(pps)] + [
        pl.BlockSpec((1, ts, NSA_Q_W), lambda bi, s, pt: (bi, 0, UB_Q // NSA_Q_W)),
        pl.BlockSpec((1,) + selt.shape[1:], lambda bi, s, pt: (bi, 0, 0)),
        pl.BlockSpec((1, ts, NSA_KV_W), lambda bi, s, pt: (bi, 0, 0)),
        pl.BlockSpec((1, ts, NSA_KV_W), lambda bi, s, pt: (bi, 0, UB_KV // NSA_KV_W + 3)),
        full((PAGE, LANE)), full((1, LANE)), full((ts, LANE)), full((1, HEAD_DIM)),
    ]
    return pl.pallas_call(
        body,
        grid_spec=pltpu.PrefetchScalarGridSpec(
            num_scalar_prefetch=1, grid=(b, 2 * n_steps), in_specs=in_specs,
            out_specs=pl.BlockSpec((1, ts, NSA_Q_W), lambda bi, s, pt: (bi, 0, 0)),
            scratch_shapes=[pltpu.VMEM((LANE, NSA_KV_W), BF16), pltpu.VMEM((PAST_LEN, LANE), F32),
                            pltpu.VMEM((ts, LANE), F32), pltpu.VMEM((1, LANE), F32), pltpu.VMEM((1, LANE), F32),
                            pltpu.VMEM((LANE, NSA_KV_W), F32)]),
        out_shape=jax.ShapeDtypeStruct((b, ts, NSA_Q_W), F32),
        compiler_params=_cparams(("arbitrary", "arbitrary")),
        name="sel_sample",
    )(table, *([cache_k] * pps), *([cache_v] * pps), ub, selt, k_new, ub, blast, bfar, bnew, q_norm_g.reshape(1, HEAD_DIM))


def _win_sample_body(q_ref, kc_ref, vc_ref, knew_ref, vnew_ref, bc_ref, bn_ref, gate_ref, z_ref, ocmp_ref, osel_ref, qg_ref,
                     o_ref, kout_ref, vout_ref, *, ts, wb):
    qall = _q_all(q_ref, qg_ref, ts)
    key_i = lax.broadcasted_iota(jnp.int32, (wb, LANE), 0)
    tok_c = lax.broadcasted_iota(jnp.int32, (wb, LANE), 1) & (ts - 1)
    tok_n = lax.broadcasted_iota(jnp.int32, (ts, LANE), 1) & (ts - 1)
    new_t = lax.broadcasted_iota(jnp.int32, (ts, LANE), 0)
    sc = _mm_nt(_load_page(kc_ref, True), qall) + bc_ref[...]
    sc = jnp.where(wb + tok_c - key_i < WINDOW, sc, NEG)
    sn = _mm_nt(knew_ref[0], qall) + bn_ref[...]
    sn = jnp.where(new_t <= tok_n, sn, NEG)
    m = jnp.maximum(jnp.max(sc, axis=0, keepdims=True), jnp.max(sn, axis=0, keepdims=True))
    pc = jnp.exp(sc - m)
    pn = jnp.exp(sn - m)
    l = jnp.sum(pc, axis=0, keepdims=True) + jnp.sum(pn, axis=0, keepdims=True)
    out = (_mm_tn(pc, _load_page(vc_ref, True)) + _mm_tn(pn, vnew_ref[0])) / _row_to_col(l, LANE)
    gates = jax.nn.sigmoid(gate_ref[0])
    for h in range(NSA_HEADS):
        g = h // NSA_HPG
        hsl = slice(h * HEAD_DIM, (h + 1) * HEAD_DIM)
        o_win = out[h * ts:(h + 1) * ts, g * HEAD_DIM:(g + 1) * HEAD_DIM]
        o = (gates[:, h:h + 1] * ocmp_ref[0, :, hsl] + gates[:, NSA_HEADS + h:NSA_HEADS + h + 1] * osel_ref[0, :, hsl]
             + gates[:, 2 * NSA_HEADS + h:2 * NSA_HEADS + h + 1] * o_win)
        o_ref[0, :, hsl] = (o * _silu(z_ref[0, :, hsl])).astype(o_ref.dtype)
    keep = (wb - ts) * NSA_G
    for out_ref, old_ref, new_ref in ((kout_ref, kc_ref, knew_ref), (vout_ref, vc_ref, vnew_ref)):
        out_ref[0, 0:keep, :] = old_ref[0, ts * NSA_G:wb * NSA_G, :]
        for g in range(NSA_G):
            out_ref[0, pl.ds(keep + g, ts, stride=NSA_G), :] = new_ref[0, :, g * HEAD_DIM:(g + 1) * HEAD_DIM]


def _win_sample(ub, cache_k, cache_v, k_new, ocmp, osel, rel_bias, q_norm_g):
    b, ts, _ = ub.shape
    wb = cache_k.shape[1] // NSA_G
    assert wb == WINDOW and ts % 8 == 0
    bc = _lane_bias(rel_bias, PAST_LEN - wb + jnp.arange(wb, dtype=jnp.int32), ts)
    bn = _lane_bias(rel_bias, PAST_LEN + jnp.arange(ts, dtype=jnp.int32), ts)
    body = functools.partial(_win_sample_body, ts=ts, wb=wb)
    tok_spec = lambda w, c: pl.BlockSpec((1, ts, w), lambda bi: (bi, 0, c))
    win_spec = pl.BlockSpec((1, wb * NSA_G, HEAD_DIM), lambda bi: (bi, 0, 0))
    return pl.pallas_call(
        body,
        grid=(b,),
        in_specs=[
            tok_spec(NSA_Q_W, UB_Q // NSA_Q_W), win_spec, win_spec,
            tok_spec(NSA_KV_W, 0), tok_spec(NSA_KV_W, UB_KV // NSA_KV_W + 5),
            pl.BlockSpec((wb, LANE), lambda bi: (0, 0)), pl.BlockSpec((ts, LANE), lambda bi: (0, 0)),
            tok_spec(LANE, UB_G // LANE), tok_spec(NSA_Q_W, UB_Z // NSA_Q_W),
            tok_spec(NSA_Q_W, 0), tok_spec(NSA_Q_W, 0),
            pl.BlockSpec((1, HEAD_DIM), lambda bi: (0, 0)),
        ],
        out_specs=[tok_spec(NSA_Q_W, 0), win_spec, win_spec],
        out_shape=[jax.ShapeDtypeStruct((b, ts, NSA_Q_W), BF16),
                   jax.ShapeDtypeStruct((b, wb * NSA_G, HEAD_DIM), F32),
                   jax.ShapeDtypeStruct((b, wb * NSA_G, HEAD_DIM), F32)],
        compiler_params=_cparams(("arbitrary",)),
        name="win_sample",
    )(ub, cache_k, cache_v, k_new, ub, bc, bn, ub, ub, ocmp, osel, q_norm_g.reshape(1, HEAD_DIM))


def _mix_body(oa_ref, ob_ref, wa_ref, wb_ref, ma_ref, mb_ref, o_ref):
    pa = jnp.dot(oa_ref[...], wa_ref[...], preferred_element_type=F32)
    pb = jnp.dot(ob_ref[...], wb_ref[...], preferred_element_type=F32)
    o_ref[...] = (jax.nn.sigmoid(ma_ref[...]) * pa + jax.nn.sigmoid(mb_ref[...]) * pb).astype(o_ref.dtype)


def _out_body(m_ref, w_ref, x_ref, y_ref):
    y_ref[...] = x_ref[...] + jnp.dot(m_ref[...], w_ref[...], preferred_element_type=F32)


def _output(x2d, oa, ob, ub2d, wa16, wb16, wo16):
    m = x2d.shape[0]
    tm = min(m, 512)
    tn = 512
    grid = (m // tm, D_MODEL // tn)
    mixed = pl.pallas_call(
        _mix_body,
        grid=grid,
        in_specs=[
            pl.BlockSpec((tm, GDN_V_W), lambda i, j: (i, 0)),
            pl.BlockSpec((tm, NSA_Q_W), lambda i, j: (i, 0)),
            pl.BlockSpec((GDN_V_W, tn), lambda i, j: (0, j)),
            pl.BlockSpec((NSA_Q_W, tn), lambda i, j: (0, j)),
            pl.BlockSpec((tm, tn), lambda i, j: (i, UB_MA // tn + j)),
            pl.BlockSpec((tm, tn), lambda i, j: (i, UB_MB // tn + j)),
        ],
        out_specs=pl.BlockSpec((tm, tn), lambda i, j: (i, j)),
        out_shape=jax.ShapeDtypeStruct((m, D_MODEL), BF16),
        compiler_params=_cparams(("arbitrary", "arbitrary")),
        name="branch_mix",
    )(oa, ob, wa16, wb16, ub2d, ub2d)
    return pl.pallas_call(
        _out_body,
        grid=grid,
        in_specs=[
            pl.BlockSpec((tm, D_MODEL), lambda i, j: (i, 0)),
            pl.BlockSpec((D_MODEL, tn), lambda i, j: (0, j)),
            pl.BlockSpec((tm, tn), lambda i, j: (i, j)),
        ],
        out_specs=pl.BlockSpec((tm, tn), lambda i, j: (i, j)),
        out_shape=jax.ShapeDtypeStruct((m, D_MODEL), F32),
        compiler_params=_cparams(("arbitrary", "arbitrary")),
        name="out_proj",
    )(mixed, wo16, x2d)


def _kv_slices(ub):
    b, t_len, _ = ub.shape
    out = []
    for i in (0, 1, 3, 5):
        a = ub[:, :, UB_KV + i * NSA_KV_W:UB_KV + (i + 1) * NSA_KV_W]
        out.append(a.reshape(b, t_len, NSA_G, HEAD_DIM))
    return out


def kernel(x_prompt, x_sample, cache_k_cmp, cache_v_cmp, cache_k_sel, cache_v_sel, cache_k_win, cache_v_win, state_conv, state_gdn, page_table, norm_g, w_in, gdn_conv_w, gdn_a_log, gdn_dt_bias, gdn_norm_g, q_norm_g, k_norm_g, cmp_pe_k, cmp_w_k, cmp_proj_k, cmp_pe_v, cmp_w_v, cmp_proj_v, rel_bias, w_branch_a, w_branch_b, w_out):
    bp, tp, _ = x_prompt.shape
    bs, ts, _ = x_sample.shape
    wb = cache_k_win.shape[1]
    n_pool = cache_k_cmp.shape[0]
    kv4 = lambda a: a.reshape(a.shape[0], a.shape[1], NSA_G, HEAD_DIM)
    cmp_p = (cmp_pe_k, cmp_w_k, cmp_proj_k, cmp_pe_v, cmp_w_v, cmp_proj_v)

    assert _IN_MB + D_MODEL == w_in.shape[1]
    w_t = w_in.T
    wa16, wb16, wo16 = w_branch_a.astype(BF16), w_branch_b.astype(BF16), w_out.astype(BF16)

    def project(x):
        x2d = x.reshape(-1, D_MODEL)
        ua = _project(x2d, norm_g, w_t, UA_OFFSETS).reshape(x.shape[0], x.shape[1], UA_W)
        ub = _project(x2d, norm_g, w_t, UB_OFFSETS).reshape(x.shape[0], x.shape[1], UB_W)
        return x2d, ua, ub

    x2d, ua, ub = project(x_prompt)
    conv0 = jnp.zeros((bp, GDN_CONV - 1, GDN_QKV_W), F32)
    s0 = jnp.zeros((bp, GDN_V_HEADS, HEAD_DIM, HEAD_DIM), F32)
    o_a, p_gdn = _gdn(ua, ub, conv0, s0, gdn_conv_w, gdn_a_log, gdn_dt_bias, gdn_norm_g)
    p_conv = ua[:, tp - (GDN_CONV - 1):, :GDN_QKV_W]
    n_pages_p = tp // PAGE
    pages = ub.reshape(bp * n_pages_p, PAGE, UB_W)
    ident = jnp.arange(bp * n_pages_p, dtype=jnp.int32).reshape(bp, n_pages_p)
    kcmp, vcmp = _compress(pages, pages, UB_KV // NSA_KV_W, UB_KV // NSA_KV_W + 1, ident, *cmp_p, k_norm_g[0])
    p_kc, p_vc, p_ks, p_vs, p_kw, p_vw, ksb, vsb, kwb, vwb = _nsa_prep_prompt(ub, k_norm_g, wb)
    nc_p = (tp - CMP_LEN) // CMP_STRIDE + 1
    bias_cmp = _cmp_bias(rel_bias, jnp.arange(tp, dtype=jnp.int32), kcmp.shape[1], nc_p)
    ocmp, nsel = _cmp_select_prompt(ub, kcmp, vcmp, bias_cmp, q_norm_g)
    o_b = _attn_prompt(ub, ocmp, nsel, ksb, vsb, kwb, vwb, rel_bias, q_norm_g)
    y_prompt = _output(x2d, o_a.reshape(-1, GDN_V_W), o_b.reshape(-1, NSA_Q_W), ub.reshape(-1, UB_W), wa16, wb16, wo16)
    cache4 = lambda a: a.reshape(bp, a.shape[1] // NSA_G, NSA_G, HEAD_DIM)
    prompt_out = (y_prompt.reshape(x_prompt.shape), cache4(p_kc), cache4(p_vc), cache4(p_ks), cache4(p_vs), cache4(p_kw),
                  cache4(p_vw), p_conv, p_gdn)

    x2d, ua, ub = project(x_sample)
    o_a, s_gdn = _gdn(ua, ub, state_conv, state_gdn, gdn_conv_w, gdn_a_log, gdn_dt_bias, gdn_norm_g)
    s_conv = ua[:, ts - (GDN_CONV - 1):, :GDN_QKV_W]
    pool = lambda c: c.reshape(n_pool, PAGE * NSA_G, HEAD_DIM)
    kcmp, vcmp = _compress(pool(cache_k_cmp), pool(cache_v_cmp), 0, 0, page_table, *cmp_p, k_norm_g[0])
    s_ks, s_kwn = _nsa_prep(ub, k_norm_g)
    total = PAST_LEN + ts
    nc_s = (total - CMP_LEN) // CMP_STRIDE + 1
    nb_s = -(-total // SEL_BLOCK)
    assert nc_s + 1 == kcmp.shape[1]
    pos_s = PAST_LEN + jnp.arange(ts, dtype=jnp.int32)
    bias_cmp = _cmp_bias(rel_bias, pos_s, kcmp.shape[1], nc_s)
    ocmp, selt = _cmp_select_sample(ub, kcmp, vcmp, bias_cmp, q_norm_g, nb_s)
    osel = _sel_sample(ub, selt, pool(cache_k_sel), pool(cache_v_sel), page_table, s_ks, rel_bias, q_norm_g)
    win3 = lambda c: c.reshape(bs, wb * NSA_G, HEAD_DIM)
    o_b, s_kw, s_vw = _win_sample(ub, win3(cache_k_win), win3(cache_v_win), s_kwn, ocmp, osel, rel_bias, q_norm_g)
    y_sample = _output(x2d, o_a.reshape(-1, GDN_V_W), o_b.reshape(-1, NSA_Q_W), ub.reshape(-1, UB_W), wa16, wb16, wo16)
    s_kc, s_vc, s_vs, _ = _kv_slices(ub)
    sample_out = (y_sample.reshape(x_sample.shape), s_kc, s_vc, kv4(s_ks), s_vs, s_kw.reshape(cache_k_win.shape),
                  s_vw.reshape(cache_v_win.shape), s_conv, s_gdn)

    return (prompt_out[0], sample_out[0]) + prompt_out[1:] + sample_out[1:]
```

```python
import functools
import math

import jax
import jax.numpy as jnp
import numpy as np
from jax import lax
from jax.experimental import pallas as pl
from jax.experimental.pallas import tpu as pltpu

F32 = jnp.float32
BF16 = jnp.bfloat16
HI = lax.Precision.HIGHEST

D_MODEL = 2048
PAST_LEN = 16384
PAGE = 128

GDN_QK_HEADS = 16
GDN_V_HEADS = 32
HEAD_DIM = 128
GDN_CONV = 4
GDN_CHUNK = 64
GDN_GROUP = 8
GDN_QK_PER_STEP = 2
GDN_QK_W = GDN_QK_HEADS * HEAD_DIM
GDN_V_W = GDN_V_HEADS * HEAD_DIM
GDN_QKV_W = 2 * GDN_QK_W + GDN_V_W

NSA_HEADS = 16
NSA_G = 4
NSA_HPG = NSA_HEADS // NSA_G
NSA_Q_W = NSA_HEADS * HEAD_DIM
NSA_KV_W = NSA_G * HEAD_DIM
CMP_LEN = 32
CMP_STRIDE = 16
SEL_BLOCK = 64
SEL_TOP = 16
WINDOW = 512
REL_BUCKETS = 32
REL_MAX_DIST = 128

EPS = 1e-6
NEG = -1e30
BIG = 1e9

LANE = 128
VMEM_LIMIT = 56 * 1024 * 1024
PAGES_PER_STEP = 16

UA_W = GDN_QKV_W + GDN_V_W
UB_Q, UB_Z, UB_MA, UB_MB, UB_KV, UB_BA, UB_G = 0, 2048, 4096, 6144, 8192, 11264, 11776
UB_W = 12288
PROJ_TN = 512
PROJ_TM = 2048

_IN_QB = UA_W + 2 * GDN_V_HEADS
_IN_KV = _IN_QB + NSA_Q_W
_IN_G = _IN_KV + 6 * NSA_KV_W
_IN_ZB = _IN_G + 3 * NSA_HEADS
_IN_MA = _IN_ZB + NSA_Q_W
_IN_MB = _IN_MA + D_MODEL


def _tiles(start, width):
    return [start + k * PROJ_TN for k in range(width // PROJ_TN)]


UA_OFFSETS = _tiles(0, UA_W)
UB_OFFSETS = (_tiles(_IN_QB, NSA_Q_W) + _tiles(_IN_ZB, NSA_Q_W) + _tiles(_IN_MA, D_MODEL) + _tiles(_IN_MB, D_MODEL)
              + _tiles(_IN_KV, 6 * NSA_KV_W) + [UA_W, _IN_G])


def _mm(a, b):
    return jnp.dot(a.astype(BF16), b.astype(BF16), preferred_element_type=F32)


def _mm_nt(a, b):
    return lax.dot_general(a.astype(BF16), b.astype(BF16), (((1,), (1,)), ((), ())), preferred_element_type=F32)


def _mm_tn(a, b):
    return lax.dot_general(a.astype(BF16), b.astype(BF16), (((0,), (0,)), ((), ())), preferred_element_type=F32)


def _mm_hi(a, b):
    return jnp.dot(a, b, precision=HI, preferred_element_type=F32)


def _mm_nt_hi(a, b):
    return lax.dot_general(a, b, (((1,), (1,)), ((), ())), precision=HI, preferred_element_type=F32)


def _silu(x):
    return x * jax.nn.sigmoid(x)


def _softplus(x):
    return jnp.maximum(x, 0.0) + jnp.log1p(jnp.exp(-jnp.abs(x)))


def _rms(x, gain):
    return x * lax.rsqrt(jnp.mean(x * x, axis=-1, keepdims=True) + EPS) * gain


def _cparams(sem):
    return pltpu.CompilerParams(dimension_semantics=sem, vmem_limit_bytes=VMEM_LIMIT)


def _proj_body(offs_ref, x_ref, g_ref, w_ref, o_ref, h_ref):
    del offs_ref

    @pl.when(pl.program_id(1) == 0)
    def _():
        h_ref[...] = _rms(x_ref[...], g_ref[...]).astype(BF16)

    o_ref[...] = lax.dot_general(h_ref[...], w_ref[...].astype(BF16), (((1,), (1,)), ((), ())),
                                 preferred_element_type=F32)


def _project(x2d, norm_g, w_t, offsets):
    m = x2d.shape[0]
    tm = min(m, PROJ_TM)
    n_tiles = len(offsets)
    return pl.pallas_call(
        _proj_body,
        grid_spec=pltpu.PrefetchScalarGridSpec(
            num_scalar_prefetch=1,
            grid=(m // tm, n_tiles),
            in_specs=[
                pl.BlockSpec((tm, D_MODEL), lambda i, j, offs: (i, 0), pipeline_mode=pl.Buffered(1)),
                pl.BlockSpec((1, D_MODEL), lambda i, j, offs: (0, 0)),
                pl.BlockSpec((pl.Element(PROJ_TN), pl.Element(D_MODEL)), lambda i, j, offs: (pl.multiple_of(offs[j], 16), 0)),
            ],
            out_specs=pl.BlockSpec((tm, PROJ_TN), lambda i, j, offs: (i, j)),
            scratch_shapes=[pltpu.VMEM((tm, D_MODEL), BF16)]),
        out_shape=jax.ShapeDtypeStruct((m, n_tiles * PROJ_TN), F32),
        compiler_params=_cparams(("arbitrary", "arbitrary")),
        name="proj",
    )(jnp.asarray(offsets, jnp.int32), x2d, norm_g.reshape(1, D_MODEL), w_t)


def _split2(a):
    hi = a.astype(BF16)
    return hi, (a - hi.astype(F32)).astype(BF16)


def _dot16(a, b):
    return jnp.dot(a, b, preferred_element_type=F32)


def _mm_x3(a2, b2):
    return _dot16(a2[0], b2[0]) + (_dot16(a2[0], b2[1]) + _dot16(a2[1], b2[0]))


def _bdot(a, b):
    return lax.dot_general(a, b, (((2,), (1,)), ((0,), (0,))), preferred_element_type=F32)


def _bdot_nt(a, b):
    return lax.dot_general(a, b, (((2,), (2,)), ((0,), (0,))), preferred_element_type=F32)


def _bmm_x3(a2, b2):
    return _bdot(a2[0], b2[0]) + (_bdot(a2[0], b2[1]) + _bdot(a2[1], b2[0]))


def _unit_lower_inverse(nmat, c):
    row = lax.broadcasted_iota(jnp.int32, (1, c, c), 1)
    col = lax.broadcasted_iota(jnp.int32, (1, c, c), 2)
    p = jnp.where(row == col, 1.0, 0.0) + nmat
    m2 = _split2(nmat)
    span = 2
    while span < c:
        m2 = _split2(_bmm_x3(m2, m2))
        p = p + _bmm_x3(_split2(p), m2)
        span *= 2
    return p


def _unit_lower_inverse_pairs(nmat, eye, left, c):
    def blockdiag(x2):
        return tuple(jnp.concatenate([jnp.where(left, x, jnp.zeros_like(x)), jnp.where(left, jnp.zeros_like(x), x)], axis=1)
                     for x in x2)

    p = jnp.where(eye, 1.0, 0.0) + nmat
    m2 = _split2(nmat)
    bd2 = blockdiag(m2)
    span = 2
    while span < c:
        m2 = _split2(_bmm_x3(m2, bd2))
        bd2 = blockdiag(m2)
        p = p + _bmm_x3(_split2(p), bd2)
        span *= 2
    return p


def _cumsum_rows(tril16, g):
    h = g.astype(BF16)
    r = g - h.astype(F32)
    m = r.astype(BF16)
    l = (r - m.astype(F32)).astype(BF16)
    return _dot16(tril16, h) + (_dot16(tril16, m) + _dot16(tril16, l))


def _gdn_body(q_ref, k_ref, v_ref, z_ref, ba_ref, cwq_ref, cwk_ref, cwv_ref, csq_ref, csk_ref, csv_ref,
              s0_ref, gp_ref, ng_ref, o_ref, sfin_ref, cq_s, ck_s, cv_s, u_s, w_s, qg_s, kd_s, at_s, gl_s, st_s,
              *, t_len, c, nq):
    nv = 2 * nq
    first_head = nv * pl.program_id(1)
    n_chunks = t_len // c
    for cs, carry in ((csq_ref, cq_s), (csk_ref, ck_s), (csv_ref, cv_s)):
        carry[...] = jnp.zeros(carry.shape, F32)
        carry[5:8, :] = cs[0]
    st_s[...] = s0_ref[0]

    gsz = min(GDN_GROUP, n_chunks)
    rg = gsz * c
    row = lax.broadcasted_iota(jnp.int32, (1, c, 2 * c), 1)
    lane2 = lax.broadcasted_iota(jnp.int32, (1, c, 2 * c), 2)
    left = lane2 < c
    col = lane2 & (c - 1)
    tril = row >= col
    strict = row > col
    eye = row == col
    lane = lax.broadcasted_iota(jnp.int32, (rg, LANE), 1)
    rr = lax.broadcasted_iota(jnp.int32, (rg, rg), 0)
    cc = lax.broadcasted_iota(jnp.int32, (rg, rg), 1)
    shift = c.bit_length() - 1
    assert 1 << shift == c
    tril16 = jnp.where((rr >= cc) & ((rr >> shift) == (cc >> shift)), 1.0, 0.0).astype(BF16)

    def conv(src_ref, carry, cw_ref, rows):
        cur = src_ref[0, rows, :]
        win = jnp.concatenate([carry[...], cur], axis=0)
        carry[...] = cur[rg - 8:rg]
        w = cw_ref[...]
        a = win[5:5 + rg] * w[0:1] + win[6:6 + rg] * w[1:2] + win[7:7 + rg] * w[2:3] + win[8:8 + rg] * w[3:4]
        return _silu(a)

    def pick(x, idx):
        return jnp.sum(jnp.where(lane == idx, x, 0.0), axis=-1, keepdims=True).reshape(gsz, c, 1)

    def per_head(x, n):
        return [x[:, i * HEAD_DIM:(i + 1) * HEAD_DIM].reshape(gsz, c, HEAD_DIM) for i in range(n)]

    def prepare(gi):
        base = gi * rg if isinstance(gi, int) else pl.multiple_of(gi * rg, rg)
        rows = pl.ds(base, rg)
        qa = per_head(conv(q_ref, cq_s, cwq_ref, rows), nq)
        ka = per_head(conv(k_ref, ck_s, cwk_ref, rows), nq)
        va = per_head(conv(v_ref, cv_s, cwv_ref, rows), nv)
        qn = [x * lax.rsqrt(jnp.sum(x * x, axis=-1, keepdims=True) + EPS) * (HEAD_DIM ** -0.5) for x in qa]
        kn = [x * lax.rsqrt(jnp.sum(x * x, axis=-1, keepdims=True) + EPS) for x in ka]
        ba = ba_ref[0, rows, :]
        beta_all = jax.nn.sigmoid(ba)
        g_all = -jnp.exp(gp_ref[0:1, :]) * _softplus(ba + gp_ref[1:2, :])
        gc_all = _cumsum_rows(tril16, g_all)
        k16 = jnp.concatenate(kn, axis=0).astype(BF16)
        k2x = jnp.concatenate([k16, k16], axis=1)
        kk = _bdot_nt(k16, k2x)
        qk = _bdot_nt(jnp.concatenate(qn, axis=0).astype(BF16), k2x)
        betas, gcols, blocks = [], [], []
        zeros = jnp.zeros((gsz, c, 2 * HEAD_DIM), F32)
        for qi in range(nq):
            pair = []
            for l in range(2):
                hh = 2 * qi + l
                beta = pick(beta_all, first_head + hh)
                gcol = pick(gc_all, GDN_V_HEADS + first_head + hh)
                egc = jnp.exp(gcol)
                glast = gcol[:, c - 1:c, :]
                rhs = jnp.concatenate([va[hh] * beta, kn[qi] * (beta * egc)], axis=2)
                pair.append((beta, gcol, rhs))
                qg_s[hh, rows, :] = (qn[qi] * egc).reshape(rg, HEAD_DIM).astype(qg_s.dtype)
                kd_s[hh, rows, :] = (kn[qi] * jnp.exp(glast - gcol)).reshape(rg, HEAD_DIM).astype(kd_s.dtype)
                egl = jnp.exp(glast)
                for j in range(gsz):
                    gl_s[hh, gi * gsz + j] = jnp.broadcast_to(egl[j], (8, HEAD_DIM))
            betas.append(jnp.where(left, pair[0][0], pair[1][0]))
            gcols.append(jnp.where(left, pair[0][1], pair[1][1]))
            blocks.append(jnp.concatenate([jnp.concatenate([pair[0][2], zeros], axis=2),
                                           jnp.concatenate([zeros, pair[1][2]], axis=2)], axis=1))
        beta = jnp.concatenate(betas, axis=0)
        gcol = jnp.concatenate(gcols, axis=0)
        grow = jnp.sum(jnp.where(eye, gcol, 0.0), axis=1, keepdims=True)
        gamma = jnp.where(tril, jnp.exp(jnp.minimum(gcol - grow, 0.0)), 0.0)
        tinv = _unit_lower_inverse_pairs(jnp.where(strict, -(kk * beta * gamma), 0.0), eye, left, c)
        rhs2 = _split2(jnp.concatenate(blocks, axis=0))
        uw = _bmm_x3(_split2(tinv), rhs2)
        attn = qk * gamma
        for qi in range(nq):
            at_s[qi, rows, :] = attn[qi * gsz:(qi + 1) * gsz].reshape(rg, 2 * c).astype(at_s.dtype)
            for l in range(2):
                uw_h = uw[qi * gsz:(qi + 1) * gsz, :, 2 * l * HEAD_DIM:2 * (l + 1) * HEAD_DIM]
                u_s[2 * qi + l, rows, :] = uw_h[:, :, :HEAD_DIM].reshape(rg, HEAD_DIM)
                w_s[2 * qi + l, rows, :] = uw_h[:, :, HEAD_DIM:].reshape(rg, HEAD_DIM).astype(w_s.dtype)

    def recur(ci):
        rows = pl.ds(ci * c if isinstance(ci, int) else pl.multiple_of(ci * c, c), c)
        s = st_s[...]
        s16 = s.astype(BF16)
        v_new = u_s[:, rows, :] - _bdot(w_s[:, rows, :].astype(BF16), s16)
        v16 = v_new.astype(BF16)
        zv = jnp.zeros((c, HEAD_DIM), BF16)
        vbd = jnp.stack([jnp.concatenate([jnp.concatenate([v16[2 * qi], zv], axis=1),
                                          jnp.concatenate([zv, v16[2 * qi + 1]], axis=1)], axis=0) for qi in range(nq)], axis=0)
        o_state = _bdot(qg_s[:, rows, :].astype(BF16), s16)
        o_attn = _bdot(at_s[:, rows, :].astype(BF16), vbd)
        kd16 = kd_s[:, rows, :].astype(BF16)
        upd = jnp.stack([_mm_tn(kd16[hh], v16[hh]) for hh in range(nv)], axis=0)
        st_s[...] = s * gl_s[:, ci][:, 0:1, :] + upd
        for hh in range(nv):
            hsl = slice(hh * HEAD_DIM, (hh + 1) * HEAD_DIM)
            o = o_state[hh] + o_attn[hh // 2][:, (hh % 2) * HEAD_DIM:(hh % 2 + 1) * HEAD_DIM]
            o_ref[0, rows, hsl] = (_rms(o, ng_ref[...]) * _silu(z_ref[0, rows, hsl])).astype(o_ref.dtype)

    if n_chunks == 1:
        prepare(0)
        recur(0)
    else:
        def prepare_step(gi, carry):
            prepare(gi)
            return carry

        def recur_step(ci, carry):
            recur(ci)
            return carry

        lax.fori_loop(0, n_chunks // gsz, prepare_step, 0)
        lax.fori_loop(0, n_chunks, recur_step, 0)
    sfin_ref[0] = st_s[...]


def _gdn(ua, ub, conv_state, s0, conv_w, a_log, dt_bias, norm_g):
    b, t_len, _ = ua.shape
    c = min(GDN_CHUNK, t_len)
    hd = HEAD_DIM
    nq = GDN_QK_PER_STEP if t_len > GDN_CHUNK else 4 * GDN_QK_PER_STEP
    nv = 2 * nq
    steps = GDN_QK_HEADS // nq
    qw, vw = nq * hd, nv * hd
    op_dtype = BF16 if c % 16 == 0 else F32
    gp = jnp.zeros((2, LANE), F32)
    gp = gp.at[0, GDN_V_HEADS:2 * GDN_V_HEADS].set(a_log).at[1, GDN_V_HEADS:2 * GDN_V_HEADS].set(dt_bias)
    body = functools.partial(_gdn_body, t_len=t_len, c=c, nq=nq)
    return pl.pallas_call(
        body,
        grid=(b, steps),
        in_specs=[
            pl.BlockSpec((1, t_len, qw), lambda bi, i: (bi, 0, i)),
            pl.BlockSpec((1, t_len, qw), lambda bi, i: (bi, 0, steps + i)),
            pl.BlockSpec((1, t_len, vw), lambda bi, i: (bi, 0, steps + i)),
            pl.BlockSpec((1, t_len, vw), lambda bi, i: (bi, 0, 2 * steps + i)),
            pl.BlockSpec((1, t_len, LANE), lambda bi, i: (bi, 0, UB_BA // LANE)),
            pl.BlockSpec((GDN_CONV, qw), lambda bi, i: (0, i)),
            pl.BlockSpec((GDN_CONV, qw), lambda bi, i: (0, steps + i)),
            pl.BlockSpec((GDN_CONV, vw), lambda bi, i: (0, steps + i)),
            pl.BlockSpec((1, GDN_CONV - 1, qw), lambda bi, i: (bi, 0, i)),
            pl.BlockSpec((1, GDN_CONV - 1, qw), lambda bi, i: (bi, 0, steps + i)),
            pl.BlockSpec((1, GDN_CONV - 1, vw), lambda bi, i: (bi, 0, steps + i)),
            pl.BlockSpec((1, nv, hd, hd), lambda bi, i: (bi, i, 0, 0)),
            pl.BlockSpec((2, LANE), lambda bi, i: (0, 0)),
            pl.BlockSpec((1, hd), lambda bi, i: (0, 0)),
        ],
        out_specs=[
            pl.BlockSpec((1, t_len, vw), lambda bi, i: (bi, 0, i)),
            pl.BlockSpec((1, nv, hd, hd), lambda bi, i: (bi, i, 0, 0)),
        ],
        out_shape=[
            jax.ShapeDtypeStruct((b, t_len, GDN_V_W), BF16),
            jax.ShapeDtypeStruct((b, GDN_V_HEADS, hd, hd), F32),
        ],
        scratch_shapes=[
            pltpu.VMEM((8, qw), F32),
            pltpu.VMEM((8, qw), F32),
            pltpu.VMEM((8, vw), F32),
            pltpu.VMEM((nv, t_len, hd), F32),
            pltpu.VMEM((nv, t_len, hd), op_dtype),
            pltpu.VMEM((nv, t_len, hd), op_dtype),
            pltpu.VMEM((nv, t_len, hd), op_dtype),
            pltpu.VMEM((nq, t_len, 2 * c), op_dtype),
            pltpu.VMEM((nv, t_len // c, 8, hd), F32),
            pltpu.VMEM((nv, hd, hd), F32),
        ],
        compiler_params=_cparams(("arbitrary", "arbitrary")),
        name="gdn",
    )(ua, ua, ua, ua, ub, conv_w, conv_w, conv_w, conv_state, conv_state, conv_state, s0, gp, norm_g.reshape(1, hd))


def _t5_bucket(rel):
    n = jnp.maximum(rel, 0)
    exact = REL_BUCKETS // 2
    nf = jnp.maximum(n, 1).astype(F32)
    large = exact + (jnp.log(nf / exact) / math.log(REL_MAX_DIST / exact) * (REL_BUCKETS - exact)).astype(jnp.int32)
    return jnp.where(n < exact, n, jnp.minimum(large, REL_BUCKETS - 1))


def _bias_of(rel_bias, rel):
    bucket = _t5_bucket(rel)[None]
    table = rel_bias.astype(F32)
    out = jnp.zeros((NSA_HEADS,) + rel.shape, F32)
    for k in range(REL_BUCKETS):
        out = jnp.where(bucket == k, table[k].reshape((NSA_HEADS,) + (1,) * rel.ndim), out)
    return out


def _cmp_bias(rel_bias, q_pos, ncp, nc):
    n = jnp.arange(ncp, dtype=jnp.int32)
    rel = q_pos[:, None] - (n * CMP_STRIDE + (CMP_LEN - 1))[None, :]
    ok = (rel >= 0) & (n < nc)[None, :]
    return jnp.where(ok[None], _bias_of(rel_bias, rel), NEG)


def _load_page(ref, interleaved):
    if not interleaved:
        return ref[0]
    n_rows = ref.shape[1] // NSA_G
    return jnp.concatenate([ref[0, pl.ds(g, n_rows, stride=NSA_G), :] for g in range(NSA_G)], axis=1)


def _compress_body(pt_ref, *refs, pps, n_steps, nch, interleaved):
    del pt_ref
    k_refs = refs[:pps]
    v_refs = refs[pps:2 * pps]
    (wabk_ref, wabv_ref, wk_ref, pek_ref, wv_ref, pev_ref, projk_ref, projv_ref, gain_ref,
     outk_ref, outv_ref, ak_s, bk_s, av_s, bv_s) = refs[2 * pps:]
    s = pl.program_id(1)

    @pl.when(s == 0)
    def _():
        bk_s[nch:nch + 8, :] = jnp.zeros((8, NSA_KV_W), F32)
        bv_s[nch:nch + 8, :] = jnp.zeros((8, NSA_KV_W), F32)

    def pool(w2, page):
        w_hi, w_lo = w2
        p_hi, p_lo = _split2(page)
        r = _dot16(jnp.concatenate([w_hi, w_lo], axis=0), p_hi)
        return r[0:16] + (r[16:32] + _dot16(w_hi, p_lo))

    wk2 = _split2(wabk_ref[...])
    wv2 = _split2(wabv_ref[...])
    for j in range(pps):
        row0 = pl.multiple_of((s * pps + j) * 8, 8)
        abk = pool(wk2, _load_page(k_refs[j], interleaved))
        ak_s[pl.ds(row0, 8), :] = abk[0:8]
        bk_s[pl.ds(row0, 8), :] = abk[8:16]
        abv = pool(wv2, _load_page(v_refs[j], interleaved))
        av_s[pl.ds(row0, 8), :] = abv[0:8]
        bv_s[pl.ds(row0, 8), :] = abv[8:16]

    @pl.when(s == n_steps - 1)
    def _():
        cpe_k = jnp.sum(wk_ref[...] * pek_ref[...], axis=0, keepdims=True)
        cpe_v = jnp.sum(wv_ref[...] * pev_ref[...], axis=0, keepdims=True)
        projk2 = _split2(projk_ref[...])
        projv2 = _split2(projv_ref[...])
        rb = min(nch, 128)

        def fin(r, carry):
            r0 = pl.multiple_of(r * rb, rb)
            pk = ak_s[pl.ds(r0, rb), :] + bk_s[pl.ds(r0, rb + 8), :][1:rb + 1]
            pv = av_s[pl.ds(r0, rb), :] + bv_s[pl.ds(r0, rb + 8), :][1:rb + 1]
            for g in range(NSA_G):
                sl = slice(g * HEAD_DIM, (g + 1) * HEAD_DIM)
                yk = _mm_x3(_split2(pk[:, sl] + cpe_k), projk2)
                outk_ref[0, pl.ds(r0, rb), sl] = _rms(yk, gain_ref[...])
                outv_ref[0, pl.ds(r0, rb), sl] = _mm_x3(_split2(pv[:, sl] + cpe_v), projv2)
            return carry

        lax.fori_loop(0, nch // rb, fin, 0)


def _pool_weights(w):
    c = np.arange(8)[:, None]
    t = np.arange(PAGE)[None, :]
    off = t - CMP_STRIDE * c
    inside = (off >= 0) & (off < CMP_STRIDE)
    idx = np.clip(off, 0, CMP_STRIDE - 1)
    wa = jnp.where(inside, w[idx], 0.0)
    wb = jnp.where(inside, w[idx + CMP_STRIDE], 0.0)
    return jnp.concatenate([wa, wb], axis=0).astype(F32)


def _compress(pages_k, pages_v, col_k, col_v, table, pe_k, w_k, proj_k, pe_v, w_v, proj_v, k_gain):
    b, n_pages = table.shape
    pps = min(PAGES_PER_STEP, n_pages)
    n_steps = n_pages // pps
    nch = 8 * n_pages
    interleaved = pages_k.shape[1] == NSA_G * PAGE
    body = functools.partial(_compress_body, pps=pps, n_steps=n_steps, nch=nch, interleaved=interleaved)

    def page_spec(j, col):
        if interleaved:
            return pl.BlockSpec((1, NSA_G * PAGE, HEAD_DIM), lambda bi, s, pt: (pt[bi, s * pps + j], 0, 0))
        return pl.BlockSpec((1, PAGE, NSA_KV_W), lambda bi, s, pt: (pt[bi, s * pps + j], 0, col))

    full = lambda shape: pl.BlockSpec(shape, lambda bi, s, pt: (0,) * len(shape))
    in_specs = [page_spec(j, col_k) for j in range(pps)] + [page_spec(j, col_v) for j in range(pps)]
    in_specs += [full((16, PAGE)), full((16, PAGE)), full((CMP_LEN, 1)), full((CMP_LEN, HEAD_DIM)),
                 full((CMP_LEN, 1)), full((CMP_LEN, HEAD_DIM)), full((HEAD_DIM, HEAD_DIM)), full((HEAD_DIM, HEAD_DIM)),
                 full((1, HEAD_DIM))]
    out_spec = pl.BlockSpec((1, nch, NSA_KV_W), lambda bi, s, pt: (bi, 0, 0))
    return pl.pallas_call(
        body,
        grid_spec=pltpu.PrefetchScalarGridSpec(
            num_scalar_prefetch=1, grid=(b, n_steps), in_specs=in_specs, out_specs=[out_spec, out_spec],
            scratch_shapes=[pltpu.VMEM((nch, NSA_KV_W), F32), pltpu.VMEM((nch + 8, NSA_KV_W), F32),
                            pltpu.VMEM((nch, NSA_KV_W), F32), pltpu.VMEM((nch + 8, NSA_KV_W), F32)]),
        out_shape=[jax.ShapeDtypeStruct((b, nch, NSA_KV_W), F32)] * 2,
        compiler_params=_cparams(("arbitrary", "arbitrary")),
        name="compress",
    )(table, *([pages_k] * pps), *([pages_v] * pps), _pool_weights(w_k), _pool_weights(w_v),
      w_k.reshape(CMP_LEN, 1), pe_k, w_v.reshape(CMP_LEN, 1), pe_v, proj_k, proj_v, k_gain.reshape(1, HEAD_DIM))


def _prep_body(ks_ref, kw_ref, kg_ref, pks_ref, pkw_ref):
    for g in range(NSA_G):
        sl = slice(g * HEAD_DIM, (g + 1) * HEAD_DIM)
        pks_ref[0, :, sl] = _rms(ks_ref[0, :, sl], kg_ref[1:2, :])
        pkw_ref[0, :, sl] = _rms(kw_ref[0, :, sl], kg_ref[2:3, :])


def _prep_prompt_body(kc_ref, vc_ref, ks_ref, vs_ref, kw_ref, vw_ref, kg_ref,
                      okc_ref, ovc_ref, oks_ref, ovs_ref, okw_ref, ovw_ref, ks16_ref, vst_ref, kw16_ref, vwt_ref):
    tp = ks_ref.shape[1]

    def put(dst, g, x):
        dst[0, pl.ds(g, tp, stride=NSA_G), :] = x

    for g in range(NSA_G):
        sl = slice(g * HEAD_DIM, (g + 1) * HEAD_DIM)
        ksn = _rms(ks_ref[0, :, sl], kg_ref[1:2, :])
        kwn = _rms(kw_ref[0, :, sl], kg_ref[2:3, :])
        ks16_ref[0, :, sl] = ksn.astype(BF16)
        kw16_ref[0, :, sl] = kwn.astype(BF16)
        put(oks_ref, g, ksn)
        put(okw_ref, g, kwn)
        for src, dst in ((kc_ref, okc_ref), (vc_ref, ovc_ref), (vs_ref, ovs_ref), (vw_ref, ovw_ref)):
            put(dst, g, src[0, :, sl])
    for src, dst in ((vs_ref, vst_ref), (vw_ref, vwt_ref)):
        for j in range(dst.shape[1]):
            dst[0, j] = src[0, j * LANE:(j + 1) * LANE, :].T.astype(BF16)


def _nsa_prep_prompt(ub, k_norm_g, wb):
    b, t_len, _ = ub.shape
    tp = min(t_len, 512)
    assert wb % tp == 0 and t_len % tp == 0 and t_len >= wb
    first_win = (t_len - wb) // tp
    kv0 = UB_KV // NSA_KV_W
    spec = lambda c: pl.BlockSpec((1, tp, NSA_KV_W), lambda bi, i: (bi, i, c))
    o_spec = pl.BlockSpec((1, tp, NSA_KV_W), lambda bi, i: (bi, i, 0))
    il_spec = pl.BlockSpec((1, NSA_G * tp, HEAD_DIM), lambda bi, i: (bi, i, 0))
    win_spec = pl.BlockSpec((1, NSA_G * tp, HEAD_DIM), lambda bi, i: (bi, jnp.maximum(i - first_win, 0), 0))
    t_spec = pl.BlockSpec((1, tp // LANE, NSA_KV_W, LANE), lambda bi, i: (bi, i, 0, 0))
    il = jax.ShapeDtypeStruct((b, NSA_G * t_len, HEAD_DIM), F32)
    il_win = jax.ShapeDtypeStruct((b, NSA_G * wb, HEAD_DIM), F32)
    k16 = jax.ShapeDtypeStruct((b, t_len, NSA_KV_W), BF16)
    v16 = jax.ShapeDtypeStruct((b, t_len // LANE, NSA_KV_W, LANE), BF16)
    return pl.pallas_call(
        _prep_prompt_body,
        grid=(b, t_len // tp),
        in_specs=[spec(kv0 + c) for c in range(6)] + [pl.BlockSpec((3, HEAD_DIM), lambda bi, i: (0, 0))],
        out_specs=[il_spec] * 4 + [win_spec] * 2 + [o_spec, t_spec, o_spec, t_spec],
        out_shape=[il] * 4 + [il_win] * 2 + [k16, v16, k16, v16],
        compiler_params=_cparams(("arbitrary", "arbitrary")),
        name="nsa_prep_prompt",
    )(ub, ub, ub, ub, ub, ub, k_norm_g)


def _nsa_prep(ub, k_norm_g):
    b, t_len, _ = ub.shape
    tp = min(t_len, 512)
    kv0 = UB_KV // NSA_KV_W
    spec = lambda c: pl.BlockSpec((1, tp, NSA_KV_W), lambda bi, i: (bi, i, c))
    o_spec = pl.BlockSpec((1, tp, NSA_KV_W), lambda bi, i: (bi, i, 0))
    return pl.pallas_call(
        _prep_body,
        grid=(b, t_len // tp),
        in_specs=[spec(kv0 + 2), spec(kv0 + 4), pl.BlockSpec((3, HEAD_DIM), lambda bi, i: (0, 0))],
        out_specs=[o_spec] * 2,
        out_shape=[jax.ShapeDtypeStruct((b, t_len, NSA_KV_W), F32)] * 2,
        compiler_params=_cparams(("arbitrary", "arbitrary")),
        name="nsa_prep",
    )(ub, ub, k_norm_g)


def _sel_matrix(nc, nb, ncp, nbp):
    j = np.arange(nb)
    lo = np.clip((SEL_BLOCK * j - CMP_LEN) // CMP_STRIDE + 1, 0, nc)
    hi = np.clip(-(-(SEL_BLOCK * (j + 1)) // CMP_STRIDE), 0, nc)
    n = np.arange(ncp)[:, None]
    m = np.zeros((ncp, nbp), np.float32)
    m[:, :nb] = (n >= lo[None, :]) & (n < hi[None, :])
    return jnp.asarray(m)


def _cmp_body(q_ref, kc_ref, vc_ref, bias_ref, mselt_ref, qg_ref, gate_ref, ocmpt_ref, nsel_ref, *, tq, nb):
    qi = pl.program_id(1)
    nbp = -(-nb // 8) * 8
    gates_t = jax.nn.sigmoid(gate_ref[0]).T
    blk = lax.broadcasted_iota(jnp.int32, (nbp, tq), 0)
    qblk = (qi * tq + lax.broadcasted_iota(jnp.int32, (nbp, tq), 1)) >> 6
    scale = HEAD_DIM ** -0.5
    for g in range(NSA_G):
        gsl = slice(g * HEAD_DIM, (g + 1) * HEAD_DIM)
        kc = kc_ref[0, :, gsl]
        vct = vc_ref[0, :, gsl].T.astype(BF16)
        imp = jnp.zeros((kc.shape[0], tq), F32)
        heads = [g * NSA_HPG + hh for hh in range(NSA_HPG)]
        kc2 = _split2(kc)
        scores = []
        for h in heads:
            q2 = _split2(_rms(q_ref[0, :, h * HEAD_DIM:(h + 1) * HEAD_DIM], qg_ref[...]) * scale)
            nt = lambda a, b: lax.dot_general(a, b, (((1,), (1,)), ((), ())), preferred_element_type=F32)
            scores.append(nt(kc2[0], q2[0]) + (nt(kc2[0], q2[1]) + nt(kc2[1], q2[0])))
        probs = []
        for h, s in zip(heads, scores):
            bias = bias_ref[h]
            s = bias + s
            e = jnp.exp(s - jnp.max(s, axis=0, keepdims=True))
            p = e * (1.0 / jnp.sum(e, axis=0, keepdims=True)) * jnp.where(bias > 0.5 * NEG, 1.0, 0.0)
            probs.append(p)
            imp = imp + p
        for h, p in zip(heads, probs):
            hsl = slice(h * HEAD_DIM, (h + 1) * HEAD_DIM)
            ocmpt_ref[0, hsl, :] = gates_t[h:h + 1, :] * _dot16(vct, p.astype(BF16))
        score = _cumsum_rows(mselt_ref[0:nbp, :].astype(BF16), imp)
        score = jnp.where(blk == 0, BIG, score)
        score = jnp.where(blk == qblk, BIG, score)
        score = jnp.where(blk == qblk - 1, BIG, score)
        score = jnp.where(blk <= qblk, score, -BIG)
        rank = jnp.zeros((nbp, tq), F32)
        for i in range(nb):
            si = score[i:i + 1, :]
            ge = jnp.where(si >= score, 1.0, 0.0)
            gt = jnp.where(si > score, 1.0, 0.0)
            rank = rank + jnp.where(blk > i, ge, gt)
        nsel_t = jnp.where(rank < SEL_TOP, 0.0, jnp.where(blk < nb, 1.0, 0.0))
        nsel = jnp.concatenate([nsel_t, jnp.zeros((LANE - nbp, tq), F32)], axis=0).T
        nsel_ref[0, :, gsl] = nsel.astype(BF16)


def _cmp_select_prompt(ub, kcmp, vcmp, bias_cmp, q_norm_g):
    b, t_len, _ = ub.shape
    tq = 256
    nb = t_len // SEL_BLOCK
    ncp = kcmp.shape[1]
    nc = (t_len - CMP_LEN) // CMP_STRIDE + 1
    assert nb <= LANE and ncp == LANE
    body = functools.partial(_cmp_body, tq=tq, nb=nb)
    return pl.pallas_call(
        body,
        grid=(b, t_len // tq),
        in_specs=[
            pl.BlockSpec((1, tq, NSA_Q_W), lambda bi, i: (bi, i, UB_Q // NSA_Q_W)),
            pl.BlockSpec((1, ncp, NSA_KV_W), lambda bi, i: (bi, 0, 0)),
            pl.BlockSpec((1, ncp, NSA_KV_W), lambda bi, i: (bi, 0, 0)),
            pl.BlockSpec((NSA_HEADS, ncp, tq), lambda bi, i: (0, 0, i)),
            pl.BlockSpec((LANE, ncp), lambda bi, i: (0, 0)),
            pl.BlockSpec((1, HEAD_DIM), lambda bi, i: (0, 0)),
            pl.BlockSpec((1, tq, LANE), lambda bi, i: (bi, i, UB_G // LANE)),
        ],
        out_specs=[
            pl.BlockSpec((1, NSA_Q_W, tq), lambda bi, i: (bi, 0, i)),
            pl.BlockSpec((1, tq, NSA_G * LANE), lambda bi, i: (bi, i, 0)),
        ],
        out_shape=[jax.ShapeDtypeStruct((b, NSA_Q_W, t_len), F32),
                   jax.ShapeDtypeStruct((b, t_len, NSA_G * LANE), BF16)],
        compiler_params=_cparams(("arbitrary", "arbitrary")),
        name="cmp_select",
    )(ub, kcmp, vcmp, jnp.swapaxes(bias_cmp, 1, 2), _sel_matrix(nc, nb, ncp, LANE).T, q_norm_g.reshape(1, HEAD_DIM), ub)


def _attn_body(q_ref, ocmp_ref, nsel_ref, ks_ref, vst_ref, kw_ref, vwt_ref, epen_ref, bt_ref, cb_ref, gate_ref, z_ref, qg_ref,
               o_ref, m_s, l_s, acc_s, osw_s, *, tq):
    qi = pl.program_id(1)
    t0 = qi * tq
    rows = NSA_HPG * tq
    tk = LANE
    def rel_of(kb, width):
        tok = t0 + (lax.broadcasted_iota(jnp.int32, (width * tk, rows), 1) & (tq - 1))
        return tok - (kb + lax.broadcasted_iota(jnp.int32, (width * tk, rows), 0))

    gates_t = jax.nn.sigmoid(gate_ref[0]).T
    scale = HEAD_DIM ** -0.5
    n_near = bt_ref.shape[0]
    j_near = jnp.maximum(qi + 1 - n_near, 0)

    groups = range(NSA_G)
    gsl = [slice(g * HEAD_DIM, (g + 1) * HEAD_DIM) for g in groups]
    heads = [[g * NSA_HPG + hh for hh in range(NSA_HPG)] for g in groups]
    q_sel, q_win = [], []
    for g in groups:
        q4 = jnp.concatenate(
            [(_rms(q_ref[0, :, h * HEAD_DIM:(h + 1) * HEAD_DIM], qg_ref[...]) * scale).astype(BF16) for h in heads[g]], axis=0)
        ns = nsel_ref[0, :, g * LANE:(g + 1) * LANE].astype(F32)
        aug_sel = jnp.concatenate([ns + cb_ref[h:h + 1, :] for h in heads[g]], axis=0).astype(BF16)
        aug_win = jnp.concatenate([jnp.broadcast_to(cb_ref[h:h + 1, :], (tq, LANE)) for h in heads[g]], axis=0).astype(BF16)
        q_sel.append(jnp.concatenate([q4, aug_sel], axis=1))
        q_win.append(jnp.concatenate([q4, aug_win], axis=1))

    def run(k_ref, vt_ref, qmats, lo, window, far_widths):
        m_s[...] = jnp.full(m_s.shape, NEG, F32)
        l_s[...] = jnp.zeros(l_s.shape, F32)
        acc_s[...] = jnp.zeros(acc_s.shape, F32)

        def step(j, near, width=1):
            kb = pl.multiple_of(j * tk, tk)
            kt = k_ref[0, pl.ds(kb, width * tk), :]
            ep = epen_ref[pl.ds(kb, width * tk), :]
            ss = [lax.dot_general(jnp.concatenate([kt[:, gsl[g]], ep], axis=1), qmats[g], (((1,), (1,)), ((), ())),
                                  preferred_element_type=F32) for g in groups]
            rel = rel_of(kb, width)
            vt = vt_ref[0, j] if width == 1 else jnp.concatenate([vt_ref[0, j + i] for i in range(width)], axis=1)
            for g in groups:
                s = ss[g]
                if near:
                    bias = [bt_ref[width - 1 - i, :, g * rows:(g + 1) * rows] for i in range(width)]
                    s = jnp.where(rel >= 0, (bias[0] if width == 1 else jnp.concatenate(bias, axis=0)) + s, NEG)
                elif window:
                    s = jnp.where(rel < WINDOW, s, NEG)
                m_old = m_s[g]
                m_new = jnp.maximum(m_old, jnp.max(s, axis=0, keepdims=True))
                alpha = jnp.exp(m_old - m_new)
                p = jnp.exp(s - m_new)
                l_s[g] = alpha * l_s[g] + jnp.sum(p, axis=0, keepdims=True)
                acc_s[g] = alpha * acc_s[g] + jnp.dot(vt[gsl[g], :], p.astype(BF16), preferred_element_type=F32)
                m_s[g] = m_new

        start = lo
        for width in far_widths:
            def far_step(i, carry, start=start, width=width):
                step(start + width * i, False, width)
                return carry

            n = jnp.maximum(j_near - start, 0) // width
            lax.fori_loop(0, n, far_step, 0)
            start = start + width * n

        @pl.when(qi >= n_near - 1)
        def _():
            step(qi + 1 - n_near, True, n_near)

        for short in range(1, n_near):
            @pl.when(qi == short - 1)
            def _(short=short):
                step(0, True, short)

    def gate_rows(branch, g):
        return jnp.concatenate([gates_t[branch * NSA_HEADS + h:branch * NSA_HEADS + h + 1, :] for h in heads[g]], axis=1)

    run(ks_ref, vst_ref, q_sel, 0, False, (4, 2, 1))
    for g in groups:
        osw_s[g] = acc_s[g] * (gate_rows(1, g) / l_s[g])
    run(kw_ref, vwt_ref, q_win, jnp.maximum(qi - WINDOW // tk, 0), True, (2, 1))
    for g in groups:
        osw = osw_s[g] + acc_s[g] * (gate_rows(2, g) / l_s[g])
        for hh, h in enumerate(heads[g]):
            hsl = slice(h * HEAD_DIM, (h + 1) * HEAD_DIM)
            o = (ocmp_ref[0, hsl, :] + osw[:, hh * tq:(hh + 1) * tq]).T
            o_ref[0, :, hsl] = (o * _silu(z_ref[0, :, hsl])).astype(o_ref.dtype)


def _far_bias(rel_bias):
    return _bias_of(rel_bias, jnp.full((1,), REL_MAX_DIST, jnp.int32))[:, 0]


def _toeplitz_bias_t(rel_bias, tq, n_near):
    d = jnp.arange(n_near, dtype=jnp.int32)[:, None, None]
    c = jnp.arange(LANE, dtype=jnp.int32)[None, :, None]
    r = jnp.arange(tq, dtype=jnp.int32)[None, None, :]
    bt = _bias_of(rel_bias, d * LANE + r - c) - _far_bias(rel_bias)[:, None, None, None]
    return jnp.transpose(bt, (1, 2, 0, 3)).reshape(n_near, LANE, NSA_HEADS * tq)


AUG_BIAS_LANES = (LANE - 2, LANE - 1)


def _far_bias_columns(rel_bias):
    c = _far_bias(rel_bias)
    hi = c.astype(BF16).astype(F32)
    lo = (c - hi).astype(BF16).astype(F32)
    return jnp.zeros((NSA_HEADS, LANE), F32).at[:, AUG_BIAS_LANES[0]].set(hi).at[:, AUG_BIAS_LANES[1]].set(lo)


def _block_penalty(t_len):
    key = np.arange(t_len)[:, None]
    j = np.arange(LANE)[None, :]
    pen = np.where(key // SEL_BLOCK == j, NEG, 0.0)
    pen[:, AUG_BIAS_LANES[0]:] = 1.0
    return jnp.asarray(pen, BF16)


def _attn_prompt(ub, ocmp, nsel, ksb, vst, kwb, vwt, rel_bias, q_norm_g):
    b, t_len, _ = ub.shape
    tq = 128
    rows = NSA_HPG * tq
    assert t_len // SEL_BLOCK <= AUG_BIAS_LANES[0]
    n_near = -(-(REL_MAX_DIST + LANE - 1) // LANE)
    bt = _toeplitz_bias_t(rel_bias, tq, n_near)
    body = functools.partial(_attn_body, tq=tq)
    k_spec = pl.BlockSpec((1, t_len, NSA_KV_W), lambda bi, i: (bi, 0, 0))
    v_spec = pl.BlockSpec((1, t_len // LANE, NSA_KV_W, LANE), lambda bi, i: (bi, 0, 0, 0))
    return pl.pallas_call(
        body,
        grid=(b, t_len // tq),
        in_specs=[
            pl.BlockSpec((1, tq, NSA_Q_W), lambda bi, i: (bi, i, UB_Q // NSA_Q_W)),
            pl.BlockSpec((1, NSA_Q_W, tq), lambda bi, i: (bi, 0, i)),
            pl.BlockSpec((1, tq, NSA_G * LANE), lambda bi, i: (bi, i, 0)),
            k_spec, v_spec, k_spec, v_spec,
            pl.BlockSpec((t_len, LANE), lambda bi, i: (0, 0)),
            pl.BlockSpec(bt.shape, lambda bi, i: (0, 0, 0)),
            pl.BlockSpec((NSA_HEADS, LANE), lambda bi, i: (0, 0)),
            pl.BlockSpec((1, tq, LANE), lambda bi, i: (bi, i, UB_G // LANE)),
            pl.BlockSpec((1, tq, NSA_Q_W), lambda bi, i: (bi, i, UB_Z // NSA_Q_W)),
            pl.BlockSpec((1, HEAD_DIM), lambda bi, i: (0, 0)),
        ],
        out_specs=pl.BlockSpec((1, tq, NSA_Q_W), lambda bi, i: (bi, i, 0)),
        out_shape=jax.ShapeDtypeStruct((b, t_len, NSA_Q_W), BF16),
        scratch_shapes=[pltpu.VMEM((NSA_G, 1, rows), F32), pltpu.VMEM((NSA_G, 1, rows), F32),
                        pltpu.VMEM((NSA_G, HEAD_DIM, rows), F32), pltpu.VMEM((NSA_G, HEAD_DIM, rows), F32)],
        compiler_params=_cparams(("arbitrary", "arbitrary")),
        name="nsa_attn",
    )(ub, ocmp, nsel, ksb, vst, kwb, vwt, _block_penalty(t_len), bt, _far_bias_columns(rel_bias), ub, ub,
      q_norm_g.reshape(1, HEAD_DIM))


def _q_all(q_ref, qg_ref, ts):
    scale = HEAD_DIM ** -0.5
    zero = jnp.zeros((NSA_HPG * ts, HEAD_DIM), BF16)
    blocks = []
    for g in range(NSA_G):
        q4 = jnp.concatenate(
            [(_rms(q_ref[0, :, (g * NSA_HPG + hh) * HEAD_DIM:(g * NSA_HPG + hh + 1) * HEAD_DIM], qg_ref[...]) * scale).astype(BF16)
             for hh in range(NSA_HPG)], axis=0)
        blocks.append(jnp.concatenate([q4 if gg == g else zero for gg in range(NSA_G)], axis=1))
    return jnp.concatenate(blocks, axis=0)


def _row_to_col(row, n):
    eye = lax.broadcasted_iota(jnp.int32, (n, n), 0) == lax.broadcasted_iota(jnp.int32, (n, n), 1)
    return jnp.sum(jnp.where(eye, row, 0.0), axis=1, keepdims=True)


def _cmp_sample_body(q_ref, kc_ref, vc_ref, bias_ref, msel_ref, rep_ref, qg_ref, ocmp_ref, selt_ref, *, ts, nb, nbp):
    scale = HEAD_DIM ** -0.5
    ncp = kc_ref.shape[1]
    scores = []
    for g in range(NSA_G):
        gsl = slice(g * HEAD_DIM, (g + 1) * HEAD_DIM)
        q4 = jnp.concatenate(
            [_rms(q_ref[0, :, (g * NSA_HPG + hh) * HEAD_DIM:(g * NSA_HPG + hh + 1) * HEAD_DIM], qg_ref[...]) * scale
             for hh in range(NSA_HPG)], axis=0)
        bias = bias_ref[g * NSA_HPG:(g + 1) * NSA_HPG].reshape(NSA_HPG * ts, ncp)
        s = _mm_nt_hi(q4, kc_ref[0, :, gsl]) + bias
        e = jnp.exp(s - jnp.max(s, axis=-1, keepdims=True))
        p = e / jnp.sum(e, axis=-1, keepdims=True) * jnp.where(bias > 0.5 * NEG, 1.0, 0.0)
        o = _mm(p, vc_ref[0, :, gsl])
        imp = p[0:ts]
        for hh in range(NSA_HPG):
            h = g * NSA_HPG + hh
            ocmp_ref[0, :, h * HEAD_DIM:(h + 1) * HEAD_DIM] = o[hh * ts:(hh + 1) * ts]
            if hh:
                imp = imp + p[hh * ts:(hh + 1) * ts]
        scores.append(_mm_hi(imp, msel_ref[...]))
    score = jnp.concatenate(scores, axis=0)
    rows = NSA_G * ts
    lane = lax.broadcasted_iota(jnp.int32, (rows, nbp), 1)
    tok = lax.broadcasted_iota(jnp.int32, (rows, nbp), 0) & (ts - 1)
    qblk = (PAST_LEN + tok) >> 6
    forced = (lane == 0) | (lane == qblk) | (lane == qblk - 1)
    score = jnp.where(forced, BIG, score)
    score = jnp.where(lane <= qblk, score, -BIG)
    sel = jnp.zeros((rows, nbp), F32)
    lane_f = lane.astype(F32)
    for _ in range(SEL_TOP):
        mx = jnp.max(score, axis=-1, keepdims=True)
        first = jnp.min(jnp.where(score == mx, lane_f, float(nbp)), axis=-1, keepdims=True)
        pick = lane_f == first
        sel = jnp.where(pick, 1.0, sel)
        score = jnp.where(pick, -3e38, score)
    selt_ref[0] = _mm_tn(sel, rep_ref[...])


def _cmp_select_sample(ub, kcmp, vcmp, bias_cmp, q_norm_g, nb):
    b, ts, _ = ub.shape
    ncp = kcmp.shape[1]
    nc = ncp - 1
    nbp = -(-nb // LANE) * LANE
    assert ts & (ts - 1) == 0 and NSA_HEADS * ts == LANE
    rep = np.zeros((NSA_G * ts, LANE), np.float32)
    for g in range(NSA_G):
        for hh in range(NSA_HPG):
            for t in range(ts):
                rep[g * ts + t, (g * NSA_HPG + hh) * ts + t] = 1.0
    body = functools.partial(_cmp_sample_body, ts=ts, nb=nb, nbp=nbp)
    return pl.pallas_call(
        body,
        grid=(b,),
        in_specs=[
            pl.BlockSpec((1, ts, NSA_Q_W), lambda bi: (bi, 0, UB_Q // NSA_Q_W)),
            pl.BlockSpec((1, ncp, NSA_KV_W), lambda bi: (bi, 0, 0)),
            pl.BlockSpec((1, ncp, NSA_KV_W), lambda bi: (bi, 0, 0)),
            pl.BlockSpec((NSA_HEADS, ts, ncp), lambda bi: (0, 0, 0)),
            pl.BlockSpec((ncp, nbp), lambda bi: (0, 0)),
            pl.BlockSpec((NSA_G * ts, LANE), lambda bi: (0, 0)),
            pl.BlockSpec((1, HEAD_DIM), lambda bi: (0, 0)),
        ],
        out_specs=[
            pl.BlockSpec((1, ts, NSA_Q_W), lambda bi: (bi, 0, 0)),
            pl.BlockSpec((1, nbp, LANE), lambda bi: (bi, 0, 0)),
        ],
        out_shape=[jax.ShapeDtypeStruct((b, ts, NSA_Q_W), F32), jax.ShapeDtypeStruct((b, nbp, LANE), F32)],
        compiler_params=_cparams(("arbitrary",)),
        name="cmp_select_sample",
    )(ub, kcmp, vcmp, bias_cmp, _sel_matrix(nc, nb, ncp, nbp), jnp.asarray(rep), q_norm_g.reshape(1, HEAD_DIM))


def _sel_sample_body(pt_ref, *refs, pps, n_steps, ts):
    del pt_ref
    k_refs = refs[:pps]
    v_refs = refs[pps:2 * pps]
    (q_ref, selt_ref, knew_ref, vnew_ref, blast_ref, bfar_ref, bnew_ref, qg_ref,
     o_ref, qall_s, sc_s, snew_s, m_s, l_s, acc_s) = refs[2 * pps:]
    s = pl.program_id(1)
    n_pages = n_steps * pps
    sub = lax.broadcasted_iota(jnp.int32, (PAGE, LANE), 0)
    tok = lax.broadcasted_iota(jnp.int32, (ts, LANE), 1) & (ts - 1)
    new_t = lax.broadcasted_iota(jnp.int32, (ts, LANE), 0)

    @pl.when(s == 0)
    def _():
        qall = _q_all(q_ref, qg_ref, ts)
        qall_s[...] = qall
        sn = _mm_nt(knew_ref[0], qall) + bnew_ref[...]
        ok = (new_t <= tok) & (selt_ref[0, 2 * n_pages:2 * n_pages + 1, :] > 0.5)
        sn = jnp.where(ok, sn, NEG)
        snew_s[...] = sn
        m_s[...] = jnp.max(sn, axis=0, keepdims=True)
        l_s[...] = jnp.zeros((1, LANE), F32)
        acc_s[...] = jnp.zeros((LANE, NSA_KV_W), F32)

    @pl.when(s < n_steps)
    def _():
        m = m_s[...]
        for j in range(pps):
            p = s * pps + j
            st = _mm_nt(_load_page(k_refs[j], True), qall_s[...])
            st = st + jnp.where(p == n_pages - 1, blast_ref[...], bfar_ref[...])
            r0 = selt_ref[0, pl.ds(2 * p, 1), :]
            r1 = selt_ref[0, pl.ds(2 * p + 1, 1), :]
            ok = jnp.where(sub < SEL_BLOCK, r0, r1) > 0.5
            st = jnp.where(ok, st, NEG)
            sc_s[pl.ds(pl.multiple_of(p * PAGE, PAGE), PAGE), :] = st
            m = jnp.maximum(m, jnp.max(st, axis=0, keepdims=True))
        m_s[...] = m

    @pl.when(s >= n_steps)
    def _():
        m = m_s[...]
        l = l_s[...]
        acc = acc_s[...]
        for j in range(pps):
            p = (s - n_steps) * pps + j
            pt = jnp.exp(sc_s[pl.ds(pl.multiple_of(p * PAGE, PAGE), PAGE), :] - m)
            l = l + jnp.sum(pt, axis=0, keepdims=True)
            acc = acc + _mm_tn(pt, _load_page(v_refs[j], True))
        l_s[...] = l
        acc_s[...] = acc

    @pl.when(s == 2 * n_steps - 1)
    def _():
        pn = jnp.exp(snew_s[...] - m_s[...])
        l = l_s[...] + jnp.sum(pn, axis=0, keepdims=True)
        acc = acc_s[...] + _mm_tn(pn, vnew_ref[0])
        out = acc / _row_to_col(l, LANE)
        for h in range(NSA_HEADS):
            g = h // NSA_HPG
            o_ref[0, :, h * HEAD_DIM:(h + 1) * HEAD_DIM] = out[h * ts:(h + 1) * ts, g * HEAD_DIM:(g + 1) * HEAD_DIM]


def _lane_bias(rel_bias, key_pos, ts):
    rel = (PAST_LEN + jnp.arange(ts, dtype=jnp.int32))[None, :] - key_pos[:, None]
    return jnp.moveaxis(_bias_of(rel_bias, rel), 0, 1).reshape(key_pos.shape[0], NSA_HEADS * ts)


def _sel_sample(ub, selt, cache_k, cache_v, table, k_new, rel_bias, q_norm_g):
    b, ts, _ = ub.shape
    n_pages = table.shape[1]
    assert PAGE >= REL_MAX_DIST and n_pages * PAGE == PAST_LEN
    pps = min(PAGES_PER_STEP, n_pages)
    n_steps = n_pages // pps
    ar = jnp.arange
    blast = _lane_bias(rel_bias, PAST_LEN - PAGE + ar(PAGE, dtype=jnp.int32), ts)
    bfar = _lane_bias(rel_bias, jnp.zeros((1,), jnp.int32), ts)
    bnew = _lane_bias(rel_bias, PAST_LEN + ar(ts, dtype=jnp.int32), ts)
    body = functools.partial(_sel_sample_body, pps=pps, n_steps=n_steps, ts=ts)
    last = n_steps - 1
    page_block = (1, NSA_G * PAGE, HEAD_DIM)
    k_spec = lambda j: pl.BlockSpec(page_block, lambda bi, s, pt: (pt[bi, jnp.minimum(s, last) * pps + j], 0, 0))
    v_spec = lambda j: pl.BlockSpec(page_block, lambda bi, s, pt: (pt[bi, jnp.maximum(s - n_steps, 0) * pps + j], 0, 0))
    full = lambda shape: pl.BlockSpec(shape, lambda bi, s, pt: (0,) * len(shape))
    in_specs = [k_spec(j) for j in range(pps)] + [v_spec(j) for j in range(pps)] + [
        pl.BlockSpec((1, ts, NSA_Q_W), lambda bi, s, pt: (bi, 0, UB_Q // NSA_Q_W)),
        pl.BlockSpec((1,) + selt.shape[1:], lambda bi, s, pt: (bi, 0, 0)),
        pl.BlockSpec((1, ts, NSA_KV_W), lambda bi, s, pt: (bi, 0, 0)),
        pl.BlockSpec((1, ts, NSA_KV_W), lambda bi, s, pt: (bi, 0, UB_KV // NSA_KV_W + 3)),
        full((PAGE, LANE)), full((1, LANE)), full((ts, LANE)), full((1, HEAD_DIM)),
    ]
    return pl.pallas_call(
        body,
        grid_spec=pltpu.PrefetchScalarGridSpec(
            num_scalar_prefetch=1, grid=(b, 2 * n_steps), in_specs=in_specs,
            out_specs=pl.BlockSpec((1, ts, NSA_Q_W), lambda bi, s, pt: (bi, 0, 0)),
            scratch_shapes=[pltpu.VMEM((LANE, NSA_KV_W), BF16), pltpu.VMEM((PAST_LEN, LANE), F32),
                            pltpu.VMEM((ts, LANE), F32), pltpu.VMEM((1, LANE), F32), pltpu.VMEM((1, LANE), F32),
                            pltpu.VMEM((LANE, NSA_KV_W), F32)]),
        out_shape=jax.ShapeDtypeStruct((b, ts, NSA_Q_W), F32),
        compiler_params=_cparams(("arbitrary", "arbitrary")),
        name="sel_sample",
    )(table, *([cache_k] * pps), *([cache_v] * pps), ub, selt, k_new, ub, blast, bfar, bnew, q_norm_g.reshape(1, HEAD_DIM))


def _win_sample_body(q_ref, kc_ref, vc_ref, knew_ref, vnew_ref, bc_ref, bn_ref, gate_ref, z_ref, ocmp_ref, osel_ref, qg_ref,
                     o_ref, kout_ref, vout_ref, *, ts, wb):
    qall = _q_all(q_ref, qg_ref, ts)
    key_i = lax.broadcasted_iota(jnp.int32, (wb, LANE), 0)
    tok_c = lax.broadcasted_iota(jnp.int32, (wb, LANE), 1) & (ts - 1)
    tok_n = lax.broadcasted_iota(jnp.int32, (ts, LANE), 1) & (ts - 1)
    new_t = lax.broadcasted_iota(jnp.int32, (ts, LANE), 0)
    sc = _mm_nt(_load_page(kc_ref, True), qall) + bc_ref[...]
    sc = jnp.where(wb + tok_c - key_i < WINDOW, sc, NEG)
    sn = _mm_nt(knew_ref[0], qall) + bn_ref[...]
    sn = jnp.where(new_t <= tok_n, sn, NEG)
    m = jnp.maximum(jnp.max(sc, axis=0, keepdims=True), jnp.max(sn, axis=0, keepdims=True))
    pc = jnp.exp(sc - m)
    pn = jnp.exp(sn - m)
    l = jnp.sum(pc, axis=0, keepdims=True) + jnp.sum(pn, axis=0, keepdims=True)
    out = (_mm_tn(pc, _load_page(vc_ref, True)) + _mm_tn(pn, vnew_ref[0])) / _row_to_col(l, LANE)
    gates = jax.nn.sigmoid(gate_ref[0])
    for h in range(NSA_HEADS):
        g = h // NSA_HPG
        hsl = slice(h * HEAD_DIM, (h + 1) * HEAD_DIM)
        o_win = out[h * ts:(h + 1) * ts, g * HEAD_DIM:(g + 1) * HEAD_DIM]
        o = (gates[:, h:h + 1] * ocmp_ref[0, :, hsl] + gates[:, NSA_HEADS + h:NSA_HEADS + h + 1] * osel_ref[0, :, hsl]
             + gates[:, 2 * NSA_HEADS + h:2 * NSA_HEADS + h + 1] * o_win)
        o_ref[0, :, hsl] = (o * _silu(z_ref[0, :, hsl])).astype(o_ref.dtype)
    keep = (wb - ts) * NSA_G
    for out_ref, old_ref, new_ref in ((kout_ref, kc_ref, knew_ref), (vout_ref, vc_ref, vnew_ref)):
        out_ref[0, 0:keep, :] = old_ref[0, ts * NSA_G:wb * NSA_G, :]
        for g in range(NSA_G):
            out_ref[0, pl.ds(keep + g, ts, stride=NSA_G), :] = new_ref[0, :, g * HEAD_DIM:(g + 1) * HEAD_DIM]


def _win_sample(ub, cache_k, cache_v, k_new, ocmp, osel, rel_bias, q_norm_g):
    b, ts, _ = ub.shape
    wb = cache_k.shape[1] // NSA_G
    assert wb == WINDOW and ts % 8 == 0
    bc = _lane_bias(rel_bias, PAST_LEN - wb + jnp.arange(wb, dtype=jnp.int32), ts)
    bn = _lane_bias(rel_bias, PAST_LEN + jnp.arange(ts, dtype=jnp.int32), ts)
    body = functools.partial(_win_sample_body, ts=ts, wb=wb)
    tok_spec = lambda w, c: pl.BlockSpec((1, ts, w), lambda bi: (bi, 0, c))
    win_spec = pl.BlockSpec((1, wb * NSA_G, HEAD_DIM), lambda bi: (bi, 0, 0))
    return pl.pallas_call(
        body,
        grid=(b,),
        in_specs=[
            tok_spec(NSA_Q_W, UB_Q // NSA_Q_W), win_spec, win_spec,
            tok_spec(NSA_KV_W, 0), tok_spec(NSA_KV_W, UB_KV // NSA_KV_W + 5),
            pl.BlockSpec((wb, LANE), lambda bi: (0, 0)), pl.BlockSpec((ts, LANE), lambda bi: (0, 0)),
            tok_spec(LANE, UB_G // LANE), tok_spec(NSA_Q_W, UB_Z // NSA_Q_W),
            tok_spec(NSA_Q_W, 0), tok_spec(NSA_Q_W, 0),
            pl.BlockSpec((1, HEAD_DIM), lambda bi: (0, 0)),
        ],
        out_specs=[tok_spec(NSA_Q_W, 0), win_spec, win_spec],
        out_shape=[jax.ShapeDtypeStruct((b, ts, NSA_Q_W), BF16),
                   jax.ShapeDtypeStruct((b, wb * NSA_G, HEAD_DIM), F32),
                   jax.ShapeDtypeStruct((b, wb * NSA_G, HEAD_DIM), F32)],
        compiler_params=_cparams(("arbitrary",)),
        name="win_sample",
    )(ub, cache_k, cache_v, k_new, ub, bc, bn, ub, ub, ocmp, osel, q_norm_g.reshape(1, HEAD_DIM))


def _mix_body(oa_ref, ob_ref, wa_ref, wb_ref, ma_ref, mb_ref, o_ref):
    pa = jnp.dot(oa_ref[...], wa_ref[...], preferred_element_type=F32)
    pb = jnp.dot(ob_ref[...], wb_ref[...], preferred_element_type=F32)
    o_ref[...] = (jax.nn.sigmoid(ma_ref[...]) * pa + jax.nn.sigmoid(mb_ref[...]) * pb).astype(o_ref.dtype)


def _out_body(m_ref, w_ref, x_ref, y_ref):
    y_ref[...] = x_ref[...] + jnp.dot(m_ref[...], w_ref[...], preferred_element_type=F32)


def _output(x2d, oa, ob, ub2d, wa16, wb16, wo16):
    m = x2d.shape[0]
    tm = min(m, 512)
    tn = 512
    grid = (m // tm, D_MODEL // tn)
    mixed = pl.pallas_call(
        _mix_body,
        grid=grid,
        in_specs=[
            pl.BlockSpec((tm, GDN_V_W), lambda i, j: (i, 0)),
            pl.BlockSpec((tm, NSA_Q_W), lambda i, j: (i, 0)),
            pl.BlockSpec((GDN_V_W, tn), lambda i, j: (0, j)),
            pl.BlockSpec((NSA_Q_W, tn), lambda i, j: (0, j)),
            pl.BlockSpec((tm, tn), lambda i, j: (i, UB_MA // tn + j)),
            pl.BlockSpec((tm, tn), lambda i, j: (i, UB_MB // tn + j)),
        ],
        out_specs=pl.BlockSpec((tm, tn), lambda i, j: (i, j)),
        out_shape=jax.ShapeDtypeStruct((m, D_MODEL), BF16),
        compiler_params=_cparams(("arbitrary", "arbitrary")),
        name="branch_mix",
    )(oa, ob, wa16, wb16, ub2d, ub2d)
    return pl.pallas_call(
        _out_body,
        grid=grid,
        in_specs=[
            pl.BlockSpec((tm, D_MODEL), lambda i, j: (i, 0)),
            pl.BlockSpec((D_MODEL, tn), lambda i, j: (0, j)),
            pl.BlockSpec((tm, tn), lambda i, j: (i, j)),
        ],
        out_specs=pl.BlockSpec((tm, tn), lambda i, j: (i, j)),
        out_shape=jax.ShapeDtypeStruct((m, D_MODEL), F32),
        compiler_params=_cparams(("arbitrary", "arbitrary")),
        name="out_proj",
    )(mixed, wo16, x2d)


def _kv_slices(ub):
    b, t_len, _ = ub.shape
    out = []
    for i in (0, 1, 3, 5):
        a = ub[:, :, UB_KV + i * NSA_KV_W:UB_KV + (i + 1) * NSA_KV_W]
        out.append(a.reshape(b, t_len, NSA_G, HEAD_DIM))
    return out


def kernel(x_prompt, x_sample, cache_k_cmp, cache_v_cmp, cache_k_sel, cache_v_sel, cache_k_win, cache_v_win, state_conv, state_gdn, page_table, norm_g, w_in, gdn_conv_w, gdn_a_log, gdn_dt_bias, gdn_norm_g, q_norm_g, k_norm_g, cmp_pe_k, cmp_w_k, cmp_proj_k, cmp_pe_v, cmp_w_v, cmp_proj_v, rel_bias, w_branch_a, w_branch_b, w_out):
    bp, tp, _ = x_prompt.shape
    bs, ts, _ = x_sample.shape
    wb = cache_k_win.shape[1]
    n_pool = cache_k_cmp.shape[0]
    kv4 = lambda a: a.reshape(a.shape[0], a.shape[1], NSA_G, HEAD_DIM)
    cmp_p = (cmp_pe_k, cmp_w_k, cmp_proj_k, cmp_pe_v, cmp_w_v, cmp_proj_v)

    assert _IN_MB + D_MODEL == w_in.shape[1]
    w_t = w_in.T
    wa16, wb16, wo16 = w_branch_a.astype(BF16), w_branch_b.astype(BF16), w_out.astype(BF16)

    def project(x):
        x2d = x.reshape(-1, D_MODEL)
        ua = _project(x2d, norm_g, w_t, UA_OFFSETS).reshape(x.shape[0], x.shape[1], UA_W)
        ub = _project(x2d, norm_g, w_t, UB_OFFSETS).reshape(x.shape[0], x.shape[1], UB_W)
        return x2d, ua, ub

    x2d, ua, ub = project(x_prompt)
    conv0 = jnp.zeros((bp, GDN_CONV - 1, GDN_QKV_W), F32)
    s0 = jnp.zeros((bp, GDN_V_HEADS, HEAD_DIM, HEAD_DIM), F32)
    o_a, p_gdn = _gdn(ua, ub, conv0, s0, gdn_conv_w, gdn_a_log, gdn_dt_bias, gdn_norm_g)
    p_conv = ua[:, tp - (GDN_CONV - 1):, :GDN_QKV_W]
    n_pages_p = tp // PAGE
    pages = ub.reshape(bp * n_pages_p, PAGE, UB_W)
    ident = jnp.arange(bp * n_pages_p, dtype=jnp.int32).reshape(bp, n_pages_p)
    kcmp, vcmp = _compress(pages, pages, UB_KV // NSA_KV_W, UB_KV // NSA_KV_W + 1, ident, *cmp_p, k_norm_g[0])
    p_kc, p_vc, p_ks, p_vs, p_kw, p_vw, ksb, vsb, kwb, vwb = _nsa_prep_prompt(ub, k_norm_g, wb)
    nc_p = (tp - CMP_LEN) // CMP_STRIDE + 1
    bias_cmp = _cmp_bias(rel_bias, jnp.arange(tp, dtype=jnp.int32), kcmp.shape[1], nc_p)
    ocmp, nsel = _cmp_select_prompt(ub, kcmp, vcmp, bias_cmp, q_norm_g)
    o_b = _attn_prompt(ub, ocmp, nsel, ksb, vsb, kwb, vwb, rel_bias, q_norm_g)
    y_prompt = _output(x2d, o_a.reshape(-1, GDN_V_W), o_b.reshape(-1, NSA_Q_W), ub.reshape(-1, UB_W), wa16, wb16, wo16)
    cache4 = lambda a: a.reshape(bp, a.shape[1] // NSA_G, NSA_G, HEAD_DIM)
    prompt_out = (y_prompt.reshape(x_prompt.shape), cache4(p_kc), cache4(p_vc), cache4(p_ks), cache4(p_vs), cache4(p_kw),
                  cache4(p_vw), p_conv, p_gdn)

    x2d, ua, ub = project(x_sample)
    o_a, s_gdn = _gdn(ua, ub, state_conv, state_gdn, gdn_conv_w, gdn_a_log, gdn_dt_bias, gdn_norm_g)
    s_conv = ua[:, ts - (GDN_CONV - 1):, :GDN_QKV_W]
    pool = lambda c: c.reshape(n_pool, PAGE * NSA_G, HEAD_DIM)
    kcmp, vcmp = _compress(pool(cache_k_cmp), pool(cache_v_cmp), 0, 0, page_table, *cmp_p, k_norm_g[0])
    s_ks, s_kwn = _nsa_prep(ub, k_norm_g)
    total = PAST_LEN + ts
    nc_s = (total - CMP_LEN) // CMP_STRIDE + 1
    nb_s = -(-total // SEL_BLOCK)
    assert nc_s + 1 == kcmp.shape[1]
    pos_s = PAST_LEN + jnp.arange(ts, dtype=jnp.int32)
    bias_cmp = _cmp_bias(rel_bias, pos_s, kcmp.shape[1], nc_s)
    ocmp, selt = _cmp_select_sample(ub, kcmp, vcmp, bias_cmp, q_norm_g, nb_s)
    osel = _sel_sample(ub, selt, pool(cache_k_sel), pool(cache_v_sel), page_table, s_ks, rel_bias, q_norm_g)
    win3 = lambda c: c.reshape(bs, wb * NSA_G, HEAD_DIM)
    o_b, s_kw, s_vw = _win_sample(ub, win3(cache_k_win), win3(cache_v_win), s_kwn, ocmp, osel, rel_bias, q_norm_g)
    y_sample = _output(x2d, o_a.reshape(-1, GDN_V_W), o_b.reshape(-1, NSA_Q_W), ub.reshape(-1, UB_W), wa16, wb16, wo16)
    s_kc, s_vc, s_vs, _ = _kv_slices(ub)
    sample_out = (y_sample.reshape(x_sample.shape), s_kc, s_vc, kv4(s_ks), s_vs, s_kw.reshape(cache_k_win.shape),
                  s_vw.reshape(cache_v_win.shape), s_conv, s_gdn)

    return (prompt_out[0], sample_out[0]) + prompt_out[1:] + sample_out[1:]
```

```python
import functools
import math

import jax
import jax.numpy as jnp
import numpy as np
from jax import lax
from jax.experimental import pallas as pl
from jax.experimental.pallas import tpu as pltpu

F32 = jnp.float32
BF16 = jnp.bfloat16
HI = lax.Precision.HIGHEST

D_MODEL = 2048
PAST_LEN = 16384
PAGE = 128

GDN_QK_HEADS = 16
GDN_V_HEADS = 32
HEAD_DIM = 128
GDN_CONV = 4
GDN_CHUNK = 64
GDN_GROUP = 8
GDN_QK_PER_STEP = 2
GDN_QK_W = GDN_QK_HEADS * HEAD_DIM
GDN_V_W = GDN_V_HEADS * HEAD_DIM
GDN_QKV_W = 2 * GDN_QK_W + GDN_V_W

NSA_HEADS = 16
NSA_G = 4
NSA_HPG = NSA_HEADS // NSA_G
NSA_Q_W = NSA_HEADS * HEAD_DIM
NSA_KV_W = NSA_G * HEAD_DIM
CMP_LEN = 32
CMP_STRIDE = 16
SEL_BLOCK = 64
SEL_TOP = 16
WINDOW = 512
REL_BUCKETS = 32
REL_MAX_DIST = 128

EPS = 1e-6
NEG = -1e30
BIG = 1e9

LANE = 128
VMEM_LIMIT = 56 * 1024 * 1024
PAGES_PER_STEP = 16

UA_W = GDN_QKV_W + GDN_V_W
UB_Q, UB_Z, UB_MA, UB_MB, UB_KV, UB_BA, UB_G = 0, 2048, 4096, 6144, 8192, 11264, 11776
UB_W = 12288
PROJ_TN = 512
PROJ_TM = 2048

_IN_QB = UA_W + 2 * GDN_V_HEADS
_IN_KV = _IN_QB + NSA_Q_W
_IN_G = _IN_KV + 6 * NSA_KV_W
_IN_ZB = _IN_G + 3 * NSA_HEADS
_IN_MA = _IN_ZB + NSA_Q_W
_IN_MB = _IN_MA + D_MODEL


def _tiles(start, width):
    return [start + k * PROJ_TN for k in range(width // PROJ_TN)]


UA_OFFSETS = _tiles(0, UA_W)
UB_OFFSETS = (_tiles(_IN_QB, NSA_Q_W) + _tiles(_IN_ZB, NSA_Q_W) + _tiles(_IN_MA, D_MODEL) + _tiles(_IN_MB, D_MODEL)
              + _tiles(_IN_KV, 6 * NSA_KV_W) + [UA_W, _IN_G])


def _mm(a, b):
    return jnp.dot(a.astype(BF16), b.astype(BF16), preferred_element_type=F32)


def _mm_nt(a, b):
    return lax.dot_general(a.astype(BF16), b.astype(BF16), (((1,), (1,)), ((), ())), preferred_element_type=F32)


def _mm_tn(a, b):
    return lax.dot_general(a.astype(BF16), b.astype(BF16), (((0,), (0,)), ((), ())), preferred_element_type=F32)


def _mm_hi(a, b):
    return jnp.dot(a, b, precision=HI, preferred_element_type=F32)


def _mm_nt_hi(a, b):
    return lax.dot_general(a, b, (((1,), (1,)), ((), ())), precision=HI, preferred_element_type=F32)


def _silu(x):
    return x * jax.nn.sigmoid(x)


def _softplus(x):
    return jnp.maximum(x, 0.0) + jnp.log1p(jnp.exp(-jnp.abs(x)))


def _rms(x, gain):
    return x * lax.rsqrt(jnp.mean(x * x, axis=-1, keepdims=True) + EPS) * gain


def _cparams(sem):
    return pltpu.CompilerParams(dimension_semantics=sem, vmem_limit_bytes=VMEM_LIMIT)


def _proj_body(offs_ref, x_ref, g_ref, w_ref, o_ref, h_ref):
    del offs_ref

    @pl.when(pl.program_id(1) == 0)
    def _():
        h_ref[...] = _rms(x_ref[...], g_ref[...]).astype(BF16)

    o_ref[...] = lax.dot_general(h_ref[...], w_ref[...].astype(BF16), (((1,), (1,)), ((), ())),
                                 preferred_element_type=F32)


def _project(x2d, norm_g, w_t, offsets):
    m = x2d.shape[0]
    tm = min(m, PROJ_TM)
    n_tiles = len(offsets)
    return pl.pallas_call(
        _proj_body,
        grid_spec=pltpu.PrefetchScalarGridSpec(
            num_scalar_prefetch=1,
            grid=(m // tm, n_tiles),
            in_specs=[
                pl.BlockSpec((tm, D_MODEL), lambda i, j, offs: (i, 0), pipeline_mode=pl.Buffered(1)),
                pl.BlockSpec((1, D_MODEL), lambda i, j, offs: (0, 0)),
                pl.BlockSpec((pl.Element(PROJ_TN), pl.Element(D_MODEL)), lambda i, j, offs: (pl.multiple_of(offs[j], 16), 0)),
            ],
            out_specs=pl.BlockSpec((tm, PROJ_TN), lambda i, j, offs: (i, j)),
            scratch_shapes=[pltpu.VMEM((tm, D_MODEL), BF16)]),
        out_shape=jax.ShapeDtypeStruct((m, n_tiles * PROJ_TN), F32),
        compiler_params=_cparams(("arbitrary", "arbitrary")),
        name="proj",
    )(jnp.asarray(offsets, jnp.int32), x2d, norm_g.reshape(1, D_MODEL), w_t)


def _split2(a):
    hi = a.astype(BF16)
    return hi, (a - hi.astype(F32)).astype(BF16)


def _dot16(a, b):
    return jnp.dot(a, b, preferred_element_type=F32)


def _mm_x3(a2, b2):
    return _dot16(a2[0], b2[0]) + (_dot16(a2[0], b2[1]) + _dot16(a2[1], b2[0]))


def _bdot(a, b):
    return lax.dot_general(a, b, (((2,), (1,)), ((0,), (0,))), preferred_element_type=F32)


def _bdot_nt(a, b):
    return lax.dot_general(a, b, (((2,), (2,)), ((0,), (0,))), preferred_element_type=F32)


def _bmm_x3(a2, b2):
    return _bdot(a2[0], b2[0]) + (_bdot(a2[0], b2[1]) + _bdot(a2[1], b2[0]))


INV_BASE = 8


def _unit_lower_inverse_pairs(nmat, row, col, left, c):
    def blockdiag(x2):
        return tuple(jnp.concatenate([jnp.where(left, x, jnp.zeros_like(x)), jnp.where(left, jnp.zeros_like(x), x)], axis=1)
                     for x in x2)

    def same_block(size):
        shift = size.bit_length() - 1
        return (row >> shift) == (col >> shift)

    base = min(INV_BASE, c)
    nd = jnp.where(same_block(base), nmat, 0.0)
    p = jnp.where(row == col, 1.0, 0.0) + nd
    m2 = _split2(nd)
    bd2 = blockdiag(m2)
    span = 2
    while span < base:
        m2 = _split2(_bmm_x3(m2, bd2))
        bd2 = blockdiag(m2)
        p = p + _bmm_x3(_split2(p), bd2)
        span *= 2
    size = base
    while size < c:
        n21 = jnp.where(same_block(2 * size), jnp.where(same_block(size), 0.0, nmat), 0.0)
        p2 = _split2(p)
        t_n = _bmm_x3(p2, blockdiag(_split2(n21)))
        p = p + _bmm_x3(_split2(t_n), blockdiag(p2))
        size *= 2
    return p


def _cumsum_rows(tril16, g):
    h = g.astype(BF16)
    r = g - h.astype(F32)
    m = r.astype(BF16)
    l = (r - m.astype(F32)).astype(BF16)
    return _dot16(tril16, h) + (_dot16(tril16, m) + _dot16(tril16, l))


def _gdn_body(q_ref, k_ref, v_ref, z_ref, ba_ref, cwq_ref, cwk_ref, cwv_ref, csq_ref, csk_ref, csv_ref,
              s0_ref, gp_ref, ng_ref, o_ref, sfin_ref, cq_s, ck_s, cv_s, u_s, w_s, qg_s, kd_s, at_s, gl_s, st_s,
              *, t_len, c, nq):
    nv = 2 * nq
    first_head = nv * pl.program_id(1)
    n_chunks = t_len // c
    for cs, carry in ((csq_ref, cq_s), (csk_ref, ck_s), (csv_ref, cv_s)):
        carry[...] = jnp.zeros(carry.shape, F32)
        carry[5:8, :] = cs[0]
    st_s[...] = s0_ref[0]

    gsz = min(GDN_GROUP, n_chunks)
    rg = gsz * c
    row = lax.broadcasted_iota(jnp.int32, (1, c, 2 * c), 1)
    lane2 = lax.broadcasted_iota(jnp.int32, (1, c, 2 * c), 2)
    left = lane2 < c
    col = lane2 & (c - 1)
    tril = row >= col
    strict = row > col
    eye = row == col
    lane = lax.broadcasted_iota(jnp.int32, (rg, LANE), 1)
    rr = lax.broadcasted_iota(jnp.int32, (rg, rg), 0)
    cc = lax.broadcasted_iota(jnp.int32, (rg, rg), 1)
    shift = c.bit_length() - 1
    assert 1 << shift == c
    tril16 = jnp.where((rr >= cc) & ((rr >> shift) == (cc >> shift)), 1.0, 0.0).astype(BF16)

    def conv(src_ref, carry, cw_ref, rows):
        cur = src_ref[0, rows, :]
        win = jnp.concatenate([carry[...], cur], axis=0)
        carry[...] = cur[rg - 8:rg]
        w = cw_ref[...]
        a = win[5:5 + rg] * w[0:1] + win[6:6 + rg] * w[1:2] + win[7:7 + rg] * w[2:3] + win[8:8 + rg] * w[3:4]
        return _silu(a)

    def pick(x, idx):
        return jnp.sum(jnp.where(lane == idx, x, 0.0), axis=-1, keepdims=True).reshape(gsz, c, 1)

    def per_head(x, n):
        return [x[:, i * HEAD_DIM:(i + 1) * HEAD_DIM].reshape(gsz, c, HEAD_DIM) for i in range(n)]

    def prepare(gi):
        base = gi * rg if isinstance(gi, int) else pl.multiple_of(gi * rg, rg)
        rows = pl.ds(base, rg)
        qa = per_head(conv(q_ref, cq_s, cwq_ref, rows), nq)
        ka = per_head(conv(k_ref, ck_s, cwk_ref, rows), nq)
        va = per_head(conv(v_ref, cv_s, cwv_ref, rows), nv)
        qn = [x * lax.rsqrt(jnp.sum(x * x, axis=-1, keepdims=True) + EPS) * (HEAD_DIM ** -0.5) for x in qa]
        kn = [x * lax.rsqrt(jnp.sum(x * x, axis=-1, keepdims=True) + EPS) for x in ka]
        ba = ba_ref[0, rows, :]
        beta_all = jax.nn.sigmoid(ba)
        g_all = -jnp.exp(gp_ref[0:1, :]) * _softplus(ba + gp_ref[1:2, :])
        gc_all = _cumsum_rows(tril16, g_all)
        k16 = jnp.concatenate(kn, axis=0).astype(BF16)
        k2x = jnp.concatenate([k16, k16], axis=1)
        kk = _bdot_nt(k16, k2x)
        qk = _bdot_nt(jnp.concatenate(qn, axis=0).astype(BF16), k2x)
        betas, gcols, blocks = [], [], []
        zeros = jnp.zeros((gsz, c, 2 * HEAD_DIM), F32)
        for qi in range(nq):
            pair = []
            for l in range(2):
                hh = 2 * qi + l
                beta = pick(beta_all, first_head + hh)
                gcol = pick(gc_all, GDN_V_HEADS + first_head + hh)
                egc = jnp.exp(gcol)
                glast = gcol[:, c - 1:c, :]
                rhs = jnp.concatenate([va[hh] * beta, kn[qi] * (beta * egc)], axis=2)
                pair.append((beta, gcol, rhs))
                qg_s[hh, rows, :] = (qn[qi] * egc).reshape(rg, HEAD_DIM).astype(qg_s.dtype)
                kd_s[hh, rows, :] = (kn[qi] * jnp.exp(glast - gcol)).reshape(rg, HEAD_DIM).astype(kd_s.dtype)
                egl = jnp.exp(glast)
                for j in range(gsz):
                    gl_s[hh, gi * gsz + j] = jnp.broadcast_to(egl[j], (8, HEAD_DIM))
            betas.append(jnp.where(left, pair[0][0], pair[1][0]))
            gcols.append(jnp.where(left, pair[0][1], pair[1][1]))
            blocks.append(jnp.concatenate([jnp.concatenate([pair[0][2], zeros], axis=2),
                                           jnp.concatenate([zeros, pair[1][2]], axis=2)], axis=1))
        beta = jnp.concatenate(betas, axis=0)
        gcol = jnp.concatenate(gcols, axis=0)
        grow = jnp.sum(jnp.where(eye, gcol, 0.0), axis=1, keepdims=True)
        gamma = jnp.where(tril, jnp.exp(jnp.minimum(gcol - grow, 0.0)), 0.0)
        tinv = _unit_lower_inverse_pairs(jnp.where(strict, -(kk * beta * gamma), 0.0), row, col, left, c)
        rhs2 = _split2(jnp.concatenate(blocks, axis=0))
        uw = _bmm_x3(_split2(tinv), rhs2)
        attn = qk * gamma
        for qi in range(nq):
            at_s[qi, rows, :] = attn[qi * gsz:(qi + 1) * gsz].reshape(rg, 2 * c).astype(at_s.dtype)
            for l in range(2):
                uw_h = uw[qi * gsz:(qi + 1) * gsz, :, 2 * l * HEAD_DIM:2 * (l + 1) * HEAD_DIM]
                u_s[2 * qi + l, rows, :] = uw_h[:, :, :HEAD_DIM].reshape(rg, HEAD_DIM)
                w_s[2 * qi + l, rows, :] = uw_h[:, :, HEAD_DIM:].reshape(rg, HEAD_DIM).astype(w_s.dtype)

    def recur(ci):
        rows = pl.ds(ci * c if isinstance(ci, int) else pl.multiple_of(ci * c, c), c)
        s = st_s[...]
        s16 = s.astype(BF16)
        v_new = u_s[:, rows, :] - _bdot(w_s[:, rows, :].astype(BF16), s16)
        v16 = v_new.astype(BF16)
        zv = jnp.zeros((c, HEAD_DIM), BF16)
        vbd = jnp.stack([jnp.concatenate([jnp.concatenate([v16[2 * qi], zv], axis=1),
                                          jnp.concatenate([zv, v16[2 * qi + 1]], axis=1)], axis=0) for qi in range(nq)], axis=0)
        o_state = _bdot(qg_s[:, rows, :].astype(BF16), s16)
        o_attn = _bdot(at_s[:, rows, :].astype(BF16), vbd)
        kd16 = kd_s[:, rows, :].astype(BF16)
        upd = jnp.stack([_mm_tn(kd16[hh], v16[hh]) for hh in range(nv)], axis=0)
        st_s[...] = s * gl_s[:, ci][:, 0:1, :] + upd
        for hh in range(nv):
            hsl = slice(hh * HEAD_DIM, (hh + 1) * HEAD_DIM)
            o = o_state[hh] + o_attn[hh // 2][:, (hh % 2) * HEAD_DIM:(hh % 2 + 1) * HEAD_DIM]
            o_ref[0, rows, hsl] = (_rms(o, ng_ref[...]) * _silu(z_ref[0, rows, hsl])).astype(o_ref.dtype)

    if n_chunks == 1:
        prepare(0)
        recur(0)
    else:
        def prepare_step(gi, carry):
            prepare(gi)
            return carry

        def recur_step(ci, carry):
            recur(ci)
            return carry

        lax.fori_loop(0, n_chunks // gsz, prepare_step, 0)
        lax.fori_loop(0, n_chunks, recur_step, 0)
    sfin_ref[0] = st_s[...]


def _gdn(ua, ub, conv_state, s0, conv_w, a_log, dt_bias, norm_g):
    b, t_len, _ = ua.shape
    c = min(GDN_CHUNK, t_len)
    hd = HEAD_DIM
    nq = GDN_QK_PER_STEP if t_len > GDN_CHUNK else 4 * GDN_QK_PER_STEP
    nv = 2 * nq
    steps = GDN_QK_HEADS // nq
    qw, vw = nq * hd, nv * hd
    op_dtype = BF16 if c % 16 == 0 else F32
    gp = jnp.zeros((2, LANE), F32)
    gp = gp.at[0, GDN_V_HEADS:2 * GDN_V_HEADS].set(a_log).at[1, GDN_V_HEADS:2 * GDN_V_HEADS].set(dt_bias)
    body = functools.partial(_gdn_body, t_len=t_len, c=c, nq=nq)
    return pl.pallas_call(
        body,
        grid=(b, steps),
        in_specs=[
            pl.BlockSpec((1, t_len, qw), lambda bi, i: (bi, 0, i)),
            pl.BlockSpec((1, t_len, qw), lambda bi, i: (bi, 0, steps + i)),
            pl.BlockSpec((1, t_len, vw), lambda bi, i: (bi, 0, steps + i)),
            pl.BlockSpec((1, t_len, vw), lambda bi, i: (bi, 0, 2 * steps + i)),
            pl.BlockSpec((1, t_len, LANE), lambda bi, i: (bi, 0, UB_BA // LANE)),
            pl.BlockSpec((GDN_CONV, qw), lambda bi, i: (0, i)),
            pl.BlockSpec((GDN_CONV, qw), lambda bi, i: (0, steps + i)),
            pl.BlockSpec((GDN_CONV, vw), lambda bi, i: (0, steps + i)),
            pl.BlockSpec((1, GDN_CONV - 1, qw), lambda bi, i: (bi, 0, i)),
            pl.BlockSpec((1, GDN_CONV - 1, qw), lambda bi, i: (bi, 0, steps + i)),
            pl.BlockSpec((1, GDN_CONV - 1, vw), lambda bi, i: (bi, 0, steps + i)),
            pl.BlockSpec((1, nv, hd, hd), lambda bi, i: (bi, i, 0, 0)),
            pl.BlockSpec((2, LANE), lambda bi, i: (0, 0)),
            pl.BlockSpec((1, hd), lambda bi, i: (0, 0)),
        ],
        out_specs=[
            pl.BlockSpec((1, t_len, vw), lambda bi, i: (bi, 0, i)),
            pl.BlockSpec((1, nv, hd, hd), lambda bi, i: (bi, i, 0, 0)),
        ],
        out_shape=[
            jax.ShapeDtypeStruct((b, t_len, GDN_V_W), BF16),
            jax.ShapeDtypeStruct((b, GDN_V_HEADS, hd, hd), F32),
        ],
        scratch_shapes=[
            pltpu.VMEM((8, qw), F32),
            pltpu.VMEM((8, qw), F32),
            pltpu.VMEM((8, vw), F32),
            pltpu.VMEM((nv, t_len, hd), F32),
            pltpu.VMEM((nv, t_len, hd), op_dtype),
            pltpu.VMEM((nv, t_len, hd), op_dtype),
            pltpu.VMEM((nv, t_len, hd), op_dtype),
            pltpu.VMEM((nq, t_len, 2 * c), op_dtype),
            pltpu.VMEM((nv, t_len // c, 8, hd), F32),
            pltpu.VMEM((nv, hd, hd), F32),
        ],
        compiler_params=_cparams(("arbitrary", "arbitrary")),
        name="gdn",
    )(ua, ua, ua, ua, ub, conv_w, conv_w, conv_w, conv_state, conv_state, conv_state, s0, gp, norm_g.reshape(1, hd))


def _t5_bucket(rel):
    n = jnp.maximum(rel, 0)
    exact = REL_BUCKETS // 2
    nf = jnp.maximum(n, 1).astype(F32)
    large = exact + (jnp.log(nf / exact) / math.log(REL_MAX_DIST / exact) * (REL_BUCKETS - exact)).astype(jnp.int32)
    return jnp.where(n < exact, n, jnp.minimum(large, REL_BUCKETS - 1))


def _bias_of(rel_bias, rel):
    bucket = _t5_bucket(rel)[None]
    table = rel_bias.astype(F32)
    out = jnp.zeros((NSA_HEADS,) + rel.shape, F32)
    for k in range(REL_BUCKETS):
        out = jnp.where(bucket == k, table[k].reshape((NSA_HEADS,) + (1,) * rel.ndim), out)
    return out


def _cmp_bias(rel_bias, q_pos, ncp, nc):
    n = jnp.arange(ncp, dtype=jnp.int32)
    rel = q_pos[:, None] - (n * CMP_STRIDE + (CMP_LEN - 1))[None, :]
    ok = (rel >= 0) & (n < nc)[None, :]
    return jnp.where(ok[None], _bias_of(rel_bias, rel), NEG)


def _load_page(ref, interleaved):
    if not interleaved:
        return ref[0]
    n_rows = ref.shape[1] // NSA_G
    return jnp.concatenate([ref[0, pl.ds(g, n_rows, stride=NSA_G), :] for g in range(NSA_G)], axis=1)


def _compress_body(pt_ref, *refs, pps, n_steps, nch, interleaved):
    del pt_ref
    k_refs = refs[:pps]
    v_refs = refs[pps:2 * pps]
    (wabk_ref, wabv_ref, wk_ref, pek_ref, wv_ref, pev_ref, projk_ref, projv_ref, gain_ref,
     outk_ref, outv_ref, ak_s, bk_s, av_s, bv_s) = refs[2 * pps:]
    s = pl.program_id(1)

    @pl.when(s == 0)
    def _():
        bk_s[nch:nch + 8, :] = jnp.zeros((8, NSA_KV_W), F32)
        bv_s[nch:nch + 8, :] = jnp.zeros((8, NSA_KV_W), F32)

    def pool(w2, page):
        w_hi, w_lo = w2
        p_hi, p_lo = _split2(page)
        r = _dot16(jnp.concatenate([w_hi, w_lo], axis=0), p_hi)
        return r[0:16] + (r[16:32] + _dot16(w_hi, p_lo))

    wk2 = _split2(wabk_ref[...])
    wv2 = _split2(wabv_ref[...])
    for j in range(pps):
        row0 = pl.multiple_of((s * pps + j) * 8, 8)
        abk = pool(wk2, _load_page(k_refs[j], interleaved))
        ak_s[pl.ds(row0, 8), :] = abk[0:8]
        bk_s[pl.ds(row0, 8), :] = abk[8:16]
        abv = pool(wv2, _load_page(v_refs[j], interleaved))
        av_s[pl.ds(row0, 8), :] = abv[0:8]
        bv_s[pl.ds(row0, 8), :] = abv[8:16]

    @pl.when(s == n_steps - 1)
    def _():
        cpe_k = jnp.sum(wk_ref[...] * pek_ref[...], axis=0, keepdims=True)
        cpe_v = jnp.sum(wv_ref[...] * pev_ref[...], axis=0, keepdims=True)
        projk2 = _split2(projk_ref[...])
        projv2 = _split2(projv_ref[...])
        rb = min(nch, 128)

        def fin(r, carry):
            r0 = pl.multiple_of(r * rb, rb)
            pk = ak_s[pl.ds(r0, rb), :] + bk_s[pl.ds(r0, rb + 8), :][1:rb + 1]
            pv = av_s[pl.ds(r0, rb), :] + bv_s[pl.ds(r0, rb + 8), :][1:rb + 1]
            for g in range(NSA_G):
                sl = slice(g * HEAD_DIM, (g + 1) * HEAD_DIM)
                yk = _mm_x3(_split2(pk[:, sl] + cpe_k), projk2)
                outk_ref[0, pl.ds(r0, rb), sl] = _rms(yk, gain_ref[...])
                outv_ref[0, pl.ds(r0, rb), sl] = _mm_x3(_split2(pv[:, sl] + cpe_v), projv2)
            return carry

        lax.fori_loop(0, nch // rb, fin, 0)


def _pool_weights(w):
    c = np.arange(8)[:, None]
    t = np.arange(PAGE)[None, :]
    off = t - CMP_STRIDE * c
    inside = (off >= 0) & (off < CMP_STRIDE)
    idx = np.clip(off, 0, CMP_STRIDE - 1)
    wa = jnp.where(inside, w[idx], 0.0)
    wb = jnp.where(inside, w[idx + CMP_STRIDE], 0.0)
    return jnp.concatenate([wa, wb], axis=0).astype(F32)


def _compress(pages_k, pages_v, col_k, col_v, table, pe_k, w_k, proj_k, pe_v, w_v, proj_v, k_gain):
    b, n_pages = table.shape
    pps = min(PAGES_PER_STEP, n_pages)
    n_steps = n_pages // pps
    nch = 8 * n_pages
    interleaved = pages_k.shape[1] == NSA_G * PAGE
    body = functools.partial(_compress_body, pps=pps, n_steps=n_steps, nch=nch, interleaved=interleaved)

    def page_spec(j, col):
        if interleaved:
            return pl.BlockSpec((1, NSA_G * PAGE, HEAD_DIM), lambda bi, s, pt: (pt[bi, s * pps + j], 0, 0))
        return pl.BlockSpec((1, PAGE, NSA_KV_W), lambda bi, s, pt: (pt[bi, s * pps + j], 0, col))

    full = lambda shape: pl.BlockSpec(shape, lambda bi, s, pt: (0,) * len(shape))
    in_specs = [page_spec(j, col_k) for j in range(pps)] + [page_spec(j, col_v) for j in range(pps)]
    in_specs += [full((16, PAGE)), full((16, PAGE)), full((CMP_LEN, 1)), full((CMP_LEN, HEAD_DIM)),
                 full((CMP_LEN, 1)), full((CMP_LEN, HEAD_DIM)), full((HEAD_DIM, HEAD_DIM)), full((HEAD_DIM, HEAD_DIM)),
                 full((1, HEAD_DIM))]
    out_spec = pl.BlockSpec((1, nch, NSA_KV_W), lambda bi, s, pt: (bi, 0, 0))
    return pl.pallas_call(
        body,
        grid_spec=pltpu.PrefetchScalarGridSpec(
            num_scalar_prefetch=1, grid=(b, n_steps), in_specs=in_specs, out_specs=[out_spec, out_spec],
            scratch_shapes=[pltpu.VMEM((nch, NSA_KV_W), F32), pltpu.VMEM((nch + 8, NSA_KV_W), F32),
                            pltpu.VMEM((nch, NSA_KV_W), F32), pltpu.VMEM((nch + 8, NSA_KV_W), F32)]),
        out_shape=[jax.ShapeDtypeStruct((b, nch, NSA_KV_W), F32)] * 2,
        compiler_params=_cparams(("arbitrary", "arbitrary")),
        name="compress",
    )(table, *([pages_k] * pps), *([pages_v] * pps), _pool_weights(w_k), _pool_weights(w_v),
      w_k.reshape(CMP_LEN, 1), pe_k, w_v.reshape(CMP_LEN, 1), pe_v, proj_k, proj_v, k_gain.reshape(1, HEAD_DIM))


def _prep_body(ks_ref, kw_ref, kg_ref, pks_ref, pkw_ref):
    for g in range(NSA_G):
        sl = slice(g * HEAD_DIM, (g + 1) * HEAD_DIM)
        pks_ref[0, :, sl] = _rms(ks_ref[0, :, sl], kg_ref[1:2, :])
        pkw_ref[0, :, sl] = _rms(kw_ref[0, :, sl], kg_ref[2:3, :])


def _prep_prompt_body(kc_ref, vc_ref, ks_ref, vs_ref, kw_ref, vw_ref, kg_ref,
                      okc_ref, ovc_ref, oks_ref, ovs_ref, okw_ref, ovw_ref, ks16_ref, vst_ref, kw16_ref, vwt_ref):
    tp = ks_ref.shape[1]

    def put(dst, g, x):
        dst[0, pl.ds(g, tp, stride=NSA_G), :] = x

    for g in range(NSA_G):
        sl = slice(g * HEAD_DIM, (g + 1) * HEAD_DIM)
        ksn = _rms(ks_ref[0, :, sl], kg_ref[1:2, :])
        kwn = _rms(kw_ref[0, :, sl], kg_ref[2:3, :])
        ks16_ref[0, :, sl] = ksn.astype(BF16)
        kw16_ref[0, :, sl] = kwn.astype(BF16)
        put(oks_ref, g, ksn)
        put(okw_ref, g, kwn)
        for src, dst in ((kc_ref, okc_ref), (vc_ref, ovc_ref), (vs_ref, ovs_ref), (vw_ref, ovw_ref)):
            put(dst, g, src[0, :, sl])
    for src, dst in ((vs_ref, vst_ref), (vw_ref, vwt_ref)):
        for j in range(dst.shape[1]):
            dst[0, j] = src[0, j * LANE:(j + 1) * LANE, :].T.astype(BF16)


def _nsa_prep_prompt(ub, k_norm_g, wb):
    b, t_len, _ = ub.shape
    tp = min(t_len, 512)
    assert wb % tp == 0 and t_len % tp == 0 and t_len >= wb
    first_win = (t_len - wb) // tp
    kv0 = UB_KV // NSA_KV_W
    spec = lambda c: pl.BlockSpec((1, tp, NSA_KV_W), lambda bi, i: (bi, i, c))
    o_spec = pl.BlockSpec((1, tp, NSA_KV_W), lambda bi, i: (bi, i, 0))
    il_spec = pl.BlockSpec((1, NSA_G * tp, HEAD_DIM), lambda bi, i: (bi, i, 0))
    win_spec = pl.BlockSpec((1, NSA_G * tp, HEAD_DIM), lambda bi, i: (bi, jnp.maximum(i - first_win, 0), 0))
    t_spec = pl.BlockSpec((1, tp // LANE, NSA_KV_W, LANE), lambda bi, i: (bi, i, 0, 0))
    il = jax.ShapeDtypeStruct((b, NSA_G * t_len, HEAD_DIM), F32)
    il_win = jax.ShapeDtypeStruct((b, NSA_G * wb, HEAD_DIM), F32)
    k16 = jax.ShapeDtypeStruct((b, t_len, NSA_KV_W), BF16)
    v16 = jax.ShapeDtypeStruct((b, t_len // LANE, NSA_KV_W, LANE), BF16)
    return pl.pallas_call(
        _prep_prompt_body,
        grid=(b, t_len // tp),
        in_specs=[spec(kv0 + c) for c in range(6)] + [pl.BlockSpec((3, HEAD_DIM), lambda bi, i: (0, 0))],
        out_specs=[il_spec] * 4 + [win_spec] * 2 + [o_spec, t_spec, o_spec, t_spec],
        out_shape=[il] * 4 + [il_win] * 2 + [k16, v16, k16, v16],
        compiler_params=_cparams(("arbitrary", "arbitrary")),
        name="nsa_prep_prompt",
    )(ub, ub, ub, ub, ub, ub, k_norm_g)


def _nsa_prep(ub, k_norm_g):
    b, t_len, _ = ub.shape
    tp = min(t_len, 512)
    kv0 = UB_KV // NSA_KV_W
    spec = lambda c: pl.BlockSpec((1, tp, NSA_KV_W), lambda bi, i: (bi, i, c))
    o_spec = pl.BlockSpec((1, tp, NSA_KV_W), lambda bi, i: (bi, i, 0))
    return pl.pallas_call(
        _prep_body,
        grid=(b, t_len // tp),
        in_specs=[spec(kv0 + 2), spec(kv0 + 4), pl.BlockSpec((3, HEAD_DIM), lambda bi, i: (0, 0))],
        out_specs=[o_spec] * 2,
        out_shape=[jax.ShapeDtypeStruct((b, t_len, NSA_KV_W), F32)] * 2,
        compiler_params=_cparams(("arbitrary", "arbitrary")),
        name="nsa_prep",
    )(ub, ub, k_norm_g)


def _sel_matrix(nc, nb, ncp, nbp):
    j = np.arange(nb)
    lo = np.clip((SEL_BLOCK * j - CMP_LEN) // CMP_STRIDE + 1, 0, nc)
    hi = np.clip(-(-(SEL_BLOCK * (j + 1)) // CMP_STRIDE), 0, nc)
    n = np.arange(ncp)[:, None]
    m = np.zeros((ncp, nbp), np.float32)
    m[:, :nb] = (n >= lo[None, :]) & (n < hi[None, :])
    return jnp.asarray(m)


def _cmp_body(q_ref, kc_ref, vc_ref, bias_ref, mselt_ref, qg_ref, gate_ref, ocmpt_ref, nsel_ref, *, tq, nb):
    qi = pl.program_id(1)
    nbp = -(-nb // 8) * 8
    gates_t = jax.nn.sigmoid(gate_ref[0]).T
    blk = lax.broadcasted_iota(jnp.int32, (nbp, tq), 0)
    qblk = (qi * tq + lax.broadcasted_iota(jnp.int32, (nbp, tq), 1)) >> 6
    scale = HEAD_DIM ** -0.5
    for g in range(NSA_G):
        gsl = slice(g * HEAD_DIM, (g + 1) * HEAD_DIM)
        kc = kc_ref[0, :, gsl]
        vct = vc_ref[0, :, gsl].T.astype(BF16)
        imp = jnp.zeros((kc.shape[0], tq), F32)
        heads = [g * NSA_HPG + hh for hh in range(NSA_HPG)]
        kc2 = _split2(kc)
        scores = []
        for h in heads:
            q2 = _split2(_rms(q_ref[0, :, h * HEAD_DIM:(h + 1) * HEAD_DIM], qg_ref[...]) * scale)
            nt = lambda a, b: lax.dot_general(a, b, (((1,), (1,)), ((), ())), preferred_element_type=F32)
            scores.append(nt(kc2[0], q2[0]) + (nt(kc2[0], q2[1]) + nt(kc2[1], q2[0])))
        probs = []
        for h, s in zip(heads, scores):
            bias = bias_ref[h]
            s = bias + s
            e = jnp.exp(s - jnp.max(s, axis=0, keepdims=True))
            p = e * (1.0 / jnp.sum(e, axis=0, keepdims=True)) * jnp.where(bias > 0.5 * NEG, 1.0, 0.0)
            probs.append(p)
            imp = imp + p
        for h, p in zip(heads, probs):
            hsl = slice(h * HEAD_DIM, (h + 1) * HEAD_DIM)
            ocmpt_ref[0, hsl, :] = gates_t[h:h + 1, :] * _dot16(vct, p.astype(BF16))
        score = _cumsum_rows(mselt_ref[0:nbp, :].astype(BF16), imp)
        score = jnp.where(blk == 0, BIG, score)
        score = jnp.where(blk == qblk, BIG, score)
        score = jnp.where(blk == qblk - 1, BIG, score)
        score = jnp.where(blk <= qblk, score, -BIG)
        rank = jnp.zeros((nbp, tq), F32)
        for i in range(nb):
            si = score[i:i + 1, :]
            ge = jnp.where(si >= score, 1.0, 0.0)
            gt = jnp.where(si > score, 1.0, 0.0)
            rank = rank + jnp.where(blk > i, ge, gt)
        nsel_t = jnp.where(rank < SEL_TOP, 0.0, jnp.where(blk < nb, 1.0, 0.0))
        nsel = jnp.concatenate([nsel_t, jnp.zeros((LANE - nbp, tq), F32)], axis=0).T
        nsel_ref[0, :, gsl] = nsel.astype(BF16)


def _cmp_select_prompt(ub, kcmp, vcmp, bias_cmp, q_norm_g):
    b, t_len, _ = ub.shape
    tq = 256
    nb = t_len // SEL_BLOCK
    ncp = kcmp.shape[1]
    nc = (t_len - CMP_LEN) // CMP_STRIDE + 1
    assert nb <= LANE and ncp == LANE
    body = functools.partial(_cmp_body, tq=tq, nb=nb)
    return pl.pallas_call(
        body,
        grid=(b, t_len // tq),
        in_specs=[
            pl.BlockSpec((1, tq, NSA_Q_W), lambda bi, i: (bi, i, UB_Q // NSA_Q_W)),
            pl.BlockSpec((1, ncp, NSA_KV_W), lambda bi, i: (bi, 0, 0)),
            pl.BlockSpec((1, ncp, NSA_KV_W), lambda bi, i: (bi, 0, 0)),
            pl.BlockSpec((NSA_HEADS, ncp, tq), lambda bi, i: (0, 0, i)),
            pl.BlockSpec((LANE, ncp), lambda bi, i: (0, 0)),
            pl.BlockSpec((1, HEAD_DIM), lambda bi, i: (0, 0)),
            pl.BlockSpec((1, tq, LANE), lambda bi, i: (bi, i, UB_G // LANE)),
        ],
        out_specs=[
            pl.BlockSpec((1, NSA_Q_W, tq), lambda bi, i: (bi, 0, i)),
            pl.BlockSpec((1, tq, NSA_G * LANE), lambda bi, i: (bi, i, 0)),
        ],
        out_shape=[jax.ShapeDtypeStruct((b, NSA_Q_W, t_len), F32),
                   jax.ShapeDtypeStruct((b, t_len, NSA_G * LANE), BF16)],
        compiler_params=_cparams(("arbitrary", "arbitrary")),
        name="cmp_select",
    )(ub, kcmp, vcmp, jnp.swapaxes(bias_cmp, 1, 2), _sel_matrix(nc, nb, ncp, LANE).T, q_norm_g.reshape(1, HEAD_DIM), ub)


def _attn_body(q_ref, ocmp_ref, nsel_ref, ks_ref, vst_ref, kw_ref, vwt_ref, epen_ref, bt_ref, cb_ref, gate_ref, z_ref, qg_ref,
               o_ref, m_s, l_s, acc_s, osw_s, *, tq):
    qi = pl.program_id(1)
    t0 = qi * tq
    rows = NSA_HPG * tq
    tk = LANE
    def rel_of(kb, width):
        tok = t0 + (lax.broadcasted_iota(jnp.int32, (width * tk, rows), 1) & (tq - 1))
        return tok - (kb + lax.broadcasted_iota(jnp.int32, (width * tk, rows), 0))

    gates_t = jax.nn.sigmoid(gate_ref[0]).T
    scale = HEAD_DIM ** -0.5
    n_near = bt_ref.shape[0]
    j_near = jnp.maximum(qi + 1 - n_near, 0)

    groups = range(NSA_G)
    gsl = [slice(g * HEAD_DIM, (g + 1) * HEAD_DIM) for g in groups]
    heads = [[g * NSA_HPG + hh for hh in range(NSA_HPG)] for g in groups]
    q_sel, q_win = [], []
    for g in groups:
        q4 = jnp.concatenate(
            [(_rms(q_ref[0, :, h * HEAD_DIM:(h + 1) * HEAD_DIM], qg_ref[...]) * scale).astype(BF16) for h in heads[g]], axis=0)
        ns = nsel_ref[0, :, g * LANE:(g + 1) * LANE].astype(F32)
        aug_sel = jnp.concatenate([ns + cb_ref[h:h + 1, :] for h in heads[g]], axis=0).astype(BF16)
        aug_win = jnp.concatenate([jnp.broadcast_to(cb_ref[h:h + 1, :], (tq, LANE)) for h in heads[g]], axis=0).astype(BF16)
        q_sel.append(jnp.concatenate([q4, aug_sel], axis=1))
        q_win.append(jnp.concatenate([q4, aug_win], axis=1))

    def run(k_ref, vt_ref, qmats, lo, window, far_widths):
        m_s[...] = jnp.full(m_s.shape, NEG, F32)
        l_s[...] = jnp.zeros(l_s.shape, F32)
        acc_s[...] = jnp.zeros(acc_s.shape, F32)

        def step(j, near, width=1):
            kb = pl.multiple_of(j * tk, tk)
            kt = k_ref[0, pl.ds(kb, width * tk), :]
            ep = epen_ref[pl.ds(kb, width * tk), :]
            ss = [lax.dot_general(jnp.concatenate([kt[:, gsl[g]], ep], axis=1), qmats[g], (((1,), (1,)), ((), ())),
                                  preferred_element_type=F32) for g in groups]
            rel = rel_of(kb, width)
            vt = vt_ref[0, j] if width == 1 else jnp.concatenate([vt_ref[0, j + i] for i in range(width)], axis=1)
            for g in groups:
                s = ss[g]
                if near:
                    bias = [bt_ref[width - 1 - i, :, g * rows:(g + 1) * rows] for i in range(width)]
                    s = jnp.where(rel >= 0, (bias[0] if width == 1 else jnp.concatenate(bias, axis=0)) + s, NEG)
                elif window:
                    s = jnp.where(rel < WINDOW, s, NEG)
                m_old = m_s[g]
                m_new = jnp.maximum(m_old, jnp.max(s, axis=0, keepdims=True))
                alpha = jnp.exp(m_old - m_new)
                p = jnp.exp(s - m_new)
                l_s[g] = alpha * l_s[g] + jnp.sum(p, axis=0, keepdims=True)
                acc_s[g] = alpha * acc_s[g] + jnp.dot(vt[gsl[g], :], p.astype(BF16), preferred_element_type=F32)
                m_s[g] = m_new

        start = lo
        for width in far_widths:
            def far_step(i, carry, start=start, width=width):
                step(start + width * i, False, width)
                return carry

            n = jnp.maximum(j_near - start, 0) // width
            lax.fori_loop(0, n, far_step, 0)
            start = start + width * n

        @pl.when(qi >= n_near - 1)
        def _():
            step(qi + 1 - n_near, True, n_near)

        for short in range(1, n_near):
            @pl.when(qi == short - 1)
            def _(short=short):
                step(0, True, short)

    def gate_rows(branch, g):
        return jnp.concatenate([gates_t[branch * NSA_HEADS + h:branch * NSA_HEADS + h + 1, :] for h in heads[g]], axis=1)

    run(ks_ref, vst_ref, q_sel, 0, False, (4, 2, 1))
    for g in groups:
        osw_s[g] = acc_s[g] * (gate_rows(1, g) / l_s[g])
    run(kw_ref, vwt_ref, q_win, jnp.maximum(qi - WINDOW // tk, 0), True, (2, 1))
    for g in groups:
        osw = osw_s[g] + acc_s[g] * (gate_rows(2, g) / l_s[g])
        for hh, h in enumerate(heads[g]):
            hsl = slice(h * HEAD_DIM, (h + 1) * HEAD_DIM)
            o = (ocmp_ref[0, hsl, :] + osw[:, hh * tq:(hh + 1) * tq]).T
            o_ref[0, :, hsl] = (o * _silu(z_ref[0, :, hsl])).astype(o_ref.dtype)


def _far_bias(rel_bias):
    return _bias_of(rel_bias, jnp.full((1,), REL_MAX_DIST, jnp.int32))[:, 0]


def _toeplitz_bias_t(rel_bias, tq, n_near):
    d = jnp.arange(n_near, dtype=jnp.int32)[:, None, None]
    c = jnp.arange(LANE, dtype=jnp.int32)[None, :, None]
    r = jnp.arange(tq, dtype=jnp.int32)[None, None, :]
    bt = _bias_of(rel_bias, d * LANE + r - c) - _far_bias(rel_bias)[:, None, None, None]
    return jnp.transpose(bt, (1, 2, 0, 3)).reshape(n_near, LANE, NSA_HEADS * tq)


AUG_BIAS_LANES = (LANE - 2, LANE - 1)


def _far_bias_columns(rel_bias):
    c = _far_bias(rel_bias)
    hi = c.astype(BF16).astype(F32)
    lo = (c - hi).astype(BF16).astype(F32)
    return jnp.zeros((NSA_HEADS, LANE), F32).at[:, AUG_BIAS_LANES[0]].set(hi).at[:, AUG_BIAS_LANES[1]].set(lo)


def _block_penalty(t_len):
    key = np.arange(t_len)[:, None]
    j = np.arange(LANE)[None, :]
    pen = np.where(key // SEL_BLOCK == j, NEG, 0.0)
    pen[:, AUG_BIAS_LANES[0]:] = 1.0
    return jnp.asarray(pen, BF16)


def _attn_prompt(ub, ocmp, nsel, ksb, vst, kwb, vwt, rel_bias, q_norm_g):
    b, t_len, _ = ub.shape
    tq = 128
    rows = NSA_HPG * tq
    assert t_len // SEL_BLOCK <= AUG_BIAS_LANES[0]
    n_near = -(-(REL_MAX_DIST + LANE - 1) // LANE)
    bt = _toeplitz_bias_t(rel_bias, tq, n_near)
    body = functools.partial(_attn_body, tq=tq)
    k_spec = pl.BlockSpec((1, t_len, NSA_KV_W), lambda bi, i: (bi, 0, 0))
    v_spec = pl.BlockSpec((1, t_len // LANE, NSA_KV_W, LANE), lambda bi, i: (bi, 0, 0, 0))
    return pl.pallas_call(
        body,
        grid=(b, t_len // tq),
        in_specs=[
            pl.BlockSpec((1, tq, NSA_Q_W), lambda bi, i: (bi, i, UB_Q // NSA_Q_W)),
            pl.BlockSpec((1, NSA_Q_W, tq), lambda bi, i: (bi, 0, i)),
            pl.BlockSpec((1, tq, NSA_G * LANE), lambda bi, i: (bi, i, 0)),
            k_spec, v_spec, k_spec, v_spec,
            pl.BlockSpec((t_len, LANE), lambda bi, i: (0, 0)),
            pl.BlockSpec(bt.shape, lambda bi, i: (0, 0, 0)),
            pl.BlockSpec((NSA_HEADS, LANE), lambda bi, i: (0, 0)),
            pl.BlockSpec((1, tq, LANE), lambda bi, i: (bi, i, UB_G // LANE)),
            pl.BlockSpec((1, tq, NSA_Q_W), lambda bi, i: (bi, i, UB_Z // NSA_Q_W)),
            pl.BlockSpec((1, HEAD_DIM), lambda bi, i: (0, 0)),
        ],
        out_specs=pl.BlockSpec((1, tq, NSA_Q_W), lambda bi, i: (bi, i, 0)),
        out_shape=jax.ShapeDtypeStruct((b, t_len, NSA_Q_W), BF16),
        scratch_shapes=[pltpu.VMEM((NSA_G, 1, rows), F32), pltpu.VMEM((NSA_G, 1, rows), F32),
                        pltpu.VMEM((NSA_G, HEAD_DIM, rows), F32), pltpu.VMEM((NSA_G, HEAD_DIM, rows), F32)],
        compiler_params=_cparams(("arbitrary", "arbitrary")),
        name="nsa_attn",
    )(ub, ocmp, nsel, ksb, vst, kwb, vwt, _block_penalty(t_len), bt, _far_bias_columns(rel_bias), ub, ub,
      q_norm_g.reshape(1, HEAD_DIM))


def _q_all(q_ref, qg_ref, ts):
    scale = HEAD_DIM ** -0.5
    zero = jnp.zeros((NSA_HPG * ts, HEAD_DIM), BF16)
    blocks = []
    for g in range(NSA_G):
        q4 = jnp.concatenate(
            [(_rms(q_ref[0, :, (g * NSA_HPG + hh) * HEAD_DIM:(g * NSA_HPG + hh + 1) * HEAD_DIM], qg_ref[...]) * scale).astype(BF16)
             for hh in range(NSA_HPG)], axis=0)
        blocks.append(jnp.concatenate([q4 if gg == g else zero for gg in range(NSA_G)], axis=1))
    return jnp.concatenate(blocks, axis=0)


def _row_to_col(row, n):
    eye = lax.broadcasted_iota(jnp.int32, (n, n), 0) == lax.broadcasted_iota(jnp.int32, (n, n), 1)
    return jnp.sum(jnp.where(eye, row, 0.0), axis=1, keepdims=True)


def _cmp_sample_body(q_ref, kc_ref, vc_ref, bias_ref, msel_ref, rep_ref, qg_ref, ocmp_ref, selt_ref, *, ts, nb, nbp):
    scale = HEAD_DIM ** -0.5
    ncp = kc_ref.shape[1]
    scores = []
    for g in range(NSA_G):
        gsl = slice(g * HEAD_DIM, (g + 1) * HEAD_DIM)
        q4 = jnp.concatenate(
            [_rms(q_ref[0, :, (g * NSA_HPG + hh) * HEAD_DIM:(g * NSA_HPG + hh + 1) * HEAD_DIM], qg_ref[...]) * scale
             for hh in range(NSA_HPG)], axis=0)
        bias = bias_ref[g * NSA_HPG:(g + 1) * NSA_HPG].reshape(NSA_HPG * ts, ncp)
        s = _mm_nt_hi(q4, kc_ref[0, :, gsl]) + bias
        e = jnp.exp(s - jnp.max(s, axis=-1, keepdims=True))
        p = e / jnp.sum(e, axis=-1, keepdims=True) * jnp.where(bias > 0.5 * NEG, 1.0, 0.0)
        o = _mm(p, vc_ref[0, :, gsl])
        imp = p[0:ts]
        for hh in range(NSA_HPG):
            h = g * NSA_HPG + hh
            ocmp_ref[0, :, h * HEAD_DIM:(h + 1) * HEAD_DIM] = o[hh * ts:(hh + 1) * ts]
            if hh:
                imp = imp + p[hh * ts:(hh + 1) * ts]
        scores.append(_mm_hi(imp, msel_ref[...]))
    score = jnp.concatenate(scores, axis=0)
    rows = NSA_G * ts
    lane = lax.broadcasted_iota(jnp.int32, (rows, nbp), 1)
    tok = lax.broadcasted_iota(jnp.int32, (rows, nbp), 0) & (ts - 1)
    qblk = (PAST_LEN + tok) >> 6
    forced = (lane == 0) | (lane == qblk) | (lane == qblk - 1)
    score = jnp.where(forced, BIG, score)
    score = jnp.where(lane <= qblk, score, -BIG)
    sel = jnp.zeros((rows, nbp), F32)
    lane_f = lane.astype(F32)
    for _ in range(SEL_TOP):
        mx = jnp.max(score, axis=-1, keepdims=True)
        first = jnp.min(jnp.where(score == mx, lane_f, float(nbp)), axis=-1, keepdims=True)
        pick = lane_f == first
        sel = jnp.where(pick, 1.0, sel)
        score = jnp.where(pick, -3e38, score)
    selt_ref[0] = _mm_tn(sel, rep_ref[...])


def _cmp_select_sample(ub, kcmp, vcmp, bias_cmp, q_norm_g, nb):
    b, ts, _ = ub.shape
    ncp = kcmp.shape[1]
    nc = ncp - 1
    nbp = -(-nb // LANE) * LANE
    assert ts & (ts - 1) == 0 and NSA_HEADS * ts == LANE
    rep = np.zeros((NSA_G * ts, LANE), np.float32)
    for g in range(NSA_G):
        for hh in range(NSA_HPG):
            for t in range(ts):
                rep[g * ts + t, (g * NSA_HPG + hh) * ts + t] = 1.0
    body = functools.partial(_cmp_sample_body, ts=ts, nb=nb, nbp=nbp)
    return pl.pallas_call(
        body,
        grid=(b,),
        in_specs=[
            pl.BlockSpec((1, ts, NSA_Q_W), lambda bi: (bi, 0, UB_Q // NSA_Q_W)),
            pl.BlockSpec((1, ncp, NSA_KV_W), lambda bi: (bi, 0, 0)),
            pl.BlockSpec((1, ncp, NSA_KV_W), lambda bi: (bi, 0, 0)),
            pl.BlockSpec((NSA_HEADS, ts, ncp), lambda bi: (0, 0, 0)),
            pl.BlockSpec((ncp, nbp), lambda bi: (0, 0)),
            pl.BlockSpec((NSA_G * ts, LANE), lambda bi: (0, 0)),
            pl.BlockSpec((1, HEAD_DIM), lambda bi: (0, 0)),
        ],
        out_specs=[
            pl.BlockSpec((1, ts, NSA_Q_W), lambda bi: (bi, 0, 0)),
            pl.BlockSpec((1, nbp, LANE), lambda bi: (bi, 0, 0)),
        ],
        out_shape=[jax.ShapeDtypeStruct((b, ts, NSA_Q_W), F32), jax.ShapeDtypeStruct((b, nbp, LANE), F32)],
        compiler_params=_cparams(("arbitrary",)),
        name="cmp_select_sample",
    )(ub, kcmp, vcmp, bias_cmp, _sel_matrix(nc, nb, ncp, nbp), jnp.asarray(rep), q_norm_g.reshape(1, HEAD_DIM))


def _sel_sample_body(pt_ref, *refs, pps, n_steps, ts):
    del pt_ref
    k_refs = refs[:pps]
    v_refs = refs[pps:2 * pps]
    (q_ref, selt_ref, knew_ref, vnew_ref, blast_ref, bfar_ref, bnew_ref, qg_ref,
     o_ref, qall_s, sc_s, snew_s, m_s, l_s, acc_s) = refs[2 * pps:]
    s = pl.program_id(1)
    n_pages = n_steps * pps
    sub = lax.broadcasted_iota(jnp.int32, (PAGE, LANE), 0)
    tok = lax.broadcasted_iota(jnp.int32, (ts, LANE), 1) & (ts - 1)
    new_t = lax.broadcasted_iota(jnp.int32, (ts, LANE), 0)

    @pl.when(s == 0)
    def _():
        qall = _q_all(q_ref, qg_ref, ts)
        qall_s[...] = qall
        sn = _mm_nt(knew_ref[0], qall) + bnew_ref[...]
        ok = (new_t <= tok) & (selt_ref[0, 2 * n_pages:2 * n_pages + 1, :] > 0.5)
        sn = jnp.where(ok, sn, NEG)
        snew_s[...] = sn
        m_s[...] = jnp.max(sn, axis=0, keepdims=True)
        l_s[...] = jnp.zeros((1, LANE), F32)
        acc_s[...] = jnp.zeros((LANE, NSA_KV_W), F32)

    @pl.when(s < n_steps)
    def _():
        m = m_s[...]
        for j in range(pps):
            p = s * pps + j
            st = _mm_nt(_load_page(k_refs[j], True), qall_s[...])
            st = st + jnp.where(p == n_pages - 1, blast_ref[...], bfar_ref[...])
            r0 = selt_ref[0, pl.ds(2 * p, 1), :]
            r1 = selt_ref[0, pl.ds(2 * p + 1, 1), :]
            ok = jnp.where(sub < SEL_BLOCK, r0, r1) > 0.5
            st = jnp.where(ok, st, NEG)
            sc_s[pl.ds(pl.multiple_of(p * PAGE, PAGE), PAGE), :] = st
            m = jnp.maximum(m, jnp.max(st, axis=0, keepdims=True))
        m_s[...] = m

    @pl.when(s >= n_steps)
    def _():
        m = m_s[...]
        l = l_s[...]
        acc = acc_s[...]
        for j in range(pps):
            p = (s - n_steps) * pps + j
            pt = jnp.exp(sc_s[pl.ds(pl.multiple_of(p * PAGE, PAGE), PAGE), :] - m)
            l = l + jnp.sum(pt, axis=0, keepdims=True)
            acc = acc + _mm_tn(pt, _load_page(v_refs[j], True))
        l_s[...] = l
        acc_s[...] = acc

    @pl.when(s == 2 * n_steps - 1)
    def _():
        pn = jnp.exp(snew_s[...] - m_s[...])
        l = l_s[...] + jnp.sum(pn, axis=0, keepdims=True)
        acc = acc_s[...] + _mm_tn(pn, vnew_ref[0])
        out = acc / _row_to_col(l, LANE)
        for h in range(NSA_HEADS):
            g = h // NSA_HPG
            o_ref[0, :, h * HEAD_DIM:(h + 1) * HEAD_DIM] = out[h * ts:(h + 1) * ts, g * HEAD_DIM:(g + 1) * HEAD_DIM]


def _lane_bias(rel_bias, key_pos, ts):
    rel = (PAST_LEN + jnp.arange(ts, dtype=jnp.int32))[None, :] - key_pos[:, None]
    return jnp.moveaxis(_bias_of(rel_bias, rel), 0, 1).reshape(key_pos.shape[0], NSA_HEADS * ts)


def _sel_sample(ub, selt, cache_k, cache_v, table, k_new, rel_bias, q_norm_g):
    b, ts, _ = ub.shape
    n_pages = table.shape[1]
    assert PAGE >= REL_MAX_DIST and n_pages * PAGE == PAST_LEN
    pps = min(PAGES_PER_STEP, n_pages)
    n_steps = n_pages // pps
    ar = jnp.arange
    blast = _lane_bias(rel_bias, PAST_LEN - PAGE + ar(PAGE, dtype=jnp.int32), ts)
    bfar = _lane_bias(rel_bias, jnp.zeros((1,), jnp.int32), ts)
    bnew = _lane_bias(rel_bias, PAST_LEN + ar(ts, dtype=jnp.int32), ts)
    body = functools.partial(_sel_sample_body, pps=pps, n_steps=n_steps, ts=ts)
    last = n_steps - 1
    page_block = (1, NSA_G * PAGE, HEAD_DIM)
    k_spec = lambda j: pl.BlockSpec(page_block, lambda bi, s, pt: (pt[bi, jnp.minimum(s, last) * pps + j], 0, 0))
    v_spec = lambda j: pl.BlockSpec(page_block, lambda bi, s, pt: (pt[bi, jnp.maximum(s - n_steps, 0) * pps + j], 0, 0))
    full = lambda shape: pl.BlockSpec(shape, lambda bi, s, pt: (0,) * len(shape))
    in_specs = [k_spec(j) for j in range(pps)] + [v_spec(j) for j in range(pps)] + [
        pl.BlockSpec((1, ts, NSA_Q_W), lambda bi, s, pt: (bi, 0, UB_Q // NSA_Q_W)),
        pl.BlockSpec((1,) + selt.shape[1:], lambda bi, s, pt: (bi, 0, 0)),
        pl.BlockSpec((1, ts, NSA_KV_W), lambda bi, s, pt: (bi, 0, 0)),
        pl.BlockSpec((1, ts, NSA_KV_W), lambda bi, s, pt: (bi, 0, UB_KV // NSA_KV_W + 3)),
        full((PAGE, LANE)), full((1, LANE)), full((ts, LANE)), full((1, HEAD_DIM)),
    ]
    return pl.pallas_call(
        body,
        grid_spec=pltpu.PrefetchScalarGridSpec(
            num_scalar_prefetch=1, grid=(b, 2 * n_steps), in_specs=in_specs,
            out_specs=pl.BlockSpec((1, ts, NSA_Q_W), lambda bi, s, pt: (bi, 0, 0)),
            scratch_shapes=[pltpu.VMEM((LANE, NSA_KV_W), BF16), pltpu.VMEM((PAST_LEN, LANE), F32),
                            pltpu.VMEM((ts, LANE), F32), pltpu.VMEM((1, LANE), F32), pltpu.VMEM((1, LANE), F32),
                            pltpu.VMEM((LANE, NSA_KV_W), F32)]),
        out_shape=jax.ShapeDtypeStruct((b, ts, NSA_Q_W), F32),
        compiler_params=_cparams(("arbitrary", "arbitrary")),
        name="sel_sample",
    )(table, *([cache_k] * pps), *([cache_v] * pps), ub, selt, k_new, ub, blast, bfar, bnew, q_norm_g.reshape(1, HEAD_DIM))


def _win_sample_body(q_ref, kc_ref, vc_ref, knew_ref, vnew_ref, bc_ref, bn_ref, gate_ref, z_ref, ocmp_ref, osel_ref, qg_ref,
                     o_ref, kout_ref, vout_ref, *, ts, wb):
    qall = _q_all(q_ref, qg_ref, ts)
    key_i = lax.broadcasted_iota(jnp.int32, (wb, LANE), 0)
    tok_c = lax.broadcasted_iota(jnp.int32, (wb, LANE), 1) & (ts - 1)
    tok_n = lax.broadcasted_iota(jnp.int32, (ts, LANE), 1) & (ts - 1)
    new_t = lax.broadcasted_iota(jnp.int32, (ts, LANE), 0)
    sc = _mm_nt(_load_page(kc_ref, True), qall) + bc_ref[...]
    sc = jnp.where(wb + tok_c - key_i < WINDOW, sc, NEG)
    sn = _mm_nt(knew_ref[0], qall) + bn_ref[...]
    sn = jnp.where(new_t <= tok_n, sn, NEG)
    m = jnp.maximum(jnp.max(sc, axis=0, keepdims=True), jnp.max(sn, axis=0, keepdims=True))
    pc = jnp.exp(sc - m)
    pn = jnp.exp(sn - m)
    l = jnp.sum(pc, axis=0, keepdims=True) + jnp.sum(pn, axis=0, keepdims=True)
    out = (_mm_tn(pc, _load_page(vc_ref, True)) + _mm_tn(pn, vnew_ref[0])) / _row_to_col(l, LANE)
    gates = jax.nn.sigmoid(gate_ref[0])
    for h in range(NSA_HEADS):
        g = h // NSA_HPG
        hsl = slice(h * HEAD_DIM, (h + 1) * HEAD_DIM)
        o_win = out[h * ts:(h + 1) * ts, g * HEAD_DIM:(g + 1) * HEAD_DIM]
        o = (gates[:, h:h + 1] * ocmp_ref[0, :, hsl] + gates[:, NSA_HEADS + h:NSA_HEADS + h + 1] * osel_ref[0, :, hsl]
             + gates[:, 2 * NSA_HEADS + h:2 * NSA_HEADS + h + 1] * o_win)
        o_ref[0, :, hsl] = (o * _silu(z_ref[0, :, hsl])).astype(o_ref.dtype)
    keep = (wb - ts) * NSA_G
    for out_ref, old_ref, new_ref in ((kout_ref, kc_ref, knew_ref), (vout_ref, vc_ref, vnew_ref)):
        out_ref[0, 0:keep, :] = old_ref[0, ts * NSA_G:wb * NSA_G, :]
        for g in range(NSA_G):
            out_ref[0, pl.ds(keep + g, ts, stride=NSA_G), :] = new_ref[0, :, g * HEAD_DIM:(g + 1) * HEAD_DIM]


def _win_sample(ub, cache_k, cache_v, k_new, ocmp, osel, rel_bias, q_norm_g):
    b, ts, _ = ub.shape
    wb = cache_k.shape[1] // NSA_G
    assert wb == WINDOW and ts % 8 == 0
    bc = _lane_bias(rel_bias, PAST_LEN - wb + jnp.arange(wb, dtype=jnp.int32), ts)
    bn = _lane_bias(rel_bias, PAST_LEN + jnp.arange(ts, dtype=jnp.int32), ts)
    body = functools.partial(_win_sample_body, ts=ts, wb=wb)
    tok_spec = lambda w, c: pl.BlockSpec((1, ts, w), lambda bi: (bi, 0, c))
    win_spec = pl.BlockSpec((1, wb * NSA_G, HEAD_DIM), lambda bi: (bi, 0, 0))
    return pl.pallas_call(
        body,
        grid=(b,),
        in_specs=[
            tok_spec(NSA_Q_W, UB_Q // NSA_Q_W), win_spec, win_spec,
            tok_spec(NSA_KV_W, 0), tok_spec(NSA_KV_W, UB_KV // NSA_KV_W + 5),
            pl.BlockSpec((wb, LANE), lambda bi: (0, 0)), pl.BlockSpec((ts, LANE), lambda bi: (0, 0)),
            tok_spec(LANE, UB_G // LANE), tok_spec(NSA_Q_W, UB_Z // NSA_Q_W),
            tok_spec(NSA_Q_W, 0), tok_spec(NSA_Q_W, 0),
            pl.BlockSpec((1, HEAD_DIM), lambda bi: (0, 0)),
        ],
        out_specs=[tok_spec(NSA_Q_W, 0), win_spec, win_spec],
        out_shape=[jax.ShapeDtypeStruct((b, ts, NSA_Q_W), BF16),
                   jax.ShapeDtypeStruct((b, wb * NSA_G, HEAD_DIM), F32),
                   jax.ShapeDtypeStruct((b, wb * NSA_G, HEAD_DIM), F32)],
        compiler_params=_cparams(("arbitrary",)),
        name="win_sample",
    )(ub, cache_k, cache_v, k_new, ub, bc, bn, ub, ub, ocmp, osel, q_norm_g.reshape(1, HEAD_DIM))


def _mix_body(oa_ref, ob_ref, wa_ref, wb_ref, ma_ref, mb_ref, o_ref):
    pa = jnp.dot(oa_ref[...], wa_ref[...], preferred_element_type=F32)
    pb = jnp.dot(ob_ref[...], wb_ref[...], preferred_element_type=F32)
    o_ref[...] = (jax.nn.sigmoid(ma_ref[...]) * pa + jax.nn.sigmoid(mb_ref[...]) * pb).astype(o_ref.dtype)


def _out_body(m_ref, w_ref, x_ref, y_ref):
    y_ref[...] = x_ref[...] + jnp.dot(m_ref[...], w_ref[...], preferred_element_type=F32)


def _output(x2d, oa, ob, ub2d, wa16, wb16, wo16):
    m = x2d.shape[0]
    tm = min(m, 512)
    tn = 512
    grid = (m // tm, D_MODEL // tn)
    mixed = pl.pallas_call(
        _mix_body,
        grid=grid,
        in_specs=[
            pl.BlockSpec((tm, GDN_V_W), lambda i, j: (i, 0)),
            pl.BlockSpec((tm, NSA_Q_W), lambda i, j: (i, 0)),
            pl.BlockSpec((GDN_V_W, tn), lambda i, j: (0, j)),
            pl.BlockSpec((NSA_Q_W, tn), lambda i, j: (0, j)),
            pl.BlockSpec((tm, tn), lambda i, j: (i, UB_MA // tn + j)),
            pl.BlockSpec((tm, tn), lambda i, j: (i, UB_MB // tn + j)),
        ],
        out_specs=pl.BlockSpec((tm, tn), lambda i, j: (i, j)),
        out_shape=jax.ShapeDtypeStruct((m, D_MODEL), BF16),
        compiler_params=_cparams(("arbitrary", "arbitrary")),
        name="branch_mix",
    )(oa, ob, wa16, wb16, ub2d, ub2d)
    return pl.pallas_call(
        _out_body,
        grid=grid,
        in_specs=[
            pl.BlockSpec((tm, D_MODEL), lambda i, j: (i, 0)),
            pl.BlockSpec((D_MODEL, tn), lambda i, j: (0, j)),
            pl.BlockSpec((tm, tn), lambda i, j: (i, j)),
        ],
        out_specs=pl.BlockSpec((tm, tn), lambda i, j: (i, j)),
        out_shape=jax.ShapeDtypeStruct((m, D_MODEL), F32),
        compiler_params=_cparams(("arbitrary", "arbitrary")),
        name="out_proj",
    )(mixed, wo16, x2d)


def _kv_slices(ub):
    b, t_len, _ = ub.shape
    out = []
    for i in (0, 1, 3, 5):
        a = ub[:, :, UB_KV + i * NSA_KV_W:UB_KV + (i + 1) * NSA_KV_W]
        out.append(a.reshape(b, t_len, NSA_G, HEAD_DIM))
    return out


def kernel(x_prompt, x_sample, cache_k_cmp, cache_v_cmp, cache_k_sel, cache_v_sel, cache_k_win, cache_v_win, state_conv, state_gdn, page_table, norm_g, w_in, gdn_conv_w, gdn_a_log, gdn_dt_bias, gdn_norm_g, q_norm_g, k_norm_g, cmp_pe_k, cmp_w_k, cmp_proj_k, cmp_pe_v, cmp_w_v, cmp_proj_v, rel_bias, w_branch_a, w_branch_b, w_out):
    bp, tp, _ = x_prompt.shape
    bs, ts, _ = x_sample.shape
    wb = cache_k_win.shape[1]
    n_pool = cache_k_cmp.shape[0]
    kv4 = lambda a: a.reshape(a.shape[0], a.shape[1], NSA_G, HEAD_DIM)
    cmp_p = (cmp_pe_k, cmp_w_k, cmp_proj_k, cmp_pe_v, cmp_w_v, cmp_proj_v)

    assert _IN_MB + D_MODEL == w_in.shape[1]
    w_t = w_in.T
    wa16, wb16, wo16 = w_branch_a.astype(BF16), w_branch_b.astype(BF16), w_out.astype(BF16)

    def project(x):
        x2d = x.reshape(-1, D_MODEL)
        ua = _project(x2d, norm_g, w_t, UA_OFFSETS).reshape(x.shape[0], x.shape[1], UA_W)
        ub = _project(x2d, norm_g, w_t, UB_OFFSETS).reshape(x.shape[0], x.shape[1], UB_W)
        return x2d, ua, ub

    x2d, ua, ub = project(x_prompt)
    conv0 = jnp.zeros((bp, GDN_CONV - 1, GDN_QKV_W), F32)
    s0 = jnp.zeros((bp, GDN_V_HEADS, HEAD_DIM, HEAD_DIM), F32)
    o_a, p_gdn = _gdn(ua, ub, conv0, s0, gdn_conv_w, gdn_a_log, gdn_dt_bias, gdn_norm_g)
    p_conv = ua[:, tp - (GDN_CONV - 1):, :GDN_QKV_W]
    n_pages_p = tp // PAGE
    pages = ub.reshape(bp * n_pages_p, PAGE, UB_W)
    ident = jnp.arange(bp * n_pages_p, dtype=jnp.int32).reshape(bp, n_pages_p)
    kcmp, vcmp = _compress(pages, pages, UB_KV // NSA_KV_W, UB_KV // NSA_KV_W + 1, ident, *cmp_p, k_norm_g[0])
    p_kc, p_vc, p_ks, p_vs, p_kw, p_vw, ksb, vsb, kwb, vwb = _nsa_prep_prompt(ub, k_norm_g, wb)
    nc_p = (tp - CMP_LEN) // CMP_STRIDE + 1
    bias_cmp = _cmp_bias(rel_bias, jnp.arange(tp, dtype=jnp.int32), kcmp.shape[1], nc_p)
    ocmp, nsel = _cmp_select_prompt(ub, kcmp, vcmp, bias_cmp, q_norm_g)
    o_b = _attn_prompt(ub, ocmp, nsel, ksb, vsb, kwb, vwb, rel_bias, q_norm_g)
    y_prompt = _output(x2d, o_a.reshape(-1, GDN_V_W), o_b.reshape(-1, NSA_Q_W), ub.reshape(-1, UB_W), wa16, wb16, wo16)
    cache4 = lambda a: a.reshape(bp, a.shape[1] // NSA_G, NSA_G, HEAD_DIM)
    prompt_out = (y_prompt.reshape(x_prompt.shape), cache4(p_kc), cache4(p_vc), cache4(p_ks), cache4(p_vs), cache4(p_kw),
                  cache4(p_vw), p_conv, p_gdn)

    x2d, ua, ub = project(x_sample)
    o_a, s_gdn = _gdn(ua, ub, state_conv, state_gdn, gdn_conv_w, gdn_a_log, gdn_dt_bias, gdn_norm_g)
    s_conv = ua[:, ts - (GDN_CONV - 1):, :GDN_QKV_W]
    pool = lambda c: c.reshape(n_pool, PAGE * NSA_G, HEAD_DIM)
    kcmp, vcmp = _compress(pool(cache_k_cmp), pool(cache_v_cmp), 0, 0, page_table, *cmp_p, k_norm_g[0])
    s_ks, s_kwn = _nsa_prep(ub, k_norm_g)
    total = PAST_LEN + ts
    nc_s = (total - CMP_LEN) // CMP_STRIDE + 1
    nb_s = -(-total // SEL_BLOCK)
    assert nc_s + 1 == kcmp.shape[1]
    pos_s = PAST_LEN + jnp.arange(ts, dtype=jnp.int32)
    bias_cmp = _cmp_bias(rel_bias, pos_s, kcmp.shape[1], nc_s)
    ocmp, selt = _cmp_select_sample(ub, kcmp, vcmp, bias_cmp, q_norm_g, nb_s)
    osel = _sel_sample(ub, selt, pool(cache_k_sel), pool(cache_v_sel), page_table, s_ks, rel_bias, q_norm_g)
    win3 = lambda c: c.reshape(bs, wb * NSA_G, HEAD_DIM)
    o_b, s_kw, s_vw = _win_sample(ub, win3(cache_k_win), win3(cache_v_win), s_kwn, ocmp, osel, rel_bias, q_norm_g)
    y_sample = _output(x2d, o_a.reshape(-1, GDN_V_W), o_b.reshape(-1, NSA_Q_W), ub.reshape(-1, UB_W), wa16, wb16, wo16)
    s_kc, s_vc, s_vs, _ = _kv_slices(ub)
    sample_out = (y_sample.reshape(x_sample.shape), s_kc, s_vc, kv4(s_ks), s_vs, s_kw.reshape(cache_k_win.shape),
                  s_vw.reshape(cache_v_win.shape), s_conv, s_gdn)

    return (prompt_out[0], sample_out[0]) + prompt_out[1:] + sample_out[1:]
```

```python
import functools
import math

import jax
import jax.numpy as jnp
import numpy as np
from jax import lax
from jax.experimental import pallas as pl
from jax.experimental.pallas import tpu as pltpu

F32 = jnp.float32
BF16 = jnp.bfloat16
HI = lax.Precision.HIGHEST

D_MODEL = 2048
PAST_LEN = 16384
PAGE = 128

GDN_QK_HEADS = 16
GDN_V_HEADS = 32
HEAD_DIM = 128
GDN_CONV = 4
GDN_CHUNK = 64
GDN_GROUP = 8
GDN_QK_PER_STEP = 2
GDN_QK_W = GDN_QK_HEADS * HEAD_DIM
GDN_V_W = GDN_V_HEADS * HEAD_DIM
GDN_QKV_W = 2 * GDN_QK_W + GDN_V_W

NSA_HEADS = 16
NSA_G = 4
NSA_HPG = NSA_HEADS // NSA_G
NSA_Q_W = NSA_HEADS * HEAD_DIM
NSA_KV_W = NSA_G * HEAD_DIM
CMP_LEN = 32
CMP_STRIDE = 16
SEL_BLOCK = 64
SEL_TOP = 16
WINDOW = 512
REL_BUCKETS = 32
REL_MAX_DIST = 128

EPS = 1e-6
NEG = -1e30
BIG = 1e9

LANE = 128
VMEM_LIMIT = 56 * 1024 * 1024
PAGES_PER_STEP = 32

UA_W = GDN_QKV_W + GDN_V_W
UB_Q, UB_Z, UB_MA, UB_MB, UB_KV, UB_BA, UB_G = 0, 2048, 4096, 6144, 8192, 11264, 11776
UB_W = 12288
PROJ_TN = 512
PROJ_TM = 2048

_IN_QB = UA_W + 2 * GDN_V_HEADS
_IN_KV = _IN_QB + NSA_Q_W
_IN_G = _IN_KV + 6 * NSA_KV_W
_IN_ZB = _IN_G + 3 * NSA_HEADS
_IN_MA = _IN_ZB + NSA_Q_W
_IN_MB = _IN_MA + D_MODEL


def _tiles(start, width):
    return [start + k * PROJ_TN for k in range(width // PROJ_TN)]


UA_OFFSETS = _tiles(0, UA_W)
UB_OFFSETS = (_tiles(_IN_QB, NSA_Q_W) + _tiles(_IN_ZB, NSA_Q_W) + _tiles(_IN_MA, D_MODEL) + _tiles(_IN_MB, D_MODEL)
              + _tiles(_IN_KV, 6 * NSA_KV_W) + [UA_W, _IN_G])


def _mm(a, b):
    return jnp.dot(a.astype(BF16), b.astype(BF16), preferred_element_type=F32)


def _mm_nt(a, b):
    return lax.dot_general(a.astype(BF16), b.astype(BF16), (((1,), (1,)), ((), ())), preferred_element_type=F32)


def _mm_tn(a, b):
    return lax.dot_general(a.astype(BF16), b.astype(BF16), (((0,), (0,)), ((), ())), preferred_element_type=F32)


def _mm_hi(a, b):
    return jnp.dot(a, b, precision=HI, preferred_element_type=F32)


def _mm_nt_hi(a, b):
    return lax.dot_general(a, b, (((1,), (1,)), ((), ())), precision=HI, preferred_element_type=F32)


def _silu(x):
    return x * jax.nn.sigmoid(x)


def _softplus(x):
    return jnp.maximum(x, 0.0) + jnp.log1p(jnp.exp(-jnp.abs(x)))


def _rms(x, gain):
    return x * lax.rsqrt(jnp.mean(x * x, axis=-1, keepdims=True) + EPS) * gain


def _cparams(sem):
    return pltpu.CompilerParams(dimension_semantics=sem, vmem_limit_bytes=VMEM_LIMIT)


def _proj_body(offs_ref, x_ref, g_ref, w_ref, o_ref, h_ref):
    del offs_ref

    @pl.when(pl.program_id(1) == 0)
    def _():
        h_ref[...] = _rms(x_ref[...], g_ref[...]).astype(BF16)

    o_ref[...] = lax.dot_general(h_ref[...], w_ref[...].astype(BF16), (((1,), (1,)), ((), ())),
                                 preferred_element_type=F32)


def _project(x2d, norm_g, w_t, offsets):
    m = x2d.shape[0]
    tm = min(m, PROJ_TM)
    n_tiles = len(offsets)
    return pl.pallas_call(
        _proj_body,
        grid_spec=pltpu.PrefetchScalarGridSpec(
            num_scalar_prefetch=1,
            grid=(m // tm, n_tiles),
            in_specs=[
                pl.BlockSpec((tm, D_MODEL), lambda i, j, offs: (i, 0), pipeline_mode=pl.Buffered(1)),
                pl.BlockSpec((1, D_MODEL), lambda i, j, offs: (0, 0)),
                pl.BlockSpec((pl.Element(PROJ_TN), pl.Element(D_MODEL)), lambda i, j, offs: (pl.multiple_of(offs[j], 16), 0)),
            ],
            out_specs=pl.BlockSpec((tm, PROJ_TN), lambda i, j, offs: (i, j)),
            scratch_shapes=[pltpu.VMEM((tm, D_MODEL), BF16)]),
        out_shape=jax.ShapeDtypeStruct((m, n_tiles * PROJ_TN), F32),
        compiler_params=_cparams(("arbitrary", "arbitrary")),
        name="proj",
    )(jnp.asarray(offsets, jnp.int32), x2d, norm_g.reshape(1, D_MODEL), w_t)


def _split2(a):
    hi = a.astype(BF16)
    return hi, (a - hi.astype(F32)).astype(BF16)


def _dot16(a, b):
    return jnp.dot(a, b, preferred_element_type=F32)


def _mm_x3(a2, b2):
    return _dot16(a2[0], b2[0]) + (_dot16(a2[0], b2[1]) + _dot16(a2[1], b2[0]))


def _bdot(a, b):
    return lax.dot_general(a, b, (((2,), (1,)), ((0,), (0,))), preferred_element_type=F32)


def _bdot_nt(a, b):
    return lax.dot_general(a, b, (((2,), (2,)), ((0,), (0,))), preferred_element_type=F32)


def _bmm_x3(a2, b2):
    return _bdot(a2[0], b2[0]) + (_bdot(a2[0], b2[1]) + _bdot(a2[1], b2[0]))


INV_BASE = 8


def _unit_lower_inverse_pairs(nmat, row, col, left, c):
    def blockdiag(x2):
        return tuple(jnp.concatenate([jnp.where(left, x, jnp.zeros_like(x)), jnp.where(left, jnp.zeros_like(x), x)], axis=1)
                     for x in x2)

    def same_block(size):
        shift = size.bit_length() - 1
        return (row >> shift) == (col >> shift)

    base = min(INV_BASE, c)
    nd = jnp.where(same_block(base), nmat, 0.0)
    p = jnp.where(row == col, 1.0, 0.0) + nd
    m2 = _split2(nd)
    bd2 = blockdiag(m2)
    span = 2
    while span < base:
        m2 = _split2(_bmm_x3(m2, bd2))
        bd2 = blockdiag(m2)
        p = p + _bmm_x3(_split2(p), bd2)
        span *= 2
    size = base
    while size < c:
        n21 = jnp.where(same_block(2 * size), jnp.where(same_block(size), 0.0, nmat), 0.0)
        p2 = _split2(p)
        t_n = _bmm_x3(p2, blockdiag(_split2(n21)))
        p = p + _bmm_x3(_split2(t_n), blockdiag(p2))
        size *= 2
    return p


def _cumsum_rows(tril16, g):
    h = g.astype(BF16)
    r = g - h.astype(F32)
    m = r.astype(BF16)
    l = (r - m.astype(F32)).astype(BF16)
    return _dot16(tril16, h) + (_dot16(tril16, m) + _dot16(tril16, l))


def _gdn_body(q_ref, k_ref, v_ref, z_ref, ba_ref, cwq_ref, cwk_ref, cwv_ref, csq_ref, csk_ref, csv_ref,
              s0_ref, gp_ref, ng_ref, o_ref, sfin_ref, cq_s, ck_s, cv_s, u_s, w_s, qg_s, kd_s, at_s, gl_s, st_s,
              *, t_len, c, nq):
    nv = 2 * nq
    first_head = nv * pl.program_id(1)
    n_chunks = t_len // c
    for cs, carry in ((csq_ref, cq_s), (csk_ref, ck_s), (csv_ref, cv_s)):
        carry[...] = jnp.zeros(carry.shape, F32)
        carry[5:8, :] = cs[0]
    st_s[...] = s0_ref[0]

    gsz = min(GDN_GROUP, n_chunks)
    rg = gsz * c
    row = lax.broadcasted_iota(jnp.int32, (1, c, 2 * c), 1)
    lane2 = lax.broadcasted_iota(jnp.int32, (1, c, 2 * c), 2)
    left = lane2 < c
    col = lane2 & (c - 1)
    tril = row >= col
    strict = row > col
    eye = row == col
    lane = lax.broadcasted_iota(jnp.int32, (rg, LANE), 1)
    rr = lax.broadcasted_iota(jnp.int32, (rg, rg), 0)
    cc = lax.broadcasted_iota(jnp.int32, (rg, rg), 1)
    shift = c.bit_length() - 1
    assert 1 << shift == c
    tril16 = jnp.where((rr >= cc) & ((rr >> shift) == (cc >> shift)), 1.0, 0.0).astype(BF16)

    def conv(src_ref, carry, cw_ref, rows):
        cur = src_ref[0, rows, :]
        win = jnp.concatenate([carry[...], cur], axis=0)
        carry[...] = cur[rg - 8:rg]
        w = cw_ref[...]
        a = win[5:5 + rg] * w[0:1] + win[6:6 + rg] * w[1:2] + win[7:7 + rg] * w[2:3] + win[8:8 + rg] * w[3:4]
        return _silu(a)

    def pick(x, idx):
        return jnp.sum(jnp.where(lane == idx, x, 0.0), axis=-1, keepdims=True).reshape(gsz, c, 1)

    def per_head(x, n):
        return [x[:, i * HEAD_DIM:(i + 1) * HEAD_DIM].reshape(gsz, c, HEAD_DIM) for i in range(n)]

    def prepare(gi):
        base = gi * rg if isinstance(gi, int) else pl.multiple_of(gi * rg, rg)
        rows = pl.ds(base, rg)
        qa = per_head(conv(q_ref, cq_s, cwq_ref, rows), nq)
        ka = per_head(conv(k_ref, ck_s, cwk_ref, rows), nq)
        va = per_head(conv(v_ref, cv_s, cwv_ref, rows), nv)
        qn = [x * lax.rsqrt(jnp.sum(x * x, axis=-1, keepdims=True) + EPS) * (HEAD_DIM ** -0.5) for x in qa]
        kn = [x * lax.rsqrt(jnp.sum(x * x, axis=-1, keepdims=True) + EPS) for x in ka]
        ba = ba_ref[0, rows, :]
        beta_all = jax.nn.sigmoid(ba)
        g_all = -jnp.exp(gp_ref[0:1, :]) * _softplus(ba + gp_ref[1:2, :])
        gc_all = _cumsum_rows(tril16, g_all)
        k16 = jnp.concatenate(kn, axis=0).astype(BF16)
        k2x = jnp.concatenate([k16, k16], axis=1)
        kk = _bdot_nt(k16, k2x)
        qk = _bdot_nt(jnp.concatenate(qn, axis=0).astype(BF16), k2x)
        betas, gcols, blocks = [], [], []
        zeros = jnp.zeros((gsz, c, 2 * HEAD_DIM), F32)
        for qi in range(nq):
            pair = []
            for l in range(2):
                hh = 2 * qi + l
                beta = pick(beta_all, first_head + hh)
                gcol = pick(gc_all, GDN_V_HEADS + first_head + hh)
                egc = jnp.exp(gcol)
                glast = gcol[:, c - 1:c, :]
                rhs = jnp.concatenate([va[hh] * beta, kn[qi] * (beta * egc)], axis=2)
                pair.append((beta, gcol, rhs))
                qg_s[hh, rows, :] = (qn[qi] * egc).reshape(rg, HEAD_DIM).astype(qg_s.dtype)
                kd_s[hh, rows, :] = (kn[qi] * jnp.exp(glast - gcol)).reshape(rg, HEAD_DIM).astype(kd_s.dtype)
                egl = jnp.exp(glast)
                for j in range(gsz):
                    gl_s[hh, gi * gsz + j] = jnp.broadcast_to(egl[j], (8, HEAD_DIM))
            betas.append(jnp.where(left, pair[0][0], pair[1][0]))
            gcols.append(jnp.where(left, pair[0][1], pair[1][1]))
            blocks.append(jnp.concatenate([jnp.concatenate([pair[0][2], zeros], axis=2),
                                           jnp.concatenate([zeros, pair[1][2]], axis=2)], axis=1))
        beta = jnp.concatenate(betas, axis=0)
        gcol = jnp.concatenate(gcols, axis=0)
        grow = jnp.sum(jnp.where(eye, gcol, 0.0), axis=1, keepdims=True)
        gamma = jnp.where(tril, jnp.exp(jnp.minimum(gcol - grow, 0.0)), 0.0)
        tinv = _unit_lower_inverse_pairs(jnp.where(strict, -(kk * beta * gamma), 0.0), row, col, left, c)
        rhs2 = _split2(jnp.concatenate(blocks, axis=0))
        uw = _bmm_x3(_split2(tinv), rhs2)
        attn = qk * gamma
        for qi in range(nq):
            at_s[qi, rows, :] = attn[qi * gsz:(qi + 1) * gsz].reshape(rg, 2 * c).astype(at_s.dtype)
            for l in range(2):
                uw_h = uw[qi * gsz:(qi + 1) * gsz, :, 2 * l * HEAD_DIM:2 * (l + 1) * HEAD_DIM]
                u_s[2 * qi + l, rows, :] = uw_h[:, :, :HEAD_DIM].reshape(rg, HEAD_DIM)
                w_s[2 * qi + l, rows, :] = uw_h[:, :, HEAD_DIM:].reshape(rg, HEAD_DIM).astype(w_s.dtype)

    def recur(ci):
        rows = pl.ds(ci * c if isinstance(ci, int) else pl.multiple_of(ci * c, c), c)
        s = st_s[...]
        s16 = s.astype(BF16)
        v_new = u_s[:, rows, :] - _bdot(w_s[:, rows, :].astype(BF16), s16)
        v16 = v_new.astype(BF16)
        zv = jnp.zeros((c, HEAD_DIM), BF16)
        vbd = jnp.stack([jnp.concatenate([jnp.concatenate([v16[2 * qi], zv], axis=1),
                                          jnp.concatenate([zv, v16[2 * qi + 1]], axis=1)], axis=0) for qi in range(nq)], axis=0)
        o_state = _bdot(qg_s[:, rows, :].astype(BF16), s16)
        o_attn = _bdot(at_s[:, rows, :].astype(BF16), vbd)
        kd16 = kd_s[:, rows, :].astype(BF16)
        upd = jnp.stack([_mm_tn(kd16[hh], v16[hh]) for hh in range(nv)], axis=0)
        st_s[...] = s * gl_s[:, ci][:, 0:1, :] + upd
        for hh in range(nv):
            hsl = slice(hh * HEAD_DIM, (hh + 1) * HEAD_DIM)
            o = o_state[hh] + o_attn[hh // 2][:, (hh % 2) * HEAD_DIM:(hh % 2 + 1) * HEAD_DIM]
            o_ref[0, rows, hsl] = (_rms(o, ng_ref[...]) * _silu(z_ref[0, rows, hsl])).astype(o_ref.dtype)

    if n_chunks == 1:
        prepare(0)
        recur(0)
    else:
        def prepare_step(gi, carry):
            prepare(gi)
            return carry

        def recur_step(ci, carry):
            recur(ci)
            return carry

        lax.fori_loop(0, n_chunks // gsz, prepare_step, 0)
        lax.fori_loop(0, n_chunks, recur_step, 0)
    sfin_ref[0] = st_s[...]


def _gdn(ua, ub, conv_state, s0, conv_w, a_log, dt_bias, norm_g):
    b, t_len, _ = ua.shape
    c = min(GDN_CHUNK, t_len)
    hd = HEAD_DIM
    nq = GDN_QK_PER_STEP if t_len > GDN_CHUNK else 4 * GDN_QK_PER_STEP
    nv = 2 * nq
    steps = GDN_QK_HEADS // nq
    qw, vw = nq * hd, nv * hd
    op_dtype = BF16 if c % 16 == 0 else F32
    gp = jnp.zeros((2, LANE), F32)
    gp = gp.at[0, GDN_V_HEADS:2 * GDN_V_HEADS].set(a_log).at[1, GDN_V_HEADS:2 * GDN_V_HEADS].set(dt_bias)
    body = functools.partial(_gdn_body, t_len=t_len, c=c, nq=nq)
    return pl.pallas_call(
        body,
        grid=(b, steps),
        in_specs=[
            pl.BlockSpec((1, t_len, qw), lambda bi, i: (bi, 0, i)),
            pl.BlockSpec((1, t_len, qw), lambda bi, i: (bi, 0, steps + i)),
            pl.BlockSpec((1, t_len, vw), lambda bi, i: (bi, 0, steps + i)),
            pl.BlockSpec((1, t_len, vw), lambda bi, i: (bi, 0, 2 * steps + i)),
            pl.BlockSpec((1, t_len, LANE), lambda bi, i: (bi, 0, UB_BA // LANE)),
            pl.BlockSpec((GDN_CONV, qw), lambda bi, i: (0, i)),
            pl.BlockSpec((GDN_CONV, qw), lambda bi, i: (0, steps + i)),
            pl.BlockSpec((GDN_CONV, vw), lambda bi, i: (0, steps + i)),
            pl.BlockSpec((1, GDN_CONV - 1, qw), lambda bi, i: (bi, 0, i)),
            pl.BlockSpec((1, GDN_CONV - 1, qw), lambda bi, i: (bi, 0, steps + i)),
            pl.BlockSpec((1, GDN_CONV - 1, vw), lambda bi, i: (bi, 0, steps + i)),
            pl.BlockSpec((1, nv, hd, hd), lambda bi, i: (bi, i, 0, 0)),
            pl.BlockSpec((2, LANE), lambda bi, i: (0, 0)),
            pl.BlockSpec((1, hd), lambda bi, i: (0, 0)),
        ],
        out_specs=[
            pl.BlockSpec((1, t_len, vw), lambda bi, i: (bi, 0, i)),
            pl.BlockSpec((1, nv, hd, hd), lambda bi, i: (bi, i, 0, 0)),
        ],
        out_shape=[
            jax.ShapeDtypeStruct((b, t_len, GDN_V_W), BF16),
            jax.ShapeDtypeStruct((b, GDN_V_HEADS, hd, hd), F32),
        ],
        scratch_shapes=[
            pltpu.VMEM((8, qw), F32),
            pltpu.VMEM((8, qw), F32),
            pltpu.VMEM((8, vw), F32),
            pltpu.VMEM((nv, t_len, hd), F32),
            pltpu.VMEM((nv, t_len, hd), op_dtype),
            pltpu.VMEM((nv, t_len, hd), op_dtype),
            pltpu.VMEM((nv, t_len, hd), op_dtype),
            pltpu.VMEM((nq, t_len, 2 * c), op_dtype),
            pltpu.VMEM((nv, t_len // c, 8, hd), F32),
            pltpu.VMEM((nv, hd, hd), F32),
        ],
        compiler_params=_cparams(("arbitrary", "arbitrary")),
        name="gdn",
    )(ua, ua, ua, ua, ub, conv_w, conv_w, conv_w, conv_state, conv_state, conv_state, s0, gp, norm_g.reshape(1, hd))


def _t5_bucket(rel):
    n = jnp.maximum(rel, 0)
    exact = REL_BUCKETS // 2
    nf = jnp.maximum(n, 1).astype(F32)
    large = exact + (jnp.log(nf / exact) / math.log(REL_MAX_DIST / exact) * (REL_BUCKETS - exact)).astype(jnp.int32)
    return jnp.where(n < exact, n, jnp.minimum(large, REL_BUCKETS - 1))


def _bias_of(rel_bias, rel):
    bucket = _t5_bucket(rel)[None]
    table = rel_bias.astype(F32)
    out = jnp.zeros((NSA_HEADS,) + rel.shape, F32)
    for k in range(REL_BUCKETS):
        out = jnp.where(bucket == k, table[k].reshape((NSA_HEADS,) + (1,) * rel.ndim), out)
    return out


def _cmp_bias(rel_bias, q_pos, ncp, nc):
    n = jnp.arange(ncp, dtype=jnp.int32)
    rel = q_pos[:, None] - (n * CMP_STRIDE + (CMP_LEN - 1))[None, :]
    ok = (rel >= 0) & (n < nc)[None, :]
    return jnp.where(ok[None], _bias_of(rel_bias, rel), NEG)


def _load_page(ref, interleaved):
    if not interleaved:
        return ref[0]
    n_rows = ref.shape[1] // NSA_G
    return jnp.concatenate([ref[0, pl.ds(g, n_rows, stride=NSA_G), :] for g in range(NSA_G)], axis=1)


def _compress_body(pt_ref, *refs, pps, n_steps, nch, interleaved):
    del pt_ref
    k_refs = refs[:pps]
    v_refs = refs[pps:2 * pps]
    (wabk_ref, wabv_ref, wk_ref, pek_ref, wv_ref, pev_ref, projk_ref, projv_ref, gain_ref,
     outk_ref, outv_ref, ak_s, bk_s, av_s, bv_s) = refs[2 * pps:]
    s = pl.program_id(1)

    @pl.when(s == 0)
    def _():
        bk_s[nch:nch + 8, :] = jnp.zeros((8, NSA_KV_W), F32)
        bv_s[nch:nch + 8, :] = jnp.zeros((8, NSA_KV_W), F32)

    def pool(w2, page):
        w_hi, w_lo = w2
        p_hi, p_lo = _split2(page)
        r = _dot16(jnp.concatenate([w_hi, w_lo], axis=0), p_hi)
        return r[0:16] + (r[16:32] + _dot16(w_hi, p_lo))

    wk2 = _split2(wabk_ref[...])
    wv2 = _split2(wabv_ref[...])
    for j in range(pps):
        row0 = pl.multiple_of((s * pps + j) * 8, 8)
        abk = pool(wk2, _load_page(k_refs[j], interleaved))
        ak_s[pl.ds(row0, 8), :] = abk[0:8]
        bk_s[pl.ds(row0, 8), :] = abk[8:16]
        abv = pool(wv2, _load_page(v_refs[j], interleaved))
        av_s[pl.ds(row0, 8), :] = abv[0:8]
        bv_s[pl.ds(row0, 8), :] = abv[8:16]

    @pl.when(s == n_steps - 1)
    def _():
        cpe_k = jnp.sum(wk_ref[...] * pek_ref[...], axis=0, keepdims=True)
        cpe_v = jnp.sum(wv_ref[...] * pev_ref[...], axis=0, keepdims=True)
        projk2 = _split2(projk_ref[...])
        projv2 = _split2(projv_ref[...])
        rb = min(nch, 128)

        def fin(r, carry):
            r0 = pl.multiple_of(r * rb, rb)
            pk = ak_s[pl.ds(r0, rb), :] + bk_s[pl.ds(r0, rb + 8), :][1:rb + 1]
            pv = av_s[pl.ds(r0, rb), :] + bv_s[pl.ds(r0, rb + 8), :][1:rb + 1]
            for g in range(NSA_G):
                sl = slice(g * HEAD_DIM, (g + 1) * HEAD_DIM)
                yk = _mm_x3(_split2(pk[:, sl] + cpe_k), projk2)
                outk_ref[0, pl.ds(r0, rb), sl] = _rms(yk, gain_ref[...])
                outv_ref[0, pl.ds(r0, rb), sl] = _mm_x3(_split2(pv[:, sl] + cpe_v), projv2)
            return carry

        lax.fori_loop(0, nch // rb, fin, 0)


def _pool_weights(w):
    c = np.arange(8)[:, None]
    t = np.arange(PAGE)[None, :]
    off = t - CMP_STRIDE * c
    inside = (off >= 0) & (off < CMP_STRIDE)
    idx = np.clip(off, 0, CMP_STRIDE - 1)
    wa = jnp.where(inside, w[idx], 0.0)
    wb = jnp.where(inside, w[idx + CMP_STRIDE], 0.0)
    return jnp.concatenate([wa, wb], axis=0).astype(F32)


def _compress(pages_k, pages_v, col_k, col_v, table, pe_k, w_k, proj_k, pe_v, w_v, proj_v, k_gain):
    b, n_pages = table.shape
    pps = min(PAGES_PER_STEP, n_pages)
    n_steps = n_pages // pps
    nch = 8 * n_pages
    interleaved = pages_k.shape[1] == NSA_G * PAGE
    body = functools.partial(_compress_body, pps=pps, n_steps=n_steps, nch=nch, interleaved=interleaved)

    def page_spec(j, col):
        if interleaved:
            return pl.BlockSpec((1, NSA_G * PAGE, HEAD_DIM), lambda bi, s, pt: (pt[bi, s * pps + j], 0, 0))
        return pl.BlockSpec((1, PAGE, NSA_KV_W), lambda bi, s, pt: (pt[bi, s * pps + j], 0, col))

    full = lambda shape: pl.BlockSpec(shape, lambda bi, s, pt: (0,) * len(shape))
    in_specs = [page_spec(j, col_k) for j in range(pps)] + [page_spec(j, col_v) for j in range(pps)]
    in_specs += [full((16, PAGE)), full((16, PAGE)), full((CMP_LEN, 1)), full((CMP_LEN, HEAD_DIM)),
                 full((CMP_LEN, 1)), full((CMP_LEN, HEAD_DIM)), full((HEAD_DIM, HEAD_DIM)), full((HEAD_DIM, HEAD_DIM)),
                 full((1, HEAD_DIM))]
    out_spec = pl.BlockSpec((1, nch, NSA_KV_W), lambda bi, s, pt: (bi, 0, 0))
    return pl.pallas_call(
        body,
        grid_spec=pltpu.PrefetchScalarGridSpec(
            num_scalar_prefetch=1, grid=(b, n_steps), in_specs=in_specs, out_specs=[out_spec, out_spec],
            scratch_shapes=[pltpu.VMEM((nch, NSA_KV_W), F32), pltpu.VMEM((nch + 8, NSA_KV_W), F32),
                            pltpu.VMEM((nch, NSA_KV_W), F32), pltpu.VMEM((nch + 8, NSA_KV_W), F32)]),
        out_shape=[jax.ShapeDtypeStruct((b, nch, NSA_KV_W), F32)] * 2,
        compiler_params=_cparams(("arbitrary", "arbitrary")),
        name="compress",
    )(table, *([pages_k] * pps), *([pages_v] * pps), _pool_weights(w_k), _pool_weights(w_v),
      w_k.reshape(CMP_LEN, 1), pe_k, w_v.reshape(CMP_LEN, 1), pe_v, proj_k, proj_v, k_gain.reshape(1, HEAD_DIM))


def _prep_body(ks_ref, kw_ref, kg_ref, pks_ref, pkw_ref):
    for g in range(NSA_G):
        sl = slice(g * HEAD_DIM, (g + 1) * HEAD_DIM)
        pks_ref[0, :, sl] = _rms(ks_ref[0, :, sl], kg_ref[1:2, :])
        pkw_ref[0, :, sl] = _rms(kw_ref[0, :, sl], kg_ref[2:3, :])


def _prep_prompt_body(kc_ref, vc_ref, ks_ref, vs_ref, kw_ref, vw_ref, kg_ref,
                      okc_ref, ovc_ref, oks_ref, ovs_ref, okw_ref, ovw_ref, ks16_ref, vst_ref, kw16_ref, vwt_ref):
    tp = ks_ref.shape[1]

    def put(dst, g, x):
        dst[0, pl.ds(g, tp, stride=NSA_G), :] = x

    for g in range(NSA_G):
        sl = slice(g * HEAD_DIM, (g + 1) * HEAD_DIM)
        ksn = _rms(ks_ref[0, :, sl], kg_ref[1:2, :])
        kwn = _rms(kw_ref[0, :, sl], kg_ref[2:3, :])
        ks16_ref[0, :, sl] = ksn.astype(BF16)
        kw16_ref[0, :, sl] = kwn.astype(BF16)
        put(oks_ref, g, ksn)
        put(okw_ref, g, kwn)
        for src, dst in ((kc_ref, okc_ref), (vc_ref, ovc_ref), (vs_ref, ovs_ref), (vw_ref, ovw_ref)):
            put(dst, g, src[0, :, sl])
    for src, dst in ((vs_ref, vst_ref), (vw_ref, vwt_ref)):
        for j in range(dst.shape[1]):
            dst[0, j] = src[0, j * LANE:(j + 1) * LANE, :].T.astype(BF16)


def _nsa_prep_prompt(ub, k_norm_g, wb):
    b, t_len, _ = ub.shape
    tp = min(t_len, 512)
    assert wb % tp == 0 and t_len % tp == 0 and t_len >= wb
    first_win = (t_len - wb) // tp
    kv0 = UB_KV // NSA_KV_W
    spec = lambda c: pl.BlockSpec((1, tp, NSA_KV_W), lambda bi, i: (bi, i, c))
    o_spec = pl.BlockSpec((1, tp, NSA_KV_W), lambda bi, i: (bi, i, 0))
    il_spec = pl.BlockSpec((1, NSA_G * tp, HEAD_DIM), lambda bi, i: (bi, i, 0))
    win_spec = pl.BlockSpec((1, NSA_G * tp, HEAD_DIM), lambda bi, i: (bi, jnp.maximum(i - first_win, 0), 0))
    t_spec = pl.BlockSpec((1, tp // LANE, NSA_KV_W, LANE), lambda bi, i: (bi, i, 0, 0))
    il = jax.ShapeDtypeStruct((b, NSA_G * t_len, HEAD_DIM), F32)
    il_win = jax.ShapeDtypeStruct((b, NSA_G * wb, HEAD_DIM), F32)
    k16 = jax.ShapeDtypeStruct((b, t_len, NSA_KV_W), BF16)
    v16 = jax.ShapeDtypeStruct((b, t_len // LANE, NSA_KV_W, LANE), BF16)
    return pl.pallas_call(
        _prep_prompt_body,
        grid=(b, t_len // tp),
        in_specs=[spec(kv0 + c) for c in range(6)] + [pl.BlockSpec((3, HEAD_DIM), lambda bi, i: (0, 0))],
        out_specs=[il_spec] * 4 + [win_spec] * 2 + [o_spec, t_spec, o_spec, t_spec],
        out_shape=[il] * 4 + [il_win] * 2 + [k16, v16, k16, v16],
        compiler_params=_cparams(("arbitrary", "arbitrary")),
        name="nsa_prep_prompt",
    )(ub, ub, ub, ub, ub, ub, k_norm_g)


def _nsa_prep(ub, k_norm_g):
    b, t_len, _ = ub.shape
    tp = min(t_len, 512)
    kv0 = UB_KV // NSA_KV_W
    spec = lambda c: pl.BlockSpec((1, tp, NSA_KV_W), lambda bi, i: (bi, i, c))
    o_spec = pl.BlockSpec((1, tp, NSA_KV_W), lambda bi, i: (bi, i, 0))
    return pl.pallas_call(
        _prep_body,
        grid=(b, t_len // tp),
        in_specs=[spec(kv0 + 2), spec(kv0 + 4), pl.BlockSpec((3, HEAD_DIM), lambda bi, i: (0, 0))],
        out_specs=[o_spec] * 2,
        out_shape=[jax.ShapeDtypeStruct((b, t_len, NSA_KV_W), F32)] * 2,
        compiler_params=_cparams(("arbitrary", "arbitrary")),
        name="nsa_prep",
    )(ub, ub, k_norm_g)


def _sel_matrix(nc, nb, ncp, nbp):
    j = np.arange(nb)
    lo = np.clip((SEL_BLOCK * j - CMP_LEN) // CMP_STRIDE + 1, 0, nc)
    hi = np.clip(-(-(SEL_BLOCK * (j + 1)) // CMP_STRIDE), 0, nc)
    n = np.arange(ncp)[:, None]
    m = np.zeros((ncp, nbp), np.float32)
    m[:, :nb] = (n >= lo[None, :]) & (n < hi[None, :])
    return jnp.asarray(m)


def _cmp_body(q_ref, kc_ref, vc_ref, bias_ref, mselt_ref, qg_ref, gate_ref, ocmpt_ref, nsel_ref, *, tq, nb):
    qi = pl.program_id(1)
    nbp = -(-nb // 8) * 8
    gates_t = jax.nn.sigmoid(gate_ref[0]).T
    blk = lax.broadcasted_iota(jnp.int32, (nbp, tq), 0)
    qblk = (qi * tq + lax.broadcasted_iota(jnp.int32, (nbp, tq), 1)) >> 6
    scale = HEAD_DIM ** -0.5
    for g in range(NSA_G):
        gsl = slice(g * HEAD_DIM, (g + 1) * HEAD_DIM)
        kc = kc_ref[0, :, gsl]
        vct = vc_ref[0, :, gsl].T.astype(BF16)
        imp = jnp.zeros((kc.shape[0], tq), F32)
        heads = [g * NSA_HPG + hh for hh in range(NSA_HPG)]
        kc2 = _split2(kc)
        scores = []
        for h in heads:
            q2 = _split2(_rms(q_ref[0, :, h * HEAD_DIM:(h + 1) * HEAD_DIM], qg_ref[...]) * scale)
            nt = lambda a, b: lax.dot_general(a, b, (((1,), (1,)), ((), ())), preferred_element_type=F32)
            scores.append(nt(kc2[0], q2[0]) + (nt(kc2[0], q2[1]) + nt(kc2[1], q2[0])))
        probs = []
        for h, s in zip(heads, scores):
            bias = bias_ref[h]
            s = bias + s
            e = jnp.exp(s - jnp.max(s, axis=0, keepdims=True))
            p = e * (1.0 / jnp.sum(e, axis=0, keepdims=True)) * jnp.where(bias > 0.5 * NEG, 1.0, 0.0)
            probs.append(p)
            imp = imp + p
        for h, p in zip(heads, probs):
            hsl = slice(h * HEAD_DIM, (h + 1) * HEAD_DIM)
            ocmpt_ref[0, hsl, :] = gates_t[h:h + 1, :] * _dot16(vct, p.astype(BF16))
        score = _cumsum_rows(mselt_ref[0:nbp, :].astype(BF16), imp)
        score = jnp.where(blk == 0, BIG, score)
        score = jnp.where(blk == qblk, BIG, score)
        score = jnp.where(blk == qblk - 1, BIG, score)
        score = jnp.where(blk <= qblk, score, -BIG)
        rank = jnp.zeros((nbp, tq), F32)
        for i in range(nb):
            si = score[i:i + 1, :]
            ge = jnp.where(si >= score, 1.0, 0.0)
            gt = jnp.where(si > score, 1.0, 0.0)
            rank = rank + jnp.where(blk > i, ge, gt)
        nsel_t = jnp.where(rank < SEL_TOP, 0.0, jnp.where(blk < nb, 1.0, 0.0))
        nsel = jnp.concatenate([nsel_t, jnp.zeros((LANE - nbp, tq), F32)], axis=0).T
        nsel_ref[0, :, gsl] = nsel.astype(BF16)


def _cmp_select_prompt(ub, kcmp, vcmp, bias_cmp, q_norm_g):
    b, t_len, _ = ub.shape
    tq = 256
    nb = t_len // SEL_BLOCK
    ncp = kcmp.shape[1]
    nc = (t_len - CMP_LEN) // CMP_STRIDE + 1
    assert nb <= LANE and ncp == LANE
    body = functools.partial(_cmp_body, tq=tq, nb=nb)
    return pl.pallas_call(
        body,
        grid=(b, t_len // tq),
        in_specs=[
            pl.BlockSpec((1, tq, NSA_Q_W), lambda bi, i: (bi, i, UB_Q // NSA_Q_W)),
            pl.BlockSpec((1, ncp, NSA_KV_W), lambda bi, i: (bi, 0, 0)),
            pl.BlockSpec((1, ncp, NSA_KV_W), lambda bi, i: (bi, 0, 0)),
            pl.BlockSpec((NSA_HEADS, ncp, tq), lambda bi, i: (0, 0, i)),
            pl.BlockSpec((LANE, ncp), lambda bi, i: (0, 0)),
            pl.BlockSpec((1, HEAD_DIM), lambda bi, i: (0, 0)),
            pl.BlockSpec((1, tq, LANE), lambda bi, i: (bi, i, UB_G // LANE)),
        ],
        out_specs=[
            pl.BlockSpec((1, NSA_Q_W, tq), lambda bi, i: (bi, 0, i)),
            pl.BlockSpec((1, tq, NSA_G * LANE), lambda bi, i: (bi, i, 0)),
        ],
        out_shape=[jax.ShapeDtypeStruct((b, NSA_Q_W, t_len), F32),
                   jax.ShapeDtypeStruct((b, t_len, NSA_G * LANE), BF16)],
        compiler_params=_cparams(("arbitrary", "arbitrary")),
        name="cmp_select",
    )(ub, kcmp, vcmp, jnp.swapaxes(bias_cmp, 1, 2), _sel_matrix(nc, nb, ncp, LANE).T, q_norm_g.reshape(1, HEAD_DIM), ub)


def _attn_body(q_ref, ocmp_ref, nsel_ref, ks_ref, vst_ref, kw_ref, vwt_ref, epen_ref, bt_ref, cb_ref, gate_ref, z_ref, qg_ref,
               o_ref, m_s, l_s, acc_s, osw_s, *, tq):
    qi = pl.program_id(1)
    t0 = qi * tq
    rows = NSA_HPG * tq
    tk = LANE
    def rel_of(kb, width):
        tok = t0 + (lax.broadcasted_iota(jnp.int32, (width * tk, rows), 1) & (tq - 1))
        return tok - (kb + lax.broadcasted_iota(jnp.int32, (width * tk, rows), 0))

    gates_t = jax.nn.sigmoid(gate_ref[0]).T
    scale = HEAD_DIM ** -0.5
    n_near = bt_ref.shape[0]
    j_near = jnp.maximum(qi + 1 - n_near, 0)

    groups = range(NSA_G)
    gsl = [slice(g * HEAD_DIM, (g + 1) * HEAD_DIM) for g in groups]
    heads = [[g * NSA_HPG + hh for hh in range(NSA_HPG)] for g in groups]
    q_sel, q_win = [], []
    for g in groups:
        q4 = jnp.concatenate(
            [(_rms(q_ref[0, :, h * HEAD_DIM:(h + 1) * HEAD_DIM], qg_ref[...]) * scale).astype(BF16) for h in heads[g]], axis=0)
        ns = nsel_ref[0, :, g * LANE:(g + 1) * LANE].astype(F32)
        aug_sel = jnp.concatenate([ns + cb_ref[h:h + 1, :] for h in heads[g]], axis=0).astype(BF16)
        aug_win = jnp.concatenate([jnp.broadcast_to(cb_ref[h:h + 1, :], (tq, LANE)) for h in heads[g]], axis=0).astype(BF16)
        q_sel.append(jnp.concatenate([q4, aug_sel], axis=1))
        q_win.append(jnp.concatenate([q4, aug_win], axis=1))

    def run(k_ref, vt_ref, qmats, lo, window, far_widths):
        m_s[...] = jnp.full(m_s.shape, NEG, F32)
        l_s[...] = jnp.zeros(l_s.shape, F32)
        acc_s[...] = jnp.zeros(acc_s.shape, F32)

        def step(j, near, width=1):
            kb = pl.multiple_of(j * tk, tk)
            kt = k_ref[0, pl.ds(kb, width * tk), :]
            ep = epen_ref[pl.ds(kb, width * tk), :]
            ss = [lax.dot_general(jnp.concatenate([kt[:, gsl[g]], ep], axis=1), qmats[g], (((1,), (1,)), ((), ())),
                                  preferred_element_type=F32) for g in groups]
            rel = rel_of(kb, width)
            vt = vt_ref[0, j] if width == 1 else jnp.concatenate([vt_ref[0, j + i] for i in range(width)], axis=1)
            for g in groups:
                s = ss[g]
                if near:
                    bias = [bt_ref[width - 1 - i, :, g * rows:(g + 1) * rows] for i in range(width)]
                    s = jnp.where(rel >= 0, (bias[0] if width == 1 else jnp.concatenate(bias, axis=0)) + s, NEG)
                elif window:
                    s = jnp.where(rel < WINDOW, s, NEG)
                m_old = m_s[g]
                m_new = jnp.maximum(m_old, jnp.max(s, axis=0, keepdims=True))
                alpha = jnp.exp(m_old - m_new)
                p = jnp.exp(s - m_new)
                l_s[g] = alpha * l_s[g] + jnp.sum(p, axis=0, keepdims=True)
                acc_s[g] = alpha * acc_s[g] + jnp.dot(vt[gsl[g], :], p.astype(BF16), preferred_element_type=F32)
                m_s[g] = m_new

        start = lo
        for width in far_widths:
            def far_step(i, carry, start=start, width=width):
                step(start + width * i, False, width)
                return carry

            n = jnp.maximum(j_near - start, 0) // width
            lax.fori_loop(0, n, far_step, 0)
            start = start + width * n

        @pl.when(qi >= n_near - 1)
        def _():
            step(qi + 1 - n_near, True, n_near)

        for short in range(1, n_near):
            @pl.when(qi == short - 1)
            def _(short=short):
                step(0, True, short)

    def gate_rows(branch, g):
        return jnp.concatenate([gates_t[branch * NSA_HEADS + h:branch * NSA_HEADS + h + 1, :] for h in heads[g]], axis=1)

    run(ks_ref, vst_ref, q_sel, 0, False, (4, 2, 1))
    for g in groups:
        osw_s[g] = acc_s[g] * (gate_rows(1, g) / l_s[g])
    run(kw_ref, vwt_ref, q_win, jnp.maximum(qi - WINDOW // tk, 0), True, (2, 1))
    for g in groups:
        osw = osw_s[g] + acc_s[g] * (gate_rows(2, g) / l_s[g])
        for hh, h in enumerate(heads[g]):
            hsl = slice(h * HEAD_DIM, (h + 1) * HEAD_DIM)
            o = (ocmp_ref[0, hsl, :] + osw[:, hh * tq:(hh + 1) * tq]).T
            o_ref[0, :, hsl] = (o * _silu(z_ref[0, :, hsl])).astype(o_ref.dtype)


def _far_bias(rel_bias):
    return _bias_of(rel_bias, jnp.full((1,), REL_MAX_DIST, jnp.int32))[:, 0]


def _toeplitz_bias_t(rel_bias, tq, n_near):
    d = jnp.arange(n_near, dtype=jnp.int32)[:, None, None]
    c = jnp.arange(LANE, dtype=jnp.int32)[None, :, None]
    r = jnp.arange(tq, dtype=jnp.int32)[None, None, :]
    bt = _bias_of(rel_bias, d * LANE + r - c) - _far_bias(rel_bias)[:, None, None, None]
    return jnp.transpose(bt, (1, 2, 0, 3)).reshape(n_near, LANE, NSA_HEADS * tq)


AUG_BIAS_LANES = (LANE - 2, LANE - 1)


def _far_bias_columns(rel_bias):
    c = _far_bias(rel_bias)
    hi = c.astype(BF16).astype(F32)
    lo = (c - hi).astype(BF16).astype(F32)
    return jnp.zeros((NSA_HEADS, LANE), F32).at[:, AUG_BIAS_LANES[0]].set(hi).at[:, AUG_BIAS_LANES[1]].set(lo)


def _block_penalty(t_len):
    key = np.arange(t_len)[:, None]
    j = np.arange(LANE)[None, :]
    pen = np.where(key // SEL_BLOCK == j, NEG, 0.0)
    pen[:, AUG_BIAS_LANES[0]:] = 1.0
    return jnp.asarray(pen, BF16)


def _attn_prompt(ub, ocmp, nsel, ksb, vst, kwb, vwt, rel_bias, q_norm_g):
    b, t_len, _ = ub.shape
    tq = 128
    rows = NSA_HPG * tq
    assert t_len // SEL_BLOCK <= AUG_BIAS_LANES[0]
    n_near = -(-(REL_MAX_DIST + LANE - 1) // LANE)
    bt = _toeplitz_bias_t(rel_bias, tq, n_near)
    body = functools.partial(_attn_body, tq=tq)
    k_spec = pl.BlockSpec((1, t_len, NSA_KV_W), lambda bi, i: (bi, 0, 0))
    v_spec = pl.BlockSpec((1, t_len // LANE, NSA_KV_W, LANE), lambda bi, i: (bi, 0, 0, 0))
    return pl.pallas_call(
        body,
        grid=(b, t_len // tq),
        in_specs=[
            pl.BlockSpec((1, tq, NSA_Q_W), lambda bi, i: (bi, i, UB_Q // NSA_Q_W)),
            pl.BlockSpec((1, NSA_Q_W, tq), lambda bi, i: (bi, 0, i)),
            pl.BlockSpec((1, tq, NSA_G * LANE), lambda bi, i: (bi, i, 0)),
            k_spec, v_spec, k_spec, v_spec,
            pl.BlockSpec((t_len, LANE), lambda bi, i: (0, 0)),
            pl.BlockSpec(bt.shape, lambda bi, i: (0, 0, 0)),
            pl.BlockSpec((NSA_HEADS, LANE), lambda bi, i: (0, 0)),
            pl.BlockSpec((1, tq, LANE), lambda bi, i: (bi, i, UB_G // LANE)),
            pl.BlockSpec((1, tq, NSA_Q_W), lambda bi, i: (bi, i, UB_Z // NSA_Q_W)),
            pl.BlockSpec((1, HEAD_DIM), lambda bi, i: (0, 0)),
        ],
        out_specs=pl.BlockSpec((1, tq, NSA_Q_W), lambda bi, i: (bi, i, 0)),
        out_shape=jax.ShapeDtypeStruct((b, t_len, NSA_Q_W), BF16),
        scratch_shapes=[pltpu.VMEM((NSA_G, 1, rows), F32), pltpu.VMEM((NSA_G, 1, rows), F32),
                        pltpu.VMEM((NSA_G, HEAD_DIM, rows), F32), pltpu.VMEM((NSA_G, HEAD_DIM, rows), F32)],
        compiler_params=_cparams(("arbitrary", "arbitrary")),
        name="nsa_attn",
    )(ub, ocmp, nsel, ksb, vst, kwb, vwt, _block_penalty(t_len), bt, _far_bias_columns(rel_bias), ub, ub,
      q_norm_g.reshape(1, HEAD_DIM))


def _q_all(q_ref, qg_ref, ts):
    scale = HEAD_DIM ** -0.5
    zero = jnp.zeros((NSA_HPG * ts, HEAD_DIM), BF16)
    blocks = []
    for g in range(NSA_G):
        q4 = jnp.concatenate(
            [(_rms(q_ref[0, :, (g * NSA_HPG + hh) * HEAD_DIM:(g * NSA_HPG + hh + 1) * HEAD_DIM], qg_ref[...]) * scale).astype(BF16)
             for hh in range(NSA_HPG)], axis=0)
        blocks.append(jnp.concatenate([q4 if gg == g else zero for gg in range(NSA_G)], axis=1))
    return jnp.concatenate(blocks, axis=0)


def _row_to_col(row, n):
    eye = lax.broadcasted_iota(jnp.int32, (n, n), 0) == lax.broadcasted_iota(jnp.int32, (n, n), 1)
    return jnp.sum(jnp.where(eye, row, 0.0), axis=1, keepdims=True)


def _cmp_sample_body(q_ref, kc_ref, vc_ref, bias_ref, msel_ref, rep_ref, qg_ref, ocmp_ref, selt_ref, *, ts, nb, nbp):
    scale = HEAD_DIM ** -0.5
    ncp = kc_ref.shape[1]
    scores = []
    for g in range(NSA_G):
        gsl = slice(g * HEAD_DIM, (g + 1) * HEAD_DIM)
        q4 = jnp.concatenate(
            [_rms(q_ref[0, :, (g * NSA_HPG + hh) * HEAD_DIM:(g * NSA_HPG + hh + 1) * HEAD_DIM], qg_ref[...]) * scale
             for hh in range(NSA_HPG)], axis=0)
        bias = bias_ref[g * NSA_HPG:(g + 1) * NSA_HPG].reshape(NSA_HPG * ts, ncp)
        s = _mm_nt_hi(q4, kc_ref[0, :, gsl]) + bias
        e = jnp.exp(s - jnp.max(s, axis=-1, keepdims=True))
        p = e / jnp.sum(e, axis=-1, keepdims=True) * jnp.where(bias > 0.5 * NEG, 1.0, 0.0)
        o = _mm(p, vc_ref[0, :, gsl])
        imp = p[0:ts]
        for hh in range(NSA_HPG):
            h = g * NSA_HPG + hh
            ocmp_ref[0, :, h * HEAD_DIM:(h + 1) * HEAD_DIM] = o[hh * ts:(hh + 1) * ts]
            if hh:
                imp = imp + p[hh * ts:(hh + 1) * ts]
        scores.append(_mm_hi(imp, msel_ref[...]))
    score = jnp.concatenate(scores, axis=0)
    rows = NSA_G * ts
    lane = lax.broadcasted_iota(jnp.int32, (rows, nbp), 1)
    tok = lax.broadcasted_iota(jnp.int32, (rows, nbp), 0) & (ts - 1)
    qblk = (PAST_LEN + tok) >> 6
    forced = (lane == 0) | (lane == qblk) | (lane == qblk - 1)
    score = jnp.where(forced, BIG, score)
    score = jnp.where(lane <= qblk, score, -BIG)
    sel = jnp.zeros((rows, nbp), F32)
    lane_f = lane.astype(F32)
    for _ in range(SEL_TOP):
        mx = jnp.max(score, axis=-1, keepdims=True)
        first = jnp.min(jnp.where(score == mx, lane_f, float(nbp)), axis=-1, keepdims=True)
        pick = lane_f == first
        sel = jnp.where(pick, 1.0, sel)
        score = jnp.where(pick, -3e38, score)
    selt_ref[0] = _mm_tn(sel, rep_ref[...])


def _cmp_select_sample(ub, kcmp, vcmp, bias_cmp, q_norm_g, nb):
    b, ts, _ = ub.shape
    ncp = kcmp.shape[1]
    nc = ncp - 1
    nbp = -(-nb // LANE) * LANE
    assert ts & (ts - 1) == 0 and NSA_HEADS * ts == LANE
    rep = np.zeros((NSA_G * ts, LANE), np.float32)
    for g in range(NSA_G):
        for hh in range(NSA_HPG):
            for t in range(ts):
                rep[g * ts + t, (g * NSA_HPG + hh) * ts + t] = 1.0
    body = functools.partial(_cmp_sample_body, ts=ts, nb=nb, nbp=nbp)
    return pl.pallas_call(
        body,
        grid=(b,),
        in_specs=[
            pl.BlockSpec((1, ts, NSA_Q_W), lambda bi: (bi, 0, UB_Q // NSA_Q_W)),
            pl.BlockSpec((1, ncp, NSA_KV_W), lambda bi: (bi, 0, 0)),
            pl.BlockSpec((1, ncp, NSA_KV_W), lambda bi: (bi, 0, 0)),
            pl.BlockSpec((NSA_HEADS, ts, ncp), lambda bi: (0, 0, 0)),
            pl.BlockSpec((ncp, nbp), lambda bi: (0, 0)),
            pl.BlockSpec((NSA_G * ts, LANE), lambda bi: (0, 0)),
            pl.BlockSpec((1, HEAD_DIM), lambda bi: (0, 0)),
        ],
        out_specs=[
            pl.BlockSpec((1, ts, NSA_Q_W), lambda bi: (bi, 0, 0)),
            pl.BlockSpec((1, nbp, LANE), lambda bi: (bi, 0, 0)),
        ],
        out_shape=[jax.ShapeDtypeStruct((b, ts, NSA_Q_W), F32), jax.ShapeDtypeStruct((b, nbp, LANE), F32)],
        compiler_params=_cparams(("arbitrary",)),
        name="cmp_select_sample",
    )(ub, kcmp, vcmp, bias_cmp, _sel_matrix(nc, nb, ncp, nbp), jnp.asarray(rep), q_norm_g.reshape(1, HEAD_DIM))


def _sel_sample_body(pt_ref, *refs, pps, n_steps, ts):
    del pt_ref
    k_refs = refs[:pps]
    v_refs = refs[pps:2 * pps]
    (q_ref, selt_ref, knew_ref, vnew_ref, blast_ref, bfar_ref, bnew_ref, qg_ref,
     o_ref, qall_s, sc_s, snew_s, m_s, l_s, acc_s) = refs[2 * pps:]
    s = pl.program_id(1)
    n_pages = n_steps * pps
    sub = lax.broadcasted_iota(jnp.int32, (PAGE, LANE), 0)
    tok = lax.broadcasted_iota(jnp.int32, (ts, LANE), 1) & (ts - 1)
    new_t = lax.broadcasted_iota(jnp.int32, (ts, LANE), 0)

    @pl.when(s == 0)
    def _():
        qall = _q_all(q_ref, qg_ref, ts)
        qall_s[...] = qall
        sn = _mm_nt(knew_ref[0], qall) + bnew_ref[...]
        ok = (new_t <= tok) & (selt_ref[0, 2 * n_pages:2 * n_pages + 1, :] > 0.5)
        sn = jnp.where(ok, sn, NEG)
        snew_s[...] = sn
        m_s[...] = jnp.max(sn, axis=0, keepdims=True)
        l_s[...] = jnp.zeros((1, LANE), F32)
        acc_s[...] = jnp.zeros((LANE, NSA_KV_W), F32)

    @pl.when(s < n_steps)
    def _():
        m = m_s[...]
        for j in range(pps):
            p = s * pps + j
            st = _mm_nt(_load_page(k_refs[j], True), qall_s[...])
            st = st + jnp.where(p == n_pages - 1, blast_ref[...], bfar_ref[...])
            r0 = selt_ref[0, pl.ds(2 * p, 1), :]
            r1 = selt_ref[0, pl.ds(2 * p + 1, 1), :]
            ok = jnp.where(sub < SEL_BLOCK, r0, r1) > 0.5
            st = jnp.where(ok, st, NEG)
            sc_s[pl.ds(pl.multiple_of(p * PAGE, PAGE), PAGE), :] = st
            m = jnp.maximum(m, jnp.max(st, axis=0, keepdims=True))
        m_s[...] = m

    @pl.when(s >= n_steps)
    def _():
        m = m_s[...]
        l = l_s[...]
        acc = acc_s[...]
        for j in range(pps):
            p = (s - n_steps) * pps + j
            pt = jnp.exp(sc_s[pl.ds(pl.multiple_of(p * PAGE, PAGE), PAGE), :] - m)
            l = l + jnp.sum(pt, axis=0, keepdims=True)
            acc = acc + _mm_tn(pt, _load_page(v_refs[j], True))
        l_s[...] = l
        acc_s[...] = acc

    @pl.when(s == 2 * n_steps - 1)
    def _():
        pn = jnp.exp(snew_s[...] - m_s[...])
        l = l_s[...] + jnp.sum(pn, axis=0, keepdims=True)
        acc = acc_s[...] + _mm_tn(pn, vnew_ref[0])
        out = acc / _row_to_col(l, LANE)
        for h in range(NSA_HEADS):
            g = h // NSA_HPG
            o_ref[0, :, h * HEAD_DIM:(h + 1) * HEAD_DIM] = out[h * ts:(h + 1) * ts, g * HEAD_DIM:(g + 1) * HEAD_DIM]


def _lane_bias(rel_bias, key_pos, ts):
    rel = (PAST_LEN + jnp.arange(ts, dtype=jnp.int32))[None, :] - key_pos[:, None]
    return jnp.moveaxis(_bias_of(rel_bias, rel), 0, 1).reshape(key_pos.shape[0], NSA_HEADS * ts)


def _sel_sample(ub, selt, cache_k, cache_v, table, k_new, rel_bias, q_norm_g):
    b, ts, _ = ub.shape
    n_pages = table.shape[1]
    assert PAGE >= REL_MAX_DIST and n_pages * PAGE == PAST_LEN
    pps = min(PAGES_PER_STEP, n_pages)
    n_steps = n_pages // pps
    ar = jnp.arange
    blast = _lane_bias(rel_bias, PAST_LEN - PAGE + ar(PAGE, dtype=jnp.int32), ts)
    bfar = _lane_bias(rel_bias, jnp.zeros((1,), jnp.int32), ts)
    bnew = _lane_bias(rel_bias, PAST_LEN + ar(ts, dtype=jnp.int32), ts)
    body = functools.partial(_sel_sample_body, pps=pps, n_steps=n_steps, ts=ts)
    last = n_steps - 1
    page_block = (1, NSA_G * PAGE, HEAD_DIM)
    k_spec = lambda j: pl.BlockSpec(page_block, lambda bi, s, pt: (pt[bi, jnp.minimum(s, last) * pps + j], 0, 0))
    v_spec = lambda j: pl.BlockSpec(page_block, lambda bi, s, pt: (pt[bi, jnp.maximum(s - n_steps, 0) * pps + j], 0, 0))
    full = lambda shape: pl.BlockSpec(shape, lambda bi, s, pt: (0,) * len(shape))
    in_specs = [k_spec(j) for j in range(pps)] + [v_spec(j) for j in range(pps)] + [
        pl.BlockSpec((1, ts, NSA_Q_W), lambda bi, s, pt: (bi, 0, UB_Q // NSA_Q_W)),
        pl.BlockSpec((1,) + selt.shape[1:], lambda bi, s, pt: (bi, 0, 0)),
        pl.BlockSpec((1, ts, NSA_KV_W), lambda bi, s, pt: (bi, 0, 0)),
        pl.BlockSpec((1, ts, NSA_KV_W), lambda bi, s, pt: (bi, 0, UB_KV // NSA_KV_W + 3)),
        full((PAGE, LANE)), full((1, LANE)), full((ts, LANE)), full((1, HEAD_DIM)),
    ]
    return pl.pallas_call(
        body,
        grid_spec=pltpu.PrefetchScalarGridSpec(
            num_scalar_prefetch=1, grid=(b, 2 * n_steps), in_specs=in_specs,
            out_specs=pl.BlockSpec((1, ts, NSA_Q_W), lambda bi, s, pt: (bi, 0, 0)),
            scratch_shapes=[pltpu.VMEM((LANE, NSA_KV_W), BF16), pltpu.VMEM((PAST_LEN, LANE), F32),
                            pltpu.VMEM((ts, LANE), F32), pltpu.VMEM((1, LANE), F32), pltpu.VMEM((1, LANE), F32),
                            pltpu.VMEM((LANE, NSA_KV_W), F32)]),
        out_shape=jax.ShapeDtypeStruct((b, ts, NSA_Q_W), F32),
        compiler_params=_cparams(("arbitrary", "arbitrary")),
        name="sel_sample",
    )(table, *([cache_k] * pps), *([cache_v] * pps), ub, selt, k_new, ub, blast, bfar, bnew, q_norm_g.reshape(1, HEAD_DIM))


def _win_sample_body(q_ref, kc_ref, vc_ref, knew_ref, vnew_ref, bc_ref, bn_ref, gate_ref, z_ref, ocmp_ref, osel_ref, qg_ref,
                     o_ref, kout_ref, vout_ref, *, ts, wb):
    qall = _q_all(q_ref, qg_ref, ts)
    key_i = lax.broadcasted_iota(jnp.int32, (wb, LANE), 0)
    tok_c = lax.broadcasted_iota(jnp.int32, (wb, LANE), 1) & (ts - 1)
    tok_n = lax.broadcasted_iota(jnp.int32, (ts, LANE), 1) & (ts - 1)
    new_t = lax.broadcasted_iota(jnp.int32, (ts, LANE), 0)
    sc = _mm_nt(_load_page(kc_ref, True), qall) + bc_ref[...]
    sc = jnp.where(wb + tok_c - key_i < WINDOW, sc, NEG)
    sn = _mm_nt(knew_ref[0], qall) + bn_ref[...]
    sn = jnp.where(new_t <= tok_n, sn, NEG)
    m = jnp.maximum(jnp.max(sc, axis=0, keepdims=True), jnp.max(sn, axis=0, keepdims=True))
    pc = jnp.exp(sc - m)
    pn = jnp.exp(sn - m)
    l = jnp.sum(pc, axis=0, keepdims=True) + jnp.sum(pn, axis=0, keepdims=True)
    out = (_mm_tn(pc, _load_page(vc_ref, True)) + _mm_tn(pn, vnew_ref[0])) / _row_to_col(l, LANE)
    gates = jax.nn.sigmoid(gate_ref[0])
    for h in range(NSA_HEADS):
        g = h // NSA_HPG
        hsl = slice(h * HEAD_DIM, (h + 1) * HEAD_DIM)
        o_win = out[h * ts:(h + 1) * ts, g * HEAD_DIM:(g + 1) * HEAD_DIM]
        o = (gates[:, h:h + 1] * ocmp_ref[0, :, hsl] + gates[:, NSA_HEADS + h:NSA_HEADS + h + 1] * osel_ref[0, :, hsl]
             + gates[:, 2 * NSA_HEADS + h:2 * NSA_HEADS + h + 1] * o_win)
        o_ref[0, :, hsl] = (o * _silu(z_ref[0, :, hsl])).astype(o_ref.dtype)
    keep = (wb - ts) * NSA_G
    for out_ref, old_ref, new_ref in ((kout_ref, kc_ref, knew_ref), (vout_ref, vc_ref, vnew_ref)):
        out_ref[0, 0:keep, :] = old_ref[0, ts * NSA_G:wb * NSA_G, :]
        for g in range(NSA_G):
            out_ref[0, pl.ds(keep + g, ts, stride=NSA_G), :] = new_ref[0, :, g * HEAD_DIM:(g + 1) * HEAD_DIM]


def _win_sample(ub, cache_k, cache_v, k_new, ocmp, osel, rel_bias, q_norm_g):
    b, ts, _ = ub.shape
    wb = cache_k.shape[1] // NSA_G
    assert wb == WINDOW and ts % 8 == 0
    bc = _lane_bias(rel_bias, PAST_LEN - wb + jnp.arange(wb, dtype=jnp.int32), ts)
    bn = _lane_bias(rel_bias, PAST_LEN + jnp.arange(ts, dtype=jnp.int32), ts)
    body = functools.partial(_win_sample_body, ts=ts, wb=wb)
    tok_spec = lambda w, c: pl.BlockSpec((1, ts, w), lambda bi: (bi, 0, c))
    win_spec = pl.BlockSpec((1, wb * NSA_G, HEAD_DIM), lambda bi: (bi, 0, 0))
    return pl.pallas_call(
        body,
        grid=(b,),
        in_specs=[
            tok_spec(NSA_Q_W, UB_Q // NSA_Q_W), win_spec, win_spec,
            tok_spec(NSA_KV_W, 0), tok_spec(NSA_KV_W, UB_KV // NSA_KV_W + 5),
            pl.BlockSpec((wb, LANE), lambda bi: (0, 0)), pl.BlockSpec((ts, LANE), lambda bi: (0, 0)),
            tok_spec(LANE, UB_G // LANE), tok_spec(NSA_Q_W, UB_Z // NSA_Q_W),
            tok_spec(NSA_Q_W, 0), tok_spec(NSA_Q_W, 0),
            pl.BlockSpec((1, HEAD_DIM), lambda bi: (0, 0)),
        ],
        out_specs=[tok_spec(NSA_Q_W, 0), win_spec, win_spec],
        out_shape=[jax.ShapeDtypeStruct((b, ts, NSA_Q_W), BF16),
                   jax.ShapeDtypeStruct((b, wb * NSA_G, HEAD_DIM), F32),
                   jax.ShapeDtypeStruct((b, wb * NSA_G, HEAD_DIM), F32)],
        compiler_params=_cparams(("arbitrary",)),
        name="win_sample",
    )(ub, cache_k, cache_v, k_new, ub, bc, bn, ub, ub, ocmp, osel, q_norm_g.reshape(1, HEAD_DIM))


def _mix_body(oa_ref, ob_ref, wa_ref, wb_ref, ma_ref, mb_ref, o_ref):
    pa = jnp.dot(oa_ref[...], wa_ref[...], preferred_element_type=F32)
    pb = jnp.dot(ob_ref[...], wb_ref[...], preferred_element_type=F32)
    o_ref[...] = (jax.nn.sigmoid(ma_ref[...]) * pa + jax.nn.sigmoid(mb_ref[...]) * pb).astype(o_ref.dtype)


def _out_body(m_ref, w_ref, x_ref, y_ref):
    y_ref[...] = x_ref[...] + jnp.dot(m_ref[...], w_ref[...], preferred_element_type=F32)


def _output(x2d, oa, ob, ub2d, wa16, wb16, wo16):
    m = x2d.shape[0]
    tm = min(m, 1024)
    tn = 512
    grid = (m // tm, D_MODEL // tn)
    mixed = pl.pallas_call(
        _mix_body,
        grid=grid,
        in_specs=[
            pl.BlockSpec((tm, GDN_V_W), lambda i, j: (i, 0)),
            pl.BlockSpec((tm, NSA_Q_W), lambda i, j: (i, 0)),
            pl.BlockSpec((GDN_V_W, tn), lambda i, j: (0, j)),
            pl.BlockSpec((NSA_Q_W, tn), lambda i, j: (0, j)),
            pl.BlockSpec((tm, tn), lambda i, j: (i, UB_MA // tn + j)),
            pl.BlockSpec((tm, tn), lambda i, j: (i, UB_MB // tn + j)),
        ],
        out_specs=pl.BlockSpec((tm, tn), lambda i, j: (i, j)),
        out_shape=jax.ShapeDtypeStruct((m, D_MODEL), BF16),
        compiler_params=_cparams(("arbitrary", "arbitrary")),
        name="branch_mix",
    )(oa, ob, wa16, wb16, ub2d, ub2d)
    return pl.pallas_call(
        _out_body,
        grid=grid,
        in_specs=[
            pl.BlockSpec((tm, D_MODEL), lambda i, j: (i, 0)),
            pl.BlockSpec((D_MODEL, tn), lambda i, j: (0, j)),
            pl.BlockSpec((tm, tn), lambda i, j: (i, j)),
        ],
        out_specs=pl.BlockSpec((tm, tn), lambda i, j: (i, j)),
        out_shape=jax.ShapeDtypeStruct((m, D_MODEL), F32),
        compiler_params=_cparams(("arbitrary", "arbitrary")),
        name="out_proj",
    )(mixed, wo16, x2d)


def _kv_slices(ub):
    b, t_len, _ = ub.shape
    out = []
    for i in (0, 1, 3, 5):
        a = ub[:, :, UB_KV + i * NSA_KV_W:UB_KV + (i + 1) * NSA_KV_W]
        out.append(a.reshape(b, t_len, NSA_G, HEAD_DIM))
    return out


def kernel(x_prompt, x_sample, cache_k_cmp, cache_v_cmp, cache_k_sel, cache_v_sel, cache_k_win, cache_v_win, state_conv, state_gdn, page_table, norm_g, w_in, gdn_conv_w, gdn_a_log, gdn_dt_bias, gdn_norm_g, q_norm_g, k_norm_g, cmp_pe_k, cmp_w_k, cmp_proj_k, cmp_pe_v, cmp_w_v, cmp_proj_v, rel_bias, w_branch_a, w_branch_b, w_out):
    bp, tp, _ = x_prompt.shape
    bs, ts, _ = x_sample.shape
    wb = cache_k_win.shape[1]
    n_pool = cache_k_cmp.shape[0]
    kv4 = lambda a: a.reshape(a.shape[0], a.shape[1], NSA_G, HEAD_DIM)
    cmp_p = (cmp_pe_k, cmp_w_k, cmp_proj_k, cmp_pe_v, cmp_w_v, cmp_proj_v)

    assert _IN_MB + D_MODEL == w_in.shape[1]
    w_t = w_in.T
    wa16, wb16, wo16 = w_branch_a.astype(BF16), w_branch_b.astype(BF16), w_out.astype(BF16)

    def project(x):
        x2d = x.reshape(-1, D_MODEL)
        ua = _project(x2d, norm_g, w_t, UA_OFFSETS).reshape(x.shape[0], x.shape[1], UA_W)
        ub = _project(x2d, norm_g, w_t, UB_OFFSETS).reshape(x.shape[0], x.shape[1], UB_W)
        return x2d, ua, ub

    x2d, ua, ub = project(x_prompt)
    conv0 = jnp.zeros((bp, GDN_CONV - 1, GDN_QKV_W), F32)
    s0 = jnp.zeros((bp, GDN_V_HEADS, HEAD_DIM, HEAD_DIM), F32)
    o_a, p_gdn = _gdn(ua, ub, conv0, s0, gdn_conv_w, gdn_a_log, gdn_dt_bias, gdn_norm_g)
    p_conv = ua[:, tp - (GDN_CONV - 1):, :GDN_QKV_W]
    n_pages_p = tp // PAGE
    pages = ub.reshape(bp * n_pages_p, PAGE, UB_W)
    ident = jnp.arange(bp * n_pages_p, dtype=jnp.int32).reshape(bp, n_pages_p)
    kcmp, vcmp = _compress(pages, pages, UB_KV // NSA_KV_W, UB_KV // NSA_KV_W + 1, ident, *cmp_p, k_norm_g[0])
    p_kc, p_vc, p_ks, p_vs, p_kw, p_vw, ksb, vsb, kwb, vwb = _nsa_prep_prompt(ub, k_norm_g, wb)
    nc_p = (tp - CMP_LEN) // CMP_STRIDE + 1
    bias_cmp = _cmp_bias(rel_bias, jnp.arange(tp, dtype=jnp.int32), kcmp.shape[1], nc_p)
    ocmp, nsel = _cmp_select_prompt(ub, kcmp, vcmp, bias_cmp, q_norm_g)
    o_b = _attn_prompt(ub, ocmp, nsel, ksb, vsb, kwb, vwb, rel_bias, q_norm_g)
    y_prompt = _output(x2d, o_a.reshape(-1, GDN_V_W), o_b.reshape(-1, NSA_Q_W), ub.reshape(-1, UB_W), wa16, wb16, wo16)
    cache4 = lambda a: a.reshape(bp, a.shape[1] // NSA_G, NSA_G, HEAD_DIM)
    prompt_out = (y_prompt.reshape(x_prompt.shape), cache4(p_kc), cache4(p_vc), cache4(p_ks), cache4(p_vs), cache4(p_kw),
                  cache4(p_vw), p_conv, p_gdn)

    x2d, ua, ub = project(x_sample)
    o_a, s_gdn = _gdn(ua, ub, state_conv, state_gdn, gdn_conv_w, gdn_a_log, gdn_dt_bias, gdn_norm_g)
    s_conv = ua[:, ts - (GDN_CONV - 1):, :GDN_QKV_W]
    pool = lambda c: c.reshape(n_pool, PAGE * NSA_G, HEAD_DIM)
    kcmp, vcmp = _compress(pool(cache_k_cmp), pool(cache_v_cmp), 0, 0, page_table, *cmp_p, k_norm_g[0])
    s_ks, s_kwn = _nsa_prep(ub, k_norm_g)
    total = PAST_LEN + ts
    nc_s = (total - CMP_LEN) // CMP_STRIDE + 1
    nb_s = -(-total // SEL_BLOCK)
    assert nc_s + 1 == kcmp.shape[1]
    pos_s = PAST_LEN + jnp.arange(ts, dtype=jnp.int32)
    bias_cmp = _cmp_bias(rel_bias, pos_s, kcmp.shape[1], nc_s)
    ocmp, selt = _cmp_select_sample(ub, kcmp, vcmp, bias_cmp, q_norm_g, nb_s)
    osel = _sel_sample(ub, selt, pool(cache_k_sel), pool(cache_v_sel), page_table, s_ks, rel_bias, q_norm_g)
    win3 = lambda c: c.reshape(bs, wb * NSA_G, HEAD_DIM)
    o_b, s_kw, s_vw = _win_sample(ub, win3(cache_k_win), win3(cache_v_win), s_kwn, ocmp, osel, rel_bias, q_norm_g)
    y_sample = _output(x2d, o_a.reshape(-1, GDN_V_W), o_b.reshape(-1, NSA_Q_W), ub.reshape(-1, UB_W), wa16, wb16, wo16)
    s_kc, s_vc, s_vs, _ = _kv_slices(ub)
    sample_out = (y_sample.reshape(x_sample.shape), s_kc, s_vc, kv4(s_ks), s_vs, s_kw.reshape(cache_k_win.shape),
                  s_vw.reshape(cache_v_win.shape), s_conv, s_gdn)

    return (prompt_out[0], sample_out[0]) + prompt_out[1:] + sample_out[1:]
```

```python
import functools
import math

import jax
import jax.numpy as jnp
import numpy as np
from jax import lax
from jax.experimental import pallas as pl
from jax.experimental.pallas import tpu as pltpu

F32 = jnp.float32
BF16 = jnp.bfloat16
HI = lax.Precision.HIGHEST

D_MODEL = 2048
PAST_LEN = 16384
PAGE = 128

GDN_QK_HEADS = 16
GDN_V_HEADS = 32
HEAD_DIM = 128
GDN_CONV = 4
GDN_CHUNK = 64
GDN_GROUP = 8
GDN_QK_PER_STEP = 2
GDN_QK_W = GDN_QK_HEADS * HEAD_DIM
GDN_V_W = GDN_V_HEADS * HEAD_DIM
GDN_QKV_W = 2 * GDN_QK_W + GDN_V_W

NSA_HEADS = 16
NSA_G = 4
NSA_HPG = NSA_HEADS // NSA_G
NSA_Q_W = NSA_HEADS * HEAD_DIM
NSA_KV_W = NSA_G * HEAD_DIM
CMP_LEN = 32
CMP_STRIDE = 16
SEL_BLOCK = 64
SEL_TOP = 16
WINDOW = 512
REL_BUCKETS = 32
REL_MAX_DIST = 128

EPS = 1e-6
NEG = -1e30
BIG = 1e9

LANE = 128
VMEM_LIMIT = 56 * 1024 * 1024
PAGES_PER_STEP = 32

UA_W = GDN_QKV_W + GDN_V_W
UB_Q, UB_Z, UB_MA, UB_MB, UB_KV, UB_BA, UB_G = 0, 2048, 4096, 6144, 8192, 11264, 11776
UB_W = 12288
PROJ_TN = 512
PROJ_TM = 2048

_IN_QB = UA_W + 2 * GDN_V_HEADS
_IN_KV = _IN_QB + NSA_Q_W
_IN_G = _IN_KV + 6 * NSA_KV_W
_IN_ZB = _IN_G + 3 * NSA_HEADS
_IN_MA = _IN_ZB + NSA_Q_W
_IN_MB = _IN_MA + D_MODEL


def _tiles(start, width):
    return [start + k * PROJ_TN for k in range(width // PROJ_TN)]


UA_OFFSETS = _tiles(0, UA_W)
UB_OFFSETS = (_tiles(_IN_QB, NSA_Q_W) + _tiles(_IN_ZB, NSA_Q_W) + _tiles(_IN_MA, D_MODEL) + _tiles(_IN_MB, D_MODEL)
              + _tiles(_IN_KV, 6 * NSA_KV_W) + [UA_W, _IN_G])


def _mm(a, b):
    return jnp.dot(a.astype(BF16), b.astype(BF16), preferred_element_type=F32)


def _mm_nt(a, b):
    return lax.dot_general(a.astype(BF16), b.astype(BF16), (((1,), (1,)), ((), ())), preferred_element_type=F32)


def _mm_tn(a, b):
    return lax.dot_general(a.astype(BF16), b.astype(BF16), (((0,), (0,)), ((), ())), preferred_element_type=F32)


def _mm_hi(a, b):
    return jnp.dot(a, b, precision=HI, preferred_element_type=F32)


def _mm_nt_hi(a, b):
    return lax.dot_general(a, b, (((1,), (1,)), ((), ())), precision=HI, preferred_element_type=F32)


def _silu(x):
    return x * jax.nn.sigmoid(x)


def _softplus(x):
    return jnp.maximum(x, 0.0) + jnp.log1p(jnp.exp(-jnp.abs(x)))


def _rms(x, gain):
    return x * lax.rsqrt(jnp.mean(x * x, axis=-1, keepdims=True) + EPS) * gain


def _cparams(sem):
    return pltpu.CompilerParams(dimension_semantics=sem, vmem_limit_bytes=VMEM_LIMIT)


def _proj_body(offs_ref, x_ref, g_ref, w_ref, o_ref, h_ref):
    del offs_ref

    @pl.when(pl.program_id(1) == 0)
    def _():
        h_ref[...] = _rms(x_ref[...], g_ref[...]).astype(BF16)

    o_ref[...] = lax.dot_general(h_ref[...], w_ref[...].astype(BF16), (((1,), (1,)), ((), ())),
                                 preferred_element_type=F32).astype(o_ref.dtype)


def _project(x2d, norm_g, w_t, offsets, out_dtype=F32):
    m = x2d.shape[0]
    tm = min(m, PROJ_TM)
    n_tiles = len(offsets)
    return pl.pallas_call(
        _proj_body,
        grid_spec=pltpu.PrefetchScalarGridSpec(
            num_scalar_prefetch=1,
            grid=(m // tm, n_tiles),
            in_specs=[
                pl.BlockSpec((tm, D_MODEL), lambda i, j, offs: (i, 0), pipeline_mode=pl.Buffered(1)),
                pl.BlockSpec((1, D_MODEL), lambda i, j, offs: (0, 0)),
                pl.BlockSpec((pl.Element(PROJ_TN), pl.Element(D_MODEL)), lambda i, j, offs: (pl.multiple_of(offs[j], 16), 0)),
            ],
            out_specs=pl.BlockSpec((tm, PROJ_TN), lambda i, j, offs: (i, j)),
            scratch_shapes=[pltpu.VMEM((tm, D_MODEL), BF16)]),
        out_shape=jax.ShapeDtypeStruct((m, n_tiles * PROJ_TN), out_dtype),
        compiler_params=_cparams(("arbitrary", "arbitrary")),
        name="proj",
    )(jnp.asarray(offsets, jnp.int32), x2d, norm_g.reshape(1, D_MODEL), w_t)


def _split2(a):
    hi = a.astype(BF16)
    return hi, (a - hi.astype(F32)).astype(BF16)


def _dot16(a, b):
    return jnp.dot(a, b, preferred_element_type=F32)


def _mm_x3(a2, b2):
    return _dot16(a2[0], b2[0]) + (_dot16(a2[0], b2[1]) + _dot16(a2[1], b2[0]))


def _bdot(a, b):
    return lax.dot_general(a, b, (((2,), (1,)), ((0,), (0,))), preferred_element_type=F32)


def _bdot_nt(a, b):
    return lax.dot_general(a, b, (((2,), (2,)), ((0,), (0,))), preferred_element_type=F32)


def _bmm_x3(a2, b2):
    return _bdot(a2[0], b2[0]) + (_bdot(a2[0], b2[1]) + _bdot(a2[1], b2[0]))


INV_BASE = 8


def _unit_lower_inverse_pairs(nmat, row, col, left, c):
    def blockdiag(x2):
        return tuple(jnp.concatenate([jnp.where(left, x, jnp.zeros_like(x)), jnp.where(left, jnp.zeros_like(x), x)], axis=1)
                     for x in x2)

    def same_block(size):
        shift = size.bit_length() - 1
        return (row >> shift) == (col >> shift)

    base = min(INV_BASE, c)
    nd = jnp.where(same_block(base), nmat, 0.0)
    p = jnp.where(row == col, 1.0, 0.0) + nd
    m2 = _split2(nd)
    bd2 = blockdiag(m2)
    span = 2
    while span < base:
        m2 = _split2(_bmm_x3(m2, bd2))
        bd2 = blockdiag(m2)
        p = p + _bmm_x3(_split2(p), bd2)
        span *= 2
    size = base
    while size < c:
        n21 = jnp.where(same_block(2 * size), jnp.where(same_block(size), 0.0, nmat), 0.0)
        p2 = _split2(p)
        t_n = _bmm_x3(p2, blockdiag(_split2(n21)))
        p = p + _bmm_x3(_split2(t_n), blockdiag(p2))
        size *= 2
    return p


def _cumsum_rows(tril16, g):
    h = g.astype(BF16)
    r = g - h.astype(F32)
    m = r.astype(BF16)
    l = (r - m.astype(F32)).astype(BF16)
    return _dot16(tril16, h) + (_dot16(tril16, m) + _dot16(tril16, l))


def _gdn_body(q_ref, k_ref, v_ref, z_ref, ba_ref, cwq_ref, cwk_ref, cwv_ref, csq_ref, csk_ref, csv_ref,
              s0_ref, gp_ref, ng_ref, o_ref, sfin_ref, cq_s, ck_s, cv_s, u_s, w_s, qg_s, kd_s, at_s, gl_s, st_s,
              *, t_len, c, nq):
    nv = 2 * nq
    first_head = nv * pl.program_id(1)
    n_chunks = t_len // c
    for cs, carry in ((csq_ref, cq_s), (csk_ref, ck_s), (csv_ref, cv_s)):
        carry[...] = jnp.zeros(carry.shape, F32)
        carry[5:8, :] = cs[0]
    st_s[...] = s0_ref[0]

    gsz = min(GDN_GROUP, n_chunks)
    rg = gsz * c
    row = lax.broadcasted_iota(jnp.int32, (1, c, 2 * c), 1)
    lane2 = lax.broadcasted_iota(jnp.int32, (1, c, 2 * c), 2)
    left = lane2 < c
    col = lane2 & (c - 1)
    tril = row >= col
    strict = row > col
    eye = row == col
    lane = lax.broadcasted_iota(jnp.int32, (rg, LANE), 1)
    rr = lax.broadcasted_iota(jnp.int32, (rg, rg), 0)
    cc = lax.broadcasted_iota(jnp.int32, (rg, rg), 1)
    shift = c.bit_length() - 1
    assert 1 << shift == c
    tril16 = jnp.where((rr >= cc) & ((rr >> shift) == (cc >> shift)), 1.0, 0.0).astype(BF16)

    def conv(src_ref, carry, cw_ref, rows):
        cur = src_ref[0, rows, :].astype(F32)
        win = jnp.concatenate([carry[...], cur], axis=0)
        carry[...] = cur[rg - 8:rg]
        w = cw_ref[...]
        a = win[5:5 + rg] * w[0:1] + win[6:6 + rg] * w[1:2] + win[7:7 + rg] * w[2:3] + win[8:8 + rg] * w[3:4]
        return _silu(a)

    def pick(x, idx):
        return jnp.sum(jnp.where(lane == idx, x, 0.0), axis=-1, keepdims=True).reshape(gsz, c, 1)

    def per_head(x, n):
        return [x[:, i * HEAD_DIM:(i + 1) * HEAD_DIM].reshape(gsz, c, HEAD_DIM) for i in range(n)]

    def prepare(gi):
        base = gi * rg if isinstance(gi, int) else pl.multiple_of(gi * rg, rg)
        rows = pl.ds(base, rg)
        qa = per_head(conv(q_ref, cq_s, cwq_ref, rows), nq)
        ka = per_head(conv(k_ref, ck_s, cwk_ref, rows), nq)
        va = per_head(conv(v_ref, cv_s, cwv_ref, rows), nv)
        qn = [x * lax.rsqrt(jnp.sum(x * x, axis=-1, keepdims=True) + EPS) * (HEAD_DIM ** -0.5) for x in qa]
        kn = [x * lax.rsqrt(jnp.sum(x * x, axis=-1, keepdims=True) + EPS) for x in ka]
        ba = ba_ref[0, rows, :]
        beta_all = jax.nn.sigmoid(ba)
        g_all = -jnp.exp(gp_ref[0:1, :]) * _softplus(ba + gp_ref[1:2, :])
        gc_all = _cumsum_rows(tril16, g_all)
        k16 = jnp.concatenate(kn, axis=0).astype(BF16)
        k2x = jnp.concatenate([k16, k16], axis=1)
        kk = _bdot_nt(k16, k2x)
        qk = _bdot_nt(jnp.concatenate(qn, axis=0).astype(BF16), k2x)
        betas, gcols, blocks = [], [], []
        zeros = jnp.zeros((gsz, c, 2 * HEAD_DIM), F32)
        for qi in range(nq):
            pair = []
            for l in range(2):
                hh = 2 * qi + l
                beta = pick(beta_all, first_head + hh)
                gcol = pick(gc_all, GDN_V_HEADS + first_head + hh)
                egc = jnp.exp(gcol)
                glast = gcol[:, c - 1:c, :]
                rhs = jnp.concatenate([va[hh] * beta, kn[qi] * (beta * egc)], axis=2)
                pair.append((beta, gcol, rhs))
                qg_s[hh, rows, :] = (qn[qi] * egc).reshape(rg, HEAD_DIM).astype(qg_s.dtype)
                kd_s[hh, rows, :] = (kn[qi] * jnp.exp(glast - gcol)).reshape(rg, HEAD_DIM).astype(kd_s.dtype)
                egl = jnp.exp(glast)
                for j in range(gsz):
                    gl_s[hh, gi * gsz + j] = jnp.broadcast_to(egl[j], (8, HEAD_DIM))
            betas.append(jnp.where(left, pair[0][0], pair[1][0]))
            gcols.append(jnp.where(left, pair[0][1], pair[1][1]))
            blocks.append(jnp.concatenate([jnp.concatenate([pair[0][2], zeros], axis=2),
                                           jnp.concatenate([zeros, pair[1][2]], axis=2)], axis=1))
        beta = jnp.concatenate(betas, axis=0)
        gcol = jnp.concatenate(gcols, axis=0)
        grow = jnp.sum(jnp.where(eye, gcol, 0.0), axis=1, keepdims=True)
        gamma = jnp.where(tril, jnp.exp(jnp.minimum(gcol - grow, 0.0)), 0.0)
        tinv = _unit_lower_inverse_pairs(jnp.where(strict, -(kk * beta * gamma), 0.0), row, col, left, c)
        rhs2 = _split2(jnp.concatenate(blocks, axis=0))
        uw = _bmm_x3(_split2(tinv), rhs2)
        attn = qk * gamma
        for qi in range(nq):
            at_s[qi, rows, :] = attn[qi * gsz:(qi + 1) * gsz].reshape(rg, 2 * c).astype(at_s.dtype)
            for l in range(2):
                uw_h = uw[qi * gsz:(qi + 1) * gsz, :, 2 * l * HEAD_DIM:2 * (l + 1) * HEAD_DIM]
                u_s[2 * qi + l, rows, :] = uw_h[:, :, :HEAD_DIM].reshape(rg, HEAD_DIM)
                w_s[2 * qi + l, rows, :] = uw_h[:, :, HEAD_DIM:].reshape(rg, HEAD_DIM).astype(w_s.dtype)

    def recur(ci):
        rows = pl.ds(ci * c if isinstance(ci, int) else pl.multiple_of(ci * c, c), c)
        s = st_s[...]
        s16 = s.astype(BF16)
        v_new = u_s[:, rows, :] - _bdot(w_s[:, rows, :].astype(BF16), s16)
        v16 = v_new.astype(BF16)
        zv = jnp.zeros((c, HEAD_DIM), BF16)
        vbd = jnp.stack([jnp.concatenate([jnp.concatenate([v16[2 * qi], zv], axis=1),
                                          jnp.concatenate([zv, v16[2 * qi + 1]], axis=1)], axis=0) for qi in range(nq)], axis=0)
        o_state = _bdot(qg_s[:, rows, :].astype(BF16), s16)
        o_attn = _bdot(at_s[:, rows, :].astype(BF16), vbd)
        kd16 = kd_s[:, rows, :].astype(BF16)
        upd = jnp.stack([_mm_tn(kd16[hh], v16[hh]) for hh in range(nv)], axis=0)
        st_s[...] = s * gl_s[:, ci][:, 0:1, :] + upd
        for hh in range(nv):
            hsl = slice(hh * HEAD_DIM, (hh + 1) * HEAD_DIM)
            o = o_state[hh] + o_attn[hh // 2][:, (hh % 2) * HEAD_DIM:(hh % 2 + 1) * HEAD_DIM]
            o_ref[0, rows, hsl] = (_rms(o, ng_ref[...]) * _silu(z_ref[0, rows, hsl].astype(F32))).astype(o_ref.dtype)

    if n_chunks == 1:
        prepare(0)
        recur(0)
    else:
        def prepare_step(gi, carry):
            prepare(gi)
            return carry

        def recur_step(ci, carry):
            recur(ci)
            return carry

        lax.fori_loop(0, n_chunks // gsz, prepare_step, 0)
        lax.fori_loop(0, n_chunks, recur_step, 0)
    sfin_ref[0] = st_s[...]


def _gdn(ua, ub, conv_state, s0, conv_w, a_log, dt_bias, norm_g):
    b, t_len, _ = ua.shape
    c = min(GDN_CHUNK, t_len)
    hd = HEAD_DIM
    nq = GDN_QK_PER_STEP if t_len > GDN_CHUNK else 4 * GDN_QK_PER_STEP
    nv = 2 * nq
    steps = GDN_QK_HEADS // nq
    qw, vw = nq * hd, nv * hd
    op_dtype = BF16 if c % 16 == 0 else F32
    gp = jnp.zeros((2, LANE), F32)
    gp = gp.at[0, GDN_V_HEADS:2 * GDN_V_HEADS].set(a_log).at[1, GDN_V_HEADS:2 * GDN_V_HEADS].set(dt_bias)
    body = functools.partial(_gdn_body, t_len=t_len, c=c, nq=nq)
    return pl.pallas_call(
        body,
        grid=(b, steps),
        in_specs=[
            pl.BlockSpec((1, t_len, qw), lambda bi, i: (bi, 0, i)),
            pl.BlockSpec((1, t_len, qw), lambda bi, i: (bi, 0, steps + i)),
            pl.BlockSpec((1, t_len, vw), lambda bi, i: (bi, 0, steps + i)),
            pl.BlockSpec((1, t_len, vw), lambda bi, i: (bi, 0, 2 * steps + i)),
            pl.BlockSpec((1, t_len, LANE), lambda bi, i: (bi, 0, UB_BA // LANE)),
            pl.BlockSpec((GDN_CONV, qw), lambda bi, i: (0, i)),
            pl.BlockSpec((GDN_CONV, qw), lambda bi, i: (0, steps + i)),
            pl.BlockSpec((GDN_CONV, vw), lambda bi, i: (0, steps + i)),
            pl.BlockSpec((1, GDN_CONV - 1, qw), lambda bi, i: (bi, 0, i)),
            pl.BlockSpec((1, GDN_CONV - 1, qw), lambda bi, i: (bi, 0, steps + i)),
            pl.BlockSpec((1, GDN_CONV - 1, vw), lambda bi, i: (bi, 0, steps + i)),
            pl.BlockSpec((1, nv, hd, hd), lambda bi, i: (bi, i, 0, 0)),
            pl.BlockSpec((2, LANE), lambda bi, i: (0, 0)),
            pl.BlockSpec((1, hd), lambda bi, i: (0, 0)),
        ],
        out_specs=[
            pl.BlockSpec((1, t_len, vw), lambda bi, i: (bi, 0, i)),
            pl.BlockSpec((1, nv, hd, hd), lambda bi, i: (bi, i, 0, 0)),
        ],
        out_shape=[
            jax.ShapeDtypeStruct((b, t_len, GDN_V_W), BF16),
            jax.ShapeDtypeStruct((b, GDN_V_HEADS, hd, hd), F32),
        ],
        scratch_shapes=[
            pltpu.VMEM((8, qw), F32),
            pltpu.VMEM((8, qw), F32),
            pltpu.VMEM((8, vw), F32),
            pltpu.VMEM((nv, t_len, hd), F32),
            pltpu.VMEM((nv, t_len, hd), op_dtype),
            pltpu.VMEM((nv, t_len, hd), op_dtype),
            pltpu.VMEM((nv, t_len, hd), op_dtype),
            pltpu.VMEM((nq, t_len, 2 * c), op_dtype),
            pltpu.VMEM((nv, t_len // c, 8, hd), F32),
            pltpu.VMEM((nv, hd, hd), F32),
        ],
        compiler_params=_cparams(("arbitrary", "arbitrary")),
        name="gdn",
    )(ua, ua, ua, ua, ub, conv_w, conv_w, conv_w, conv_state, conv_state, conv_state, s0, gp, norm_g.reshape(1, hd))


def _t5_bucket(rel):
    n = jnp.maximum(rel, 0)
    exact = REL_BUCKETS // 2
    nf = jnp.maximum(n, 1).astype(F32)
    large = exact + (jnp.log(nf / exact) / math.log(REL_MAX_DIST / exact) * (REL_BUCKETS - exact)).astype(jnp.int32)
    return jnp.where(n < exact, n, jnp.minimum(large, REL_BUCKETS - 1))


def _bias_of(rel_bias, rel):
    bucket = _t5_bucket(rel)[None]
    table = rel_bias.astype(F32)
    out = jnp.zeros((NSA_HEADS,) + rel.shape, F32)
    for k in range(REL_BUCKETS):
        out = jnp.where(bucket == k, table[k].reshape((NSA_HEADS,) + (1,) * rel.ndim), out)
    return out


def _cmp_bias(rel_bias, q_pos, ncp, nc):
    n = jnp.arange(ncp, dtype=jnp.int32)
    rel = q_pos[:, None] - (n * CMP_STRIDE + (CMP_LEN - 1))[None, :]
    ok = (rel >= 0) & (n < nc)[None, :]
    return jnp.where(ok[None], _bias_of(rel_bias, rel), NEG)


def _load_page(ref, interleaved):
    if not interleaved:
        return ref[0]
    n_rows = ref.shape[1] // NSA_G
    return jnp.concatenate([ref[0, pl.ds(g, n_rows, stride=NSA_G), :] for g in range(NSA_G)], axis=1)


def _compress_body(pt_ref, *refs, pps, n_steps, nch, interleaved):
    del pt_ref
    k_refs = refs[:pps]
    v_refs = refs[pps:2 * pps]
    (wabk_ref, wabv_ref, wk_ref, pek_ref, wv_ref, pev_ref, projk_ref, projv_ref, gain_ref,
     outk_ref, outv_ref, ak_s, bk_s, av_s, bv_s) = refs[2 * pps:]
    s = pl.program_id(1)

    @pl.when(s == 0)
    def _():
        bk_s[nch:nch + 8, :] = jnp.zeros((8, NSA_KV_W), F32)
        bv_s[nch:nch + 8, :] = jnp.zeros((8, NSA_KV_W), F32)

    def pool(w2, page):
        w_hi, w_lo = w2
        p_hi, p_lo = _split2(page)
        r = _dot16(jnp.concatenate([w_hi, w_lo], axis=0), p_hi)
        return r[0:16] + (r[16:32] + _dot16(w_hi, p_lo))

    wk2 = _split2(wabk_ref[...])
    wv2 = _split2(wabv_ref[...])
    for j in range(pps):
        row0 = pl.multiple_of((s * pps + j) * 8, 8)
        abk = pool(wk2, _load_page(k_refs[j], interleaved))
        ak_s[pl.ds(row0, 8), :] = abk[0:8]
        bk_s[pl.ds(row0, 8), :] = abk[8:16]
        abv = pool(wv2, _load_page(v_refs[j], interleaved))
        av_s[pl.ds(row0, 8), :] = abv[0:8]
        bv_s[pl.ds(row0, 8), :] = abv[8:16]

    @pl.when(s == n_steps - 1)
    def _():
        cpe_k = jnp.sum(wk_ref[...] * pek_ref[...], axis=0, keepdims=True)
        cpe_v = jnp.sum(wv_ref[...] * pev_ref[...], axis=0, keepdims=True)
        projk2 = _split2(projk_ref[...])
        projv2 = _split2(projv_ref[...])
        rb = min(nch, 128)

        def fin(r, carry):
            r0 = pl.multiple_of(r * rb, rb)
            pk = ak_s[pl.ds(r0, rb), :] + bk_s[pl.ds(r0, rb + 8), :][1:rb + 1]
            pv = av_s[pl.ds(r0, rb), :] + bv_s[pl.ds(r0, rb + 8), :][1:rb + 1]
            for g in range(NSA_G):
                sl = slice(g * HEAD_DIM, (g + 1) * HEAD_DIM)
                yk = _mm_x3(_split2(pk[:, sl] + cpe_k), projk2)
                outk_ref[0, pl.ds(r0, rb), sl] = _rms(yk, gain_ref[...])
                outv_ref[0, pl.ds(r0, rb), sl] = _mm_x3(_split2(pv[:, sl] + cpe_v), projv2)
            return carry

        lax.fori_loop(0, nch // rb, fin, 0)


def _pool_weights(w):
    c = np.arange(8)[:, None]
    t = np.arange(PAGE)[None, :]
    off = t - CMP_STRIDE * c
    inside = (off >= 0) & (off < CMP_STRIDE)
    idx = np.clip(off, 0, CMP_STRIDE - 1)
    wa = jnp.where(inside, w[idx], 0.0)
    wb = jnp.where(inside, w[idx + CMP_STRIDE], 0.0)
    return jnp.concatenate([wa, wb], axis=0).astype(F32)


def _compress(pages_k, pages_v, col_k, col_v, table, pe_k, w_k, proj_k, pe_v, w_v, proj_v, k_gain):
    b, n_pages = table.shape
    pps = min(PAGES_PER_STEP, n_pages)
    n_steps = n_pages // pps
    nch = 8 * n_pages
    interleaved = pages_k.shape[1] == NSA_G * PAGE
    body = functools.partial(_compress_body, pps=pps, n_steps=n_steps, nch=nch, interleaved=interleaved)

    def page_spec(j, col):
        if interleaved:
            return pl.BlockSpec((1, NSA_G * PAGE, HEAD_DIM), lambda bi, s, pt: (pt[bi, s * pps + j], 0, 0))
        return pl.BlockSpec((1, PAGE, NSA_KV_W), lambda bi, s, pt: (pt[bi, s * pps + j], 0, col))

    full = lambda shape: pl.BlockSpec(shape, lambda bi, s, pt: (0,) * len(shape))
    in_specs = [page_spec(j, col_k) for j in range(pps)] + [page_spec(j, col_v) for j in range(pps)]
    in_specs += [full((16, PAGE)), full((16, PAGE)), full((CMP_LEN, 1)), full((CMP_LEN, HEAD_DIM)),
                 full((CMP_LEN, 1)), full((CMP_LEN, HEAD_DIM)), full((HEAD_DIM, HEAD_DIM)), full((HEAD_DIM, HEAD_DIM)),
                 full((1, HEAD_DIM))]
    out_spec = pl.BlockSpec((1, nch, NSA_KV_W), lambda bi, s, pt: (bi, 0, 0))
    return pl.pallas_call(
        body,
        grid_spec=pltpu.PrefetchScalarGridSpec(
            num_scalar_prefetch=1, grid=(b, n_steps), in_specs=in_specs, out_specs=[out_spec, out_spec],
            scratch_shapes=[pltpu.VMEM((nch, NSA_KV_W), F32), pltpu.VMEM((nch + 8, NSA_KV_W), F32),
                            pltpu.VMEM((nch, NSA_KV_W), F32), pltpu.VMEM((nch + 8, NSA_KV_W), F32)]),
        out_shape=[jax.ShapeDtypeStruct((b, nch, NSA_KV_W), F32)] * 2,
        compiler_params=_cparams(("arbitrary", "arbitrary")),
        name="compress",
    )(table, *([pages_k] * pps), *([pages_v] * pps), _pool_weights(w_k), _pool_weights(w_v),
      w_k.reshape(CMP_LEN, 1), pe_k, w_v.reshape(CMP_LEN, 1), pe_v, proj_k, proj_v, k_gain.reshape(1, HEAD_DIM))


def _prep_body(ks_ref, kw_ref, kg_ref, pks_ref, pkw_ref):
    for g in range(NSA_G):
        sl = slice(g * HEAD_DIM, (g + 1) * HEAD_DIM)
        pks_ref[0, :, sl] = _rms(ks_ref[0, :, sl], kg_ref[1:2, :])
        pkw_ref[0, :, sl] = _rms(kw_ref[0, :, sl], kg_ref[2:3, :])


def _prep_prompt_body(kc_ref, vc_ref, ks_ref, vs_ref, kw_ref, vw_ref, kg_ref,
                      okc_ref, ovc_ref, oks_ref, ovs_ref, okw_ref, ovw_ref, ks16_ref, vst_ref, kw16_ref, vwt_ref):
    tp = ks_ref.shape[1]

    def put(dst, g, x):
        dst[0, pl.ds(g, tp, stride=NSA_G), :] = x

    for g in range(NSA_G):
        sl = slice(g * HEAD_DIM, (g + 1) * HEAD_DIM)
        ksn = _rms(ks_ref[0, :, sl], kg_ref[1:2, :])
        kwn = _rms(kw_ref[0, :, sl], kg_ref[2:3, :])
        ks16_ref[0, :, sl] = ksn.astype(BF16)
        kw16_ref[0, :, sl] = kwn.astype(BF16)
        put(oks_ref, g, ksn)
        put(okw_ref, g, kwn)
        for src, dst in ((kc_ref, okc_ref), (vc_ref, ovc_ref), (vs_ref, ovs_ref), (vw_ref, ovw_ref)):
            put(dst, g, src[0, :, sl])
    for src, dst in ((vs_ref, vst_ref), (vw_ref, vwt_ref)):
        for j in range(dst.shape[1]):
            dst[0, j] = src[0, j * LANE:(j + 1) * LANE, :].T.astype(BF16)


def _nsa_prep_prompt(ub, k_norm_g, wb):
    b, t_len, _ = ub.shape
    tp = min(t_len, 512)
    assert wb % tp == 0 and t_len % tp == 0 and t_len >= wb
    first_win = (t_len - wb) // tp
    kv0 = UB_KV // NSA_KV_W
    spec = lambda c: pl.BlockSpec((1, tp, NSA_KV_W), lambda bi, i: (bi, i, c))
    o_spec = pl.BlockSpec((1, tp, NSA_KV_W), lambda bi, i: (bi, i, 0))
    il_spec = pl.BlockSpec((1, NSA_G * tp, HEAD_DIM), lambda bi, i: (bi, i, 0))
    win_spec = pl.BlockSpec((1, NSA_G * tp, HEAD_DIM), lambda bi, i: (bi, jnp.maximum(i - first_win, 0), 0))
    t_spec = pl.BlockSpec((1, tp // LANE, NSA_KV_W, LANE), lambda bi, i: (bi, i, 0, 0))
    il = jax.ShapeDtypeStruct((b, NSA_G * t_len, HEAD_DIM), F32)
    il_win = jax.ShapeDtypeStruct((b, NSA_G * wb, HEAD_DIM), F32)
    k16 = jax.ShapeDtypeStruct((b, t_len, NSA_KV_W), BF16)
    v16 = jax.ShapeDtypeStruct((b, t_len // LANE, NSA_KV_W, LANE), BF16)
    return pl.pallas_call(
        _prep_prompt_body,
        grid=(b, t_len // tp),
        in_specs=[spec(kv0 + c) for c in range(6)] + [pl.BlockSpec((3, HEAD_DIM), lambda bi, i: (0, 0))],
        out_specs=[il_spec] * 4 + [win_spec] * 2 + [o_spec, t_spec, o_spec, t_spec],
        out_shape=[il] * 4 + [il_win] * 2 + [k16, v16, k16, v16],
        compiler_params=_cparams(("arbitrary", "arbitrary")),
        name="nsa_prep_prompt",
    )(ub, ub, ub, ub, ub, ub, k_norm_g)


def _nsa_prep(ub, k_norm_g):
    b, t_len, _ = ub.shape
    tp = min(t_len, 512)
    kv0 = UB_KV // NSA_KV_W
    spec = lambda c: pl.BlockSpec((1, tp, NSA_KV_W), lambda bi, i: (bi, i, c))
    o_spec = pl.BlockSpec((1, tp, NSA_KV_W), lambda bi, i: (bi, i, 0))
    return pl.pallas_call(
        _prep_body,
        grid=(b, t_len // tp),
        in_specs=[spec(kv0 + 2), spec(kv0 + 4), pl.BlockSpec((3, HEAD_DIM), lambda bi, i: (0, 0))],
        out_specs=[o_spec] * 2,
        out_shape=[jax.ShapeDtypeStruct((b, t_len, NSA_KV_W), F32)] * 2,
        compiler_params=_cparams(("arbitrary", "arbitrary")),
        name="nsa_prep",
    )(ub, ub, k_norm_g)


def _sel_matrix(nc, nb, ncp, nbp):
    j = np.arange(nb)
    lo = np.clip((SEL_BLOCK * j - CMP_LEN) // CMP_STRIDE + 1, 0, nc)
    hi = np.clip(-(-(SEL_BLOCK * (j + 1)) // CMP_STRIDE), 0, nc)
    n = np.arange(ncp)[:, None]
    m = np.zeros((ncp, nbp), np.float32)
    m[:, :nb] = (n >= lo[None, :]) & (n < hi[None, :])
    return jnp.asarray(m)


def _cmp_body(q_ref, kc_ref, vc_ref, bias_ref, mselt_ref, qg_ref, gate_ref, ocmpt_ref, nsel_ref, *, tq, nb):
    qi = pl.program_id(1)
    nbp = -(-nb // 8) * 8
    gates_t = jax.nn.sigmoid(gate_ref[0]).T
    blk = lax.broadcasted_iota(jnp.int32, (nbp, tq), 0)
    qblk = (qi * tq + lax.broadcasted_iota(jnp.int32, (nbp, tq), 1)) >> 6
    scale = HEAD_DIM ** -0.5
    for g in range(NSA_G):
        gsl = slice(g * HEAD_DIM, (g + 1) * HEAD_DIM)
        kc = kc_ref[0, :, gsl]
        vct = vc_ref[0, :, gsl].T.astype(BF16)
        imp = jnp.zeros((kc.shape[0], tq), F32)
        heads = [g * NSA_HPG + hh for hh in range(NSA_HPG)]
        kc2 = _split2(kc)
        scores = []
        for h in heads:
            q2 = _split2(_rms(q_ref[0, :, h * HEAD_DIM:(h + 1) * HEAD_DIM], qg_ref[...]) * scale)
            nt = lambda a, b: lax.dot_general(a, b, (((1,), (1,)), ((), ())), preferred_element_type=F32)
            scores.append(nt(kc2[0], q2[0]) + (nt(kc2[0], q2[1]) + nt(kc2[1], q2[0])))
        probs = []
        for h, s in zip(heads, scores):
            bias = bias_ref[h]
            s = bias + s
            e = jnp.exp(s - jnp.max(s, axis=0, keepdims=True))
            p = e * (1.0 / jnp.sum(e, axis=0, keepdims=True)) * jnp.where(bias > 0.5 * NEG, 1.0, 0.0)
            probs.append(p)
            imp = imp + p
        for h, p in zip(heads, probs):
            hsl = slice(h * HEAD_DIM, (h + 1) * HEAD_DIM)
            ocmpt_ref[0, hsl, :] = gates_t[h:h + 1, :] * _dot16(vct, p.astype(BF16))
        score = _cumsum_rows(mselt_ref[0:nbp, :].astype(BF16), imp)
        score = jnp.where(blk == 0, BIG, score)
        score = jnp.where(blk == qblk, BIG, score)
        score = jnp.where(blk == qblk - 1, BIG, score)
        score = jnp.where(blk <= qblk, score, -BIG)
        rank = jnp.zeros((nbp, tq), F32)
        for i in range(nb):
            si = score[i:i + 1, :]
            ge = jnp.where(si >= score, 1.0, 0.0)
            gt = jnp.where(si > score, 1.0, 0.0)
            rank = rank + jnp.where(blk > i, ge, gt)
        nsel_t = jnp.where(rank < SEL_TOP, 0.0, jnp.where(blk < nb, 1.0, 0.0))
        nsel = jnp.concatenate([nsel_t, jnp.zeros((LANE - nbp, tq), F32)], axis=0).T
        nsel_ref[0, :, gsl] = nsel.astype(BF16)


def _cmp_select_prompt(ub, kcmp, vcmp, bias_cmp, q_norm_g):
    b, t_len, _ = ub.shape
    tq = 256
    nb = t_len // SEL_BLOCK
    ncp = kcmp.shape[1]
    nc = (t_len - CMP_LEN) // CMP_STRIDE + 1
    assert nb <= LANE and ncp == LANE
    body = functools.partial(_cmp_body, tq=tq, nb=nb)
    return pl.pallas_call(
        body,
        grid=(b, t_len // tq),
        in_specs=[
            pl.BlockSpec((1, tq, NSA_Q_W), lambda bi, i: (bi, i, UB_Q // NSA_Q_W)),
            pl.BlockSpec((1, ncp, NSA_KV_W), lambda bi, i: (bi, 0, 0)),
            pl.BlockSpec((1, ncp, NSA_KV_W), lambda bi, i: (bi, 0, 0)),
            pl.BlockSpec((NSA_HEADS, ncp, tq), lambda bi, i: (0, 0, i)),
            pl.BlockSpec((LANE, ncp), lambda bi, i: (0, 0)),
            pl.BlockSpec((1, HEAD_DIM), lambda bi, i: (0, 0)),
            pl.BlockSpec((1, tq, LANE), lambda bi, i: (bi, i, UB_G // LANE)),
        ],
        out_specs=[
            pl.BlockSpec((1, NSA_Q_W, tq), lambda bi, i: (bi, 0, i)),
            pl.BlockSpec((1, tq, NSA_G * LANE), lambda bi, i: (bi, i, 0)),
        ],
        out_shape=[jax.ShapeDtypeStruct((b, NSA_Q_W, t_len), F32),
                   jax.ShapeDtypeStruct((b, t_len, NSA_G * LANE), BF16)],
        compiler_params=_cparams(("arbitrary", "arbitrary")),
        name="cmp_select",
    )(ub, kcmp, vcmp, jnp.swapaxes(bias_cmp, 1, 2), _sel_matrix(nc, nb, ncp, LANE).T, q_norm_g.reshape(1, HEAD_DIM), ub)


def _attn_body(q_ref, ocmp_ref, nsel_ref, ks_ref, vst_ref, kw_ref, vwt_ref, epen_ref, bt_ref, cb_ref, gate_ref, z_ref, qg_ref,
               o_ref, m_s, l_s, acc_s, osw_s, *, tq):
    qi = pl.program_id(1)
    t0 = qi * tq
    rows = NSA_HPG * tq
    tk = LANE
    def rel_of(kb, width):
        tok = t0 + (lax.broadcasted_iota(jnp.int32, (width * tk, rows), 1) & (tq - 1))
        return tok - (kb + lax.broadcasted_iota(jnp.int32, (width * tk, rows), 0))

    gates_t = jax.nn.sigmoid(gate_ref[0]).T
    scale = HEAD_DIM ** -0.5
    n_near = bt_ref.shape[0]
    j_near = jnp.maximum(qi + 1 - n_near, 0)

    groups = range(NSA_G)
    gsl = [slice(g * HEAD_DIM, (g + 1) * HEAD_DIM) for g in groups]
    heads = [[g * NSA_HPG + hh for hh in range(NSA_HPG)] for g in groups]
    q_sel, q_win = [], []
    for g in groups:
        q4 = jnp.concatenate(
            [(_rms(q_ref[0, :, h * HEAD_DIM:(h + 1) * HEAD_DIM], qg_ref[...]) * scale).astype(BF16) for h in heads[g]], axis=0)
        ns = nsel_ref[0, :, g * LANE:(g + 1) * LANE].astype(F32)
        aug_sel = jnp.concatenate([ns + cb_ref[h:h + 1, :] for h in heads[g]], axis=0).astype(BF16)
        aug_win = jnp.concatenate([jnp.broadcast_to(cb_ref[h:h + 1, :], (tq, LANE)) for h in heads[g]], axis=0).astype(BF16)
        q_sel.append(jnp.concatenate([q4, aug_sel], axis=1))
        q_win.append(jnp.concatenate([q4, aug_win], axis=1))

    def run(k_ref, vt_ref, qmats, lo, window, far_widths):
        m_s[...] = jnp.full(m_s.shape, NEG, F32)
        l_s[...] = jnp.zeros(l_s.shape, F32)
        acc_s[...] = jnp.zeros(acc_s.shape, F32)

        def step(j, near, width=1):
            kb = pl.multiple_of(j * tk, tk)
            kt = k_ref[0, pl.ds(kb, width * tk), :]
            ep = epen_ref[pl.ds(kb, width * tk), :]
            ss = [lax.dot_general(jnp.concatenate([kt[:, gsl[g]], ep], axis=1), qmats[g], (((1,), (1,)), ((), ())),
                                  preferred_element_type=F32) for g in groups]
            rel = rel_of(kb, width)
            vt = vt_ref[0, j] if width == 1 else jnp.concatenate([vt_ref[0, j + i] for i in range(width)], axis=1)
            for g in groups:
                s = ss[g]
                if near:
                    bias = [bt_ref[width - 1 - i, :, g * rows:(g + 1) * rows] for i in range(width)]
                    s = jnp.where(rel >= 0, (bias[0] if width == 1 else jnp.concatenate(bias, axis=0)) + s, NEG)
                elif window:
                    s = jnp.where(rel < WINDOW, s, NEG)
                m_old = m_s[g]
                m_new = jnp.maximum(m_old, jnp.max(s, axis=0, keepdims=True))
                alpha = jnp.exp(m_old - m_new)
                p = jnp.exp(s - m_new)
                l_s[g] = alpha * l_s[g] + jnp.sum(p, axis=0, keepdims=True)
                acc_s[g] = alpha * acc_s[g] + jnp.dot(vt[gsl[g], :], p.astype(BF16), preferred_element_type=F32)
                m_s[g] = m_new

        start = lo
        for width in far_widths:
            def far_step(i, carry, start=start, width=width):
                step(start + width * i, False, width)
                return carry

            n = jnp.maximum(j_near - start, 0) // width
            lax.fori_loop(0, n, far_step, 0)
            start = start + width * n

        @pl.when(qi >= n_near - 1)
        def _():
            step(qi + 1 - n_near, True, n_near)

        for short in range(1, n_near):
            @pl.when(qi == short - 1)
            def _(short=short):
                step(0, True, short)

    def gate_rows(branch, g):
        return jnp.concatenate([gates_t[branch * NSA_HEADS + h:branch * NSA_HEADS + h + 1, :] for h in heads[g]], axis=1)

    run(ks_ref, vst_ref, q_sel, 0, False, (4, 2, 1))
    for g in groups:
        osw_s[g] = acc_s[g] * (gate_rows(1, g) / l_s[g])
    run(kw_ref, vwt_ref, q_win, jnp.maximum(qi - WINDOW // tk, 0), True, (2, 1))
    for g in groups:
        osw = osw_s[g] + acc_s[g] * (gate_rows(2, g) / l_s[g])
        for hh, h in enumerate(heads[g]):
            hsl = slice(h * HEAD_DIM, (h + 1) * HEAD_DIM)
            o = (ocmp_ref[0, hsl, :] + osw[:, hh * tq:(hh + 1) * tq]).T
            o_ref[0, :, hsl] = (o * _silu(z_ref[0, :, hsl])).astype(o_ref.dtype)


def _far_bias(rel_bias):
    return _bias_of(rel_bias, jnp.full((1,), REL_MAX_DIST, jnp.int32))[:, 0]


def _toeplitz_bias_t(rel_bias, tq, n_near):
    d = jnp.arange(n_near, dtype=jnp.int32)[:, None, None]
    c = jnp.arange(LANE, dtype=jnp.int32)[None, :, None]
    r = jnp.arange(tq, dtype=jnp.int32)[None, None, :]
    bt = _bias_of(rel_bias, d * LANE + r - c) - _far_bias(rel_bias)[:, None, None, None]
    return jnp.transpose(bt, (1, 2, 0, 3)).reshape(n_near, LANE, NSA_HEADS * tq)


AUG_BIAS_LANES = (LANE - 2, LANE - 1)


def _far_bias_columns(rel_bias):
    c = _far_bias(rel_bias)
    hi = c.astype(BF16).astype(F32)
    lo = (c - hi).astype(BF16).astype(F32)
    return jnp.zeros((NSA_HEADS, LANE), F32).at[:, AUG_BIAS_LANES[0]].set(hi).at[:, AUG_BIAS_LANES[1]].set(lo)


def _block_penalty(t_len):
    key = np.arange(t_len)[:, None]
    j = np.arange(LANE)[None, :]
    pen = np.where(key // SEL_BLOCK == j, NEG, 0.0)
    pen[:, AUG_BIAS_LANES[0]:] = 1.0
    return jnp.asarray(pen, BF16)


def _attn_prompt(ub, ocmp, nsel, ksb, vst, kwb, vwt, rel_bias, q_norm_g):
    b, t_len, _ = ub.shape
    tq = 128
    rows = NSA_HPG * tq
    assert t_len // SEL_BLOCK <= AUG_BIAS_LANES[0]
    n_near = -(-(REL_MAX_DIST + LANE - 1) // LANE)
    bt = _toeplitz_bias_t(rel_bias, tq, n_near)
    body = functools.partial(_attn_body, tq=tq)
    k_spec = pl.BlockSpec((1, t_len, NSA_KV_W), lambda bi, i: (bi, 0, 0))
    v_spec = pl.BlockSpec((1, t_len // LANE, NSA_KV_W, LANE), lambda bi, i: (bi, 0, 0, 0))
    return pl.pallas_call(
        body,
        grid=(b, t_len // tq),
        in_specs=[
            pl.BlockSpec((1, tq, NSA_Q_W), lambda bi, i: (bi, i, UB_Q // NSA_Q_W)),
            pl.BlockSpec((1, NSA_Q_W, tq), lambda bi, i: (bi, 0, i)),
            pl.BlockSpec((1, tq, NSA_G * LANE), lambda bi, i: (bi, i, 0)),
            k_spec, v_spec, k_spec, v_spec,
            pl.BlockSpec((t_len, LANE), lambda bi, i: (0, 0)),
            pl.BlockSpec(bt.shape, lambda bi, i: (0, 0, 0)),
            pl.BlockSpec((NSA_HEADS, LANE), lambda bi, i: (0, 0)),
            pl.BlockSpec((1, tq, LANE), lambda bi, i: (bi, i, UB_G // LANE)),
            pl.BlockSpec((1, tq, NSA_Q_W), lambda bi, i: (bi, i, UB_Z // NSA_Q_W)),
            pl.BlockSpec((1, HEAD_DIM), lambda bi, i: (0, 0)),
        ],
        out_specs=pl.BlockSpec((1, tq, NSA_Q_W), lambda bi, i: (bi, i, 0)),
        out_shape=jax.ShapeDtypeStruct((b, t_len, NSA_Q_W), BF16),
        scratch_shapes=[pltpu.VMEM((NSA_G, 1, rows), F32), pltpu.VMEM((NSA_G, 1, rows), F32),
                        pltpu.VMEM((NSA_G, HEAD_DIM, rows), F32), pltpu.VMEM((NSA_G, HEAD_DIM, rows), F32)],
        compiler_params=_cparams(("arbitrary", "arbitrary")),
        name="nsa_attn",
    )(ub, ocmp, nsel, ksb, vst, kwb, vwt, _block_penalty(t_len), bt, _far_bias_columns(rel_bias), ub, ub,
      q_norm_g.reshape(1, HEAD_DIM))


def _q_all(q_ref, qg_ref, ts):
    scale = HEAD_DIM ** -0.5
    zero = jnp.zeros((NSA_HPG * ts, HEAD_DIM), BF16)
    blocks = []
    for g in range(NSA_G):
        q4 = jnp.concatenate(
            [(_rms(q_ref[0, :, (g * NSA_HPG + hh) * HEAD_DIM:(g * NSA_HPG + hh + 1) * HEAD_DIM], qg_ref[...]) * scale).astype(BF16)
             for hh in range(NSA_HPG)], axis=0)
        blocks.append(jnp.concatenate([q4 if gg == g else zero for gg in range(NSA_G)], axis=1))
    return jnp.concatenate(blocks, axis=0)


def _row_to_col(row, n):
    eye = lax.broadcasted_iota(jnp.int32, (n, n), 0) == lax.broadcasted_iota(jnp.int32, (n, n), 1)
    return jnp.sum(jnp.where(eye, row, 0.0), axis=1, keepdims=True)


def _cmp_sample_body(q_ref, kc_ref, vc_ref, bias_ref, msel_ref, rep_ref, qg_ref, ocmp_ref, selt_ref, *, ts, nb, nbp):
    scale = HEAD_DIM ** -0.5
    ncp = kc_ref.shape[1]
    scores = []
    for g in range(NSA_G):
        gsl = slice(g * HEAD_DIM, (g + 1) * HEAD_DIM)
        q4 = jnp.concatenate(
            [_rms(q_ref[0, :, (g * NSA_HPG + hh) * HEAD_DIM:(g * NSA_HPG + hh + 1) * HEAD_DIM], qg_ref[...]) * scale
             for hh in range(NSA_HPG)], axis=0)
        bias = bias_ref[g * NSA_HPG:(g + 1) * NSA_HPG].reshape(NSA_HPG * ts, ncp)
        s = _mm_nt_hi(q4, kc_ref[0, :, gsl]) + bias
        e = jnp.exp(s - jnp.max(s, axis=-1, keepdims=True))
        p = e / jnp.sum(e, axis=-1, keepdims=True) * jnp.where(bias > 0.5 * NEG, 1.0, 0.0)
        o = _mm(p, vc_ref[0, :, gsl])
        imp = p[0:ts]
        for hh in range(NSA_HPG):
            h = g * NSA_HPG + hh
            ocmp_ref[0, :, h * HEAD_DIM:(h + 1) * HEAD_DIM] = o[hh * ts:(hh + 1) * ts]
            if hh:
                imp = imp + p[hh * ts:(hh + 1) * ts]
        scores.append(_mm_hi(imp, msel_ref[...]))
    score = jnp.concatenate(scores, axis=0)
    rows = NSA_G * ts
    lane = lax.broadcasted_iota(jnp.int32, (rows, nbp), 1)
    tok = lax.broadcasted_iota(jnp.int32, (rows, nbp), 0) & (ts - 1)
    qblk = (PAST_LEN + tok) >> 6
    forced = (lane == 0) | (lane == qblk) | (lane == qblk - 1)
    score = jnp.where(forced, BIG, score)
    score = jnp.where(lane <= qblk, score, -BIG)
    sel = jnp.zeros((rows, nbp), F32)
    lane_f = lane.astype(F32)
    for _ in range(SEL_TOP):
        mx = jnp.max(score, axis=-1, keepdims=True)
        first = jnp.min(jnp.where(score == mx, lane_f, float(nbp)), axis=-1, keepdims=True)
        pick = lane_f == first
        sel = jnp.where(pick, 1.0, sel)
        score = jnp.where(pick, -3e38, score)
    selt_ref[0] = _mm_tn(sel, rep_ref[...])


def _cmp_select_sample(ub, kcmp, vcmp, bias_cmp, q_norm_g, nb):
    b, ts, _ = ub.shape
    ncp = kcmp.shape[1]
    nc = ncp - 1
    nbp = -(-nb // LANE) * LANE
    assert ts & (ts - 1) == 0 and NSA_HEADS * ts == LANE
    rep = np.zeros((NSA_G * ts, LANE), np.float32)
    for g in range(NSA_G):
        for hh in range(NSA_HPG):
            for t in range(ts):
                rep[g * ts + t, (g * NSA_HPG + hh) * ts + t] = 1.0
    body = functools.partial(_cmp_sample_body, ts=ts, nb=nb, nbp=nbp)
    return pl.pallas_call(
        body,
        grid=(b,),
        in_specs=[
            pl.BlockSpec((1, ts, NSA_Q_W), lambda bi: (bi, 0, UB_Q // NSA_Q_W)),
            pl.BlockSpec((1, ncp, NSA_KV_W), lambda bi: (bi, 0, 0)),
            pl.BlockSpec((1, ncp, NSA_KV_W), lambda bi: (bi, 0, 0)),
            pl.BlockSpec((NSA_HEADS, ts, ncp), lambda bi: (0, 0, 0)),
            pl.BlockSpec((ncp, nbp), lambda bi: (0, 0)),
            pl.BlockSpec((NSA_G * ts, LANE), lambda bi: (0, 0)),
            pl.BlockSpec((1, HEAD_DIM), lambda bi: (0, 0)),
        ],
        out_specs=[
            pl.BlockSpec((1, ts, NSA_Q_W), lambda bi: (bi, 0, 0)),
            pl.BlockSpec((1, nbp, LANE), lambda bi: (bi, 0, 0)),
        ],
        out_shape=[jax.ShapeDtypeStruct((b, ts, NSA_Q_W), F32), jax.ShapeDtypeStruct((b, nbp, LANE), F32)],
        compiler_params=_cparams(("arbitrary",)),
        name="cmp_select_sample",
    )(ub, kcmp, vcmp, bias_cmp, _sel_matrix(nc, nb, ncp, nbp), jnp.asarray(rep), q_norm_g.reshape(1, HEAD_DIM))


def _sel_sample_body(pt_ref, *refs, pps, n_steps, ts):
    del pt_ref
    k_refs = refs[:pps]
    v_refs = refs[pps:2 * pps]
    (q_ref, selt_ref, knew_ref, vnew_ref, blast_ref, bfar_ref, bnew_ref, qg_ref,
     o_ref, qall_s, sc_s, snew_s, m_s, l_s, acc_s) = refs[2 * pps:]
    s = pl.program_id(1)
    n_pages = n_steps * pps
    sub = lax.broadcasted_iota(jnp.int32, (PAGE, LANE), 0)
    tok = lax.broadcasted_iota(jnp.int32, (ts, LANE), 1) & (ts - 1)
    new_t = lax.broadcasted_iota(jnp.int32, (ts, LANE), 0)

    @pl.when(s == 0)
    def _():
        qall = _q_all(q_ref, qg_ref, ts)
        qall_s[...] = qall
        sn = _mm_nt(knew_ref[0], qall) + bnew_ref[...]
        ok = (new_t <= tok) & (selt_ref[0, 2 * n_pages:2 * n_pages + 1, :] > 0.5)
        sn = jnp.where(ok, sn, NEG)
        snew_s[...] = sn
        m_s[...] = jnp.max(sn, axis=0, keepdims=True)
        l_s[...] = jnp.zeros((1, LANE), F32)
        acc_s[...] = jnp.zeros((LANE, NSA_KV_W), F32)

    @pl.when(s < n_steps)
    def _():
        m = m_s[...]
        for j in range(pps):
            p = s * pps + j
            st = _mm_nt(_load_page(k_refs[j], True), qall_s[...])
            st = st + jnp.where(p == n_pages - 1, blast_ref[...], bfar_ref[...])
            r0 = selt_ref[0, pl.ds(2 * p, 1), :]
            r1 = selt_ref[0, pl.ds(2 * p + 1, 1), :]
            ok = jnp.where(sub < SEL_BLOCK, r0, r1) > 0.5
            st = jnp.where(ok, st, NEG)
            sc_s[pl.ds(pl.multiple_of(p * PAGE, PAGE), PAGE), :] = st
            m = jnp.maximum(m, jnp.max(st, axis=0, keepdims=True))
        m_s[...] = m

    @pl.when(s >= n_steps)
    def _():
        m = m_s[...]
        l = l_s[...]
        acc = acc_s[...]
        for j in range(pps):
            p = (s - n_steps) * pps + j
            pt = jnp.exp(sc_s[pl.ds(pl.multiple_of(p * PAGE, PAGE), PAGE), :] - m)
            l = l + jnp.sum(pt, axis=0, keepdims=True)
            acc = acc + _mm_tn(pt, _load_page(v_refs[j], True))
        l_s[...] = l
        acc_s[...] = acc

    @pl.when(s == 2 * n_steps - 1)
    def _():
        pn = jnp.exp(snew_s[...] - m_s[...])
        l = l_s[...] + jnp.sum(pn, axis=0, keepdims=True)
        acc = acc_s[...] + _mm_tn(pn, vnew_ref[0])
        out = acc / _row_to_col(l, LANE)
        for h in range(NSA_HEADS):
            g = h // NSA_HPG
            o_ref[0, :, h * HEAD_DIM:(h + 1) * HEAD_DIM] = out[h * ts:(h + 1) * ts, g * HEAD_DIM:(g + 1) * HEAD_DIM]


def _lane_bias(rel_bias, key_pos, ts):
    rel = (PAST_LEN + jnp.arange(ts, dtype=jnp.int32))[None, :] - key_pos[:, None]
    return jnp.moveaxis(_bias_of(rel_bias, rel), 0, 1).reshape(key_pos.shape[0], NSA_HEADS * ts)


def _sel_sample(ub, selt, cache_k, cache_v, table, k_new, rel_bias, q_norm_g):
    b, ts, _ = ub.shape
    n_pages = table.shape[1]
    assert PAGE >= REL_MAX_DIST and n_pages * PAGE == PAST_LEN
    pps = min(PAGES_PER_STEP, n_pages)
    n_steps = n_pages // pps
    ar = jnp.arange
    blast = _lane_bias(rel_bias, PAST_LEN - PAGE + ar(PAGE, dtype=jnp.int32), ts)
    bfar = _lane_bias(rel_bias, jnp.zeros((1,), jnp.int32), ts)
    bnew = _lane_bias(rel_bias, PAST_LEN + ar(ts, dtype=jnp.int32), ts)
    body = functools.partial(_sel_sample_body, pps=pps, n_steps=n_steps, ts=ts)
    last = n_steps - 1
    page_block = (1, NSA_G * PAGE, HEAD_DIM)
    k_spec = lambda j: pl.BlockSpec(page_block, lambda bi, s, pt: (pt[bi, jnp.minimum(s, last) * pps + j], 0, 0))
    v_spec = lambda j: pl.BlockSpec(page_block, lambda bi, s, pt: (pt[bi, jnp.maximum(s - n_steps, 0) * pps + j], 0, 0))
    full = lambda shape: pl.BlockSpec(shape, lambda bi, s, pt: (0,) * len(shape))
    in_specs = [k_spec(j) for j in range(pps)] + [v_spec(j) for j in range(pps)] + [
        pl.BlockSpec((1, ts, NSA_Q_W), lambda bi, s, pt: (bi, 0, UB_Q // NSA_Q_W)),
        pl.BlockSpec((1,) + selt.shape[1:], lambda bi, s, pt: (bi, 0, 0)),
        pl.BlockSpec((1, ts, NSA_KV_W), lambda bi, s, pt: (bi, 0, 0)),
        pl.BlockSpec((1, ts, NSA_KV_W), lambda bi, s, pt: (bi, 0, UB_KV // NSA_KV_W + 3)),
        full((PAGE, LANE)), full((1, LANE)), full((ts, LANE)), full((1, HEAD_DIM)),
    ]
    return pl.pallas_call(
        body,
        grid_spec=pltpu.PrefetchScalarGridSpec(
            num_scalar_prefetch=1, grid=(b, 2 * n_steps), in_specs=in_specs,
            out_specs=pl.BlockSpec((1, ts, NSA_Q_W), lambda bi, s, pt: (bi, 0, 0)),
            scratch_shapes=[pltpu.VMEM((LANE, NSA_KV_W), BF16), pltpu.VMEM((PAST_LEN, LANE), F32),
                            pltpu.VMEM((ts, LANE), F32), pltpu.VMEM((1, LANE), F32), pltpu.VMEM((1, LANE), F32),
                            pltpu.VMEM((LANE, NSA_KV_W), F32)]),
        out_shape=jax.ShapeDtypeStruct((b, ts, NSA_Q_W), F32),
        compiler_params=_cparams(("arbitrary", "arbitrary")),
        name="sel_sample",
    )(table, *([cache_k] * pps), *([cache_v] * pps), ub, selt, k_new, ub, blast, bfar, bnew, q_norm_g.reshape(1, HEAD_DIM))


def _win_sample_body(q_ref, kc_ref, vc_ref, knew_ref, vnew_ref, bc_ref, bn_ref, gate_ref, z_ref, ocmp_ref, osel_ref, qg_ref,
                     o_ref, kout_ref, vout_ref, *, ts, wb):
    qall = _q_all(q_ref, qg_ref, ts)
    key_i = lax.broadcasted_iota(jnp.int32, (wb, LANE), 0)
    tok_c = lax.broadcasted_iota(jnp.int32, (wb, LANE), 1) & (ts - 1)
    tok_n = lax.broadcasted_iota(jnp.int32, (ts, LANE), 1) & (ts - 1)
    new_t = lax.broadcasted_iota(jnp.int32, (ts, LANE), 0)
    sc = _mm_nt(_load_page(kc_ref, True), qall) + bc_ref[...]
    sc = jnp.where(wb + tok_c - key_i < WINDOW, sc, NEG)
    sn = _mm_nt(knew_ref[0], qall) + bn_ref[...]
    sn = jnp.where(new_t <= tok_n, sn, NEG)
    m = jnp.maximum(jnp.max(sc, axis=0, keepdims=True), jnp.max(sn, axis=0, keepdims=True))
    pc = jnp.exp(sc - m)
    pn = jnp.exp(sn - m)
    l = jnp.sum(pc, axis=0, keepdims=True) + jnp.sum(pn, axis=0, keepdims=True)
    out = (_mm_tn(pc, _load_page(vc_ref, True)) + _mm_tn(pn, vnew_ref[0])) / _row_to_col(l, LANE)
    gates = jax.nn.sigmoid(gate_ref[0])
    for h in range(NSA_HEADS):
        g = h // NSA_HPG
        hsl = slice(h * HEAD_DIM, (h + 1) * HEAD_DIM)
        o_win = out[h * ts:(h + 1) * ts, g * HEAD_DIM:(g + 1) * HEAD_DIM]
        o = (gates[:, h:h + 1] * ocmp_ref[0, :, hsl] + gates[:, NSA_HEADS + h:NSA_HEADS + h + 1] * osel_ref[0, :, hsl]
             + gates[:, 2 * NSA_HEADS + h:2 * NSA_HEADS + h + 1] * o_win)
        o_ref[0, :, hsl] = (o * _silu(z_ref[0, :, hsl])).astype(o_ref.dtype)
    keep = (wb - ts) * NSA_G
    for out_ref, old_ref, new_ref in ((kout_ref, kc_ref, knew_ref), (vout_ref, vc_ref, vnew_ref)):
        out_ref[0, 0:keep, :] = old_ref[0, ts * NSA_G:wb * NSA_G, :]
        for g in range(NSA_G):
            out_ref[0, pl.ds(keep + g, ts, stride=NSA_G), :] = new_ref[0, :, g * HEAD_DIM:(g + 1) * HEAD_DIM]


def _win_sample(ub, cache_k, cache_v, k_new, ocmp, osel, rel_bias, q_norm_g):
    b, ts, _ = ub.shape
    wb = cache_k.shape[1] // NSA_G
    assert wb == WINDOW and ts % 8 == 0
    bc = _lane_bias(rel_bias, PAST_LEN - wb + jnp.arange(wb, dtype=jnp.int32), ts)
    bn = _lane_bias(rel_bias, PAST_LEN + jnp.arange(ts, dtype=jnp.int32), ts)
    body = functools.partial(_win_sample_body, ts=ts, wb=wb)
    tok_spec = lambda w, c: pl.BlockSpec((1, ts, w), lambda bi: (bi, 0, c))
    win_spec = pl.BlockSpec((1, wb * NSA_G, HEAD_DIM), lambda bi: (bi, 0, 0))
    return pl.pallas_call(
        body,
        grid=(b,),
        in_specs=[
            tok_spec(NSA_Q_W, UB_Q // NSA_Q_W), win_spec, win_spec,
            tok_spec(NSA_KV_W, 0), tok_spec(NSA_KV_W, UB_KV // NSA_KV_W + 5),
            pl.BlockSpec((wb, LANE), lambda bi: (0, 0)), pl.BlockSpec((ts, LANE), lambda bi: (0, 0)),
            tok_spec(LANE, UB_G // LANE), tok_spec(NSA_Q_W, UB_Z // NSA_Q_W),
            tok_spec(NSA_Q_W, 0), tok_spec(NSA_Q_W, 0),
            pl.BlockSpec((1, HEAD_DIM), lambda bi: (0, 0)),
        ],
        out_specs=[tok_spec(NSA_Q_W, 0), win_spec, win_spec],
        out_shape=[jax.ShapeDtypeStruct((b, ts, NSA_Q_W), BF16),
                   jax.ShapeDtypeStruct((b, wb * NSA_G, HEAD_DIM), F32),
                   jax.ShapeDtypeStruct((b, wb * NSA_G, HEAD_DIM), F32)],
        compiler_params=_cparams(("arbitrary",)),
        name="win_sample",
    )(ub, cache_k, cache_v, k_new, ub, bc, bn, ub, ub, ocmp, osel, q_norm_g.reshape(1, HEAD_DIM))


def _mix_body(oa_ref, ob_ref, wa_ref, wb_ref, ma_ref, mb_ref, o_ref):
    pa = jnp.dot(oa_ref[...], wa_ref[...], preferred_element_type=F32)
    pb = jnp.dot(ob_ref[...], wb_ref[...], preferred_element_type=F32)
    o_ref[...] = (jax.nn.sigmoid(ma_ref[...]) * pa + jax.nn.sigmoid(mb_ref[...]) * pb).astype(o_ref.dtype)


def _out_body(m_ref, w_ref, x_ref, y_ref):
    y_ref[...] = x_ref[...] + jnp.dot(m_ref[...], w_ref[...], preferred_element_type=F32)


def _output(x2d, oa, ob, ub2d, wa16, wb16, wo16):
    m = x2d.shape[0]
    tm = min(m, 1024)
    tn = 512
    grid = (m // tm, D_MODEL // tn)
    mixed = pl.pallas_call(
        _mix_body,
        grid=grid,
        in_specs=[
            pl.BlockSpec((tm, GDN_V_W), lambda i, j: (i, 0)),
            pl.BlockSpec((tm, NSA_Q_W), lambda i, j: (i, 0)),
            pl.BlockSpec((GDN_V_W, tn), lambda i, j: (0, j)),
            pl.BlockSpec((NSA_Q_W, tn), lambda i, j: (0, j)),
            pl.BlockSpec((tm, tn), lambda i, j: (i, UB_MA // tn + j)),
            pl.BlockSpec((tm, tn), lambda i, j: (i, UB_MB // tn + j)),
        ],
        out_specs=pl.BlockSpec((tm, tn), lambda i, j: (i, j)),
        out_shape=jax.ShapeDtypeStruct((m, D_MODEL), BF16),
        compiler_params=_cparams(("arbitrary", "arbitrary")),
        name="branch_mix",
    )(oa, ob, wa16, wb16, ub2d, ub2d)
    return pl.pallas_call(
        _out_body,
        grid=grid,
        in_specs=[
            pl.BlockSpec((tm, D_MODEL), lambda i, j: (i, 0)),
            pl.BlockSpec((D_MODEL, tn), lambda i, j: (0, j)),
            pl.BlockSpec((tm, tn), lambda i, j: (i, j)),
        ],
        out_specs=pl.BlockSpec((tm, tn), lambda i, j: (i, j)),
        out_shape=jax.ShapeDtypeStruct((m, D_MODEL), F32),
        compiler_params=_cparams(("arbitrary", "arbitrary")),
        name="out_proj",
    )(mixed, wo16, x2d)


def _kv_slices(ub):
    b, t_len, _ = ub.shape
    out = []
    for i in (0, 1, 3, 5):
        a = ub[:, :, UB_KV + i * NSA_KV_W:UB_KV + (i + 1) * NSA_KV_W]
        out.append(a.reshape(b, t_len, NSA_G, HEAD_DIM))
    return out


def kernel(x_prompt, x_sample, cache_k_cmp, cache_v_cmp, cache_k_sel, cache_v_sel, cache_k_win, cache_v_win, state_conv, state_gdn, page_table, norm_g, w_in, gdn_conv_w, gdn_a_log, gdn_dt_bias, gdn_norm_g, q_norm_g, k_norm_g, cmp_pe_k, cmp_w_k, cmp_proj_k, cmp_pe_v, cmp_w_v, cmp_proj_v, rel_bias, w_branch_a, w_branch_b, w_out):
    bp, tp, _ = x_prompt.shape
    bs, ts, _ = x_sample.shape
    wb = cache_k_win.shape[1]
    n_pool = cache_k_cmp.shape[0]
    kv4 = lambda a: a.reshape(a.shape[0], a.shape[1], NSA_G, HEAD_DIM)
    cmp_p = (cmp_pe_k, cmp_w_k, cmp_proj_k, cmp_pe_v, cmp_w_v, cmp_proj_v)

    assert _IN_MB + D_MODEL == w_in.shape[1]
    w_t = w_in.T
    wa16, wb16, wo16 = w_branch_a.astype(BF16), w_branch_b.astype(BF16), w_out.astype(BF16)

    def project(x):
        x2d = x.reshape(-1, D_MODEL)
        ua_dtype = BF16 if x.shape[1] >= 2 * GDN_CHUNK else F32
        ua = _project(x2d, norm_g, w_t, UA_OFFSETS, ua_dtype).reshape(x.shape[0], x.shape[1], UA_W)
        ub = _project(x2d, norm_g, w_t, UB_OFFSETS).reshape(x.shape[0], x.shape[1], UB_W)
        return x2d, ua, ub

    x2d, ua, ub = project(x_prompt)
    conv0 = jnp.zeros((bp, GDN_CONV - 1, GDN_QKV_W), F32)
    s0 = jnp.zeros((bp, GDN_V_HEADS, HEAD_DIM, HEAD_DIM), F32)
    o_a, p_gdn = _gdn(ua, ub, conv0, s0, gdn_conv_w, gdn_a_log, gdn_dt_bias, gdn_norm_g)
    p_conv = ua[:, tp - (GDN_CONV - 1):, :GDN_QKV_W].astype(F32)
    n_pages_p = tp // PAGE
    pages = ub.reshape(bp * n_pages_p, PAGE, UB_W)
    ident = jnp.arange(bp * n_pages_p, dtype=jnp.int32).reshape(bp, n_pages_p)
    kcmp, vcmp = _compress(pages, pages, UB_KV // NSA_KV_W, UB_KV // NSA_KV_W + 1, ident, *cmp_p, k_norm_g[0])
    p_kc, p_vc, p_ks, p_vs, p_kw, p_vw, ksb, vsb, kwb, vwb = _nsa_prep_prompt(ub, k_norm_g, wb)
    nc_p = (tp - CMP_LEN) // CMP_STRIDE + 1
    bias_cmp = _cmp_bias(rel_bias, jnp.arange(tp, dtype=jnp.int32), kcmp.shape[1], nc_p)
    ocmp, nsel = _cmp_select_prompt(ub, kcmp, vcmp, bias_cmp, q_norm_g)
    o_b = _attn_prompt(ub, ocmp, nsel, ksb, vsb, kwb, vwb, rel_bias, q_norm_g)
    y_prompt = _output(x2d, o_a.reshape(-1, GDN_V_W), o_b.reshape(-1, NSA_Q_W), ub.reshape(-1, UB_W), wa16, wb16, wo16)
    cache4 = lambda a: a.reshape(bp, a.shape[1] // NSA_G, NSA_G, HEAD_DIM)
    prompt_out = (y_prompt.reshape(x_prompt.shape), cache4(p_kc), cache4(p_vc), cache4(p_ks), cache4(p_vs), cache4(p_kw),
                  cache4(p_vw), p_conv, p_gdn)

    x2d, ua, ub = project(x_sample)
    o_a, s_gdn = _gdn(ua, ub, state_conv, state_gdn, gdn_conv_w, gdn_a_log, gdn_dt_bias, gdn_norm_g)
    s_conv = ua[:, ts - (GDN_CONV - 1):, :GDN_QKV_W]
    pool = lambda c: c.reshape(n_pool, PAGE * NSA_G, HEAD_DIM)
    kcmp, vcmp = _compress(pool(cache_k_cmp), pool(cache_v_cmp), 0, 0, page_table, *cmp_p, k_norm_g[0])
    s_ks, s_kwn = _nsa_prep(ub, k_norm_g)
    total = PAST_LEN + ts
    nc_s = (total - CMP_LEN) // CMP_STRIDE + 1
    nb_s = -(-total // SEL_BLOCK)
    assert nc_s + 1 == kcmp.shape[1]
    pos_s = PAST_LEN + jnp.arange(ts, dtype=jnp.int32)
    bias_cmp = _cmp_bias(rel_bias, pos_s, kcmp.shape[1], nc_s)
    ocmp, selt = _cmp_select_sample(ub, kcmp, vcmp, bias_cmp, q_norm_g, nb_s)
    osel = _sel_sample(ub, selt, pool(cache_k_sel), pool(cache_v_sel), page_table, s_ks, rel_bias, q_norm_g)
    win3 = lambda c: c.reshape(bs, wb * NSA_G, HEAD_DIM)
    o_b, s_kw, s_vw = _win_sample(ub, win3(cache_k_win), win3(cache_v_win), s_kwn, ocmp, osel, rel_bias, q_norm_g)
    y_sample = _output(x2d, o_a.reshape(-1, GDN_V_W), o_b.reshape(-1, NSA_Q_W), ub.reshape(-1, UB_W), wa16, wb16, wo16)
    s_kc, s_vc, s_vs, _ = _kv_slices(ub)
    sample_out = (y_sample.reshape(x_sample.shape), s_kc, s_vc, kv4(s_ks), s_vs, s_kw.reshape(cache_k_win.shape),
                  s_vw.reshape(cache_v_win.shape), s_conv, s_gdn)

    return (prompt_out[0], sample_out[0]) + prompt_out[1:] + sample_out[1:]
```
